```python
import jax
import jax.numpy as jnp
from jax import lax
import numpy as np


D_MODEL = 1024
BATCH = 4
SEQ = 4096
DEPTH = 2

GRID_W = 64
CTX_LEN = 256
HEAD_DIM = 128
N_Q_HEADS = 8
N_KV_HEADS = 2
Q_PER_KV = N_Q_HEADS // N_KV_HEADS
ATTN_WIDTH = N_Q_HEADS * HEAD_DIM
KV_WIDTH = N_KV_HEADS * HEAD_DIM
WINDOW = 128
BLOCK = 128
BAND = BLOCK + 2 * WINDOW
N_FREQ = HEAD_DIM // 4
ROPE_BASE = 10000.0
D_RNN = D_MODEL
N_RNN_BLOCKS = 8
RNN_BLOCK_W = D_RNN // N_RNN_BLOCKS
CONV_W = 4
CONV_LEFT = 2
LRU_C = 8.0
N_BRANCH = 2
D_FF = ((8 * D_MODEL + 3 * 256 - 1) // (3 * 256)) * 256
IN_SIZES = (D_RNN, D_RNN, ATTN_WIDTH, KV_WIDTH, KV_WIDTH, N_BRANCH * D_MODEL)
IN_WIDTH = sum(IN_SIZES)
MOD_CHUNKS = 6
EPS = 1e-6
NEG_INF = -1e30

kernel_name = 'hybrid_rglru_swa_diffusion_block'


def _rms_norm(x, g):
    xf = x.astype(jnp.float32)
    y = xf * lax.rsqrt(jnp.mean(xf * xf, axis=-1, keepdims=True) + EPS)
    return y.astype(x.dtype) * g


def _split_in(p):
    offs = np.cumsum(IN_SIZES)[:-1].tolist()
    return jnp.split(p, offs, axis=-1)


def _rope_tables(seq_len):
    rows = seq_len // GRID_W
    row = jnp.repeat(jnp.arange(rows, dtype=jnp.int32), GRID_W)
    col = jnp.tile(jnp.arange(GRID_W, dtype=jnp.int32), rows)
    inv = ROPE_BASE ** (-jnp.arange(N_FREQ, dtype=jnp.float32) / N_FREQ)
    ang_r = row.astype(jnp.float32)[:, None] * inv[None, :]
    ang_c = col.astype(jnp.float32)[:, None] * inv[None, :]
    return (jnp.cos(ang_r), jnp.sin(ang_r), jnp.cos(ang_c), jnp.sin(ang_c))


def _rotate(x, cos, sin):
    x1, x2 = jnp.split(x, 2, axis=-1)
    cos = cos[None, :, None, :].astype(x.dtype)
    sin = sin[None, :, None, :].astype(x.dtype)
    return jnp.concatenate([x1 * cos - x2 * sin, x2 * cos + x1 * sin], axis=-1)


def _rope_2d(x, rope):
    cr, sr, cc, sc = rope
    xr, xc = jnp.split(x, 2, axis=-1)
    return jnp.concatenate([_rotate(xr, cr, sr), _rotate(xc, cc, sc)], axis=-1)


def _dwconv_centred(x, w, b):
    s = x.shape[1]
    xp = jnp.pad(x, ((0, 0), (CONV_LEFT, CONV_W - 1 - CONV_LEFT), (0, 0)))
    y = xp[:, 0:s] * w[0]
    for k in range(1, CONV_W):
        y = y + xp[:, k:k + s] * w[k]
    return y + b


def _block_diag(x, w, b):
    xb = x.reshape(x.shape[:-1] + (N_RNN_BLOCKS, RNN_BLOCK_W))
    y = jnp.einsum('bsni,nij->bsnj', xb, w)
    return y.reshape(x.shape) + b


def _lru_coeffs(x, wa, ba, wx, bx, lam):
    r = jax.nn.sigmoid(_block_diag(x, wa, ba)).astype(jnp.float32)
    i = jax.nn.sigmoid(_block_diag(x, wx, bx))
    log_a = LRU_C * r * jax.nn.log_sigmoid(lam.astype(jnp.float32))
    a = jnp.exp(log_a)
    b = jnp.sqrt(-jnp.expm1(2.0 * log_a)) * (i * x).astype(jnp.float32)
    return a, b


def _linear_scan(a, b, h0, reverse):
    def combine(l, r):
        return l[0] * r[0], r[0] * l[1] + r[1]
    a_cum, h = lax.associative_scan(combine, (a, b), axis=1, reverse=reverse)
    if h0 is None:
        return h
    return h + a_cum * h0[:, None, :]


def _rglru_branch(x_lat, x_ctx, conv_w, conv_b, wa, ba, wx, bx, lam, need_ctx):
    xl = _dwconv_centred(x_lat, conv_w, conv_b)
    xc = _dwconv_centred(x_ctx, conv_w, conv_b)
    ac_f, bc_f = _lru_coeffs(xc, wa[0], ba[0], wx[0], bx[0], lam[0])
    ac_b, bc_b = _lru_coeffs(xc, wa[1], ba[1], wx[1], bx[1], lam[1])
    hc_f = _linear_scan(ac_f, bc_f, None, False)
    hc_b = _linear_scan(ac_b, bc_b, None, True)
    al_f, bl_f = _lru_coeffs(xl, wa[0], ba[0], wx[0], bx[0], lam[0])
    al_b, bl_b = _lru_coeffs(xl, wa[1], ba[1], wx[1], bx[1], lam[1])
    hl = (_linear_scan(al_f, bl_f, hc_f[:, -1], False)
          + _linear_scan(al_b, bl_b, hc_b[:, 0], True))
    y_lat = hl.astype(x_lat.dtype)
    y_ctx = (hc_f + hc_b).astype(x_ctx.dtype) if need_ctx else None
    return y_lat, y_ctx


def _band(t, nb):
    n_side = WINDOW // BLOCK
    tp = jnp.pad(t, ((0, 0), (WINDOW, WINDOW), (0, 0), (0, 0)))
    tp = tp.reshape(t.shape[0], nb + 2 * n_side, BLOCK, t.shape[2], t.shape[3])
    return jnp.concatenate([tp[:, j:j + nb] for j in range(2 * n_side + 1)], axis=2)


def _latent_attention(q, k, v, kc, vc, sink):
    bsz, s = q.shape[0], q.shape[1]
    nb = s // BLOCK
    n_ctx = kc.shape[1]
    scale = HEAD_DIM ** -0.5
    qb = q.reshape(bsz, nb, BLOCK, N_KV_HEADS, Q_PER_KV, HEAD_DIM)
    kb = _band(k, nb)
    vb = _band(v, nb)
    s_band = jnp.einsum('bnqkgd,bnpkd->bnkgqp', qb, kb).astype(jnp.float32) * scale
    q_pos = jnp.arange(nb)[:, None] * BLOCK + jnp.arange(BLOCK)[None, :]
    k_pos = jnp.arange(nb)[:, None] * BLOCK - WINDOW + jnp.arange(BAND)[None, :]
    kp = k_pos[:, None, :]
    valid = (jnp.abs(kp - q_pos[:, :, None]) <= WINDOW) & (kp >= 0) & (kp < s)
    s_band = jnp.where(valid[None, :, None, None], s_band, NEG_INF)
    s_ctx = jnp.einsum('bnqkgd,bckd->bnkgqc', qb, kc).astype(jnp.float32) * scale
    s_sink = jnp.broadcast_to(
        sink.astype(jnp.float32).reshape(1, 1, N_KV_HEADS, Q_PER_KV, 1, 1),
        s_band.shape[:-1] + (1,))
    p = jax.nn.softmax(jnp.concatenate([s_band, s_ctx, s_sink], axis=-1), axis=-1)
    p_band = p[..., :BAND].astype(v.dtype)
    p_ctx = p[..., BAND:BAND + n_ctx].astype(v.dtype)
    o = (jnp.einsum('bnkgqp,bnpkd->bnqkgd', p_band, vb)
         + jnp.einsum('bnkgqc,bckd->bnqkgd', p_ctx, vc))
    return o.reshape(bsz, s, ATTN_WIDTH)


def _context_attention(qc, kc, vc, sink):
    bsz, n_ctx = qc.shape[0], qc.shape[1]
    scale = HEAD_DIM ** -0.5
    qg = qc.reshape(bsz, n_ctx, N_KV_HEADS, Q_PER_KV, HEAD_DIM)
    sc = jnp.einsum('bqkgd,bckd->bkgqc', qg, kc).astype(jnp.float32) * scale
    s_sink = jnp.broadcast_to(
        sink.astype(jnp.float32).reshape(1, N_KV_HEADS, Q_PER_KV, 1, 1),
        sc.shape[:-1] + (1,))
    p = jax.nn.softmax(jnp.concatenate([sc, s_sink], axis=-1), axis=-1)
    o = jnp.einsum('bkgqc,bckd->bqkgd', p[..., :n_ctx].astype(vc.dtype), vc)
    return o.reshape(bsz, n_ctx, ATTN_WIDTH)


def _merge(y_rnn, g_rnn, y_attn, gate_logits, w_o_rnn, w_o_attn, w_out):
    ya = (y_rnn * jax.nn.gelu(g_rnn)) @ w_o_rnn
    yb = y_attn @ w_o_attn
    ga, gb = jnp.split(jax.nn.sigmoid(gate_logits), N_BRANCH, axis=-1)
    return (ga * ya + gb * yb) @ w_out


def _mixer(h, hc, rope, w_in, conv_w, conv_b, wa, ba, wx, bx, lam, sink,
           w_o_rnn, w_o_attn, w_out, need_ctx):
    bsz, s = h.shape[0], h.shape[1]
    n_ctx = hc.shape[1]
    xr, gr, q, k, v, gl = _split_in(h @ w_in)
    xrc, grc, qc, kc, vc, glc = _split_in(hc @ w_in)
    y_rnn, y_rnn_c = _rglru_branch(xr, xrc, conv_w, conv_b, wa, ba, wx, bx, lam, need_ctx)
    q = _rope_2d(q.reshape(bsz, s, N_Q_HEADS, HEAD_DIM), rope)
    k = _rope_2d(k.reshape(bsz, s, N_KV_HEADS, HEAD_DIM), rope)
    v = v.reshape(bsz, s, N_KV_HEADS, HEAD_DIM)
    kc = kc.reshape(bsz, n_ctx, N_KV_HEADS, HEAD_DIM)
    vc = vc.reshape(bsz, n_ctx, N_KV_HEADS, HEAD_DIM)
    o = _latent_attention(q, k, v, kc, vc, sink)
    out = _merge(y_rnn, gr, o, gl, w_o_rnn, w_o_attn, w_out)
    out_c = None
    if need_ctx:
        oc = _context_attention(qc.reshape(bsz, n_ctx, N_Q_HEADS, HEAD_DIM), kc, vc, sink)
        out_c = _merge(y_rnn_c, grc, oc, glc, w_o_rnn, w_o_attn, w_out)
    return out, out_c


def _swiglu(h, w_ffn_in, w_ffn_out):
    gate, up = jnp.split(h @ w_ffn_in, 2, axis=-1)
    return (jax.nn.silu(gate) * up) @ w_ffn_out


def setup_inputs(seed: int = 0) -> dict:
    key = jax.random.key(seed)
    ks = jax.random.split(key, 24)
    f32 = jnp.float32

    def nrm(k, shape, scale):
        return jax.random.normal(k, shape, f32) * scale

    L = DEPTH
    a0 = jax.random.uniform(ks[15], (L, 2, D_RNN), f32, 0.9, 0.999)
    return {
        'x': nrm(ks[0], (BATCH, SEQ, D_MODEL), 1.0),
        'c': nrm(ks[1], (BATCH, D_MODEL), 1.0),
        'ctx': nrm(ks[2], (BATCH, CTX_LEN, D_MODEL), 1.0),
        'c_ctx': nrm(ks[3], (D_MODEL,), 1.0),
        'w_mod': nrm(ks[4], (L, D_MODEL, MOD_CHUNKS * D_MODEL), 0.5 * D_MODEL ** -0.5),
        'b_mod': nrm(ks[5], (L, MOD_CHUNKS * D_MODEL), 0.02),
        'g_mix_pre': 1.0 + nrm(ks[6], (L, D_MODEL), 0.02),
        'g_mix_post': 1.0 + nrm(ks[7], (L, D_MODEL), 0.02),
        'g_ffn_pre': 1.0 + nrm(ks[8], (L, D_MODEL), 0.02),
        'g_ffn_post': 1.0 + nrm(ks[9], (L, D_MODEL), 0.02),
        'w_in': nrm(ks[10], (L, D_MODEL, IN_WIDTH), D_MODEL ** -0.5),
        'conv_w': nrm(ks[11], (L, CONV_W, D_RNN), CONV_W ** -0.5),
        'conv_b': nrm(ks[12], (L, D_RNN), 0.02),
        'lru_wa': nrm(ks[13], (L, 2, N_RNN_BLOCKS, RNN_BLOCK_W, RNN_BLOCK_W), RNN_BLOCK_W ** -0.5),
        'lru_ba': nrm(ks[14], (L, 2, D_RNN), 0.02),
        'lru_wx': nrm(ks[16], (L, 2, N_RNN_BLOCKS, RNN_BLOCK_W, RNN_BLOCK_W), RNN_BLOCK_W ** -0.5),
        'lru_bx': nrm(ks[17], (L, 2, D_RNN), 0.02),
        'lru_lam': jnp.log(a0) - jnp.log1p(-a0),
        'attn_sink': nrm(ks[18], (L, N_Q_HEADS), 0.5),
        'w_o_rnn': nrm(ks[19], (L, D_RNN, D_MODEL), D_RNN ** -0.5),
        'w_o_attn': nrm(ks[20], (L, ATTN_WIDTH, D_MODEL), ATTN_WIDTH ** -0.5),
        'w_out': nrm(ks[21], (L, D_MODEL, D_MODEL), D_MODEL ** -0.5),
        'w_ffn_in': nrm(ks[22], (L, D_MODEL, 2 * D_FF), D_MODEL ** -0.5),
        'w_ffn_out': nrm(ks[23], (L, D_FF, D_MODEL), D_FF ** -0.5),
    }


def reference(x, c, ctx, c_ctx, w_mod, b_mod, g_mix_pre, g_mix_post, g_ffn_pre,
              g_ffn_post, w_in, conv_w, conv_b, lru_wa, lru_ba, lru_wx, lru_bx,
              lru_lam, attn_sink, w_o_rnn, w_o_attn, w_out, w_ffn_in, w_ffn_out):
    rope = _rope_tables(x.shape[1])
    for l in range(DEPTH):
        need_ctx = l < DEPTH - 1
        mod = jax.nn.silu(c) @ w_mod[l] + b_mod[l]
        sh1, sc1, ga1, sh2, sc2, ga2 = jnp.split(mod[:, None, :], MOD_CHUNKS, axis=-1)
        mod_c = jax.nn.silu(c_ctx) @ w_mod[l] + b_mod[l]
        sh1c, sc1c, ga1c, sh2c, sc2c, ga2c = jnp.split(mod_c, MOD_CHUNKS, axis=-1)

        h = _rms_norm(x, g_mix_pre[l]) * (1.0 + sc1) + sh1
        hc = _rms_norm(ctx, g_mix_pre[l]) * (1.0 + sc1c) + sh1c
        m, mc = _mixer(h, hc, rope, w_in[l], conv_w[l], conv_b[l], lru_wa[l], lru_ba[l],
                       lru_wx[l], lru_bx[l], lru_lam[l], attn_sink[l], w_o_rnn[l],
                       w_o_attn[l], w_out[l], need_ctx)
        x = x + ga1 * _rms_norm(m, g_mix_post[l])
        if need_ctx:
            ctx = ctx + ga1c * _rms_norm(mc, g_mix_post[l])

        h = _rms_norm(x, g_ffn_pre[l]) * (1.0 + sc2) + sh2
        x = x + ga2 * _rms_norm(_swiglu(h, w_ffn_in[l], w_ffn_out[l]), g_ffn_post[l])
        if need_ctx:
            hc = _rms_norm(ctx, g_ffn_pre[l]) * (1.0 + sc2c) + sh2c
            ctx = ctx + ga2c * _rms_norm(_swiglu(hc, w_ffn_in[l], w_ffn_out[l]), g_ffn_post[l])
    return x
```

```python
import functools
import math

import jax
import jax.numpy as jnp
from jax import lax
from jax.experimental import pallas as pl
from jax.experimental.pallas import tpu as pltpu

D_MODEL = 1024
HEAD_DIM = 128
N_Q_HEADS = 8
N_KV_HEADS = 2
Q_PER_KV = N_Q_HEADS // N_KV_HEADS
KV_WIDTH = N_KV_HEADS * HEAD_DIM
WINDOW = 128
GRID_W = 64
N_FREQ = HEAD_DIM // 4
ROPE_BASE = 10000.0
N_RNN_BLOCKS = 8
RNN_BLOCK_W = D_MODEL // N_RNN_BLOCKS
LRU_C = 8.0
D_FF = 2816
EPS = 1e-6
NEG_INF = -1e30
MOD_CHUNKS = 6

LANES = 128
SUBLANES = 8
VMEM_LIMIT_BYTES = 56 * 1024 * 1024

CHUNK = 256
Q_BLOCK = 128
BAND = Q_BLOCK + 2 * WINDOW

BF16 = jnp.bfloat16
F32 = jnp.float32


def _dot(a, b):
    return jnp.dot(a, b, preferred_element_type=F32)


def _sigmoid(x):
    return 0.5 * jnp.tanh(0.5 * x) + 0.5


def _gelu_tanh(x):
    c = math.sqrt(2.0 / math.pi)
    return 0.5 * x * (1.0 + jnp.tanh(c * (x + 0.044715 * (x * x * x))))


def _rms(x):
    return x * lax.rsqrt(jnp.mean(x * x, axis=-1, keepdims=True) + EPS)


def _resident(shape):
    nd = len(shape)
    return pl.BlockSpec(shape, lambda *_: (0,) * nd, pipeline_mode=pl.Buffered(1))


def _mod_kernel(c_ref, w_ref, b_ref, o_ref):
    c = c_ref[...]
    s = (c * _sigmoid(c)).astype(BF16)
    o_ref[...] = _dot(s, w_ref[...].astype(BF16)) + b_ref[...]


def _modulation(cs, w_mod, b_mod):
    n_layers, d, width = w_mod.shape
    tn = 1536
    return pl.pallas_call(
        _mod_kernel,
        grid=(n_layers, width // tn),
        in_specs=[
            pl.BlockSpec((SUBLANES, d), lambda l, j: (0, 0)),
            pl.BlockSpec((None, d, tn), lambda l, j: (l, 0, j)),
            pl.BlockSpec((None, 1, tn), lambda l, j: (l, 0, j)),
        ],
        out_specs=pl.BlockSpec((None, SUBLANES, tn), lambda l, j: (l, 0, j)),
        out_shape=jax.ShapeDtypeStruct((n_layers, SUBLANES, width), F32),
        compiler_params=pltpu.CompilerParams(
            dimension_semantics=("arbitrary", "arbitrary"),
            vmem_limit_bytes=VMEM_LIMIT_BYTES),
        name="modulation",
    )(cs, w_mod, b_mod.reshape(n_layers, 1, width))


def _rope(t, cos_ref, sin_ref, n_heads):
    cos = cos_ref[...]
    sin = sin_ref[...]
    lane = lax.broadcasted_iota(jnp.int32, cos.shape, 1)
    first = (lane & (2 * N_FREQ - 1)) < N_FREQ
    outs = []
    for h in range(n_heads):
        th = t[:, h * HEAD_DIM:(h + 1) * HEAD_DIM]
        partner = jnp.where(first,
                            pltpu.roll(th, HEAD_DIM - N_FREQ, axis=1),
                            pltpu.roll(th, N_FREQ, axis=1))
        outs.append(th * cos + partner * sin)
    return jnp.concatenate(outs, axis=-1)


def _in_kernel(x_ref, mod_ref, g_ref, w_ref, cq_ref, sq_ref, ck_ref, sk_ref,
               xr_ref, gr_ref, q_ref, kv_ref, gl_ref):
    d = D_MODEL
    x = x_ref[...]
    h = (_rms(x) * g_ref[...]) * (1.0 + mod_ref[1:2, :]) + mod_ref[0:1, :]
    h = h.astype(BF16)
    xr_ref[...] = _dot(h, w_ref[:, 0:d])
    gr_ref[...] = _dot(h, w_ref[:, d:2 * d])
    q = _dot(h, w_ref[:, 2 * d:3 * d])
    q_ref[...] = _rope(q, cq_ref, sq_ref, N_Q_HEADS).astype(BF16)
    k = _dot(h, w_ref[:, 3 * d:3 * d + KV_WIDTH])
    kv_ref[:, 0:KV_WIDTH] = _rope(k, ck_ref, sk_ref, N_KV_HEADS).astype(BF16)
    v = _dot(h, w_ref[:, 3 * d + KV_WIDTH:3 * d + 2 * KV_WIDTH])
    kv_ref[:, KV_WIDTH:2 * KV_WIDTH] = v.astype(BF16)
    gl_ref[...] = _dot(h, w_ref[:, 3 * d + 2 * KV_WIDTH:5 * d + 2 * KV_WIDTH])


def _in_proj(x_all, mod_l, g_pre, w_in, tables, geo):
    nt, d = x_all.shape
    tm = geo["tm"]
    n_tiles = nt // tm
    row = lambda i: (i, 0)
    tab = pl.BlockSpec((tm, LANES), lambda i: (geo["table_block"](i), 0))
    return pl.pallas_call(
        _in_kernel,
        grid=(n_tiles,),
        in_specs=[
            pl.BlockSpec((tm, d), row),
            pl.BlockSpec((None, MOD_CHUNKS, d), lambda i: (geo["mod_row"](i), 0, 0)),
            _resident((1, d)),
            _resident(w_in.shape),
            tab, tab, tab, tab,
        ],
        out_specs=[
            pl.BlockSpec((tm, d), row),
            pl.BlockSpec((tm, d), row),
            pl.BlockSpec((tm, d), row),
            pl.BlockSpec((tm, 2 * KV_WIDTH), row),
            pl.BlockSpec((tm, 2 * d), row),
        ],
        out_shape=[
            jax.ShapeDtypeStruct((nt, d), F32),
            jax.ShapeDtypeStruct((nt, d), F32),
            jax.ShapeDtypeStruct((nt, d), BF16),
            jax.ShapeDtypeStruct((nt, 2 * KV_WIDTH), BF16),
            jax.ShapeDtypeStruct((nt, 2 * d), F32),
        ],
        compiler_params=pltpu.CompilerParams(
            dimension_semantics=("arbitrary",),
            vmem_limit_bytes=VMEM_LIMIT_BYTES),
        name="in_proj",
    )(x_all, mod_l, g_pre.reshape(1, d), w_in, *tables)


def _rnn_kernel(xf_ref, xfp_ref, xfn_ref, xb_ref, xbp_ref, xbn_ref,
                cw_ref, cb_ref, wg_ref, bg_ref, lam_ref,
                hf_ref, hb_ref,
                xe_s, a_s, b_s, h_s, carry_s):
    j = pl.program_id(1)
    n_steps = pl.num_programs(1)
    nb, bw = N_RNN_BLOCKS, RNN_BLOCK_W

    @pl.when(j == 0)
    def _():
        carry_s[...] = jnp.zeros_like(carry_s)

    def coeffs(direction, x_ref, xp_ref, xn_ref, has_prev, has_next):
        xe_s[0:SUBLANES, :] = jnp.where(has_prev, xp_ref[...], 0.0)
        xe_s[SUBLANES:SUBLANES + CHUNK, :] = x_ref[...]
        xe_s[SUBLANES + CHUNK:2 * SUBLANES + CHUNK, :] = jnp.where(has_next, xn_ref[...], 0.0)
        xc = cb_ref[...] + cw_ref[0:1, :] * xe_s[pl.ds(SUBLANES - 2, CHUNK), :]
        xc = xc + cw_ref[1:2, :] * xe_s[pl.ds(SUBLANES - 1, CHUNK), :]
        xc = xc + cw_ref[2:3, :] * xe_s[pl.ds(SUBLANES, CHUNK), :]
        xc = xc + cw_ref[3:4, :] * xe_s[pl.ds(SUBLANES + 1, CHUNK), :]
        lam = lam_ref[direction]
        c8 = LRU_C * (jnp.minimum(lam, 0.0) - jnp.log1p(jnp.exp(-jnp.abs(lam))))
        for n in range(nb):
            sl = slice(n * bw, (n + 1) * bw)
            xn = xc[:, sl]
            pre = _dot(xn.astype(BF16), wg_ref[direction, n]) + bg_ref[direction, n]
            r = _sigmoid(pre[:, 0:bw])
            i = _sigmoid(pre[:, bw:2 * bw])
            log_a = c8[:, sl] * r
            a = jnp.exp(log_a)
            b = jnp.sqrt(1.0 - a * a) * (i * xn)
            a_s[direction, pl.ds(n, CHUNK, stride=nb), :] = a
            b_s[direction, pl.ds(n, CHUNK, stride=nb), :] = b

    coeffs(0, xf_ref, xfp_ref, xfn_ref, j >= 2, jnp.logical_and(j >= 1, j < n_steps - 1))
    coeffs(1, xb_ref, xbp_ref, xbn_ref, jnp.logical_and(j >= 1, j < n_steps - 1), j >= 2)

    def step(t, carry):
        hf, hb = carry
        tf = pl.multiple_of(t * nb, nb)
        tb = pl.multiple_of((CHUNK - 1 - t) * nb, nb)
        hf = a_s[0, pl.ds(tf, nb), :] * hf + b_s[0, pl.ds(tf, nb), :]
        hb = a_s[1, pl.ds(tb, nb), :] * hb + b_s[1, pl.ds(tb, nb), :]
        h_s[0, pl.ds(tf, nb), :] = hf
        h_s[1, pl.ds(tb, nb), :] = hb
        return hf, hb

    hf, hb = lax.fori_loop(0, CHUNK, step, (carry_s[0], carry_s[1]), unroll=8)
    carry_s[0] = hf
    carry_s[1] = hb

    for n in range(nb):
        sl = slice(n * bw, (n + 1) * bw)
        hf_ref[:, sl] = h_s[0, pl.ds(n, CHUNK, stride=nb), :]
        hb_ref[:, sl] = h_s[1, pl.ds(n, CHUNK, stride=nb), :]


def _rnn_branch(xr, conv_w, conv_b, w_gate, b_gate, lam, geo):
    nt, d = xr.shape
    bsz, n_lat = geo["batch"], geo["lat_chunks"]
    n_steps = n_lat + 1
    n_halo = nt // SUBLANES
    per = CHUNK // SUBLANES

    def fwd_blk(b, j):
        return jnp.where(j == 0, bsz * n_lat + b, b * n_lat + j - 1)

    def bwd_blk(b, j):
        return jnp.where(j == 0, bsz * n_lat + b, b * n_lat + n_lat - j)

    def chunk(blk):
        return pl.BlockSpec((CHUNK, d), lambda b, j: (blk(b, j), 0))

    def prev(blk):
        return pl.BlockSpec(
            (SUBLANES, d), lambda b, j: (jnp.maximum(blk(b, j) * per - 1, 0), 0))

    def nxt(blk):
        return pl.BlockSpec(
            (SUBLANES, d), lambda b, j: (jnp.minimum((blk(b, j) + 1) * per, n_halo - 1), 0))

    nb, bw = N_RNN_BLOCKS, RNN_BLOCK_W
    return pl.pallas_call(
        _rnn_kernel,
        grid=(bsz, n_steps),
        in_specs=[
            chunk(fwd_blk), prev(fwd_blk), nxt(fwd_blk),
            chunk(bwd_blk), prev(bwd_blk), nxt(bwd_blk),
            _resident(conv_w.shape),
            _resident((1, d)),
            _resident(w_gate.shape),
            _resident(b_gate.shape),
            _resident(lam.shape),
        ],
        out_specs=[chunk(fwd_blk), chunk(bwd_blk)],
        out_shape=[jax.ShapeDtypeStruct((nt, d), F32)] * 2,
        scratch_shapes=[
            pltpu.VMEM((CHUNK + 2 * SUBLANES, d), F32),
            pltpu.VMEM((2, CHUNK * nb, bw), F32),
            pltpu.VMEM((2, CHUNK * nb, bw), F32),
            pltpu.VMEM((2, CHUNK * nb, bw), F32),
            pltpu.VMEM((2, nb, bw), F32),
        ],
        compiler_params=pltpu.CompilerParams(
            dimension_semantics=("arbitrary", "arbitrary"),
            vmem_limit_bytes=VMEM_LIMIT_BYTES),
        name="rglru",
    )(xr, xr, xr, xr, xr, xr, conv_w, conv_b.reshape(1, d), w_gate, b_gate, lam)


def _stack_heads(q, g):
    base = g * Q_PER_KV * HEAD_DIM
    return jnp.concatenate(
        [q[:, base + h * HEAD_DIM: base + (h + 1) * HEAD_DIM] for h in range(Q_PER_KV)], axis=0)


def _sink_column(sink_ref, g, rows):
    return jnp.concatenate(
        [jnp.full((rows, 1), sink_ref[g * Q_PER_KV + h], F32) for h in range(Q_PER_KV)], axis=0)


def _nt_dot(a, b):
    return lax.dot_general(a, b, (((1,), (1,)), ((), ())), preferred_element_type=F32)


def _lat_attn_kernel(sink_ref, q_ref, kv_ref, kvc_ref, o_ref, *, seq_len):
    qb = pl.program_id(1)
    start = jnp.clip((qb - 1) * Q_BLOCK, 0, seq_len - BAND)
    start = pl.multiple_of(start, Q_BLOCK)
    q = q_ref[...]
    rows = Q_PER_KV * Q_BLOCK
    tok = lax.broadcasted_iota(jnp.int32, (rows, BAND), 0) & (Q_BLOCK - 1)
    col = lax.broadcasted_iota(jnp.int32, (rows, BAND), 1)
    valid = jnp.abs(col - tok + (start - qb * Q_BLOCK)) <= WINDOW
    for g in range(N_KV_HEADS):
        ks = slice(g * HEAD_DIM, (g + 1) * HEAD_DIM)
        vs = slice(KV_WIDTH + g * HEAD_DIM, KV_WIDTH + (g + 1) * HEAD_DIM)
        qs = _stack_heads(q, g)
        s_band = jnp.where(valid, _nt_dot(qs, kv_ref[pl.ds(start, BAND), ks]), NEG_INF)
        s_ctx = _nt_dot(qs, kvc_ref[:, ks])
        sink = _sink_column(sink_ref, g, Q_BLOCK)
        m = jnp.maximum(jnp.maximum(jnp.max(s_band, axis=1, keepdims=True),
                                    jnp.max(s_ctx, axis=1, keepdims=True)), sink)
        p_band = jnp.exp(s_band - m)
        p_ctx = jnp.exp(s_ctx - m)
        denom = (jnp.sum(p_band, axis=1, keepdims=True)
                 + jnp.sum(p_ctx, axis=1, keepdims=True) + jnp.exp(sink - m))
        o = _dot(p_band.astype(BF16), kv_ref[pl.ds(start, BAND), vs])
        o = (o + _dot(p_ctx.astype(BF16), kvc_ref[:, vs])) / denom
        for h in range(Q_PER_KV):
            c0 = (g * Q_PER_KV + h) * HEAD_DIM
            o_ref[:, c0:c0 + HEAD_DIM] = o[h * Q_BLOCK:(h + 1) * Q_BLOCK].astype(BF16)


def _ctx_attn_kernel(sink_ref, q_ref, kvc_ref, o_any, o_ref):
    del o_any
    q = q_ref[...]
    n_ctx = q.shape[0]
    for g in range(N_KV_HEADS):
        ks = slice(g * HEAD_DIM, (g + 1) * HEAD_DIM)
        vs = slice(KV_WIDTH + g * HEAD_DIM, KV_WIDTH + (g + 1) * HEAD_DIM)
        qs = _stack_heads(q, g)
        s_ctx = _nt_dot(qs, kvc_ref[:, ks])
        sink = _sink_column(sink_ref, g, n_ctx)
        m = jnp.maximum(jnp.max(s_ctx, axis=1, keepdims=True), sink)
        p_ctx = jnp.exp(s_ctx - m)
        denom = jnp.sum(p_ctx, axis=1, keepdims=True) + jnp.exp(sink - m)
        o = _dot(p_ctx.astype(BF16), kvc_ref[:, vs]) / denom
        for h in range(Q_PER_KV):
            c0 = (g * Q_PER_KV + h) * HEAD_DIM
            o_ref[:, c0:c0 + HEAD_DIM] = o[h * n_ctx:(h + 1) * n_ctx].astype(BF16)


def _attention(q, kv, sink, geo, with_ctx_queries):
    nt, d = q.shape
    bsz, seq_len, n_ctx = geo["batch"], geo["seq"], geo["ctx"]
    nqb = seq_len // Q_BLOCK
    ctx_blk0 = bsz * seq_len // n_ctx
    smem = pl.BlockSpec(memory_space=pltpu.SMEM)
    o = pl.pallas_call(
        functools.partial(_lat_attn_kernel, seq_len=seq_len),
        grid=(bsz, nqb),
        in_specs=[
            smem,
            pl.BlockSpec((Q_BLOCK, d), lambda b, i: (b * nqb + i, 0)),
            pl.BlockSpec((seq_len, 2 * KV_WIDTH), lambda b, i: (b, 0)),
            pl.BlockSpec((n_ctx, 2 * KV_WIDTH), lambda b, i: (ctx_blk0 + b, 0)),
        ],
        out_specs=pl.BlockSpec((Q_BLOCK, d), lambda b, i: (b * nqb + i, 0)),
        out_shape=jax.ShapeDtypeStruct((nt if with_ctx_queries else bsz * seq_len, d), BF16),
        compiler_params=pltpu.CompilerParams(
            dimension_semantics=("arbitrary", "arbitrary"),
            vmem_limit_bytes=VMEM_LIMIT_BYTES),
        name="lat_attention",
    )(sink, q, kv, kv)
    if not with_ctx_queries:
        return o
    return pl.pallas_call(
        _ctx_attn_kernel,
        grid=(bsz,),
        in_specs=[
            smem,
            pl.BlockSpec((n_ctx, d), lambda b: (ctx_blk0 + b, 0)),
            pl.BlockSpec((n_ctx, 2 * KV_WIDTH), lambda b: (ctx_blk0 + b, 0)),
            pl.BlockSpec(memory_space=pl.ANY),
        ],
        out_specs=pl.BlockSpec((n_ctx, d), lambda b: (ctx_blk0 + b, 0)),
        out_shape=jax.ShapeDtypeStruct((nt, d), BF16),
        input_output_aliases={3: 0},
        compiler_params=pltpu.CompilerParams(
            dimension_semantics=("arbitrary",),
            vmem_limit_bytes=VMEM_LIMIT_BYTES),
        name="ctx_attention",
    )(sink, q, kv, o)


def _merge_kernel(x_ref, hf_ref, hb_ref, gr_ref, o_ref, gl_ref, mod_ref, g_ref,
                  wr_ref, wa_ref, wo_ref, out_ref):
    d = D_MODEL
    y = (hf_ref[...] + hb_ref[...]) * _gelu_tanh(gr_ref[...])
    ya = _dot(y.astype(BF16), wr_ref[...])
    yb = _dot(o_ref[...], wa_ref[...])
    ga = _sigmoid(gl_ref[:, 0:d])
    gb = _sigmoid(gl_ref[:, d:2 * d])
    m = _dot((ga * ya + gb * yb).astype(BF16), wo_ref[...])
    out_ref[...] = x_ref[...] + mod_ref[2:3, :] * (_rms(m) * g_ref[...])


def _merge(x_all, hf, hb, gr, o, gl, mod_l, g_post, w_o_rnn, w_o_attn, w_out, geo, n_tiles):
    d = x_all.shape[1]
    tm = geo["tm"]
    row = lambda i: (i, 0)
    tile = pl.BlockSpec((tm, d), row)
    return pl.pallas_call(
        _merge_kernel,
        grid=(n_tiles,),
        in_specs=[
            tile, tile, tile, tile, tile,
            pl.BlockSpec((tm, 2 * d), row),
            pl.BlockSpec((None, MOD_CHUNKS, d), lambda i: (geo["mod_row"](i), 0, 0)),
            _resident((1, d)),
            _resident(w_o_rnn.shape), _resident(w_o_attn.shape), _resident(w_out.shape),
        ],
        out_specs=tile,
        out_shape=jax.ShapeDtypeStruct((n_tiles * tm, d), F32),
        compiler_params=pltpu.CompilerParams(
            dimension_semantics=("arbitrary",),
            vmem_limit_bytes=VMEM_LIMIT_BYTES),
        name="merge",
    )(x_all, hf, hb, gr, o, gl, mod_l, g_post.reshape(1, d), w_o_rnn, w_o_attn, w_out)


def _ffn_kernel(x_ref, mod_ref, gpre_ref, gpost_ref, w1_ref, w2_ref, out_ref):
    x = x_ref[...]
    h = (_rms(x) * gpre_ref[...]) * (1.0 + mod_ref[4:5, :]) + mod_ref[3:4, :]
    h = h.astype(BF16)
    gate = _dot(h, w1_ref[:, 0:D_FF])
    up = _dot(h, w1_ref[:, D_FF:2 * D_FF])
    act = (gate * _sigmoid(gate) * up).astype(BF16)
    f = _dot(act, w2_ref[...])
    out_ref[...] = x + mod_ref[5:6, :] * (_rms(f) * gpost_ref[...])


def _ffn(x_all, mod_l, g_pre, g_post, w1, w2, geo, n_tiles):
    d = x_all.shape[1]
    tm = geo["tm"]
    tile = pl.BlockSpec((tm, d), lambda i: (i, 0))
    return pl.pallas_call(
        _ffn_kernel,
        grid=(n_tiles,),
        in_specs=[
            tile,
            pl.BlockSpec((None, MOD_CHUNKS, d), lambda i: (geo["mod_row"](i), 0, 0)),
            _resident((1, d)), _resident((1, d)),
            _resident(w1.shape), _resident(w2.shape),
        ],
        out_specs=tile,
        out_shape=jax.ShapeDtypeStruct((n_tiles * tm, d), F32),
        compiler_params=pltpu.CompilerParams(
            dimension_semantics=("arbitrary",),
            vmem_limit_bytes=VMEM_LIMIT_BYTES),
        name="ffn",
    )(x_all, mod_l, g_pre.reshape(1, d), g_post.reshape(1, d), w1, w2)


def _rope_tables(seq_len, pad_rows):
    pos = jnp.arange(seq_len, dtype=jnp.int32)
    inv = ROPE_BASE ** (-jnp.arange(N_FREQ, dtype=F32) / N_FREQ)
    ang_r = (pos // GRID_W).astype(F32)[:, None] * inv[None, :]
    ang_c = (pos % GRID_W).astype(F32)[:, None] * inv[None, :]
    cos = jnp.concatenate([jnp.cos(ang_r)] * 2 + [jnp.cos(ang_c)] * 2, axis=-1)
    sin = jnp.concatenate([-jnp.sin(ang_r), jnp.sin(ang_r), -jnp.sin(ang_c), jnp.sin(ang_c)], axis=-1)
    cos = jnp.concatenate([cos, jnp.ones((pad_rows, HEAD_DIM), F32)], axis=0)
    sin = jnp.concatenate([sin, jnp.zeros((pad_rows, HEAD_DIM), F32)], axis=0)
    scale = HEAD_DIM ** -0.5
    return cos * scale, sin * scale, cos, sin


def kernel(x, c, ctx, c_ctx, w_mod, b_mod, g_mix_pre, g_mix_post, g_ffn_pre, g_ffn_post, w_in, conv_w, conv_b, lru_wa, lru_ba, lru_wx, lru_bx, lru_lam, attn_sink, w_o_rnn, w_o_attn, w_out, w_ffn_in, w_ffn_out):
    bsz, seq_len, d = x.shape
    n_ctx = ctx.shape[1]
    depth = w_mod.shape[0]
    assert d == D_MODEL and n_ctx == CHUNK and seq_len % CHUNK == 0 and seq_len >= BAND
    assert bsz + 1 <= SUBLANES and seq_len % GRID_W == 0

    n_lat_rows, n_ctx_rows = bsz * seq_len, bsz * n_ctx
    tm = 512 if (n_ctx_rows % 512 == 0 and seq_len % 512 == 0) else CHUNK
    n_lat_tiles, n_ctx_tiles = n_lat_rows // tm, n_ctx_rows // tm
    tiles_per_batch = seq_len // tm
    geo = {
        "batch": bsz, "seq": seq_len, "ctx": n_ctx, "tm": tm,
        "lat_chunks": seq_len // CHUNK,
        "mod_row": lambda i: jnp.where(i < n_lat_tiles, i // tiles_per_batch, bsz),
        "table_block": lambda i: jnp.where(
            i < n_lat_tiles, i % tiles_per_batch, tiles_per_batch + i - n_lat_tiles),
    }

    cs = jnp.concatenate(
        [c, c_ctx[None, :], jnp.zeros((SUBLANES - bsz - 1, d), F32)], axis=0)
    mod = _modulation(cs, w_mod, b_mod).reshape(depth, SUBLANES, MOD_CHUNKS, d)

    tables = _rope_tables(seq_len, n_ctx_rows)

    w_gate = jnp.concatenate([lru_wa, lru_wx], axis=-1).astype(BF16)
    b_gate = jnp.concatenate(
        [lru_ba.reshape(depth, 2, N_RNN_BLOCKS, 1, RNN_BLOCK_W),
         lru_bx.reshape(depth, 2, N_RNN_BLOCKS, 1, RNN_BLOCK_W)], axis=-1)
    lam = lru_lam.reshape(depth, 2, 1, d)

    x_all = jnp.concatenate([x.reshape(n_lat_rows, d), ctx.reshape(n_ctx_rows, d)], axis=0)
    for l in range(depth):
        need_ctx = l < depth - 1
        n_out_tiles = n_lat_tiles + n_ctx_tiles if need_ctx else n_lat_tiles
        xr, gr, q, kv, gl = _in_proj(
            x_all, mod[l], g_mix_pre[l], w_in[l].astype(BF16), tables, geo)
        hf, hb = _rnn_branch(xr, conv_w[l], conv_b[l], w_gate[l], b_gate[l], lam[l], geo)
        o = _attention(q, kv, attn_sink[l], geo, need_ctx)
        x_all = _merge(x_all, hf, hb, gr, o, gl, mod[l], g_mix_post[l],
                       w_o_rnn[l].astype(BF16), w_o_attn[l].astype(BF16),
                       w_out[l].astype(BF16), geo, n_out_tiles)
        x_all = _ffn(x_all, mod[l], g_ffn_pre[l], g_ffn_post[l],
                     w_ffn_in[l].astype(BF16), w_ffn_out[l].astype(BF16), geo, n_out_tiles)
    return x_all[:n_lat_rows].reshape(bsz, seq_len, d)
```

```python
import functools
import math

import jax
import jax.numpy as jnp
from jax import lax
from jax.experimental import pallas as pl
from jax.experimental.pallas import tpu as pltpu

D_MODEL = 1024
HEAD_DIM = 128
N_Q_HEADS = 8
N_KV_HEADS = 2
Q_PER_KV = N_Q_HEADS // N_KV_HEADS
KV_WIDTH = N_KV_HEADS * HEAD_DIM
WINDOW = 128
GRID_W = 64
N_FREQ = HEAD_DIM // 4
ROPE_BASE = 10000.0
N_RNN_BLOCKS = 8
RNN_BLOCK_W = D_MODEL // N_RNN_BLOCKS
LRU_C = 8.0
D_FF = 2816
EPS = 1e-6
NEG_INF = -1e30
MOD_CHUNKS = 6

LANES = 128
SUBLANES = 8
VMEM_LIMIT_BYTES = 56 * 1024 * 1024

CHUNK = 256
Q_BLOCK = 128
KEY_BLOCK = 128
BAND = Q_BLOCK + 2 * WINDOW
LOG2_E = math.log2(math.e)

BF16 = jnp.bfloat16
F32 = jnp.float32


def _dot(a, b):
    return jnp.dot(a, b, preferred_element_type=F32)


def _sigmoid(x):
    return 0.5 * jnp.tanh(0.5 * x) + 0.5


def _gelu_tanh(x):
    c = math.sqrt(2.0 / math.pi)
    return 0.5 * x * (1.0 + jnp.tanh(c * (x + 0.044715 * (x * x * x))))


def _rms(x):
    return x * lax.rsqrt(jnp.mean(x * x, axis=-1, keepdims=True) + EPS)


def _resident(shape):
    nd = len(shape)
    return pl.BlockSpec(shape, lambda *_: (0,) * nd, pipeline_mode=pl.Buffered(1))


def _mod_kernel(c_ref, w_ref, b_ref, o_ref):
    c = c_ref[...]
    s = (c * _sigmoid(c)).astype(BF16)
    o_ref[...] = _dot(s, w_ref[...].astype(BF16)) + b_ref[...]


def _modulation(cs, w_mod, b_mod):
    n_layers, d, width = w_mod.shape
    tn = 1536
    return pl.pallas_call(
        _mod_kernel,
        grid=(n_layers, width // tn),
        in_specs=[
            pl.BlockSpec((SUBLANES, d), lambda l, j: (0, 0)),
            pl.BlockSpec((None, d, tn), lambda l, j: (l, 0, j)),
            pl.BlockSpec((None, 1, tn), lambda l, j: (l, 0, j)),
        ],
        out_specs=pl.BlockSpec((None, SUBLANES, tn), lambda l, j: (l, 0, j)),
        out_shape=jax.ShapeDtypeStruct((n_layers, SUBLANES, width), F32),
        compiler_params=pltpu.CompilerParams(
            dimension_semantics=("arbitrary", "arbitrary"),
            vmem_limit_bytes=VMEM_LIMIT_BYTES),
        name="modulation",
    )(cs, w_mod, b_mod.reshape(n_layers, 1, width))


def _rope(t, cos_ref, sin_ref, n_heads):
    cos = cos_ref[...]
    sin = sin_ref[...]
    lane = lax.broadcasted_iota(jnp.int32, cos.shape, 1)
    first = (lane & (2 * N_FREQ - 1)) < N_FREQ
    outs = []
    for h in range(n_heads):
        th = t[:, h * HEAD_DIM:(h + 1) * HEAD_DIM]
        partner = jnp.where(first,
                            pltpu.roll(th, HEAD_DIM - N_FREQ, axis=1),
                            pltpu.roll(th, N_FREQ, axis=1))
        outs.append(th * cos + partner * sin)
    return jnp.concatenate(outs, axis=-1)


def _in_kernel(x_ref, mod_ref, g_ref, w_ref, cq_ref, sq_ref, ck_ref, sk_ref,
               xr_ref, gr_ref, q_ref, k_ref, vt_ref, gl_ref):
    d = D_MODEL
    x = x_ref[...]
    h = (_rms(x) * g_ref[...]) * (1.0 + mod_ref[1:2, :]) + mod_ref[0:1, :]
    h = h.astype(BF16)
    xr_ref[...] = _dot(h, w_ref[:, 0:d])
    gr_ref[...] = _dot(h, w_ref[:, d:2 * d])
    q = _dot(h, w_ref[:, 2 * d:3 * d])
    q_ref[...] = _rope(q, cq_ref, sq_ref, N_Q_HEADS).astype(BF16)
    k = _dot(h, w_ref[:, 3 * d:3 * d + KV_WIDTH])
    k_ref[...] = _rope(k, ck_ref, sk_ref, N_KV_HEADS).astype(BF16)
    v = _dot(h, w_ref[:, 3 * d + KV_WIDTH:3 * d + 2 * KV_WIDTH])
    for t in range(vt_ref.shape[0]):
        vt_ref[t] = v[t * KEY_BLOCK:(t + 1) * KEY_BLOCK, :].T.astype(BF16)
    gl_ref[...] = _dot(h, w_ref[:, 3 * d + 2 * KV_WIDTH:5 * d + 2 * KV_WIDTH])


def _in_proj(x_all, mod_l, g_pre, w_in, tables, geo):
    nt, d = x_all.shape
    tm = geo["tm"]
    n_tiles = nt // tm
    row = lambda i: (i, 0)
    tab = pl.BlockSpec((tm, LANES), lambda i: (geo["table_block"](i), 0))
    return pl.pallas_call(
        _in_kernel,
        grid=(n_tiles,),
        in_specs=[
            pl.BlockSpec((tm, d), row),
            pl.BlockSpec((None, MOD_CHUNKS, d), lambda i: (geo["mod_row"](i), 0, 0)),
            _resident((1, d)),
            _resident(w_in.shape),
            tab, tab, tab, tab,
        ],
        out_specs=[
            pl.BlockSpec((tm, d), row),
            pl.BlockSpec((tm, d), row),
            pl.BlockSpec((tm, d), row),
            pl.BlockSpec((tm, KV_WIDTH), row),
            pl.BlockSpec((tm // KEY_BLOCK, KV_WIDTH, KEY_BLOCK), lambda i: (i, 0, 0)),
            pl.BlockSpec((tm, 2 * d), row),
        ],
        out_shape=[
            jax.ShapeDtypeStruct((nt, d), F32),
            jax.ShapeDtypeStruct((nt, d), F32),
            jax.ShapeDtypeStruct((nt, d), BF16),
            jax.ShapeDtypeStruct((nt, KV_WIDTH), BF16),
            jax.ShapeDtypeStruct((nt // KEY_BLOCK, KV_WIDTH, KEY_BLOCK), BF16),
            jax.ShapeDtypeStruct((nt, 2 * d), F32),
        ],
        compiler_params=pltpu.CompilerParams(
            dimension_semantics=("arbitrary",),
            vmem_limit_bytes=VMEM_LIMIT_BYTES),
        name="in_proj",
    )(x_all, mod_l, g_pre.reshape(1, d), w_in, *tables)


def _rnn_kernel(xf_ref, xfp_ref, xfn_ref, xb_ref, xbp_ref, xbn_ref,
                cw_ref, cb_ref, wg_ref, bg_ref, lam_ref,
                hf_ref, hb_ref,
                xe_s, a_s, b_s, h_s, carry_s):
    j = pl.program_id(1)
    n_steps = pl.num_programs(1)
    nb, bw = N_RNN_BLOCKS, RNN_BLOCK_W

    @pl.when(j == 0)
    def _():
        carry_s[...] = jnp.zeros_like(carry_s)

    def coeffs(direction, x_ref, xp_ref, xn_ref, has_prev, has_next):
        xe_s[0:SUBLANES, :] = jnp.where(has_prev, xp_ref[...], 0.0)
        xe_s[SUBLANES:SUBLANES + CHUNK, :] = x_ref[...]
        xe_s[SUBLANES + CHUNK:2 * SUBLANES + CHUNK, :] = jnp.where(has_next, xn_ref[...], 0.0)
        xc = cb_ref[...] + cw_ref[0:1, :] * xe_s[pl.ds(SUBLANES - 2, CHUNK), :]
        xc = xc + cw_ref[1:2, :] * xe_s[pl.ds(SUBLANES - 1, CHUNK), :]
        xc = xc + cw_ref[2:3, :] * xe_s[pl.ds(SUBLANES, CHUNK), :]
        xc = xc + cw_ref[3:4, :] * xe_s[pl.ds(SUBLANES + 1, CHUNK), :]
        lam = lam_ref[direction]
        c8 = LRU_C * (jnp.minimum(lam, 0.0) - jnp.log1p(jnp.exp(-jnp.abs(lam))))
        for n in range(nb):
            sl = slice(n * bw, (n + 1) * bw)
            xn = xc[:, sl]
            pre = _dot(xn.astype(BF16), wg_ref[direction, n]) + bg_ref[direction, n]
            r = _sigmoid(pre[:, 0:bw])
            i = _sigmoid(pre[:, bw:2 * bw])
            log_a = c8[:, sl] * r
            a = jnp.exp(log_a)
            b = jnp.sqrt(1.0 - a * a) * (i * xn)
            a_s[direction, pl.ds(n, CHUNK, stride=nb), :] = a
            b_s[direction, pl.ds(n, CHUNK, stride=nb), :] = b

    coeffs(0, xf_ref, xfp_ref, xfn_ref, j >= 2, jnp.logical_and(j >= 1, j < n_steps - 1))
    coeffs(1, xb_ref, xbp_ref, xbn_ref, jnp.logical_and(j >= 1, j < n_steps - 1), j >= 2)

    def step(t, carry):
        hf, hb = carry
        tf = pl.multiple_of(t * nb, nb)
        tb = pl.multiple_of((CHUNK - 1 - t) * nb, nb)
        hf = a_s[0, pl.ds(tf, nb), :] * hf + b_s[0, pl.ds(tf, nb), :]
        hb = a_s[1, pl.ds(tb, nb), :] * hb + b_s[1, pl.ds(tb, nb), :]
        h_s[0, pl.ds(tf, nb), :] = hf
        h_s[1, pl.ds(tb, nb), :] = hb
        return hf, hb

    hf, hb = lax.fori_loop(0, CHUNK, step, (carry_s[0], carry_s[1]), unroll=8)
    carry_s[0] = hf
    carry_s[1] = hb

    for n in range(nb):
        sl = slice(n * bw, (n + 1) * bw)
        hf_ref[:, sl] = h_s[0, pl.ds(n, CHUNK, stride=nb), :]
        hb_ref[:, sl] = h_s[1, pl.ds(n, CHUNK, stride=nb), :]


def _rnn_branch(xr, conv_w, conv_b, w_gate, b_gate, lam, geo):
    nt, d = xr.shape
    bsz, n_lat = geo["batch"], geo["lat_chunks"]
    n_steps = n_lat + 1
    n_halo = nt // SUBLANES
    per = CHUNK // SUBLANES

    def fwd_blk(b, j):
        return jnp.where(j == 0, bsz * n_lat + b, b * n_lat + j - 1)

    def bwd_blk(b, j):
        return jnp.where(j == 0, bsz * n_lat + b, b * n_lat + n_lat - j)

    def chunk(blk):
        return pl.BlockSpec((CHUNK, d), lambda b, j: (blk(b, j), 0))

    def prev(blk):
        return pl.BlockSpec(
            (SUBLANES, d), lambda b, j: (jnp.maximum(blk(b, j) * per - 1, 0), 0))

    def nxt(blk):
        return pl.BlockSpec(
            (SUBLANES, d), lambda b, j: (jnp.minimum((blk(b, j) + 1) * per, n_halo - 1), 0))

    nb, bw = N_RNN_BLOCKS, RNN_BLOCK_W
    return pl.pallas_call(
        _rnn_kernel,
        grid=(bsz, n_steps),
        in_specs=[
            chunk(fwd_blk), prev(fwd_blk), nxt(fwd_blk),
            chunk(bwd_blk), prev(bwd_blk), nxt(bwd_blk),
            _resident(conv_w.shape),
            _resident((1, d)),
            _resident(w_gate.shape),
            _resident(b_gate.shape),
            _resident(lam.shape),
        ],
        out_specs=[chunk(fwd_blk), chunk(bwd_blk)],
        out_shape=[jax.ShapeDtypeStruct((nt, d), F32)] * 2,
        scratch_shapes=[
            pltpu.VMEM((CHUNK + 2 * SUBLANES, d), F32),
            pltpu.VMEM((2, CHUNK * nb, bw), F32),
            pltpu.VMEM((2, CHUNK * nb, bw), F32),
            pltpu.VMEM((2, CHUNK * nb, bw), F32),
            pltpu.VMEM((2, nb, bw), F32),
        ],
        compiler_params=pltpu.CompilerParams(
            dimension_semantics=("arbitrary", "arbitrary"),
            vmem_limit_bytes=VMEM_LIMIT_BYTES),
        name="rglru",
    )(xr, xr, xr, xr, xr, xr, conv_w, conv_b.reshape(1, d), w_gate, b_gate, lam)


def _stack_heads(q, g):
    base = g * Q_PER_KV * HEAD_DIM
    return jnp.concatenate(
        [q[:, base + h * HEAD_DIM: base + (h + 1) * HEAD_DIM] for h in range(Q_PER_KV)], axis=0)


def _nt_dot(a, b):
    return lax.dot_general(a, b, (((1,), (1,)), ((), ())), preferred_element_type=F32)


def _attend_group(sink_ref, g, q, n_q, key_parts, o_store):
    qs = _stack_heads(q, g)
    scores = [_nt_dot(k, qs) for k, _, _ in key_parts]
    probs = [[] for _ in key_parts]
    inv = []
    for h in range(Q_PER_KV):
        hs = slice(h * n_q, (h + 1) * n_q)
        sink = sink_ref[g * Q_PER_KV + h] * LOG2_E
        sh = [s[:, hs] if bias is None else s[:, hs] + bias
              for s, (_, _, bias) in zip(scores, key_parts)]
        m = sink
        for s in sh:
            m = jnp.maximum(m, jnp.max(s, axis=0, keepdims=True))
        denom = jnp.exp2(sink - m)
        for part, s in zip(probs, sh):
            p = jnp.exp2(s - m)
            denom = denom + jnp.sum(p, axis=0, keepdims=True)
            part.append(p.astype(BF16))
        inv.append(1.0 / denom)
    ot = None
    for (_, vt, _), part in zip(key_parts, probs):
        contrib = _dot(vt, jnp.concatenate(part, axis=1))
        ot = contrib if ot is None else ot + contrib
    ot = ot * jnp.concatenate(inv, axis=1)
    for h in range(Q_PER_KV):
        o_store(h, ot[:, h * n_q:(h + 1) * n_q].T)


def _lat_attn_kernel(sink_ref, q_ref, k_ref, vt_ref, kc_ref, vtc_ref, o_ref, *, seq_len):
    n_blk = q_ref.shape[0] // Q_BLOCK
    band_blocks = BAND // KEY_BLOCK
    last_start = seq_len // KEY_BLOCK - band_blocks
    rel = (lax.broadcasted_iota(jnp.int32, (BAND, Q_BLOCK), 0)
           - lax.broadcasted_iota(jnp.int32, (BAND, Q_BLOCK), 1))

    def body(i, carry):
        qb = pl.program_id(1) * n_blk + i
        jb0 = jnp.clip(qb - WINDOW // KEY_BLOCK, 0, last_start)
        start = pl.multiple_of(jb0 * KEY_BLOCK, KEY_BLOCK)
        bias = jnp.where(jnp.abs(rel + (jb0 - qb) * KEY_BLOCK) <= WINDOW, 0.0, NEG_INF)
        rows = pl.ds(pl.multiple_of(i * Q_BLOCK, Q_BLOCK), Q_BLOCK)
        q = q_ref[rows, :]
        for g in range(N_KV_HEADS):
            gs = slice(g * HEAD_DIM, (g + 1) * HEAD_DIM)
            vt_band = jnp.concatenate([vt_ref[jb0 + t, gs, :] for t in range(band_blocks)], axis=1)
            vt_ctx = jnp.concatenate([vtc_ref[t, gs, :] for t in range(vtc_ref.shape[0])], axis=1)

            def o_store(h, tile, g=g):
                c0 = (g * Q_PER_KV + h) * HEAD_DIM
                o_ref[rows, c0:c0 + HEAD_DIM] = tile.astype(BF16)

            _attend_group(sink_ref, g, q, Q_BLOCK,
                          [(k_ref[pl.ds(start, BAND), gs], vt_band, bias),
                           (kc_ref[:, gs], vt_ctx, None)], o_store)
        return carry

    lax.fori_loop(0, n_blk, body, 0)


def _ctx_attn_kernel(sink_ref, q_ref, kc_ref, vtc_ref, o_any, o_ref):
    del o_any
    n_ctx = q_ref.shape[0]
    q = q_ref[...]
    for g in range(N_KV_HEADS):
        gs = slice(g * HEAD_DIM, (g + 1) * HEAD_DIM)
        vt_ctx = jnp.concatenate([vtc_ref[t, gs, :] for t in range(vtc_ref.shape[0])], axis=1)

        def o_store(h, tile, g=g):
            c0 = (g * Q_PER_KV + h) * HEAD_DIM
            o_ref[:, c0:c0 + HEAD_DIM] = tile.astype(BF16)

        _attend_group(sink_ref, g, q, n_ctx, [(kc_ref[:, gs], vt_ctx, None)], o_store)


def _attention(q, k, vt, sink, geo, with_ctx_queries):
    nt, d = q.shape
    bsz, seq_len, n_ctx = geo["batch"], geo["seq"], geo["ctx"]
    q_sup = next(s for s in (1024, 512, 256, Q_BLOCK) if seq_len % s == 0)
    n_sup = seq_len // q_sup
    ctx_blk0 = bsz * seq_len // n_ctx
    smem = pl.BlockSpec(memory_space=pltpu.SMEM)
    ctx_keys = pl.BlockSpec((n_ctx, KV_WIDTH), lambda b, *_: (ctx_blk0 + b, 0))
    ctx_vals = pl.BlockSpec((n_ctx // KEY_BLOCK, KV_WIDTH, KEY_BLOCK),
                            lambda b, *_: (ctx_blk0 + b, 0, 0))
    o = pl.pallas_call(
        functools.partial(_lat_attn_kernel, seq_len=seq_len),
        grid=(bsz, n_sup),
        in_specs=[
            smem,
            pl.BlockSpec((q_sup, d), lambda b, i: (b * n_sup + i, 0)),
            pl.BlockSpec((seq_len, KV_WIDTH), lambda b, i: (b, 0)),
            pl.BlockSpec((seq_len // KEY_BLOCK, KV_WIDTH, KEY_BLOCK), lambda b, i: (b, 0, 0)),
            ctx_keys, ctx_vals,
        ],
        out_specs=pl.BlockSpec((q_sup, d), lambda b, i: (b * n_sup + i, 0)),
        out_shape=jax.ShapeDtypeStruct((nt if with_ctx_queries else bsz * seq_len, d), BF16),
        compiler_params=pltpu.CompilerParams(
            dimension_semantics=("arbitrary", "arbitrary"),
            vmem_limit_bytes=VMEM_LIMIT_BYTES),
        name="lat_attention",
    )(sink, q, k, vt, k, vt)
    if not with_ctx_queries:
        return o
    return pl.pallas_call(
        _ctx_attn_kernel,
        grid=(bsz,),
        in_specs=[
            smem,
            pl.BlockSpec((n_ctx, d), lambda b: (ctx_blk0 + b, 0)),
            ctx_keys, ctx_vals,
            pl.BlockSpec(memory_space=pl.ANY),
        ],
        out_specs=pl.BlockSpec((n_ctx, d), lambda b: (ctx_blk0 + b, 0)),
        out_shape=jax.ShapeDtypeStruct((nt, d), BF16),
        input_output_aliases={4: 0},
        compiler_params=pltpu.CompilerParams(
            dimension_semantics=("arbitrary",),
            vmem_limit_bytes=VMEM_LIMIT_BYTES),
        name="ctx_attention",
    )(sink, q, k, vt, o)


def _merge_kernel(x_ref, hf_ref, hb_ref, gr_ref, o_ref, gl_ref, mod_ref, g_ref,
                  wr_ref, wa_ref, wo_ref, out_ref):
    d = D_MODEL
    y = (hf_ref[...] + hb_ref[...]) * _gelu_tanh(gr_ref[...])
    ya = _dot(y.astype(BF16), wr_ref[...])
    yb = _dot(o_ref[...], wa_ref[...])
    ga = _sigmoid(gl_ref[:, 0:d])
    gb = _sigmoid(gl_ref[:, d:2 * d])
    m = _dot((ga * ya + gb * yb).astype(BF16), wo_ref[...])
    out_ref[...] = x_ref[...] + mod_ref[2:3, :] * (_rms(m) * g_ref[...])


def _merge(x_all, hf, hb, gr, o, gl, mod_l, g_post, w_o_rnn, w_o_attn, w_out, geo, n_tiles):
    d = x_all.shape[1]
    tm = geo["tm"]
    row = lambda i: (i, 0)
    tile = pl.BlockSpec((tm, d), row)
    return pl.pallas_call(
        _merge_kernel,
        grid=(n_tiles,),
        in_specs=[
            tile, tile, tile, tile, tile,
            pl.BlockSpec((tm, 2 * d), row),
            pl.BlockSpec((None, MOD_CHUNKS, d), lambda i: (geo["mod_row"](i), 0, 0)),
            _resident((1, d)),
            _resident(w_o_rnn.shape), _resident(w_o_attn.shape), _resident(w_out.shape),
        ],
        out_specs=tile,
        out_shape=jax.ShapeDtypeStruct((n_tiles * tm, d), F32),
        compiler_params=pltpu.CompilerParams(
            dimension_semantics=("arbitrary",),
            vmem_limit_bytes=VMEM_LIMIT_BYTES),
        name="merge",
    )(x_all, hf, hb, gr, o, gl, mod_l, g_post.reshape(1, d), w_o_rnn, w_o_attn, w_out)


def _ffn_kernel(x_ref, mod_ref, gpre_ref, gpost_ref, w1_ref, w2_ref, out_ref):
    x = x_ref[...]
    h = (_rms(x) * gpre_ref[...]) * (1.0 + mod_ref[4:5, :]) + mod_ref[3:4, :]
    h = h.astype(BF16)
    gate = _dot(h, w1_ref[:, 0:D_FF])
    up = _dot(h, w1_ref[:, D_FF:2 * D_FF])
    act = (gate * _sigmoid(gate) * up).astype(BF16)
    f = _dot(act, w2_ref[...])
    out_ref[...] = x + mod_ref[5:6, :] * (_rms(f) * gpost_ref[...])


def _ffn(x_all, mod_l, g_pre, g_post, w1, w2, geo, n_tiles):
    d = x_all.shape[1]
    tm = geo["tm"]
    tile = pl.BlockSpec((tm, d), lambda i: (i, 0))
    return pl.pallas_call(
        _ffn_kernel,
        grid=(n_tiles,),
        in_specs=[
            tile,
            pl.BlockSpec((None, MOD_CHUNKS, d), lambda i: (geo["mod_row"](i), 0, 0)),
            _resident((1, d)), _resident((1, d)),
            _resident(w1.shape), _resident(w2.shape),
        ],
        out_specs=tile,
        out_shape=jax.ShapeDtypeStruct((n_tiles * tm, d), F32),
        compiler_params=pltpu.CompilerParams(
            dimension_semantics=("arbitrary",),
            vmem_limit_bytes=VMEM_LIMIT_BYTES),
        name="ffn",
    )(x_all, mod_l, g_pre.reshape(1, d), g_post.reshape(1, d), w1, w2)


def _rope_tables(seq_len, pad_rows):
    pos = jnp.arange(seq_len, dtype=jnp.int32)
    inv = ROPE_BASE ** (-jnp.arange(N_FREQ, dtype=F32) / N_FREQ)
    ang_r = (pos // GRID_W).astype(F32)[:, None] * inv[None, :]
    ang_c = (pos % GRID_W).astype(F32)[:, None] * inv[None, :]
    cos = jnp.concatenate([jnp.cos(ang_r)] * 2 + [jnp.cos(ang_c)] * 2, axis=-1)
    sin = jnp.concatenate([-jnp.sin(ang_r), jnp.sin(ang_r), -jnp.sin(ang_c), jnp.sin(ang_c)], axis=-1)
    cos = jnp.concatenate([cos, jnp.ones((pad_rows, HEAD_DIM), F32)], axis=0)
    sin = jnp.concatenate([sin, jnp.zeros((pad_rows, HEAD_DIM), F32)], axis=0)
    scale = HEAD_DIM ** -0.5 * LOG2_E
    return cos * scale, sin * scale, cos, sin


def kernel(x, c, ctx, c_ctx, w_mod, b_mod, g_mix_pre, g_mix_post, g_ffn_pre, g_ffn_post, w_in, conv_w, conv_b, lru_wa, lru_ba, lru_wx, lru_bx, lru_lam, attn_sink, w_o_rnn, w_o_attn, w_out, w_ffn_in, w_ffn_out):
    bsz, seq_len, d = x.shape
    n_ctx = ctx.shape[1]
    depth = w_mod.shape[0]
    assert d == D_MODEL and n_ctx == CHUNK and seq_len % CHUNK == 0 and seq_len >= BAND
    assert bsz + 1 <= SUBLANES and seq_len % GRID_W == 0

    n_lat_rows, n_ctx_rows = bsz * seq_len, bsz * n_ctx
    tm = 512 if (n_ctx_rows % 512 == 0 and seq_len % 512 == 0) else CHUNK
    n_lat_tiles, n_ctx_tiles = n_lat_rows // tm, n_ctx_rows // tm
    tiles_per_batch = seq_len // tm
    geo = {
        "batch": bsz, "seq": seq_len, "ctx": n_ctx, "tm": tm,
        "lat_chunks": seq_len // CHUNK,
        "mod_row": lambda i: jnp.where(i < n_lat_tiles, i // tiles_per_batch, bsz),
        "table_block": lambda i: jnp.where(
            i < n_lat_tiles, i % tiles_per_batch, tiles_per_batch + i - n_lat_tiles),
    }

    cs = jnp.concatenate(
        [c, c_ctx[None, :], jnp.zeros((SUBLANES - bsz - 1, d), F32)], axis=0)
    mod = _modulation(cs, w_mod, b_mod).reshape(depth, SUBLANES, MOD_CHUNKS, d)

    tables = _rope_tables(seq_len, n_ctx_rows)

    w_gate = jnp.concatenate([lru_wa, lru_wx], axis=-1).astype(BF16)
    b_gate = jnp.concatenate(
        [lru_ba.reshape(depth, 2, N_RNN_BLOCKS, 1, RNN_BLOCK_W),
         lru_bx.reshape(depth, 2, N_RNN_BLOCKS, 1, RNN_BLOCK_W)], axis=-1)
    lam = lru_lam.reshape(depth, 2, 1, d)

    x_all = jnp.concatenate([x.reshape(n_lat_rows, d), ctx.reshape(n_ctx_rows, d)], axis=0)
    for l in range(depth):
        need_ctx = l < depth - 1
        n_out_tiles = n_lat_tiles + n_ctx_tiles if need_ctx else n_lat_tiles
        xr, gr, q, k, vt, gl = _in_proj(
            x_all, mod[l], g_mix_pre[l], w_in[l].astype(BF16), tables, geo)
        hf, hb = _rnn_branch(xr, conv_w[l], conv_b[l], w_gate[l], b_gate[l], lam[l], geo)
        o = _attention(q, k, vt, attn_sink[l], geo, need_ctx)
        x_all = _merge(x_all, hf, hb, gr, o, gl, mod[l], g_mix_post[l],
                       w_o_rnn[l].astype(BF16), w_o_attn[l].astype(BF16),
                       w_out[l].astype(BF16), geo, n_out_tiles)
        x_all = _ffn(x_all, mod[l], g_ffn_pre[l], g_ffn_post[l],
                     w_ffn_in[l].astype(BF16), w_ffn_out[l].astype(BF16), geo, n_out_tiles)
    return x_all[:n_lat_rows].reshape(bsz, seq_len, d)
```

```python
import functools
import math

import jax
import jax.numpy as jnp
from jax import lax
from jax.experimental import pallas as pl
from jax.experimental.pallas import tpu as pltpu

D_MODEL = 1024
HEAD_DIM = 128
N_Q_HEADS = 8
N_KV_HEADS = 2
Q_PER_KV = N_Q_HEADS // N_KV_HEADS
KV_WIDTH = N_KV_HEADS * HEAD_DIM
WINDOW = 128
GRID_W = 64
N_FREQ = HEAD_DIM // 4
ROPE_BASE = 10000.0
N_RNN_BLOCKS = 8
RNN_BLOCK_W = D_MODEL // N_RNN_BLOCKS
LRU_C = 8.0
CONV_W = 4
CONV_LEFT = 2
D_FF = 2816
EPS = 1e-6
NEG_INF = -1e30
MOD_CHUNKS = 6

LANES = 128
SUBLANES = 8
VMEM_LIMIT_BYTES = 56 * 1024 * 1024

CHUNK = 256
CONV_T = 16
Q_BLOCK = 128
KEY_BLOCK = 128
BAND = Q_BLOCK + 2 * WINDOW
LOG2_E = math.log2(math.e)

BF16 = jnp.bfloat16
F32 = jnp.float32
F32_TINY = float(jnp.finfo(jnp.float32).tiny)


def _dot(a, b):
    return jnp.dot(a, b, preferred_element_type=F32)


def _sigmoid(x):
    return 0.5 * jnp.tanh(0.5 * x) + 0.5


def _gelu_tanh(x):
    c = math.sqrt(2.0 / math.pi)
    return 0.5 * x * (1.0 + jnp.tanh(c * (x + 0.044715 * (x * x * x))))


def _rms(x):
    return x * lax.rsqrt(jnp.mean(x * x, axis=-1, keepdims=True) + EPS)


def _resident(shape):
    nd = len(shape)
    return pl.BlockSpec(shape, lambda *_: (0,) * nd, pipeline_mode=pl.Buffered(1))


def _mod_kernel(c_ref, w_ref, b_ref, o_ref):
    c = c_ref[...]
    s = (c * _sigmoid(c)).astype(BF16)
    o_ref[...] = _dot(s, w_ref[...].astype(BF16)) + b_ref[...]


def _modulation(cs, w_mod, b_mod):
    n_layers, d, width = w_mod.shape
    tn = 1536
    return pl.pallas_call(
        _mod_kernel,
        grid=(n_layers, width // tn),
        in_specs=[
            pl.BlockSpec((SUBLANES, d), lambda l, j: (0, 0)),
            pl.BlockSpec((None, d, tn), lambda l, j: (l, 0, j)),
            pl.BlockSpec((None, 1, tn), lambda l, j: (l, 0, j)),
        ],
        out_specs=pl.BlockSpec((None, SUBLANES, tn), lambda l, j: (l, 0, j)),
        out_shape=jax.ShapeDtypeStruct((n_layers, SUBLANES, width), F32),
        compiler_params=pltpu.CompilerParams(
            dimension_semantics=("arbitrary", "arbitrary"),
            vmem_limit_bytes=VMEM_LIMIT_BYTES),
        name="modulation",
    )(cs, w_mod, b_mod.reshape(n_layers, 1, width))


def _rope(t, cos_ref, sin_ref, n_heads):
    cos = cos_ref[...]
    sin = sin_ref[...]
    lane = lax.broadcasted_iota(jnp.int32, cos.shape, 1)
    first = (lane & (2 * N_FREQ - 1)) < N_FREQ
    outs = []
    for h in range(n_heads):
        th = t[:, h * HEAD_DIM:(h + 1) * HEAD_DIM]
        partner = jnp.where(first,
                            pltpu.roll(th, HEAD_DIM - N_FREQ, axis=1),
                            pltpu.roll(th, N_FREQ, axis=1))
        outs.append(th * cos + partner * sin)
    return jnp.concatenate(outs, axis=-1)


def _in_kernel(x_ref, mod_ref, g_ref, w_ref, cq_ref, sq_ref, ck_ref, sk_ref,
               xr_ref, gr_ref, q_ref, k_ref, vt_ref, gl_ref):
    d = D_MODEL
    x = x_ref[...]
    h = (_rms(x) * g_ref[...]) * (1.0 + mod_ref[1:2, :]) + mod_ref[0:1, :]
    h = h.astype(BF16)
    xr = _dot(h, w_ref[:, 0:d])
    for n in range(N_RNN_BLOCKS):
        xr_ref[pl.ds(n, x.shape[0], stride=N_RNN_BLOCKS), :] = (
            xr[:, n * RNN_BLOCK_W:(n + 1) * RNN_BLOCK_W])
    gr_ref[...] = _dot(h, w_ref[:, d:2 * d])
    q = _dot(h, w_ref[:, 2 * d:3 * d])
    q_ref[...] = _rope(q, cq_ref, sq_ref, N_Q_HEADS).astype(BF16)
    k = _dot(h, w_ref[:, 3 * d:3 * d + KV_WIDTH])
    k_ref[...] = _rope(k, ck_ref, sk_ref, N_KV_HEADS).astype(BF16)
    v = _dot(h, w_ref[:, 3 * d + KV_WIDTH:3 * d + 2 * KV_WIDTH])
    for t in range(vt_ref.shape[0]):
        vt_ref[t] = v[t * KEY_BLOCK:(t + 1) * KEY_BLOCK, :].T.astype(BF16)
    gl_ref[...] = _dot(h, w_ref[:, 3 * d + 2 * KV_WIDTH:5 * d + 2 * KV_WIDTH])


def _in_proj(x_all, mod_l, g_pre, w_in, tables, geo):
    nt, d = x_all.shape
    tm = geo["tm"]
    n_tiles = nt // tm
    row = lambda i: (i, 0)
    tab = pl.BlockSpec((tm, LANES), lambda i: (geo["table_block"](i), 0))
    return pl.pallas_call(
        _in_kernel,
        grid=(n_tiles,),
        in_specs=[
            pl.BlockSpec((tm, d), row),
            pl.BlockSpec((None, MOD_CHUNKS, d), lambda i: (geo["mod_row"](i), 0, 0)),
            _resident((1, d)),
            _resident(w_in.shape),
            tab, tab, tab, tab,
        ],
        out_specs=[
            pl.BlockSpec((tm * N_RNN_BLOCKS, RNN_BLOCK_W), row),
            pl.BlockSpec((tm, d), row),
            pl.BlockSpec((tm, d), row),
            pl.BlockSpec((tm, KV_WIDTH), row),
            pl.BlockSpec((tm // KEY_BLOCK, KV_WIDTH, KEY_BLOCK), lambda i: (i, 0, 0)),
            pl.BlockSpec((tm, 2 * d), row),
        ],
        out_shape=[
            jax.ShapeDtypeStruct((nt * N_RNN_BLOCKS, RNN_BLOCK_W), F32),
            jax.ShapeDtypeStruct((nt, d), F32),
            jax.ShapeDtypeStruct((nt, d), BF16),
            jax.ShapeDtypeStruct((nt, KV_WIDTH), BF16),
            jax.ShapeDtypeStruct((nt // KEY_BLOCK, KV_WIDTH, KEY_BLOCK), BF16),
            jax.ShapeDtypeStruct((nt, 2 * d), F32),
        ],
        compiler_params=pltpu.CompilerParams(
            dimension_semantics=("arbitrary",),
            vmem_limit_bytes=VMEM_LIMIT_BYTES),
        name="in_proj",
    )(x_all, mod_l, g_pre.reshape(1, d), w_in, *tables)


def _rnn_kernel(xf_ref, xfp_ref, xfn_ref, xb_ref, xbp_ref, xbn_ref,
                cw_ref, cb_ref, wg_ref, bg_ref, lam_ref,
                hf_ref, hb_ref,
                xc_s, a_s, b_s, carry_s):
    j = pl.program_id(1)
    n_steps = pl.num_programs(1)
    nb, bw = N_RNN_BLOCKS, RNN_BLOCK_W
    n_in = CONV_T + CONV_W - 1
    right = CONV_W - 1 - CONV_LEFT

    @pl.when(j == 0)
    def _():
        carry_s[...] = jnp.zeros_like(carry_s)

    def coeffs(direction, x_ref, xp_ref, xn_ref, has_prev, has_next):
        def conv_piece(v, base):
            v = v.reshape(n_in, nb, bw)
            acc = cb_ref[...] + v[0:CONV_T] * cw_ref[0]
            for k in range(1, CONV_W):
                acc = acc + v[k:k + CONV_T] * cw_ref[k]
            xc_s[pl.ds(base, CONV_T * nb), :] = acc.reshape(CONV_T * nb, bw)

        halo_rows = xp_ref.shape[0]
        left = jnp.where(has_prev, xp_ref[halo_rows - CONV_LEFT * nb:halo_rows, :], 0.0)
        conv_piece(jnp.concatenate([left, x_ref[0:(n_in - CONV_LEFT) * nb, :]], axis=0), 0)
        tail = jnp.where(has_next, xn_ref[0:right * nb, :], 0.0)
        last = CHUNK - CONV_T
        conv_piece(jnp.concatenate(
            [x_ref[(last - CONV_LEFT) * nb:CHUNK * nb, :], tail], axis=0), last * nb)

        def conv_body(c, carry):
            base = pl.multiple_of(c * (CONV_T * nb), CONV_T * nb)
            conv_piece(x_ref[pl.ds(base - CONV_LEFT * nb, n_in * nb), :], base)
            return carry

        lax.fori_loop(1, CHUNK // CONV_T - 1, conv_body, 0)

        for n in range(nb):
            rows = pl.ds(n, CHUNK, stride=nb)
            lam = lam_ref[direction, n]
            c_half = (0.5 * LRU_C * LOG2_E) * (
                jnp.minimum(lam, 0.0) - jnp.log1p(jnp.exp(-jnp.abs(lam))))
            xh = xc_s[rows, :]
            z = _dot(xh.astype(BF16), wg_ref[direction, n]) + bg_ref[direction, n]
            a = jnp.exp2(c_half * jnp.tanh(z[:, 0:bw]) + c_half)
            gated = xh * jnp.tanh(z[:, bw:2 * bw]) + xh
            om = 1.0 - a * a
            a_s[direction, rows, :] = a
            b_s[direction, rows, :] = (om * lax.rsqrt(jnp.maximum(om, F32_TINY))) * gated

    coeffs(0, xf_ref, xfp_ref, xfn_ref, j >= 2, jnp.logical_and(j >= 1, j < n_steps - 1))
    coeffs(1, xb_ref, xbp_ref, xbn_ref, jnp.logical_and(j >= 1, j < n_steps - 1), j >= 2)

    def step(t, carry):
        hf, hb = carry
        tf = pl.multiple_of(t * nb, nb)
        tb = pl.multiple_of((CHUNK - 1 - t) * nb, nb)
        hf = a_s[0, pl.ds(tf, nb), :] * hf + b_s[0, pl.ds(tf, nb), :]
        hb = a_s[1, pl.ds(tb, nb), :] * hb + b_s[1, pl.ds(tb, nb), :]
        hf_ref[pl.ds(tf, nb), :] = hf
        hb_ref[pl.ds(tb, nb), :] = hb
        return hf, hb

    hf, hb = lax.fori_loop(0, CHUNK, step, (carry_s[0], carry_s[1]), unroll=8)
    carry_s[0] = hf
    carry_s[1] = hb


def _rnn_branch(xr, conv_w, conv_b, w_gate, b_gate, lam, geo):
    nb, bw = N_RNN_BLOCKS, RNN_BLOCK_W
    bsz, n_lat = geo["batch"], geo["lat_chunks"]
    n_steps = n_lat + 1
    halo = SUBLANES
    n_halo = xr.shape[0] // (halo * nb)
    per = CHUNK // halo

    def fwd_blk(b, j):
        return jnp.where(j == 0, bsz * n_lat + b, b * n_lat + j - 1)

    def bwd_blk(b, j):
        return jnp.where(j == 0, bsz * n_lat + b, b * n_lat + n_lat - j)

    def chunk(blk):
        return pl.BlockSpec((CHUNK * nb, bw), lambda b, j: (blk(b, j), 0))

    def prev(blk):
        return pl.BlockSpec(
            (halo * nb, bw), lambda b, j: (jnp.maximum(blk(b, j) * per - 1, 0), 0))

    def nxt(blk):
        return pl.BlockSpec(
            (halo * nb, bw), lambda b, j: (jnp.minimum((blk(b, j) + 1) * per, n_halo - 1), 0))

    return pl.pallas_call(
        _rnn_kernel,
        grid=(bsz, n_steps),
        in_specs=[
            chunk(fwd_blk), prev(fwd_blk), nxt(fwd_blk),
            chunk(bwd_blk), prev(bwd_blk), nxt(bwd_blk),
            _resident((CONV_W, nb, bw)),
            _resident((nb, bw)),
            _resident(w_gate.shape),
            _resident(b_gate.shape),
            _resident(lam.shape),
        ],
        out_specs=[chunk(fwd_blk), chunk(bwd_blk)],
        out_shape=[jax.ShapeDtypeStruct(xr.shape, F32)] * 2,
        scratch_shapes=[
            pltpu.VMEM((CHUNK * nb, bw), F32),
            pltpu.VMEM((2, CHUNK * nb, bw), F32),
            pltpu.VMEM((2, CHUNK * nb, bw), F32),
            pltpu.VMEM((2, nb, bw), F32),
        ],
        compiler_params=pltpu.CompilerParams(
            dimension_semantics=("arbitrary", "arbitrary"),
            vmem_limit_bytes=VMEM_LIMIT_BYTES),
        name="rglru",
    )(xr, xr, xr, xr, xr, xr, conv_w.reshape(CONV_W, nb, bw), conv_b.reshape(nb, bw),
      w_gate, b_gate, lam)


def _stack_heads(q, g):
    base = g * Q_PER_KV * HEAD_DIM
    return jnp.concatenate(
        [q[:, base + h * HEAD_DIM: base + (h + 1) * HEAD_DIM] for h in range(Q_PER_KV)], axis=0)


def _nt_dot(a, b):
    return lax.dot_general(a, b, (((1,), (1,)), ((), ())), preferred_element_type=F32)


def _attend_group(sink_ref, g, q, n_q, key_parts, o_store):
    qs = _stack_heads(q, g)
    scores = [_nt_dot(k, qs) for k, _, _ in key_parts]
    probs = [[] for _ in key_parts]
    inv = []
    for h in range(Q_PER_KV):
        hs = slice(h * n_q, (h + 1) * n_q)
        sink = sink_ref[g * Q_PER_KV + h] * LOG2_E
        sh = [s[:, hs] if bias is None else s[:, hs] + bias
              for s, (_, _, bias) in zip(scores, key_parts)]
        m = sink
        for s in sh:
            m = jnp.maximum(m, jnp.max(s, axis=0, keepdims=True))
        denom = jnp.exp2(sink - m)
        for part, s in zip(probs, sh):
            p = jnp.exp2(s - m)
            denom = denom + jnp.sum(p, axis=0, keepdims=True)
            part.append(p.astype(BF16))
        inv.append(1.0 / denom)
    ot = None
    for (_, vt, _), part in zip(key_parts, probs):
        contrib = _dot(vt, jnp.concatenate(part, axis=1))
        ot = contrib if ot is None else ot + contrib
    ot = ot * jnp.concatenate(inv, axis=1)
    for h in range(Q_PER_KV):
        o_store(h, ot[:, h * n_q:(h + 1) * n_q].T)


def _lat_attn_kernel(sink_ref, q_ref, k_ref, vt_ref, kc_ref, vtc_ref, o_ref, *, seq_len):
    n_blk = q_ref.shape[0] // Q_BLOCK
    band_blocks = BAND // KEY_BLOCK
    last_start = seq_len // KEY_BLOCK - band_blocks
    rel = (lax.broadcasted_iota(jnp.int32, (BAND, Q_BLOCK), 0)
           - lax.broadcasted_iota(jnp.int32, (BAND, Q_BLOCK), 1))

    def body(i, carry):
        qb = pl.program_id(1) * n_blk + i
        jb0 = jnp.clip(qb - WINDOW // KEY_BLOCK, 0, last_start)
        start = pl.multiple_of(jb0 * KEY_BLOCK, KEY_BLOCK)
        bias = jnp.where(jnp.abs(rel + (jb0 - qb) * KEY_BLOCK) <= WINDOW, 0.0, NEG_INF)
        rows = pl.ds(pl.multiple_of(i * Q_BLOCK, Q_BLOCK), Q_BLOCK)
        q = q_ref[rows, :]
        for g in range(N_KV_HEADS):
            gs = slice(g * HEAD_DIM, (g + 1) * HEAD_DIM)
            vt_band = jnp.concatenate([vt_ref[jb0 + t, gs, :] for t in range(band_blocks)], axis=1)
            vt_ctx = jnp.concatenate([vtc_ref[t, gs, :] for t in range(vtc_ref.shape[0])], axis=1)

            def o_store(h, tile, g=g):
                c0 = (g * Q_PER_KV + h) * HEAD_DIM
                o_ref[rows, c0:c0 + HEAD_DIM] = tile.astype(BF16)

            _attend_group(sink_ref, g, q, Q_BLOCK,
                          [(k_ref[pl.ds(start, BAND), gs], vt_band, bias),
                           (kc_ref[:, gs], vt_ctx, None)], o_store)
        return carry

    lax.fori_loop(0, n_blk, body, 0)


def _ctx_attn_kernel(sink_ref, q_ref, kc_ref, vtc_ref, o_any, o_ref):
    del o_any
    n_ctx = q_ref.shape[0]
    q = q_ref[...]
    for g in range(N_KV_HEADS):
        gs = slice(g * HEAD_DIM, (g + 1) * HEAD_DIM)
        vt_ctx = jnp.concatenate([vtc_ref[t, gs, :] for t in range(vtc_ref.shape[0])], axis=1)

        def o_store(h, tile, g=g):
            c0 = (g * Q_PER_KV + h) * HEAD_DIM
            o_ref[:, c0:c0 + HEAD_DIM] = tile.astype(BF16)

        _attend_group(sink_ref, g, q, n_ctx, [(kc_ref[:, gs], vt_ctx, None)], o_store)


def _attention(q, k, vt, sink, geo, with_ctx_queries):
    nt, d = q.shape
    bsz, seq_len, n_ctx = geo["batch"], geo["seq"], geo["ctx"]
    q_sup = next(s for s in (1024, 512, 256, Q_BLOCK) if seq_len % s == 0)
    n_sup = seq_len // q_sup
    ctx_blk0 = bsz * seq_len // n_ctx
    smem = pl.BlockSpec(memory_space=pltpu.SMEM)
    ctx_keys = pl.BlockSpec((n_ctx, KV_WIDTH), lambda b, *_: (ctx_blk0 + b, 0))
    ctx_vals = pl.BlockSpec((n_ctx // KEY_BLOCK, KV_WIDTH, KEY_BLOCK),
                            lambda b, *_: (ctx_blk0 + b, 0, 0))
    o = pl.pallas_call(
        functools.partial(_lat_attn_kernel, seq_len=seq_len),
        grid=(bsz, n_sup),
        in_specs=[
            smem,
            pl.BlockSpec((q_sup, d), lambda b, i: (b * n_sup + i, 0)),
            pl.BlockSpec((seq_len, KV_WIDTH), lambda b, i: (b, 0)),
            pl.BlockSpec((seq_len // KEY_BLOCK, KV_WIDTH, KEY_BLOCK), lambda b, i: (b, 0, 0)),
            ctx_keys, ctx_vals,
        ],
        out_specs=pl.BlockSpec((q_sup, d), lambda b, i: (b * n_sup + i, 0)),
        out_shape=jax.ShapeDtypeStruct((nt if with_ctx_queries else bsz * seq_len, d), BF16),
        compiler_params=pltpu.CompilerParams(
            dimension_semantics=("arbitrary", "arbitrary"),
            vmem_limit_bytes=VMEM_LIMIT_BYTES),
        name="lat_attention",
    )(sink, q, k, vt, k, vt)
    if not with_ctx_queries:
        return o
    return pl.pallas_call(
        _ctx_attn_kernel,
        grid=(bsz,),
        in_specs=[
            smem,
            pl.BlockSpec((n_ctx, d), lambda b: (ctx_blk0 + b, 0)),
            ctx_keys, ctx_vals,
            pl.BlockSpec(memory_space=pl.ANY),
        ],
        out_specs=pl.BlockSpec((n_ctx, d), lambda b: (ctx_blk0 + b, 0)),
        out_shape=jax.ShapeDtypeStruct((nt, d), BF16),
        input_output_aliases={4: 0},
        compiler_params=pltpu.CompilerParams(
            dimension_semantics=("arbitrary",),
            vmem_limit_bytes=VMEM_LIMIT_BYTES),
        name="ctx_attention",
    )(sink, q, k, vt, o)


def _merge_kernel(x_ref, hf_ref, hb_ref, gr_ref, o_ref, gl_ref, mod_ref, g_ref,
                  wr_ref, wa_ref, wo_ref, out_ref):
    d = D_MODEL
    tm = x_ref.shape[0]
    h = jnp.concatenate(
        [hf_ref[pl.ds(n, tm, stride=N_RNN_BLOCKS), :] + hb_ref[pl.ds(n, tm, stride=N_RNN_BLOCKS), :]
         for n in range(N_RNN_BLOCKS)], axis=-1)
    y = h * _gelu_tanh(gr_ref[...])
    ya = _dot(y.astype(BF16), wr_ref[...])
    yb = _dot(o_ref[...], wa_ref[...])
    ga = _sigmoid(gl_ref[:, 0:d])
    gb = _sigmoid(gl_ref[:, d:2 * d])
    m = _dot((ga * ya + gb * yb).astype(BF16), wo_ref[...])
    out_ref[...] = x_ref[...] + mod_ref[2:3, :] * (_rms(m) * g_ref[...])


def _merge(x_all, hf, hb, gr, o, gl, mod_l, g_post, w_o_rnn, w_o_attn, w_out, geo, n_tiles):
    d = x_all.shape[1]
    tm = geo["tm"]
    row = lambda i: (i, 0)
    tile = pl.BlockSpec((tm, d), row)
    slab = pl.BlockSpec((tm * N_RNN_BLOCKS, RNN_BLOCK_W), row)
    return pl.pallas_call(
        _merge_kernel,
        grid=(n_tiles,),
        in_specs=[
            tile, slab, slab, tile, tile,
            pl.BlockSpec((tm, 2 * d), row),
            pl.BlockSpec((None, MOD_CHUNKS, d), lambda i: (geo["mod_row"](i), 0, 0)),
            _resident((1, d)),
            _resident(w_o_rnn.shape), _resident(w_o_attn.shape), _resident(w_out.shape),
        ],
        out_specs=tile,
        out_shape=jax.ShapeDtypeStruct((n_tiles * tm, d), F32),
        compiler_params=pltpu.CompilerParams(
            dimension_semantics=("arbitrary",),
            vmem_limit_bytes=VMEM_LIMIT_BYTES),
        name="merge",
    )(x_all, hf, hb, gr, o, gl, mod_l, g_post.reshape(1, d), w_o_rnn, w_o_attn, w_out)


def _ffn_kernel(x_ref, mod_ref, gpre_ref, gpost_ref, w1_ref, w2_ref, out_ref):
    x = x_ref[...]
    h = (_rms(x) * gpre_ref[...]) * (1.0 + mod_ref[4:5, :]) + mod_ref[3:4, :]
    h = h.astype(BF16)
    gate = _dot(h, w1_ref[:, 0:D_FF])
    up = _dot(h, w1_ref[:, D_FF:2 * D_FF])
    act = (gate * _sigmoid(gate) * up).astype(BF16)
    f = _dot(act, w2_ref[...])
    out_ref[...] = x + mod_ref[5:6, :] * (_rms(f) * gpost_ref[...])


def _ffn(x_all, mod_l, g_pre, g_post, w1, w2, geo, n_tiles):
    d = x_all.shape[1]
    tm = geo["tm"]
    tile = pl.BlockSpec((tm, d), lambda i: (i, 0))
    return pl.pallas_call(
        _ffn_kernel,
        grid=(n_tiles,),
        in_specs=[
            tile,
            pl.BlockSpec((None, MOD_CHUNKS, d), lambda i: (geo["mod_row"](i), 0, 0)),
            _resident((1, d)), _resident((1, d)),
            _resident(w1.shape), _resident(w2.shape),
        ],
        out_specs=tile,
        out_shape=jax.ShapeDtypeStruct((n_tiles * tm, d), F32),
        compiler_params=pltpu.CompilerParams(
            dimension_semantics=("arbitrary",),
            vmem_limit_bytes=VMEM_LIMIT_BYTES),
        name="ffn",
    )(x_all, mod_l, g_pre.reshape(1, d), g_post.reshape(1, d), w1, w2)


def _rope_tables(seq_len, pad_rows):
    pos = jnp.arange(seq_len, dtype=jnp.int32)
    inv = ROPE_BASE ** (-jnp.arange(N_FREQ, dtype=F32) / N_FREQ)
    ang_r = (pos // GRID_W).astype(F32)[:, None] * inv[None, :]
    ang_c = (pos % GRID_W).astype(F32)[:, None] * inv[None, :]
    cos = jnp.concatenate([jnp.cos(ang_r)] * 2 + [jnp.cos(ang_c)] * 2, axis=-1)
    sin = jnp.concatenate([-jnp.sin(ang_r), jnp.sin(ang_r), -jnp.sin(ang_c), jnp.sin(ang_c)], axis=-1)
    cos = jnp.concatenate([cos, jnp.ones((pad_rows, HEAD_DIM), F32)], axis=0)
    sin = jnp.concatenate([sin, jnp.zeros((pad_rows, HEAD_DIM), F32)], axis=0)
    scale = HEAD_DIM ** -0.5 * LOG2_E
    return cos * scale, sin * scale, cos, sin


def kernel(x, c, ctx, c_ctx, w_mod, b_mod, g_mix_pre, g_mix_post, g_ffn_pre, g_ffn_post, w_in, conv_w, conv_b, lru_wa, lru_ba, lru_wx, lru_bx, lru_lam, attn_sink, w_o_rnn, w_o_attn, w_out, w_ffn_in, w_ffn_out):
    bsz, seq_len, d = x.shape
    n_ctx = ctx.shape[1]
    depth = w_mod.shape[0]
    assert d == D_MODEL and n_ctx == CHUNK and seq_len % CHUNK == 0 and seq_len >= BAND
    assert bsz + 1 <= SUBLANES and seq_len % GRID_W == 0

    n_lat_rows, n_ctx_rows = bsz * seq_len, bsz * n_ctx
    tm = 512 if (n_ctx_rows % 512 == 0 and seq_len % 512 == 0) else CHUNK
    n_lat_tiles, n_ctx_tiles = n_lat_rows // tm, n_ctx_rows // tm
    tiles_per_batch = seq_len // tm
    geo = {
        "batch": bsz, "seq": seq_len, "ctx": n_ctx, "tm": tm,
        "lat_chunks": seq_len // CHUNK,
        "mod_row": lambda i: jnp.where(i < n_lat_tiles, i // tiles_per_batch, bsz),
        "table_block": lambda i: jnp.where(
            i < n_lat_tiles, i % tiles_per_batch, tiles_per_batch + i - n_lat_tiles),
    }

    cs = jnp.concatenate(
        [c, c_ctx[None, :], jnp.zeros((SUBLANES - bsz - 1, d), F32)], axis=0)
    mod = _modulation(cs, w_mod, b_mod).reshape(depth, SUBLANES, MOD_CHUNKS, d)

    tables = _rope_tables(seq_len, n_ctx_rows)

    w_gate = jnp.concatenate([lru_wa, lru_wx], axis=-1).astype(BF16)
    b_gate = 0.5 * jnp.concatenate(
        [lru_ba.reshape(depth, 2, N_RNN_BLOCKS, 1, RNN_BLOCK_W),
         lru_bx.reshape(depth, 2, N_RNN_BLOCKS, 1, RNN_BLOCK_W)], axis=-1)
    lam = lru_lam.reshape(depth, 2, N_RNN_BLOCKS, 1, RNN_BLOCK_W)
    conv_w_half, conv_b_half = 0.5 * conv_w, 0.5 * conv_b

    x_all = jnp.concatenate([x.reshape(n_lat_rows, d), ctx.reshape(n_ctx_rows, d)], axis=0)
    for l in range(depth):
        need_ctx = l < depth - 1
        n_out_tiles = n_lat_tiles + n_ctx_tiles if need_ctx else n_lat_tiles
        xr, gr, q, k, vt, gl = _in_proj(
            x_all, mod[l], g_mix_pre[l], w_in[l].astype(BF16), tables, geo)
        hf, hb = _rnn_branch(xr, conv_w_half[l], conv_b_half[l], w_gate[l], b_gate[l], lam[l], geo)
        o = _attention(q, k, vt, attn_sink[l], geo, need_ctx)
        x_all = _merge(x_all, hf, hb, gr, o, gl, mod[l], g_mix_post[l],
                       w_o_rnn[l].astype(BF16), w_o_attn[l].astype(BF16),
                       w_out[l].astype(BF16), geo, n_out_tiles)
        x_all = _ffn(x_all, mod[l], g_ffn_pre[l], g_ffn_post[l],
                     w_ffn_in[l].astype(BF16), w_ffn_out[l].astype(BF16), geo, n_out_tiles)
    return x_all[:n_lat_rows].reshape(bsz, seq_len, d)
```

```python
import functools
import math

import jax
import jax.numpy as jnp
from jax import lax
from jax.experimental import pallas as pl
from jax.experimental.pallas import tpu as pltpu

D_MODEL = 1024
HEAD_DIM = 128
N_Q_HEADS = 8
N_KV_HEADS = 2
Q_PER_KV = N_Q_HEADS // N_KV_HEADS
KV_WIDTH = N_KV_HEADS * HEAD_DIM
WINDOW = 128
GRID_W = 64
N_FREQ = HEAD_DIM // 4
ROPE_BASE = 10000.0
N_RNN_BLOCKS = 8
RNN_BLOCK_W = D_MODEL // N_RNN_BLOCKS
LRU_C = 8.0
CONV_W = 4
CONV_LEFT = 2
D_FF = 2816
EPS = 1e-6
NEG_INF = -1e30
MOD_CHUNKS = 6

LANES = 128
SUBLANES = 8
VMEM_LIMIT_BYTES = 56 * 1024 * 1024

CHUNK = 256
CONV_T = 16
Q_BLOCK = 128
KEY_BLOCK = 128
BAND = Q_BLOCK + 2 * WINDOW
LOG2_E = math.log2(math.e)

BF16 = jnp.bfloat16
F32 = jnp.float32
F32_TINY = float(jnp.finfo(jnp.float32).tiny)


def _dot(a, b):
    return jnp.dot(a, b, preferred_element_type=F32)


def _sigmoid(x):
    return 0.5 * jnp.tanh(0.5 * x) + 0.5


def _gelu_tanh(x):
    c = math.sqrt(2.0 / math.pi)
    return 0.5 * x * (1.0 + jnp.tanh(c * (x + 0.044715 * (x * x * x))))


def _rms(x):
    return x * lax.rsqrt(jnp.mean(x * x, axis=-1, keepdims=True) + EPS)


def _resident(shape):
    nd = len(shape)
    return pl.BlockSpec(shape, lambda *_: (0,) * nd, pipeline_mode=pl.Buffered(1))


def _mod_kernel(c_ref, w_ref, b_ref, o_ref):
    c = c_ref[...]
    s = (c * _sigmoid(c)).astype(BF16)
    o_ref[...] = _dot(s, w_ref[...].astype(BF16)) + b_ref[...]


def _modulation(cs, w_mod, b_mod):
    n_layers, d, width = w_mod.shape
    tn = 1536
    return pl.pallas_call(
        _mod_kernel,
        grid=(n_layers, width // tn),
        in_specs=[
            pl.BlockSpec((SUBLANES, d), lambda l, j: (0, 0)),
            pl.BlockSpec((None, d, tn), lambda l, j: (l, 0, j)),
            pl.BlockSpec((None, 1, tn), lambda l, j: (l, 0, j)),
        ],
        out_specs=pl.BlockSpec((None, SUBLANES, tn), lambda l, j: (l, 0, j)),
        out_shape=jax.ShapeDtypeStruct((n_layers, SUBLANES, width), F32),
        compiler_params=pltpu.CompilerParams(
            dimension_semantics=("arbitrary", "arbitrary"),
            vmem_limit_bytes=VMEM_LIMIT_BYTES),
        name="modulation",
    )(cs, w_mod, b_mod.reshape(n_layers, 1, width))


def _rope(t, cos_ref, sin_ref, n_heads):
    cos = cos_ref[...]
    sin = sin_ref[...]
    lane = lax.broadcasted_iota(jnp.int32, cos.shape, 1)
    first = (lane & (2 * N_FREQ - 1)) < N_FREQ
    outs = []
    for h in range(n_heads):
        th = t[:, h * HEAD_DIM:(h + 1) * HEAD_DIM]
        partner = jnp.where(first,
                            pltpu.roll(th, HEAD_DIM - N_FREQ, axis=1),
                            pltpu.roll(th, N_FREQ, axis=1))
        outs.append(th * cos + partner * sin)
    return jnp.concatenate(outs, axis=-1)


def _stream_specs(stream, tm, d, n_lat_tiles):
    _, ctx, ctx_off = stream
    last_ctx = ctx.shape[0] // tm - 1
    lat_spec = pl.BlockSpec((tm, d), lambda i: (jnp.minimum(i, n_lat_tiles - 1), 0))
    ctx_spec = pl.BlockSpec(
        (tm, d), lambda i: (jnp.minimum(jnp.maximum(i - n_lat_tiles, 0) + ctx_off, last_ctx), 0))
    return lat_spec, ctx_spec


def _stream_tile(lat_ref, ctx_ref, n_lat_tiles):
    return jnp.where(pl.program_id(0) < n_lat_tiles, lat_ref[...], ctx_ref[...])


def _in_kernel(xl_ref, xc_ref, mod_ref, g_ref, w_ref, cq_ref, sq_ref, ck_ref, sk_ref,
               xr_ref, gr_ref, q_ref, k_ref, vt_ref, gl_ref, *, n_lat_tiles):
    d = D_MODEL
    x = _stream_tile(xl_ref, xc_ref, n_lat_tiles)
    h = (_rms(x) * g_ref[...]) * (1.0 + mod_ref[1:2, :]) + mod_ref[0:1, :]
    h = h.astype(BF16)
    xr = _dot(h, w_ref[:, 0:d])
    for n in range(N_RNN_BLOCKS):
        xr_ref[pl.ds(n, x.shape[0], stride=N_RNN_BLOCKS), :] = (
            xr[:, n * RNN_BLOCK_W:(n + 1) * RNN_BLOCK_W])
    gr_ref[...] = _dot(h, w_ref[:, d:2 * d]).astype(BF16)
    q = _dot(h, w_ref[:, 2 * d:3 * d])
    q_ref[...] = _rope(q, cq_ref, sq_ref, N_Q_HEADS).astype(BF16)
    k = _dot(h, w_ref[:, 3 * d:3 * d + KV_WIDTH])
    k_ref[...] = _rope(k, ck_ref, sk_ref, N_KV_HEADS).astype(BF16)
    v = _dot(h, w_ref[:, 3 * d + KV_WIDTH:3 * d + 2 * KV_WIDTH])
    for t in range(vt_ref.shape[0]):
        vt_ref[t] = v[t * KEY_BLOCK:(t + 1) * KEY_BLOCK, :].T.astype(BF16)
    gl_ref[...] = _dot(h, w_ref[:, 3 * d + 2 * KV_WIDTH:5 * d + 2 * KV_WIDTH]).astype(BF16)


def _in_proj(stream, mod_l, g_pre, w_in, tables, geo):
    d = D_MODEL
    tm = geo["tm"]
    n_tiles = geo["n_lat_tiles"] + geo["n_ctx_tiles"]
    nt = n_tiles * tm
    row = lambda i: (i, 0)
    tab = pl.BlockSpec((tm, LANES), lambda i: (geo["table_block"](i), 0))
    return pl.pallas_call(
        functools.partial(_in_kernel, n_lat_tiles=geo["n_lat_tiles"]),
        grid=(n_tiles,),
        in_specs=[
            *_stream_specs(stream, tm, d, geo["n_lat_tiles"]),
            pl.BlockSpec((None, MOD_CHUNKS, d), lambda i: (geo["mod_row"](i), 0, 0)),
            _resident((1, d)),
            _resident(w_in.shape),
            tab, tab, tab, tab,
        ],
        out_specs=[
            pl.BlockSpec((tm * N_RNN_BLOCKS, RNN_BLOCK_W), row),
            pl.BlockSpec((tm, d), row),
            pl.BlockSpec((tm, d), row),
            pl.BlockSpec((tm, KV_WIDTH), row),
            pl.BlockSpec((tm // KEY_BLOCK, KV_WIDTH, KEY_BLOCK), lambda i: (i, 0, 0)),
            pl.BlockSpec((tm, 2 * d), row),
        ],
        out_shape=[
            jax.ShapeDtypeStruct((nt * N_RNN_BLOCKS, RNN_BLOCK_W), F32),
            jax.ShapeDtypeStruct((nt, d), BF16),
            jax.ShapeDtypeStruct((nt, d), BF16),
            jax.ShapeDtypeStruct((nt, KV_WIDTH), BF16),
            jax.ShapeDtypeStruct((nt // KEY_BLOCK, KV_WIDTH, KEY_BLOCK), BF16),
            jax.ShapeDtypeStruct((nt, 2 * d), BF16),
        ],
        compiler_params=pltpu.CompilerParams(
            dimension_semantics=("arbitrary",),
            vmem_limit_bytes=VMEM_LIMIT_BYTES),
        name="in_proj",
    )(stream[0], stream[1], mod_l, g_pre.reshape(1, d), w_in, *tables)


def _rnn_kernel(xf_ref, xfp_ref, xfn_ref, xb_ref, xbp_ref, xbn_ref,
                cw_ref, cb_ref, wg_ref, bg_ref, lam_ref,
                hf_ref, hb_ref,
                xc_s, a_s, b_s, carry_s):
    j = pl.program_id(1)
    n_steps = pl.num_programs(1)
    nb, bw = N_RNN_BLOCKS, RNN_BLOCK_W
    n_in = CONV_T + CONV_W - 1
    right = CONV_W - 1 - CONV_LEFT

    @pl.when(j == 0)
    def _():
        carry_s[...] = jnp.zeros_like(carry_s)

    def coeffs(direction, x_ref, xp_ref, xn_ref, has_prev, has_next):
        def conv_piece(v, base):
            v = v.reshape(n_in, nb, bw)
            acc = cb_ref[...] + v[0:CONV_T] * cw_ref[0]
            for k in range(1, CONV_W):
                acc = acc + v[k:k + CONV_T] * cw_ref[k]
            xc_s[pl.ds(base, CONV_T * nb), :] = acc.reshape(CONV_T * nb, bw)

        halo_rows = xp_ref.shape[0]
        left = jnp.where(has_prev, xp_ref[halo_rows - CONV_LEFT * nb:halo_rows, :], 0.0)
        conv_piece(jnp.concatenate([left, x_ref[0:(n_in - CONV_LEFT) * nb, :]], axis=0), 0)
        tail = jnp.where(has_next, xn_ref[0:right * nb, :], 0.0)
        last = CHUNK - CONV_T
        conv_piece(jnp.concatenate(
            [x_ref[(last - CONV_LEFT) * nb:CHUNK * nb, :], tail], axis=0), last * nb)

        def conv_body(c, carry):
            base = pl.multiple_of(c * (CONV_T * nb), CONV_T * nb)
            conv_piece(x_ref[pl.ds(base - CONV_LEFT * nb, n_in * nb), :], base)
            return carry

        lax.fori_loop(1, CHUNK // CONV_T - 1, conv_body, 0)

        for n in range(nb):
            rows = pl.ds(n, CHUNK, stride=nb)
            lam = lam_ref[direction, n]
            c_half = (0.5 * LRU_C * LOG2_E) * (
                jnp.minimum(lam, 0.0) - jnp.log1p(jnp.exp(-jnp.abs(lam))))
            xh = xc_s[rows, :]
            z = _dot(xh.astype(BF16), wg_ref[direction, n]) + bg_ref[direction, n]
            a = jnp.exp2(c_half * jnp.tanh(z[:, 0:bw]) + c_half)
            gated = xh * jnp.tanh(z[:, bw:2 * bw]) + xh
            om = 1.0 - a * a
            a_s[direction, rows, :] = a
            b_s[direction, rows, :] = (om * lax.rsqrt(jnp.maximum(om, F32_TINY))) * gated

    coeffs(0, xf_ref, xfp_ref, xfn_ref, j >= 2, jnp.logical_and(j >= 1, j < n_steps - 1))
    coeffs(1, xb_ref, xbp_ref, xbn_ref, jnp.logical_and(j >= 1, j < n_steps - 1), j >= 2)

    def step(t, carry):
        hf, hb = carry
        tf = pl.multiple_of(t * nb, nb)
        tb = pl.multiple_of((CHUNK - 1 - t) * nb, nb)
        hf = a_s[0, pl.ds(tf, nb), :] * hf + b_s[0, pl.ds(tf, nb), :]
        hb = a_s[1, pl.ds(tb, nb), :] * hb + b_s[1, pl.ds(tb, nb), :]
        hf_ref[pl.ds(tf, nb), :] = hf
        hb_ref[pl.ds(tb, nb), :] = hb
        return hf, hb

    hf, hb = lax.fori_loop(0, CHUNK, step, (carry_s[0], carry_s[1]), unroll=8)
    carry_s[0] = hf
    carry_s[1] = hb


def _rnn_branch(xr, conv_w, conv_b, w_gate, b_gate, lam, geo):
    nb, bw = N_RNN_BLOCKS, RNN_BLOCK_W
    bsz, n_lat = geo["batch"], geo["lat_chunks"]
    n_steps = n_lat + 1
    halo = SUBLANES
    n_halo = xr.shape[0] // (halo * nb)
    per = CHUNK // halo

    def fwd_blk(b, j):
        return jnp.where(j == 0, bsz * n_lat + b, b * n_lat + j - 1)

    def bwd_blk(b, j):
        return jnp.where(j == 0, bsz * n_lat + b, b * n_lat + n_lat - j)

    def chunk(blk):
        return pl.BlockSpec((CHUNK * nb, bw), lambda b, j: (blk(b, j), 0))

    def prev(blk):
        return pl.BlockSpec(
            (halo * nb, bw), lambda b, j: (jnp.maximum(blk(b, j) * per - 1, 0), 0))

    def nxt(blk):
        return pl.BlockSpec(
            (halo * nb, bw), lambda b, j: (jnp.minimum((blk(b, j) + 1) * per, n_halo - 1), 0))

    return pl.pallas_call(
        _rnn_kernel,
        grid=(bsz, n_steps),
        in_specs=[
            chunk(fwd_blk), prev(fwd_blk), nxt(fwd_blk),
            chunk(bwd_blk), prev(bwd_blk), nxt(bwd_blk),
            _resident((CONV_W, nb, bw)),
            _resident((nb, bw)),
            _resident(w_gate.shape),
            _resident(b_gate.shape),
            _resident(lam.shape),
        ],
        out_specs=[chunk(fwd_blk), chunk(bwd_blk)],
        out_shape=[jax.ShapeDtypeStruct(xr.shape, F32)] * 2,
        scratch_shapes=[
            pltpu.VMEM((CHUNK * nb, bw), F32),
            pltpu.VMEM((2, CHUNK * nb, bw), F32),
            pltpu.VMEM((2, CHUNK * nb, bw), F32),
            pltpu.VMEM((2, nb, bw), F32),
        ],
        compiler_params=pltpu.CompilerParams(
            dimension_semantics=("arbitrary", "arbitrary"),
            vmem_limit_bytes=VMEM_LIMIT_BYTES),
        name="rglru",
    )(xr, xr, xr, xr, xr, xr, conv_w.reshape(CONV_W, nb, bw), conv_b.reshape(nb, bw),
      w_gate, b_gate, lam)


def _stack_heads(q, g):
    base = g * Q_PER_KV * HEAD_DIM
    return jnp.concatenate(
        [q[:, base + h * HEAD_DIM: base + (h + 1) * HEAD_DIM] for h in range(Q_PER_KV)], axis=0)


def _nt_dot(a, b):
    return lax.dot_general(a, b, (((1,), (1,)), ((), ())), preferred_element_type=F32)


def _attend_group(sink_ref, g, q, n_q, key_parts, o_store):
    qs = _stack_heads(q, g)
    scores = [_nt_dot(k, qs) for k, _, _ in key_parts]
    probs = [[] for _ in key_parts]
    inv = []
    for h in range(Q_PER_KV):
        hs = slice(h * n_q, (h + 1) * n_q)
        sink = sink_ref[g * Q_PER_KV + h] * LOG2_E
        sh = [s[:, hs] if bias is None else s[:, hs] + bias
              for s, (_, _, bias) in zip(scores, key_parts)]
        m = sink
        for s in sh:
            m = jnp.maximum(m, jnp.max(s, axis=0, keepdims=True))
        denom = jnp.exp2(sink - m)
        for part, s in zip(probs, sh):
            p = jnp.exp2(s - m)
            denom = denom + jnp.sum(p, axis=0, keepdims=True)
            part.append(p.astype(BF16))
        inv.append(1.0 / denom)
    ot = None
    for (_, vt, _), part in zip(key_parts, probs):
        contrib = _dot(vt, jnp.concatenate(part, axis=1))
        ot = contrib if ot is None else ot + contrib
    ot = ot * jnp.concatenate(inv, axis=1)
    for h in range(Q_PER_KV):
        o_store(h, ot[:, h * n_q:(h + 1) * n_q].T)


def _lat_attn_kernel(sink_ref, q_ref, k_ref, vt_ref, kc_ref, vtc_ref, o_ref, *, seq_len):
    n_blk = q_ref.shape[0] // Q_BLOCK
    band_blocks = BAND // KEY_BLOCK
    last_start = seq_len // KEY_BLOCK - band_blocks
    rel = (lax.broadcasted_iota(jnp.int32, (BAND, Q_BLOCK), 0)
           - lax.broadcasted_iota(jnp.int32, (BAND, Q_BLOCK), 1))

    def body(i, carry):
        qb = pl.program_id(1) * n_blk + i
        jb0 = jnp.clip(qb - WINDOW // KEY_BLOCK, 0, last_start)
        start = pl.multiple_of(jb0 * KEY_BLOCK, KEY_BLOCK)
        bias = jnp.where(jnp.abs(rel + (jb0 - qb) * KEY_BLOCK) <= WINDOW, 0.0, NEG_INF)
        rows = pl.ds(pl.multiple_of(i * Q_BLOCK, Q_BLOCK), Q_BLOCK)
        q = q_ref[rows, :]
        for g in range(N_KV_HEADS):
            gs = slice(g * HEAD_DIM, (g + 1) * HEAD_DIM)
            vt_band = jnp.concatenate([vt_ref[jb0 + t, gs, :] for t in range(band_blocks)], axis=1)
            vt_ctx = jnp.concatenate([vtc_ref[t, gs, :] for t in range(vtc_ref.shape[0])], axis=1)

            def o_store(h, tile, g=g):
                c0 = (g * Q_PER_KV + h) * HEAD_DIM
                o_ref[rows, c0:c0 + HEAD_DIM] = tile.astype(BF16)

            _attend_group(sink_ref, g, q, Q_BLOCK,
                          [(k_ref[pl.ds(start, BAND), gs], vt_band, bias),
                           (kc_ref[:, gs], vt_ctx, None)], o_store)
        return carry

    lax.fori_loop(0, n_blk, body, 0)


def _ctx_attn_kernel(sink_ref, q_ref, kc_ref, vtc_ref, o_ref):
    n_ctx = q_ref.shape[0]
    q = q_ref[...]
    for g in range(N_KV_HEADS):
        gs = slice(g * HEAD_DIM, (g + 1) * HEAD_DIM)
        vt_ctx = jnp.concatenate([vtc_ref[t, gs, :] for t in range(vtc_ref.shape[0])], axis=1)

        def o_store(h, tile, g=g):
            c0 = (g * Q_PER_KV + h) * HEAD_DIM
            o_ref[:, c0:c0 + HEAD_DIM] = tile.astype(BF16)

        _attend_group(sink_ref, g, q, n_ctx, [(kc_ref[:, gs], vt_ctx, None)], o_store)


def _attention(q, k, vt, sink, geo, with_ctx_queries):
    nt, d = q.shape
    bsz, seq_len, n_ctx = geo["batch"], geo["seq"], geo["ctx"]
    q_sup = next(s for s in (1024, 512, 256, Q_BLOCK) if seq_len % s == 0)
    n_sup = seq_len // q_sup
    ctx_blk0 = bsz * seq_len // n_ctx
    smem = pl.BlockSpec(memory_space=pltpu.SMEM)
    ctx_keys = pl.BlockSpec((n_ctx, KV_WIDTH), lambda b, *_: (ctx_blk0 + b, 0))
    ctx_vals = pl.BlockSpec((n_ctx // KEY_BLOCK, KV_WIDTH, KEY_BLOCK),
                            lambda b, *_: (ctx_blk0 + b, 0, 0))
    o = pl.pallas_call(
        functools.partial(_lat_attn_kernel, seq_len=seq_len),
        grid=(bsz, n_sup),
        in_specs=[
            smem,
            pl.BlockSpec((q_sup, d), lambda b, i: (b * n_sup + i, 0)),
            pl.BlockSpec((seq_len, KV_WIDTH), lambda b, i: (b, 0)),
            pl.BlockSpec((seq_len // KEY_BLOCK, KV_WIDTH, KEY_BLOCK), lambda b, i: (b, 0, 0)),
            ctx_keys, ctx_vals,
        ],
        out_specs=pl.BlockSpec((q_sup, d), lambda b, i: (b * n_sup + i, 0)),
        out_shape=jax.ShapeDtypeStruct((bsz * seq_len, d), BF16),
        compiler_params=pltpu.CompilerParams(
            dimension_semantics=("arbitrary", "arbitrary"),
            vmem_limit_bytes=VMEM_LIMIT_BYTES),
        name="lat_attention",
    )(sink, q, k, vt, k, vt)
    if not with_ctx_queries:
        return (o, o, 0)
    o_ctx = pl.pallas_call(
        _ctx_attn_kernel,
        grid=(bsz,),
        in_specs=[
            smem,
            pl.BlockSpec((n_ctx, d), lambda b: (ctx_blk0 + b, 0)),
            ctx_keys, ctx_vals,
        ],
        out_specs=pl.BlockSpec((n_ctx, d), lambda b: (b, 0)),
        out_shape=jax.ShapeDtypeStruct((bsz * n_ctx, d), BF16),
        compiler_params=pltpu.CompilerParams(
            dimension_semantics=("arbitrary",),
            vmem_limit_bytes=VMEM_LIMIT_BYTES),
        name="ctx_attention",
    )(sink, q, k, vt)
    return (o, o_ctx, 0)


def _merge_kernel(xl_ref, xc_ref, hf_ref, hb_ref, gr_ref, ol_ref, oc_ref, gl_ref, mod_ref,
                  g_ref, wr_ref, wa_ref, wo_ref, out_ref, *, n_lat_tiles):
    d = D_MODEL
    tm = xl_ref.shape[0]
    x = _stream_tile(xl_ref, xc_ref, n_lat_tiles)
    o = _stream_tile(ol_ref, oc_ref, n_lat_tiles)
    h = jnp.concatenate(
        [hf_ref[pl.ds(n, tm, stride=N_RNN_BLOCKS), :] + hb_ref[pl.ds(n, tm, stride=N_RNN_BLOCKS), :]
         for n in range(N_RNN_BLOCKS)], axis=-1)
    y = h * _gelu_tanh(gr_ref[...].astype(F32))
    ya = _dot(y.astype(BF16), wr_ref[...])
    yb = _dot(o, wa_ref[...])
    ga = _sigmoid(gl_ref[:, 0:d].astype(F32))
    gb = _sigmoid(gl_ref[:, d:2 * d].astype(F32))
    m = _dot((ga * ya + gb * yb).astype(BF16), wo_ref[...])
    out_ref[...] = x + mod_ref[2:3, :] * (_rms(m) * g_ref[...])


def _merge(x_stream, hf, hb, gr, o_stream, gl, mod_l, g_post, w_o_rnn, w_o_attn, w_out,
           geo, n_tiles):
    d = D_MODEL
    tm = geo["tm"]
    n_lat_tiles = geo["n_lat_tiles"]
    row = lambda i: (i, 0)
    tile = pl.BlockSpec((tm, d), row)
    slab = pl.BlockSpec((tm * N_RNN_BLOCKS, RNN_BLOCK_W), row)
    return pl.pallas_call(
        functools.partial(_merge_kernel, n_lat_tiles=n_lat_tiles),
        grid=(n_tiles,),
        in_specs=[
            *_stream_specs(x_stream, tm, d, n_lat_tiles), slab, slab, tile,
            *_stream_specs(o_stream, tm, d, n_lat_tiles),
            pl.BlockSpec((tm, 2 * d), row),
            pl.BlockSpec((None, MOD_CHUNKS, d), lambda i: (geo["mod_row"](i), 0, 0)),
            _resident((1, d)),
            _resident(w_o_rnn.shape), _resident(w_o_attn.shape), _resident(w_out.shape),
        ],
        out_specs=tile,
        out_shape=jax.ShapeDtypeStruct((n_tiles * tm, d), F32),
        compiler_params=pltpu.CompilerParams(
            dimension_semantics=("arbitrary",),
            vmem_limit_bytes=VMEM_LIMIT_BYTES),
        name="merge",
    )(x_stream[0], x_stream[1], hf, hb, gr, o_stream[0], o_stream[1], gl, mod_l,
      g_post.reshape(1, d), w_o_rnn, w_o_attn, w_out)


def _ffn_kernel(x_ref, mod_ref, gpre_ref, gpost_ref, w1_ref, w2_ref, out_ref):
    x = x_ref[...]
    h = (_rms(x) * gpre_ref[...]) * (1.0 + mod_ref[4:5, :]) + mod_ref[3:4, :]
    h = h.astype(BF16)
    gate = _dot(h, w1_ref[:, 0:D_FF])
    up = _dot(h, w1_ref[:, D_FF:2 * D_FF])
    act = (gate * _sigmoid(gate) * up).astype(BF16)
    f = _dot(act, w2_ref[...])
    out_ref[...] = x + mod_ref[5:6, :] * (_rms(f) * gpost_ref[...])


def _ffn(x_all, mod_l, g_pre, g_post, w1, w2, geo, n_tiles):
    d = x_all.shape[1]
    tm = geo["tm"]
    tile = pl.BlockSpec((tm, d), lambda i: (i, 0))
    return pl.pallas_call(
        _ffn_kernel,
        grid=(n_tiles,),
        in_specs=[
            tile,
            pl.BlockSpec((None, MOD_CHUNKS, d), lambda i: (geo["mod_row"](i), 0, 0)),
            _resident((1, d)), _resident((1, d)),
            _resident(w1.shape), _resident(w2.shape),
        ],
        out_specs=tile,
        out_shape=jax.ShapeDtypeStruct((n_tiles * tm, d), F32),
        compiler_params=pltpu.CompilerParams(
            dimension_semantics=("arbitrary",),
            vmem_limit_bytes=VMEM_LIMIT_BYTES),
        name="ffn",
    )(x_all, mod_l, g_pre.reshape(1, d), g_post.reshape(1, d), w1, w2)


def _rope_tables(seq_len, pad_rows):
    pos = jnp.arange(seq_len, dtype=jnp.int32)
    inv = ROPE_BASE ** (-jnp.arange(N_FREQ, dtype=F32) / N_FREQ)
    ang_r = (pos // GRID_W).astype(F32)[:, None] * inv[None, :]
    ang_c = (pos % GRID_W).astype(F32)[:, None] * inv[None, :]
    cos = jnp.concatenate([jnp.cos(ang_r)] * 2 + [jnp.cos(ang_c)] * 2, axis=-1)
    sin = jnp.concatenate([-jnp.sin(ang_r), jnp.sin(ang_r), -jnp.sin(ang_c), jnp.sin(ang_c)], axis=-1)
    cos = jnp.concatenate([cos, jnp.ones((pad_rows, HEAD_DIM), F32)], axis=0)
    sin = jnp.concatenate([sin, jnp.zeros((pad_rows, HEAD_DIM), F32)], axis=0)
    scale = HEAD_DIM ** -0.5 * LOG2_E
    return cos * scale, sin * scale, cos, sin


def kernel(x, c, ctx, c_ctx, w_mod, b_mod, g_mix_pre, g_mix_post, g_ffn_pre, g_ffn_post, w_in, conv_w, conv_b, lru_wa, lru_ba, lru_wx, lru_bx, lru_lam, attn_sink, w_o_rnn, w_o_attn, w_out, w_ffn_in, w_ffn_out):
    bsz, seq_len, d = x.shape
    n_ctx = ctx.shape[1]
    depth = w_mod.shape[0]
    assert d == D_MODEL and n_ctx == CHUNK and seq_len % CHUNK == 0 and seq_len >= BAND
    assert bsz + 1 <= SUBLANES and seq_len % GRID_W == 0

    n_lat_rows, n_ctx_rows = bsz * seq_len, bsz * n_ctx
    tm = 512 if (n_ctx_rows % 512 == 0 and seq_len % 512 == 0) else CHUNK
    n_lat_tiles, n_ctx_tiles = n_lat_rows // tm, n_ctx_rows // tm
    tiles_per_batch = seq_len // tm
    geo = {
        "batch": bsz, "seq": seq_len, "ctx": n_ctx, "tm": tm,
        "lat_chunks": seq_len // CHUNK,
        "n_lat_tiles": n_lat_tiles, "n_ctx_tiles": n_ctx_tiles,
        "mod_row": lambda i: jnp.where(i < n_lat_tiles, i // tiles_per_batch, bsz),
        "table_block": lambda i: jnp.where(
            i < n_lat_tiles, i % tiles_per_batch, tiles_per_batch + i - n_lat_tiles),
    }

    cs = jnp.concatenate(
        [c, c_ctx[None, :], jnp.zeros((SUBLANES - bsz - 1, d), F32)], axis=0)
    mod = _modulation(cs, w_mod, b_mod).reshape(depth, SUBLANES, MOD_CHUNKS, d)

    tables = _rope_tables(seq_len, n_ctx_rows)

    w_gate = jnp.concatenate([lru_wa, lru_wx], axis=-1).astype(BF16)
    b_gate = 0.5 * jnp.concatenate(
        [lru_ba.reshape(depth, 2, N_RNN_BLOCKS, 1, RNN_BLOCK_W),
         lru_bx.reshape(depth, 2, N_RNN_BLOCKS, 1, RNN_BLOCK_W)], axis=-1)
    lam = lru_lam.reshape(depth, 2, N_RNN_BLOCKS, 1, RNN_BLOCK_W)
    conv_w_half, conv_b_half = 0.5 * conv_w, 0.5 * conv_b

    stream = (x.reshape(n_lat_rows, d), ctx.reshape(n_ctx_rows, d), 0)
    for l in range(depth):
        need_ctx = l < depth - 1
        n_out_tiles = n_lat_tiles + n_ctx_tiles if need_ctx else n_lat_tiles
        xr, gr, q, k, vt, gl = _in_proj(
            stream, mod[l], g_mix_pre[l], w_in[l].astype(BF16), tables, geo)
        hf, hb = _rnn_branch(xr, conv_w_half[l], conv_b_half[l], w_gate[l], b_gate[l], lam[l], geo)
        o_stream = _attention(q, k, vt, attn_sink[l], geo, need_ctx)
        x_all = _merge(stream, hf, hb, gr, o_stream, gl, mod[l], g_mix_post[l],
                       w_o_rnn[l].astype(BF16), w_o_attn[l].astype(BF16),
                       w_out[l].astype(BF16), geo, n_out_tiles)
        x_all = _ffn(x_all, mod[l], g_ffn_pre[l], g_ffn_post[l],
                     w_ffn_in[l].astype(BF16), w_ffn_out[l].astype(BF16), geo, n_out_tiles)
        stream = (x_all, x_all, n_lat_tiles)
    return x_all[:n_lat_rows].reshape(bsz, seq_len, d)
```

```python
import functools
import math

import jax
import jax.numpy as jnp
from jax import lax
from jax.experimental import pallas as pl
from jax.experimental.pallas import tpu as pltpu

D_MODEL = 1024
HEAD_DIM = 128
N_Q_HEADS = 8
N_KV_HEADS = 2
Q_PER_KV = N_Q_HEADS // N_KV_HEADS
KV_WIDTH = N_KV_HEADS * HEAD_DIM
WINDOW = 128
GRID_W = 64
N_FREQ = HEAD_DIM // 4
ROPE_BASE = 10000.0
N_RNN_BLOCKS = 8
RNN_BLOCK_W = D_MODEL // N_RNN_BLOCKS
LRU_C = 8.0
CONV_W = 4
CONV_LEFT = 2
D_FF = 2816
EPS = 1e-6
NEG_INF = -1e30
MOD_CHUNKS = 6

LANES = 128
SUBLANES = 8
VMEM_LIMIT_BYTES = 56 * 1024 * 1024

CHUNK = 256
CONV_T = 16
Q_BLOCK = 128
Q_SLOTS = 2
KEY_BLOCK = 128
BAND = Q_BLOCK + 2 * WINDOW
LOG2_E = math.log2(math.e)

BF16 = jnp.bfloat16
F32 = jnp.float32
F32_TINY = float(jnp.finfo(jnp.float32).tiny)


def _dot(a, b):
    return jnp.dot(a, b, preferred_element_type=F32)


def _sigmoid(x):
    return 0.5 * jnp.tanh(0.5 * x) + 0.5


def _gelu_tanh(x):
    c = math.sqrt(2.0 / math.pi)
    return 0.5 * x * (1.0 + jnp.tanh(c * (x + 0.044715 * (x * x * x))))


def _rms(x):
    return x * lax.rsqrt(jnp.mean(x * x, axis=-1, keepdims=True) + EPS)


def _resident(shape):
    nd = len(shape)
    return pl.BlockSpec(shape, lambda *_: (0,) * nd, pipeline_mode=pl.Buffered(1))


def _resident_layer(stacked, layer):
    tail = stacked.shape[1:]
    return pl.BlockSpec((None,) + tail, lambda *_: (layer,) + (0,) * len(tail),
                        pipeline_mode=pl.Buffered(1))


def _cast_kernel(w_ref, o_ref):
    o_ref[...] = w_ref[...].astype(BF16)


def _to_bf16(w, row_block):
    depth, rows, cols = w.shape
    spec = pl.BlockSpec((None, row_block, cols), lambda l, r: (l, r, 0))
    return pl.pallas_call(
        _cast_kernel,
        grid=(depth, rows // row_block),
        in_specs=[spec],
        out_specs=spec,
        out_shape=jax.ShapeDtypeStruct(w.shape, BF16),
        compiler_params=pltpu.CompilerParams(
            dimension_semantics=("arbitrary", "arbitrary"),
            vmem_limit_bytes=VMEM_LIMIT_BYTES),
        name="cast_bf16",
    )(w)


def _mod_kernel(c_ref, w_ref, b_ref, o_ref):
    c = c_ref[...]
    s = (c * _sigmoid(c)).astype(BF16)
    o_ref[...] = _dot(s, w_ref[...].astype(BF16)) + b_ref[...]


def _modulation(cs, w_mod, b_mod):
    n_layers, d, width = w_mod.shape
    tn = 1536
    return pl.pallas_call(
        _mod_kernel,
        grid=(n_layers, width // tn),
        in_specs=[
            pl.BlockSpec((SUBLANES, d), lambda l, j: (0, 0)),
            pl.BlockSpec((None, d, tn), lambda l, j: (l, 0, j)),
            pl.BlockSpec((None, 1, tn), lambda l, j: (l, 0, j)),
        ],
        out_specs=pl.BlockSpec((None, SUBLANES, tn), lambda l, j: (l, 0, j)),
        out_shape=jax.ShapeDtypeStruct((n_layers, SUBLANES, width), F32),
        compiler_params=pltpu.CompilerParams(
            dimension_semantics=("arbitrary", "arbitrary"),
            vmem_limit_bytes=VMEM_LIMIT_BYTES),
        name="modulation",
    )(cs, w_mod, b_mod.reshape(n_layers, 1, width))


def _rope(t, cos_ref, sin_ref, n_heads):
    cos = cos_ref[...]
    sin = sin_ref[...]
    lane = lax.broadcasted_iota(jnp.int32, cos.shape, 1)
    first = (lane & (2 * N_FREQ - 1)) < N_FREQ
    outs = []
    for h in range(n_heads):
        th = t[:, h * HEAD_DIM:(h + 1) * HEAD_DIM]
        partner = jnp.where(first,
                            pltpu.roll(th, HEAD_DIM - N_FREQ, axis=1),
                            pltpu.roll(th, N_FREQ, axis=1))
        outs.append(th * cos + partner * sin)
    return jnp.concatenate(outs, axis=-1)


def _stream_specs(stream, tm, d, n_lat_tiles):
    _, ctx, ctx_off = stream
    last_ctx = ctx.shape[0] // tm - 1
    lat_spec = pl.BlockSpec((tm, d), lambda i: (jnp.minimum(i, n_lat_tiles - 1), 0))
    ctx_spec = pl.BlockSpec(
        (tm, d), lambda i: (jnp.minimum(jnp.maximum(i - n_lat_tiles, 0) + ctx_off, last_ctx), 0))
    return lat_spec, ctx_spec


def _stream_tile(lat_ref, ctx_ref, n_lat_tiles):
    return jnp.where(pl.program_id(0) < n_lat_tiles, lat_ref[...], ctx_ref[...])


def _in_kernel(xl_ref, xc_ref, mod_ref, g_ref, w_ref, cq_ref, sq_ref, ck_ref, sk_ref,
               xr_ref, gr_ref, q_ref, k_ref, vt_ref, gl_ref, *, n_lat_tiles):
    d = D_MODEL
    x = _stream_tile(xl_ref, xc_ref, n_lat_tiles)
    h = (_rms(x) * g_ref[...]) * (1.0 + mod_ref[1:2, :]) + mod_ref[0:1, :]
    h = h.astype(BF16)
    xr = _dot(h, w_ref[:, 0:d])
    for n in range(N_RNN_BLOCKS):
        xr_ref[pl.ds(n, x.shape[0], stride=N_RNN_BLOCKS), :] = (
            xr[:, n * RNN_BLOCK_W:(n + 1) * RNN_BLOCK_W])
    gr_ref[...] = _dot(h, w_ref[:, d:2 * d]).astype(BF16)
    q = _dot(h, w_ref[:, 2 * d:3 * d])
    q_ref[...] = _rope(q, cq_ref, sq_ref, N_Q_HEADS).astype(BF16)
    k = _dot(h, w_ref[:, 3 * d:3 * d + KV_WIDTH])
    k_ref[...] = _rope(k, ck_ref, sk_ref, N_KV_HEADS).astype(BF16)
    v = _dot(h, w_ref[:, 3 * d + KV_WIDTH:3 * d + 2 * KV_WIDTH])
    for t in range(vt_ref.shape[0]):
        vt_ref[t] = v[t * KEY_BLOCK:(t + 1) * KEY_BLOCK, :].T.astype(BF16)
    gl_ref[...] = _dot(h, w_ref[:, 3 * d + 2 * KV_WIDTH:5 * d + 2 * KV_WIDTH]).astype(BF16)


def _in_proj(stream, mod_l, g_pre, w_in, layer, tables, geo):
    d = D_MODEL
    tm = geo["tm"]
    n_tiles = geo["n_lat_tiles"] + geo["n_ctx_tiles"]
    nt = n_tiles * tm
    row = lambda i: (i, 0)
    tab = pl.BlockSpec((tm, LANES), lambda i: (geo["table_block"](i), 0))
    return pl.pallas_call(
        functools.partial(_in_kernel, n_lat_tiles=geo["n_lat_tiles"]),
        grid=(n_tiles,),
        in_specs=[
            *_stream_specs(stream, tm, d, geo["n_lat_tiles"]),
            pl.BlockSpec((None, MOD_CHUNKS, d), lambda i: (geo["mod_row"](i), 0, 0)),
            _resident((1, d)),
            _resident_layer(w_in, layer),
            tab, tab, tab, tab,
        ],
        out_specs=[
            pl.BlockSpec((tm * N_RNN_BLOCKS, RNN_BLOCK_W), row),
            pl.BlockSpec((tm, d), row),
            pl.BlockSpec((tm, d), row),
            pl.BlockSpec((tm, KV_WIDTH), row),
            pl.BlockSpec((tm // KEY_BLOCK, KV_WIDTH, KEY_BLOCK), lambda i: (i, 0, 0)),
            pl.BlockSpec((tm, 2 * d), row),
        ],
        out_shape=[
            jax.ShapeDtypeStruct((nt * N_RNN_BLOCKS, RNN_BLOCK_W), F32),
            jax.ShapeDtypeStruct((nt, d), BF16),
            jax.ShapeDtypeStruct((nt, d), BF16),
            jax.ShapeDtypeStruct((nt, KV_WIDTH), BF16),
            jax.ShapeDtypeStruct((nt // KEY_BLOCK, KV_WIDTH, KEY_BLOCK), BF16),
            jax.ShapeDtypeStruct((nt, 2 * d), BF16),
        ],
        compiler_params=pltpu.CompilerParams(
            dimension_semantics=("arbitrary",),
            vmem_limit_bytes=VMEM_LIMIT_BYTES),
        name="in_proj",
    )(stream[0], stream[1], mod_l, g_pre.reshape(1, d), w_in, *tables)


def _rnn_kernel(xf_ref, xfp_ref, xfn_ref, xb_ref, xbp_ref, xbn_ref,
                cw_ref, cb_ref, wg_ref, bg_ref, lam_ref,
                hf_ref, hb_ref,
                xc_s, a_s, b_s, carry_s):
    j = pl.program_id(1)
    n_steps = pl.num_programs(1)
    nb, bw = N_RNN_BLOCKS, RNN_BLOCK_W
    n_in = CONV_T + CONV_W - 1
    right = CONV_W - 1 - CONV_LEFT

    @pl.when(j == 0)
    def _():
        carry_s[...] = jnp.zeros_like(carry_s)

    def coeffs(direction, x_ref, xp_ref, xn_ref, has_prev, has_next):
        def conv_piece(v, base):
            v = v.reshape(n_in, nb, bw)
            acc = cb_ref[...] + v[0:CONV_T] * cw_ref[0]
            for k in range(1, CONV_W):
                acc = acc + v[k:k + CONV_T] * cw_ref[k]
            xc_s[pl.ds(base, CONV_T * nb), :] = acc.reshape(CONV_T * nb, bw)

        halo_rows = xp_ref.shape[0]
        left = jnp.where(has_prev, xp_ref[halo_rows - CONV_LEFT * nb:halo_rows, :], 0.0)
        conv_piece(jnp.concatenate([left, x_ref[0:(n_in - CONV_LEFT) * nb, :]], axis=0), 0)
        tail = jnp.where(has_next, xn_ref[0:right * nb, :], 0.0)
        last = CHUNK - CONV_T
        conv_piece(jnp.concatenate(
            [x_ref[(last - CONV_LEFT) * nb:CHUNK * nb, :], tail], axis=0), last * nb)

        def conv_body(c, carry):
            base = pl.multiple_of(c * (CONV_T * nb), CONV_T * nb)
            conv_piece(x_ref[pl.ds(base - CONV_LEFT * nb, n_in * nb), :], base)
            return carry

        lax.fori_loop(1, CHUNK // CONV_T - 1, conv_body, 0)

        for n in range(nb):
            rows = pl.ds(n, CHUNK, stride=nb)
            lam = lam_ref[direction, n]
            c_half = (0.5 * LRU_C * LOG2_E) * (
                jnp.minimum(lam, 0.0) - jnp.log1p(jnp.exp(-jnp.abs(lam))))
            xh = xc_s[rows, :]
            z = _dot(xh.astype(BF16), wg_ref[direction, n]) + bg_ref[direction, n]
            a = jnp.exp2(c_half * jnp.tanh(z[:, 0:bw]) + c_half)
            gated = xh * jnp.tanh(z[:, bw:2 * bw]) + xh
            om = 1.0 - a * a
            a_s[direction, rows, :] = a
            b_s[direction, rows, :] = (om * lax.rsqrt(jnp.maximum(om, F32_TINY))) * gated

    coeffs(0, xf_ref, xfp_ref, xfn_ref, j >= 2, jnp.logical_and(j >= 1, j < n_steps - 1))
    coeffs(1, xb_ref, xbp_ref, xbn_ref, jnp.logical_and(j >= 1, j < n_steps - 1), j >= 2)

    def step(t, carry):
        hf, hb = carry
        tf = pl.multiple_of(t * nb, nb)
        tb = pl.multiple_of((CHUNK - 1 - t) * nb, nb)
        hf = a_s[0, pl.ds(tf, nb), :] * hf + b_s[0, pl.ds(tf, nb), :]
        hb = a_s[1, pl.ds(tb, nb), :] * hb + b_s[1, pl.ds(tb, nb), :]
        hf_ref[pl.ds(tf, nb), :] = hf
        hb_ref[pl.ds(tb, nb), :] = hb
        return hf, hb

    hf, hb = lax.fori_loop(0, CHUNK, step, (carry_s[0], carry_s[1]), unroll=8)
    carry_s[0] = hf
    carry_s[1] = hb


def _rnn_branch(xr, conv_w, conv_b, w_gate, b_gate, lam, geo):
    nb, bw = N_RNN_BLOCKS, RNN_BLOCK_W
    bsz, n_lat = geo["batch"], geo["lat_chunks"]
    n_steps = n_lat + 1
    halo = SUBLANES
    n_halo = xr.shape[0] // (halo * nb)
    per = CHUNK // halo

    def fwd_blk(b, j):
        return jnp.where(j == 0, bsz * n_lat + b, b * n_lat + j - 1)

    def bwd_blk(b, j):
        return jnp.where(j == 0, bsz * n_lat + b, b * n_lat + n_lat - j)

    def chunk(blk):
        return pl.BlockSpec((CHUNK * nb, bw), lambda b, j: (blk(b, j), 0))

    def prev(blk):
        return pl.BlockSpec(
            (halo * nb, bw), lambda b, j: (jnp.maximum(blk(b, j) * per - 1, 0), 0))

    def nxt(blk):
        return pl.BlockSpec(
            (halo * nb, bw), lambda b, j: (jnp.minimum((blk(b, j) + 1) * per, n_halo - 1), 0))

    return pl.pallas_call(
        _rnn_kernel,
        grid=(bsz, n_steps),
        in_specs=[
            chunk(fwd_blk), prev(fwd_blk), nxt(fwd_blk),
            chunk(bwd_blk), prev(bwd_blk), nxt(bwd_blk),
            _resident((CONV_W, nb, bw)),
            _resident((nb, bw)),
            _resident(w_gate.shape),
            _resident(b_gate.shape),
            _resident(lam.shape),
        ],
        out_specs=[chunk(fwd_blk), chunk(bwd_blk)],
        out_shape=[jax.ShapeDtypeStruct(xr.shape, F32)] * 2,
        scratch_shapes=[
            pltpu.VMEM((CHUNK * nb, bw), F32),
            pltpu.VMEM((2, CHUNK * nb, bw), F32),
            pltpu.VMEM((2, CHUNK * nb, bw), F32),
            pltpu.VMEM((2, nb, bw), F32),
        ],
        compiler_params=pltpu.CompilerParams(
            dimension_semantics=("arbitrary", "arbitrary"),
            vmem_limit_bytes=VMEM_LIMIT_BYTES),
        name="rglru",
    )(xr, xr, xr, xr, xr, xr, conv_w.reshape(CONV_W, nb, bw), conv_b.reshape(nb, bw),
      w_gate, b_gate, lam)


def _stack_heads(q, g):
    base = g * Q_PER_KV * HEAD_DIM
    return jnp.concatenate(
        [q[:, base + h * HEAD_DIM: base + (h + 1) * HEAD_DIM] for h in range(Q_PER_KV)], axis=0)


def _nt_dot(a, b):
    return lax.dot_general(a, b, (((1,), (1,)), ((), ())), preferred_element_type=F32)


def _attend_group(sink_ref, g, q, n_q, key_blocks, vt, s_s, p_s, o_store):
    width = Q_PER_KV * n_q
    folds = KEY_BLOCK // SUBLANES
    qs = _stack_heads(q, g)
    sink = jnp.concatenate(
        [jnp.full((1, n_q), sink_ref[g * Q_PER_KV + h] * LOG2_E, F32) for h in range(Q_PER_KV)],
        axis=1)
    m8 = jnp.broadcast_to(sink, (SUBLANES, width))
    for j, (k, bias) in enumerate(key_blocks):
        s = _nt_dot(k, qs)
        if bias is not None:
            s = s + jnp.concatenate([bias] * Q_PER_KV, axis=1)
        s_s[j * KEY_BLOCK:(j + 1) * KEY_BLOCK, :] = s
        m8 = jnp.maximum(m8, jnp.max(s.reshape(folds, SUBLANES, width), axis=0))
    m = jnp.max(m8, axis=0, keepdims=True)
    l8 = jnp.zeros((SUBLANES, width), F32)
    for j in range(len(key_blocks)):
        p = jnp.exp2(s_s[j * KEY_BLOCK:(j + 1) * KEY_BLOCK, :] - m)
        l8 = l8 + jnp.sum(p.reshape(folds, SUBLANES, width), axis=0)
        p_s[j * KEY_BLOCK:(j + 1) * KEY_BLOCK, :] = p.astype(BF16)
    denom = jnp.sum(l8, axis=0, keepdims=True) + jnp.exp2(sink - m)
    ot = _dot(vt, p_s[0:len(key_blocks) * KEY_BLOCK, :]) * (1.0 / denom)
    for h in range(Q_PER_KV):
        o_store(h, ot[:, h * n_q:(h + 1) * n_q].T)


def _lat_attn_kernel(sink_ref, q_ref, k_ref, vt_ref, kc_ref, vtc_ref, o_ref, s_s, p_s,
                     *, seq_len):
    n_blk = q_ref.shape[0] // Q_BLOCK
    band_blocks = BAND // KEY_BLOCK
    ctx_blocks = vtc_ref.shape[0]
    last_start = seq_len // KEY_BLOCK - band_blocks
    rel = (lax.broadcasted_iota(jnp.int32, (KEY_BLOCK, Q_BLOCK), 0)
           - lax.broadcasted_iota(jnp.int32, (KEY_BLOCK, Q_BLOCK), 1))

    def one_block(i, slot):
        qb = pl.program_id(1) * n_blk + i
        jb0 = jnp.clip(qb - WINDOW // KEY_BLOCK, 0, last_start)
        biases = [jnp.where(jnp.abs(rel + (jb0 + t - qb) * KEY_BLOCK) <= WINDOW, 0.0, NEG_INF)
                  for t in range(band_blocks)]
        rows = pl.ds(pl.multiple_of(i * Q_BLOCK, Q_BLOCK), Q_BLOCK)
        q = q_ref[rows, :]
        for g in range(N_KV_HEADS):
            gs = slice(g * HEAD_DIM, (g + 1) * HEAD_DIM)
            key_blocks = [
                (k_ref[pl.ds(pl.multiple_of((jb0 + t) * KEY_BLOCK, KEY_BLOCK), KEY_BLOCK), gs],
                 biases[t]) for t in range(band_blocks)]
            key_blocks += [(kc_ref[t * KEY_BLOCK:(t + 1) * KEY_BLOCK, gs], None)
                           for t in range(ctx_blocks)]
            vt = jnp.concatenate([vt_ref[jb0 + t, gs, :] for t in range(band_blocks)]
                                 + [vtc_ref[t, gs, :] for t in range(ctx_blocks)], axis=1)

            def o_store(h, tile, g=g):
                c0 = (g * Q_PER_KV + h) * HEAD_DIM
                o_ref[rows, c0:c0 + HEAD_DIM] = tile.astype(BF16)

            _attend_group(sink_ref, g, q, Q_BLOCK, key_blocks, vt,
                          s_s.at[slot, g], p_s.at[slot, g], o_store)

    def body(it, carry):
        for slot in range(Q_SLOTS):
            one_block(it * Q_SLOTS + slot, slot)
        return carry

    lax.fori_loop(0, n_blk // Q_SLOTS, body, 0)


def _ctx_attn_kernel(sink_ref, q_ref, kc_ref, vtc_ref, o_ref, s_s, p_s):
    n_ctx = q_ref.shape[0]
    ctx_blocks = vtc_ref.shape[0]
    q = q_ref[...]
    for g in range(N_KV_HEADS):
        gs = slice(g * HEAD_DIM, (g + 1) * HEAD_DIM)
        key_blocks = [(kc_ref[t * KEY_BLOCK:(t + 1) * KEY_BLOCK, gs], None)
                      for t in range(ctx_blocks)]
        vt = jnp.concatenate([vtc_ref[t, gs, :] for t in range(ctx_blocks)], axis=1)

        def o_store(h, tile, g=g):
            c0 = (g * Q_PER_KV + h) * HEAD_DIM
            o_ref[:, c0:c0 + HEAD_DIM] = tile.astype(BF16)

        _attend_group(sink_ref, g, q, n_ctx, key_blocks, vt, s_s.at[g], p_s.at[g], o_store)


def _attention(q, k, vt, sink, geo, with_ctx_queries):
    nt, d = q.shape
    bsz, seq_len, n_ctx = geo["batch"], geo["seq"], geo["ctx"]
    q_sup = next(s for s in (1024, 512, 256, Q_BLOCK) if seq_len % s == 0)
    n_sup = seq_len // q_sup
    ctx_blk0 = bsz * seq_len // n_ctx
    smem = pl.BlockSpec(memory_space=pltpu.SMEM)
    ctx_keys = pl.BlockSpec((n_ctx, KV_WIDTH), lambda b, *_: (ctx_blk0 + b, 0))
    ctx_vals = pl.BlockSpec((n_ctx // KEY_BLOCK, KV_WIDTH, KEY_BLOCK),
                            lambda b, *_: (ctx_blk0 + b, 0, 0))
    o = pl.pallas_call(
        functools.partial(_lat_attn_kernel, seq_len=seq_len),
        grid=(bsz, n_sup),
        in_specs=[
            smem,
            pl.BlockSpec((q_sup, d), lambda b, i: (b * n_sup + i, 0)),
            pl.BlockSpec((seq_len, KV_WIDTH), lambda b, i: (b, 0)),
            pl.BlockSpec((seq_len // KEY_BLOCK, KV_WIDTH, KEY_BLOCK), lambda b, i: (b, 0, 0)),
            ctx_keys, ctx_vals,
        ],
        out_specs=pl.BlockSpec((q_sup, d), lambda b, i: (b * n_sup + i, 0)),
        out_shape=jax.ShapeDtypeStruct((bsz * seq_len, d), BF16),
        scratch_shapes=[
            pltpu.VMEM((Q_SLOTS, N_KV_HEADS, BAND + n_ctx, Q_PER_KV * Q_BLOCK), F32),
            pltpu.VMEM((Q_SLOTS, N_KV_HEADS, BAND + n_ctx, Q_PER_KV * Q_BLOCK), BF16),
        ],
        compiler_params=pltpu.CompilerParams(
            dimension_semantics=("arbitrary", "arbitrary"),
            vmem_limit_bytes=VMEM_LIMIT_BYTES),
        name="lat_attention",
    )(sink, q, k, vt, k, vt)
    if not with_ctx_queries:
        return (o, o, 0)
    o_ctx = pl.pallas_call(
        _ctx_attn_kernel,
        grid=(bsz,),
        in_specs=[
            smem,
            pl.BlockSpec((n_ctx, d), lambda b: (ctx_blk0 + b, 0)),
            ctx_keys, ctx_vals,
        ],
        out_specs=pl.BlockSpec((n_ctx, d), lambda b: (b, 0)),
        out_shape=jax.ShapeDtypeStruct((bsz * n_ctx, d), BF16),
        scratch_shapes=[
            pltpu.VMEM((N_KV_HEADS, n_ctx, Q_PER_KV * n_ctx), F32),
            pltpu.VMEM((N_KV_HEADS, n_ctx, Q_PER_KV * n_ctx), BF16),
        ],
        compiler_params=pltpu.CompilerParams(
            dimension_semantics=("arbitrary",),
            vmem_limit_bytes=VMEM_LIMIT_BYTES),
        name="ctx_attention",
    )(sink, q, k, vt)
    return (o, o_ctx, 0)


def _merge_kernel(xl_ref, xc_ref, hf_ref, hb_ref, gr_ref, ol_ref, oc_ref, gl_ref, mod_ref,
                  g_ref, wr_ref, wa_ref, wo_ref, out_ref, *, n_lat_tiles):
    d = D_MODEL
    tm = xl_ref.shape[0]
    x = _stream_tile(xl_ref, xc_ref, n_lat_tiles)
    o = _stream_tile(ol_ref, oc_ref, n_lat_tiles)
    h = jnp.concatenate(
        [hf_ref[pl.ds(n, tm, stride=N_RNN_BLOCKS), :] + hb_ref[pl.ds(n, tm, stride=N_RNN_BLOCKS), :]
         for n in range(N_RNN_BLOCKS)], axis=-1)
    y = h * _gelu_tanh(gr_ref[...].astype(F32))
    ya = _dot(y.astype(BF16), wr_ref[...])
    yb = _dot(o, wa_ref[...])
    ga = _sigmoid(gl_ref[:, 0:d].astype(F32))
    gb = _sigmoid(gl_ref[:, d:2 * d].astype(F32))
    m = _dot((ga * ya + gb * yb).astype(BF16), wo_ref[...])
    out_ref[...] = x + mod_ref[2:3, :] * (_rms(m) * g_ref[...])


def _merge(x_stream, hf, hb, gr, o_stream, gl, mod_l, g_post, w_o_rnn, w_o_attn, w_out,
           layer, geo, n_tiles):
    d = D_MODEL
    tm = geo["tm"]
    n_lat_tiles = geo["n_lat_tiles"]
    row = lambda i: (i, 0)
    tile = pl.BlockSpec((tm, d), row)
    slab = pl.BlockSpec((tm * N_RNN_BLOCKS, RNN_BLOCK_W), row)
    return pl.pallas_call(
        functools.partial(_merge_kernel, n_lat_tiles=n_lat_tiles),
        grid=(n_tiles,),
        in_specs=[
            *_stream_specs(x_stream, tm, d, n_lat_tiles), slab, slab, tile,
            *_stream_specs(o_stream, tm, d, n_lat_tiles),
            pl.BlockSpec((tm, 2 * d), row),
            pl.BlockSpec((None, MOD_CHUNKS, d), lambda i: (geo["mod_row"](i), 0, 0)),
            _resident((1, d)),
            _resident_layer(w_o_rnn, layer), _resident_layer(w_o_attn, layer),
            _resident_layer(w_out, layer),
        ],
        out_specs=tile,
        out_shape=jax.ShapeDtypeStruct((n_tiles * tm, d), F32),
        compiler_params=pltpu.CompilerParams(
            dimension_semantics=("arbitrary",),
            vmem_limit_bytes=VMEM_LIMIT_BYTES),
        name="merge",
    )(x_stream[0], x_stream[1], hf, hb, gr, o_stream[0], o_stream[1], gl, mod_l,
      g_post.reshape(1, d), w_o_rnn, w_o_attn, w_out)


def _ffn_kernel(x_ref, mod_ref, gpre_ref, gpost_ref, w1_ref, w2_ref, out_ref):
    x = x_ref[...]
    h = (_rms(x) * gpre_ref[...]) * (1.0 + mod_ref[4:5, :]) + mod_ref[3:4, :]
    h = h.astype(BF16)
    gate = _dot(h, w1_ref[:, 0:D_FF])
    up = _dot(h, w1_ref[:, D_FF:2 * D_FF])
    act = (gate * _sigmoid(gate) * up).astype(BF16)
    f = _dot(act, w2_ref[...])
    out_ref[...] = x + mod_ref[5:6, :] * (_rms(f) * gpost_ref[...])


def _ffn(x_all, mod_l, g_pre, g_post, w1, w2, layer, geo, n_tiles):
    d = x_all.shape[1]
    tm = geo["tm"]
    tile = pl.BlockSpec((tm, d), lambda i: (i, 0))
    return pl.pallas_call(
        _ffn_kernel,
        grid=(n_tiles,),
        in_specs=[
            tile,
            pl.BlockSpec((None, MOD_CHUNKS, d), lambda i: (geo["mod_row"](i), 0, 0)),
            _resident((1, d)), _resident((1, d)),
            _resident_layer(w1, layer), _resident_layer(w2, layer),
        ],
        out_specs=tile,
        out_shape=jax.ShapeDtypeStruct((n_tiles * tm, d), F32),
        compiler_params=pltpu.CompilerParams(
            dimension_semantics=("arbitrary",),
            vmem_limit_bytes=VMEM_LIMIT_BYTES),
        name="ffn",
    )(x_all, mod_l, g_pre.reshape(1, d), g_post.reshape(1, d), w1, w2)


def _rope_tables(seq_len, pad_rows):
    pos = jnp.arange(seq_len, dtype=jnp.int32)
    inv = ROPE_BASE ** (-jnp.arange(N_FREQ, dtype=F32) / N_FREQ)
    ang_r = (pos // GRID_W).astype(F32)[:, None] * inv[None, :]
    ang_c = (pos % GRID_W).astype(F32)[:, None] * inv[None, :]
    cos = jnp.concatenate([jnp.cos(ang_r)] * 2 + [jnp.cos(ang_c)] * 2, axis=-1)
    sin = jnp.concatenate([-jnp.sin(ang_r), jnp.sin(ang_r), -jnp.sin(ang_c), jnp.sin(ang_c)], axis=-1)
    cos = jnp.concatenate([cos, jnp.ones((pad_rows, HEAD_DIM), F32)], axis=0)
    sin = jnp.concatenate([sin, jnp.zeros((pad_rows, HEAD_DIM), F32)], axis=0)
    scale = HEAD_DIM ** -0.5 * LOG2_E
    return cos * scale, sin * scale, cos, sin


def kernel(x, c, ctx, c_ctx, w_mod, b_mod, g_mix_pre, g_mix_post, g_ffn_pre, g_ffn_post, w_in, conv_w, conv_b, lru_wa, lru_ba, lru_wx, lru_bx, lru_lam, attn_sink, w_o_rnn, w_o_attn, w_out, w_ffn_in, w_ffn_out):
    bsz, seq_len, d = x.shape
    n_ctx = ctx.shape[1]
    depth = w_mod.shape[0]
    assert d == D_MODEL and n_ctx == CHUNK and seq_len % CHUNK == 0 and seq_len >= BAND
    assert bsz + 1 <= SUBLANES and seq_len % GRID_W == 0

    n_lat_rows, n_ctx_rows = bsz * seq_len, bsz * n_ctx
    tm = 512 if (n_ctx_rows % 512 == 0 and seq_len % 512 == 0) else CHUNK
    n_lat_tiles, n_ctx_tiles = n_lat_rows // tm, n_ctx_rows // tm
    tiles_per_batch = seq_len // tm
    geo = {
        "batch": bsz, "seq": seq_len, "ctx": n_ctx, "tm": tm,
        "lat_chunks": seq_len // CHUNK,
        "n_lat_tiles": n_lat_tiles, "n_ctx_tiles": n_ctx_tiles,
        "mod_row": lambda i: jnp.where(i < n_lat_tiles, i // tiles_per_batch, bsz),
        "table_block": lambda i: jnp.where(
            i < n_lat_tiles, i % tiles_per_batch, tiles_per_batch + i - n_lat_tiles),
    }

    cs = jnp.concatenate(
        [c, c_ctx[None, :], jnp.zeros((SUBLANES - bsz - 1, d), F32)], axis=0)
    mod = _modulation(cs, w_mod, b_mod).reshape(depth, SUBLANES, MOD_CHUNKS, d)

    tables = _rope_tables(seq_len, n_ctx_rows)

    w_gate = jnp.concatenate([lru_wa, lru_wx], axis=-1).astype(BF16)
    b_gate = 0.5 * jnp.concatenate(
        [lru_ba.reshape(depth, 2, N_RNN_BLOCKS, 1, RNN_BLOCK_W),
         lru_bx.reshape(depth, 2, N_RNN_BLOCKS, 1, RNN_BLOCK_W)], axis=-1)
    lam = lru_lam.reshape(depth, 2, N_RNN_BLOCKS, 1, RNN_BLOCK_W)
    conv_w_half, conv_b_half = 0.5 * conv_w, 0.5 * conv_b

    stream = (x.reshape(n_lat_rows, d), ctx.reshape(n_ctx_rows, d), 0)
    w_in_b = _to_bf16(w_in, 256)
    w_o_rnn_b, w_o_attn_b, w_out_b = (_to_bf16(w, 512) for w in (w_o_rnn, w_o_attn, w_out))
    w_ffn_in_b = _to_bf16(w_ffn_in, 256)
    w_ffn_out_b = _to_bf16(w_ffn_out, D_FF // 4)
    for l in range(depth):
        need_ctx = l < depth - 1
        n_out_tiles = n_lat_tiles + n_ctx_tiles if need_ctx else n_lat_tiles
        xr, gr, q, k, vt, gl = _in_proj(stream, mod[l], g_mix_pre[l], w_in_b, l, tables, geo)
        hf, hb = _rnn_branch(xr, conv_w_half[l], conv_b_half[l], w_gate[l], b_gate[l], lam[l], geo)
        o_stream = _attention(q, k, vt, attn_sink[l], geo, need_ctx)
        x_all = _merge(stream, hf, hb, gr, o_stream, gl, mod[l], g_mix_post[l],
                       w_o_rnn_b, w_o_attn_b, w_out_b, l, geo, n_out_tiles)
        x_all = _ffn(x_all, mod[l], g_ffn_pre[l], g_ffn_post[l],
                     w_ffn_in_b, w_ffn_out_b, l, geo, n_out_tiles)
        stream = (x_all, x_all, n_lat_tiles)
    return x_all[:n_lat_rows].reshape(bsz, seq_len, d)
```

```python
import functools
import math

import jax
import jax.numpy as jnp
from jax import lax
from jax.experimental import pallas as pl
from jax.experimental.pallas import tpu as pltpu

D_MODEL = 1024
HEAD_DIM = 128
N_Q_HEADS = 8
N_KV_HEADS = 2
Q_PER_KV = N_Q_HEADS // N_KV_HEADS
KV_WIDTH = N_KV_HEADS * HEAD_DIM
WINDOW = 128
GRID_W = 64
N_FREQ = HEAD_DIM // 4
ROPE_BASE = 10000.0
N_RNN_BLOCKS = 8
RNN_BLOCK_W = D_MODEL // N_RNN_BLOCKS
LRU_C = 8.0
CONV_W = 4
CONV_LEFT = 2
D_FF = 2816
EPS = 1e-6
NEG_INF = -1e30
MOD_CHUNKS = 6

LANES = 128
SUBLANES = 8
VMEM_LIMIT_BYTES = 56 * 1024 * 1024

CHUNK = 256
CONV_T = 16
MERGE_PARTS = 1
Q_BLOCK = 128
Q_SLOTS = 2
KEY_BLOCK = 128
BAND = Q_BLOCK + 2 * WINDOW
LOG2_E = math.log2(math.e)

BF16 = jnp.bfloat16
F32 = jnp.float32
F32_TINY = float(jnp.finfo(jnp.float32).tiny)


def _dot(a, b):
    return jnp.dot(a, b, preferred_element_type=F32)


def _sigmoid(x):
    return 0.5 * jnp.tanh(0.5 * x) + 0.5


def _gelu_tanh(x):
    c = math.sqrt(2.0 / math.pi)
    return 0.5 * x * (1.0 + jnp.tanh(c * (x + 0.044715 * (x * x * x))))


def _rms(x):
    return x * lax.rsqrt(jnp.mean(x * x, axis=-1, keepdims=True) + EPS)


def _resident(shape):
    nd = len(shape)
    return pl.BlockSpec(shape, lambda *_: (0,) * nd, pipeline_mode=pl.Buffered(1))


def _resident_layer(stacked, layer):
    tail = stacked.shape[1:]
    return pl.BlockSpec((None,) + tail, lambda *_: (layer,) + (0,) * len(tail),
                        pipeline_mode=pl.Buffered(1))


def _cast_kernel(w_ref, scale_ref, o_ref):
    o_ref[...] = (w_ref[...] * scale_ref[...]).astype(BF16)


def _to_bf16(w, row_block, col_scale):
    depth, rows, cols = w.shape
    spec = pl.BlockSpec((None, row_block, cols), lambda l, r: (l, r, 0))
    return pl.pallas_call(
        _cast_kernel,
        grid=(depth, rows // row_block),
        in_specs=[spec, pl.BlockSpec((1, cols), lambda l, r: (0, 0))],
        out_specs=spec,
        out_shape=jax.ShapeDtypeStruct(w.shape, BF16),
        compiler_params=pltpu.CompilerParams(
            dimension_semantics=("arbitrary", "arbitrary"),
            vmem_limit_bytes=VMEM_LIMIT_BYTES),
        name="cast_bf16",
    )(w, col_scale)


def _mod_kernel(c_ref, w_ref, b_ref, o_ref):
    c = c_ref[...]
    s = (c * _sigmoid(c)).astype(BF16)
    o_ref[...] = _dot(s, w_ref[...].astype(BF16)) + b_ref[...]


def _modulation(cs, w_mod, b_mod):
    n_layers, d, width = w_mod.shape
    tn = 1536
    return pl.pallas_call(
        _mod_kernel,
        grid=(n_layers, width // tn),
        in_specs=[
            pl.BlockSpec((SUBLANES, d), lambda l, j: (0, 0)),
            pl.BlockSpec((None, d, tn), lambda l, j: (l, 0, j)),
            pl.BlockSpec((None, 1, tn), lambda l, j: (l, 0, j)),
        ],
        out_specs=pl.BlockSpec((None, SUBLANES, tn), lambda l, j: (l, 0, j)),
        out_shape=jax.ShapeDtypeStruct((n_layers, SUBLANES, width), F32),
        compiler_params=pltpu.CompilerParams(
            dimension_semantics=("arbitrary", "arbitrary"),
            vmem_limit_bytes=VMEM_LIMIT_BYTES),
        name="modulation",
    )(cs, w_mod, b_mod.reshape(n_layers, 1, width))


def _rope(t, cos_ref, sin_ref, n_heads):
    cos = cos_ref[...]
    sin = sin_ref[...]
    lane = lax.broadcasted_iota(jnp.int32, cos.shape, 1)
    first = (lane & (2 * N_FREQ - 1)) < N_FREQ
    outs = []
    for h in range(n_heads):
        th = t[:, h * HEAD_DIM:(h + 1) * HEAD_DIM]
        partner = jnp.where(first,
                            pltpu.roll(th, HEAD_DIM - N_FREQ, axis=1),
                            pltpu.roll(th, N_FREQ, axis=1))
        outs.append(th * cos + partner * sin)
    return jnp.concatenate(outs, axis=-1)


def _stream_specs(stream, tm, d, n_lat_tiles):
    _, ctx, ctx_off = stream
    last_ctx = ctx.shape[0] // tm - 1
    lat_spec = pl.BlockSpec((tm, d), lambda i: (jnp.minimum(i, n_lat_tiles - 1), 0))
    ctx_spec = pl.BlockSpec(
        (tm, d), lambda i: (jnp.minimum(jnp.maximum(i - n_lat_tiles, 0) + ctx_off, last_ctx), 0))
    return lat_spec, ctx_spec


def _stream_tile(lat_ref, ctx_ref, n_lat_tiles):
    return jnp.where(pl.program_id(0) < n_lat_tiles, lat_ref[...], ctx_ref[...])


def _in_kernel(xl_ref, xc_ref, mod_ref, g_ref, w_ref, cq_ref, sq_ref, ck_ref, sk_ref,
               xr_ref, gr_ref, q_ref, k_ref, vt_ref, gl_ref, *, n_lat_tiles):
    d = D_MODEL
    x = _stream_tile(xl_ref, xc_ref, n_lat_tiles)
    h = (_rms(x) * g_ref[...]) * (1.0 + mod_ref[1:2, :]) + mod_ref[0:1, :]
    h = h.astype(BF16)
    xr = _dot(h, w_ref[:, 0:d])
    for n in range(N_RNN_BLOCKS):
        xr_ref[pl.ds(n, x.shape[0], stride=N_RNN_BLOCKS), :] = (
            xr[:, n * RNN_BLOCK_W:(n + 1) * RNN_BLOCK_W])
    gr_ref[...] = _dot(h, w_ref[:, d:2 * d]).astype(BF16)
    q = _dot(h, w_ref[:, 2 * d:3 * d])
    q_ref[...] = _rope(q, cq_ref, sq_ref, N_Q_HEADS).astype(BF16)
    k = _dot(h, w_ref[:, 3 * d:3 * d + KV_WIDTH])
    k_ref[...] = _rope(k, ck_ref, sk_ref, N_KV_HEADS).astype(BF16)
    v = _dot(h, w_ref[:, 3 * d + KV_WIDTH:3 * d + 2 * KV_WIDTH])
    for t in range(vt_ref.shape[0]):
        vt_ref[t] = v[t * KEY_BLOCK:(t + 1) * KEY_BLOCK, :].T.astype(BF16)
    gl_ref[...] = _dot(h, w_ref[:, 3 * d + 2 * KV_WIDTH:5 * d + 2 * KV_WIDTH]).astype(BF16)


def _in_proj(stream, mod_l, g_pre, w_in, layer, tables, geo):
    d = D_MODEL
    tm = geo["tm"]
    n_tiles = geo["n_lat_tiles"] + geo["n_ctx_tiles"]
    nt = n_tiles * tm
    row = lambda i: (i, 0)
    tab = pl.BlockSpec((tm, LANES), lambda i: (geo["table_block"](i), 0))
    return pl.pallas_call(
        functools.partial(_in_kernel, n_lat_tiles=geo["n_lat_tiles"]),
        grid=(n_tiles,),
        in_specs=[
            *_stream_specs(stream, tm, d, geo["n_lat_tiles"]),
            pl.BlockSpec((None, MOD_CHUNKS, d), lambda i: (geo["mod_row"](i), 0, 0)),
            _resident((1, d)),
            _resident_layer(w_in, layer),
            tab, tab, tab, tab,
        ],
        out_specs=[
            pl.BlockSpec((tm * N_RNN_BLOCKS, RNN_BLOCK_W), row),
            pl.BlockSpec((tm, d), row),
            pl.BlockSpec((tm, d), row),
            pl.BlockSpec((tm, KV_WIDTH), row),
            pl.BlockSpec((tm // KEY_BLOCK, KV_WIDTH, KEY_BLOCK), lambda i: (i, 0, 0)),
            pl.BlockSpec((tm, 2 * d), row),
        ],
        out_shape=[
            jax.ShapeDtypeStruct((nt * N_RNN_BLOCKS, RNN_BLOCK_W), F32),
            jax.ShapeDtypeStruct((nt, d), BF16),
            jax.ShapeDtypeStruct((nt, d), BF16),
            jax.ShapeDtypeStruct((nt, KV_WIDTH), BF16),
            jax.ShapeDtypeStruct((nt // KEY_BLOCK, KV_WIDTH, KEY_BLOCK), BF16),
            jax.ShapeDtypeStruct((nt, 2 * d), BF16),
        ],
        compiler_params=pltpu.CompilerParams(
            dimension_semantics=("arbitrary",),
            vmem_limit_bytes=VMEM_LIMIT_BYTES),
        name="in_proj",
    )(stream[0], stream[1], mod_l, g_pre.reshape(1, d), w_in, *tables)


def _rnn_kernel(xf_ref, xfp_ref, xfn_ref, xb_ref, xbp_ref, xbn_ref,
                cw_ref, cb_ref, wg_ref, bg_ref, lam_ref,
                hf_ref, hb_ref,
                xc_s, a_s, b_s, carry_s):
    j = pl.program_id(1)
    n_steps = pl.num_programs(1)
    nb, bw = N_RNN_BLOCKS, RNN_BLOCK_W
    n_in = CONV_T + CONV_W - 1
    right = CONV_W - 1 - CONV_LEFT

    @pl.when(j == 0)
    def _():
        carry_s[...] = jnp.zeros_like(carry_s)

    def coeffs(direction, x_ref, xp_ref, xn_ref, has_prev, has_next):
        def conv_piece(v, base):
            v = v.reshape(n_in, nb, bw)
            acc = cb_ref[...] + v[0:CONV_T] * cw_ref[0]
            for k in range(1, CONV_W):
                acc = acc + v[k:k + CONV_T] * cw_ref[k]
            xc_s[pl.ds(base, CONV_T * nb), :] = acc.reshape(CONV_T * nb, bw)

        halo_rows = xp_ref.shape[0]
        left = jnp.where(has_prev, xp_ref[halo_rows - CONV_LEFT * nb:halo_rows, :], 0.0)
        conv_piece(jnp.concatenate([left, x_ref[0:(n_in - CONV_LEFT) * nb, :]], axis=0), 0)
        tail = jnp.where(has_next, xn_ref[0:right * nb, :], 0.0)
        last = CHUNK - CONV_T
        conv_piece(jnp.concatenate(
            [x_ref[(last - CONV_LEFT) * nb:CHUNK * nb, :], tail], axis=0), last * nb)

        def conv_body(c, carry):
            base = pl.multiple_of(c * (CONV_T * nb), CONV_T * nb)
            conv_piece(x_ref[pl.ds(base - CONV_LEFT * nb, n_in * nb), :], base)
            return carry

        lax.fori_loop(1, CHUNK // CONV_T - 1, conv_body, 0)

        for n in range(nb):
            rows = pl.ds(n, CHUNK, stride=nb)
            lam = lam_ref[direction, n]
            c_half = (0.5 * LRU_C * LOG2_E) * (
                jnp.minimum(lam, 0.0) - jnp.log1p(jnp.exp(-jnp.abs(lam))))
            xh = xc_s[rows, :]
            z = _dot(xh.astype(BF16), wg_ref[direction, n]) + bg_ref[direction, n]
            a = jnp.exp2(c_half * jnp.tanh(z[:, 0:bw]) + c_half)
            gated = xh * jnp.tanh(z[:, bw:2 * bw]) + xh
            om = 1.0 - a * a
            a_s[direction, rows, :] = a
            b_s[direction, rows, :] = (om * lax.rsqrt(jnp.maximum(om, F32_TINY))) * gated

    coeffs(0, xf_ref, xfp_ref, xfn_ref, j >= 2, jnp.logical_and(j >= 1, j < n_steps - 1))
    coeffs(1, xb_ref, xbp_ref, xbn_ref, jnp.logical_and(j >= 1, j < n_steps - 1), j >= 2)

    def step(t, carry):
        hf, hb = carry
        tf = pl.multiple_of(t * nb, nb)
        tb = pl.multiple_of((CHUNK - 1 - t) * nb, nb)
        hf = a_s[0, pl.ds(tf, nb), :] * hf + b_s[0, pl.ds(tf, nb), :]
        hb = a_s[1, pl.ds(tb, nb), :] * hb + b_s[1, pl.ds(tb, nb), :]
        hf_ref[pl.ds(tf, nb), :] = hf
        hb_ref[pl.ds(tb, nb), :] = hb
        return hf, hb

    hf, hb = lax.fori_loop(0, CHUNK, step, (carry_s[0], carry_s[1]), unroll=8)
    carry_s[0] = hf
    carry_s[1] = hb


def _rnn_branch(xr, conv_w, conv_b, w_gate, b_gate, lam, geo):
    nb, bw = N_RNN_BLOCKS, RNN_BLOCK_W
    bsz, n_lat = geo["batch"], geo["lat_chunks"]
    n_steps = n_lat + 1
    halo = SUBLANES
    n_halo = xr.shape[0] // (halo * nb)
    per = CHUNK // halo

    def fwd_blk(b, j):
        return jnp.where(j == 0, bsz * n_lat + b, b * n_lat + j - 1)

    def bwd_blk(b, j):
        return jnp.where(j == 0, bsz * n_lat + b, b * n_lat + n_lat - j)

    def chunk(blk):
        return pl.BlockSpec((CHUNK * nb, bw), lambda b, j: (blk(b, j), 0))

    def prev(blk):
        return pl.BlockSpec(
            (halo * nb, bw), lambda b, j: (jnp.maximum(blk(b, j) * per - 1, 0), 0))

    def nxt(blk):
        return pl.BlockSpec(
            (halo * nb, bw), lambda b, j: (jnp.minimum((blk(b, j) + 1) * per, n_halo - 1), 0))

    return pl.pallas_call(
        _rnn_kernel,
        grid=(bsz, n_steps),
        in_specs=[
            chunk(fwd_blk), prev(fwd_blk), nxt(fwd_blk),
            chunk(bwd_blk), prev(bwd_blk), nxt(bwd_blk),
            _resident((CONV_W, nb, bw)),
            _resident((nb, bw)),
            _resident(w_gate.shape),
            _resident(b_gate.shape),
            _resident(lam.shape),
        ],
        out_specs=[chunk(fwd_blk), chunk(bwd_blk)],
        out_shape=[jax.ShapeDtypeStruct(xr.shape, F32)] * 2,
        scratch_shapes=[
            pltpu.VMEM((CHUNK * nb, bw), F32),
            pltpu.VMEM((2, CHUNK * nb, bw), F32),
            pltpu.VMEM((2, CHUNK * nb, bw), F32),
            pltpu.VMEM((2, nb, bw), F32),
        ],
        compiler_params=pltpu.CompilerParams(
            dimension_semantics=("arbitrary", "arbitrary"),
            vmem_limit_bytes=VMEM_LIMIT_BYTES),
        name="rglru",
    )(xr, xr, xr, xr, xr, xr, conv_w.reshape(CONV_W, nb, bw), conv_b.reshape(nb, bw),
      w_gate, b_gate, lam)


def _stack_heads(q, g):
    base = g * Q_PER_KV * HEAD_DIM
    return jnp.concatenate(
        [q[:, base + h * HEAD_DIM: base + (h + 1) * HEAD_DIM] for h in range(Q_PER_KV)], axis=0)


def _nt_dot(a, b):
    return lax.dot_general(a, b, (((1,), (1,)), ((), ())), preferred_element_type=F32)


def _attend_group(sink_ref, g, q, n_q, key_blocks, vt, s_s, p_s, o_store):
    width = Q_PER_KV * n_q
    folds = KEY_BLOCK // SUBLANES
    qs = _stack_heads(q, g)
    sink = jnp.concatenate(
        [jnp.full((1, n_q), sink_ref[g * Q_PER_KV + h] * LOG2_E, F32) for h in range(Q_PER_KV)],
        axis=1)
    m8 = jnp.broadcast_to(sink, (SUBLANES, width))
    for j, (k, bias) in enumerate(key_blocks):
        s = _nt_dot(k, qs)
        if bias is not None:
            s = s + jnp.concatenate([bias] * Q_PER_KV, axis=1)
        s_s[j * KEY_BLOCK:(j + 1) * KEY_BLOCK, :] = s
        m8 = jnp.maximum(m8, jnp.max(s.reshape(folds, SUBLANES, width), axis=0))
    m = jnp.max(m8, axis=0, keepdims=True)
    l8 = jnp.zeros((SUBLANES, width), F32)
    for j in range(len(key_blocks)):
        p = jnp.exp2(s_s[j * KEY_BLOCK:(j + 1) * KEY_BLOCK, :] - m)
        l8 = l8 + jnp.sum(p.reshape(folds, SUBLANES, width), axis=0)
        p_s[j * KEY_BLOCK:(j + 1) * KEY_BLOCK, :] = p.astype(BF16)
    denom = jnp.sum(l8, axis=0, keepdims=True) + jnp.exp2(sink - m)
    ot = _dot(vt, p_s[0:len(key_blocks) * KEY_BLOCK, :]) * (1.0 / denom)
    for h in range(Q_PER_KV):
        o_store(h, ot[:, h * n_q:(h + 1) * n_q].T)


def _lat_attn_kernel(sink_ref, q_ref, k_ref, vt_ref, kc_ref, vtc_ref, o_ref, s_s, p_s,
                     *, seq_len):
    n_blk = q_ref.shape[0] // Q_BLOCK
    band_blocks = BAND // KEY_BLOCK
    ctx_blocks = vtc_ref.shape[0]
    last_start = seq_len // KEY_BLOCK - band_blocks
    rel = (lax.broadcasted_iota(jnp.int32, (KEY_BLOCK, Q_BLOCK), 0)
           - lax.broadcasted_iota(jnp.int32, (KEY_BLOCK, Q_BLOCK), 1))

    def one_block(i, slot):
        qb = pl.program_id(1) * n_blk + i
        jb0 = jnp.clip(qb - WINDOW // KEY_BLOCK, 0, last_start)
        biases = [jnp.where(jnp.abs(rel + (jb0 + t - qb) * KEY_BLOCK) <= WINDOW, 0.0, NEG_INF)
                  for t in range(band_blocks)]
        rows = pl.ds(pl.multiple_of(i * Q_BLOCK, Q_BLOCK), Q_BLOCK)
        q = q_ref[rows, :]
        for g in range(N_KV_HEADS):
            gs = slice(g * HEAD_DIM, (g + 1) * HEAD_DIM)
            key_blocks = [
                (k_ref[pl.ds(pl.multiple_of((jb0 + t) * KEY_BLOCK, KEY_BLOCK), KEY_BLOCK), gs],
                 biases[t]) for t in range(band_blocks)]
            key_blocks += [(kc_ref[t * KEY_BLOCK:(t + 1) * KEY_BLOCK, gs], None)
                           for t in range(ctx_blocks)]
            vt = jnp.concatenate([vt_ref[jb0 + t, gs, :] for t in range(band_blocks)]
                                 + [vtc_ref[t, gs, :] for t in range(ctx_blocks)], axis=1)

            def o_store(h, tile, g=g):
                c0 = (g * Q_PER_KV + h) * HEAD_DIM
                o_ref[rows, c0:c0 + HEAD_DIM] = tile.astype(BF16)

            _attend_group(sink_ref, g, q, Q_BLOCK, key_blocks, vt,
                          s_s.at[slot, g], p_s.at[slot, g], o_store)

    def body(it, carry):
        for slot in range(Q_SLOTS):
            one_block(it * Q_SLOTS + slot, slot)
        return carry

    lax.fori_loop(0, n_blk // Q_SLOTS, body, 0)


def _ctx_attn_kernel(sink_ref, q_ref, kc_ref, vtc_ref, o_ref, s_s, p_s):
    n_ctx = q_ref.shape[0]
    ctx_blocks = vtc_ref.shape[0]
    q = q_ref[...]
    for g in range(N_KV_HEADS):
        gs = slice(g * HEAD_DIM, (g + 1) * HEAD_DIM)
        key_blocks = [(kc_ref[t * KEY_BLOCK:(t + 1) * KEY_BLOCK, gs], None)
                      for t in range(ctx_blocks)]
        vt = jnp.concatenate([vtc_ref[t, gs, :] for t in range(ctx_blocks)], axis=1)

        def o_store(h, tile, g=g):
            c0 = (g * Q_PER_KV + h) * HEAD_DIM
            o_ref[:, c0:c0 + HEAD_DIM] = tile.astype(BF16)

        _attend_group(sink_ref, g, q, n_ctx, key_blocks, vt, s_s.at[g], p_s.at[g], o_store)


def _attention(q, k, vt, sink, geo, with_ctx_queries):
    nt, d = q.shape
    bsz, seq_len, n_ctx = geo["batch"], geo["seq"], geo["ctx"]
    q_sup = next(s for s in (1024, 512, 256, Q_BLOCK) if seq_len % s == 0)
    n_sup = seq_len // q_sup
    ctx_blk0 = bsz * seq_len // n_ctx
    smem = pl.BlockSpec(memory_space=pltpu.SMEM)
    ctx_keys = pl.BlockSpec((n_ctx, KV_WIDTH), lambda b, *_: (ctx_blk0 + b, 0))
    ctx_vals = pl.BlockSpec((n_ctx // KEY_BLOCK, KV_WIDTH, KEY_BLOCK),
                            lambda b, *_: (ctx_blk0 + b, 0, 0))
    o = pl.pallas_call(
        functools.partial(_lat_attn_kernel, seq_len=seq_len),
        grid=(bsz, n_sup),
        in_specs=[
            smem,
            pl.BlockSpec((q_sup, d), lambda b, i: (b * n_sup + i, 0)),
            pl.BlockSpec((seq_len, KV_WIDTH), lambda b, i: (b, 0)),
            pl.BlockSpec((seq_len // KEY_BLOCK, KV_WIDTH, KEY_BLOCK), lambda b, i: (b, 0, 0)),
            ctx_keys, ctx_vals,
        ],
        out_specs=pl.BlockSpec((q_sup, d), lambda b, i: (b * n_sup + i, 0)),
        out_shape=jax.ShapeDtypeStruct((bsz * seq_len, d), BF16),
        scratch_shapes=[
            pltpu.VMEM((Q_SLOTS, N_KV_HEADS, BAND + n_ctx, Q_PER_KV * Q_BLOCK), F32),
            pltpu.VMEM((Q_SLOTS, N_KV_HEADS, BAND + n_ctx, Q_PER_KV * Q_BLOCK), BF16),
        ],
        compiler_params=pltpu.CompilerParams(
            dimension_semantics=("arbitrary", "arbitrary"),
            vmem_limit_bytes=VMEM_LIMIT_BYTES),
        name="lat_attention",
    )(sink, q, k, vt, k, vt)
    if not with_ctx_queries:
        return (o, o, 0)
    o_ctx = pl.pallas_call(
        _ctx_attn_kernel,
        grid=(bsz,),
        in_specs=[
            smem,
            pl.BlockSpec((n_ctx, d), lambda b: (ctx_blk0 + b, 0)),
            ctx_keys, ctx_vals,
        ],
        out_specs=pl.BlockSpec((n_ctx, d), lambda b: (b, 0)),
        out_shape=jax.ShapeDtypeStruct((bsz * n_ctx, d), BF16),
        scratch_shapes=[
            pltpu.VMEM((N_KV_HEADS, n_ctx, Q_PER_KV * n_ctx), F32),
            pltpu.VMEM((N_KV_HEADS, n_ctx, Q_PER_KV * n_ctx), BF16),
        ],
        compiler_params=pltpu.CompilerParams(
            dimension_semantics=("arbitrary",),
            vmem_limit_bytes=VMEM_LIMIT_BYTES),
        name="ctx_attention",
    )(sink, q, k, vt)
    return (o, o_ctx, 0)


def _merge_kernel(xl_ref, xc_ref, hf_ref, hb_ref, gr_ref, ol_ref, oc_ref, gl_ref, mod_ref,
                  g_ref, wr_ref, wa_ref, wo_ref, out_ref, *, n_lat_tiles):
    d = D_MODEL
    nb = N_RNN_BLOCKS
    rows_per_part = xl_ref.shape[0] // MERGE_PARTS
    is_lat = pl.program_id(0) < n_lat_tiles
    c1 = math.sqrt(2.0 / math.pi)
    c2 = 0.044715 * c1
    parts = [slice(p * rows_per_part, (p + 1) * rows_per_part) for p in range(MERGE_PARTS)]

    def rnn_input(p):
        h = jnp.concatenate(
            [hf_ref[pl.ds(p * rows_per_part * nb + n, rows_per_part, stride=nb), :]
             + hb_ref[pl.ds(p * rows_per_part * nb + n, rows_per_part, stride=nb), :]
             for n in range(nb)], axis=-1)
        g = gr_ref[parts[p], :].astype(F32)
        hg = h * g
        return (hg * jnp.tanh(g * (c1 + c2 * (g * g))) + hg).astype(BF16)

    def gated_mix(p, y):
        r = parts[p]
        ya = _dot(y, wr_ref[...])
        yb = _dot(jnp.where(is_lat, ol_ref[r, :], oc_ref[r, :]), wa_ref[...])
        ta = jnp.tanh(gl_ref[r, 0:d].astype(F32))
        tb = jnp.tanh(gl_ref[r, d:2 * d].astype(F32))
        return ((ya * ta + ya) + (yb * tb + yb)).astype(BF16)

    def project(p, mix):
        r = parts[p]
        m = _dot(mix, wo_ref[...])
        x = jnp.where(is_lat, xl_ref[r, :], xc_ref[r, :])
        out_ref[r, :] = x + mod_ref[2:3, :] * (_rms(m) * g_ref[...])

    ys = [rnn_input(p) for p in range(MERGE_PARTS)]
    mixes = [gated_mix(p, ys[p]) for p in range(MERGE_PARTS)]
    for p in range(MERGE_PARTS):
        project(p, mixes[p])


def _merge(x_stream, hf, hb, gr, o_stream, gl, mod_l, g_post, w_o_rnn, w_o_attn, w_out,
           layer, geo, n_tiles):
    d = D_MODEL
    tm = geo["tm"]
    n_lat_tiles = geo["n_lat_tiles"]
    row = lambda i: (i, 0)
    tile = pl.BlockSpec((tm, d), row)
    slab = pl.BlockSpec((tm * N_RNN_BLOCKS, RNN_BLOCK_W), row)
    return pl.pallas_call(
        functools.partial(_merge_kernel, n_lat_tiles=n_lat_tiles),
        grid=(n_tiles,),
        in_specs=[
            *_stream_specs(x_stream, tm, d, n_lat_tiles), slab, slab, tile,
            *_stream_specs(o_stream, tm, d, n_lat_tiles),
            pl.BlockSpec((tm, 2 * d), row),
            pl.BlockSpec((None, MOD_CHUNKS, d), lambda i: (geo["mod_row"](i), 0, 0)),
            _resident((1, d)),
            _resident_layer(w_o_rnn, layer), _resident_layer(w_o_attn, layer),
            _resident_layer(w_out, layer),
        ],
        out_specs=tile,
        out_shape=jax.ShapeDtypeStruct((n_tiles * tm, d), F32),
        compiler_params=pltpu.CompilerParams(
            dimension_semantics=("arbitrary",),
            vmem_limit_bytes=VMEM_LIMIT_BYTES),
        name="merge",
    )(x_stream[0], x_stream[1], hf, hb, gr, o_stream[0], o_stream[1], gl, mod_l,
      g_post.reshape(1, d), w_o_rnn, w_o_attn, w_out)


def _ffn_kernel(x_ref, mod_ref, gpre_ref, gpost_ref, w1_ref, w2_ref, out_ref):
    x = x_ref[...]
    h = (_rms(x) * gpre_ref[...]) * (1.0 + mod_ref[4:5, :]) + mod_ref[3:4, :]
    h = h.astype(BF16)
    gate = _dot(h, w1_ref[:, 0:D_FF])
    up = _dot(h, w1_ref[:, D_FF:2 * D_FF])
    act = (gate * _sigmoid(gate) * up).astype(BF16)
    f = _dot(act, w2_ref[...])
    out_ref[...] = x + mod_ref[5:6, :] * (_rms(f) * gpost_ref[...])


def _ffn(x_all, mod_l, g_pre, g_post, w1, w2, layer, geo, n_tiles):
    d = x_all.shape[1]
    tm = geo["tm"]
    tile = pl.BlockSpec((tm, d), lambda i: (i, 0))
    return pl.pallas_call(
        _ffn_kernel,
        grid=(n_tiles,),
        in_specs=[
            tile,
            pl.BlockSpec((None, MOD_CHUNKS, d), lambda i: (geo["mod_row"](i), 0, 0)),
            _resident((1, d)), _resident((1, d)),
            _resident_layer(w1, layer), _resident_layer(w2, layer),
        ],
        out_specs=tile,
        out_shape=jax.ShapeDtypeStruct((n_tiles * tm, d), F32),
        compiler_params=pltpu.CompilerParams(
            dimension_semantics=("arbitrary",),
            vmem_limit_bytes=VMEM_LIMIT_BYTES),
        name="ffn",
    )(x_all, mod_l, g_pre.reshape(1, d), g_post.reshape(1, d), w1, w2)


def _rope_tables(seq_len, pad_rows):
    n_rows = seq_len // GRID_W
    inv = ROPE_BASE ** (-jnp.arange(N_FREQ, dtype=F32) / N_FREQ)
    ang = jnp.arange(max(n_rows, GRID_W), dtype=F32)[:, None] * inv[None, :]
    cos_u, sin_u = jnp.cos(ang), jnp.sin(ang)
    by_row = lambda t: jnp.repeat(t[:n_rows], GRID_W, axis=0)
    by_col = lambda t: jnp.tile(t[:GRID_W], (n_rows, 1))
    cos = jnp.concatenate([by_row(cos_u)] * 2 + [by_col(cos_u)] * 2, axis=-1)
    sin = jnp.concatenate(
        [-by_row(sin_u), by_row(sin_u), -by_col(sin_u), by_col(sin_u)], axis=-1)
    cos = jnp.concatenate([cos, jnp.ones((pad_rows, HEAD_DIM), F32)], axis=0)
    sin = jnp.concatenate([sin, jnp.zeros((pad_rows, HEAD_DIM), F32)], axis=0)
    scale = HEAD_DIM ** -0.5 * LOG2_E
    return cos * scale, sin * scale, cos, sin


def kernel(x, c, ctx, c_ctx, w_mod, b_mod, g_mix_pre, g_mix_post, g_ffn_pre, g_ffn_post, w_in, conv_w, conv_b, lru_wa, lru_ba, lru_wx, lru_bx, lru_lam, attn_sink, w_o_rnn, w_o_attn, w_out, w_ffn_in, w_ffn_out):
    bsz, seq_len, d = x.shape
    n_ctx = ctx.shape[1]
    depth = w_mod.shape[0]
    assert d == D_MODEL and n_ctx == CHUNK and seq_len % CHUNK == 0 and seq_len >= BAND
    assert bsz + 1 <= SUBLANES and seq_len % GRID_W == 0

    n_lat_rows, n_ctx_rows = bsz * seq_len, bsz * n_ctx
    tm = 512 if (n_ctx_rows % 512 == 0 and seq_len % 512 == 0) else CHUNK
    n_lat_tiles, n_ctx_tiles = n_lat_rows // tm, n_ctx_rows // tm
    tiles_per_batch = seq_len // tm
    geo = {
        "batch": bsz, "seq": seq_len, "ctx": n_ctx, "tm": tm,
        "lat_chunks": seq_len // CHUNK,
        "n_lat_tiles": n_lat_tiles, "n_ctx_tiles": n_ctx_tiles,
        "mod_row": lambda i: jnp.where(i < n_lat_tiles, i // tiles_per_batch, bsz),
        "table_block": lambda i: jnp.where(
            i < n_lat_tiles, i % tiles_per_batch, tiles_per_batch + i - n_lat_tiles),
    }

    cs = jnp.concatenate(
        [c, c_ctx[None, :], jnp.zeros((SUBLANES - bsz - 1, d), F32)], axis=0)
    mod = _modulation(cs, w_mod, b_mod).reshape(depth, SUBLANES, MOD_CHUNKS, d)

    tables = _rope_tables(seq_len, n_ctx_rows)

    w_gate = jnp.concatenate([lru_wa, lru_wx], axis=-1).astype(BF16)
    b_gate = 0.5 * jnp.concatenate(
        [lru_ba.reshape(depth, 2, N_RNN_BLOCKS, 1, RNN_BLOCK_W),
         lru_bx.reshape(depth, 2, N_RNN_BLOCKS, 1, RNN_BLOCK_W)], axis=-1)
    lam = lru_lam.reshape(depth, 2, N_RNN_BLOCKS, 1, RNN_BLOCK_W)
    conv_w_half, conv_b_half = 0.5 * conv_w, 0.5 * conv_b

    stream = (x.reshape(n_lat_rows, d), ctx.reshape(n_ctx_rows, d), 0)
    ones = lambda n: jnp.ones((1, n), F32)
    halves = lambda n: jnp.full((1, n), 0.5, F32)
    gl_cols = 2 * d
    w_in_b = _to_bf16(w_in, 256, jnp.concatenate(
        [ones(w_in.shape[-1] - gl_cols), halves(gl_cols)], axis=1))
    w_o_rnn_b = _to_bf16(w_o_rnn, 512, halves(d))
    w_o_attn_b = _to_bf16(w_o_attn, 512, ones(d))
    w_out_b = _to_bf16(w_out, 512, halves(d))
    w_ffn_in_b = _to_bf16(w_ffn_in, 256, ones(2 * D_FF))
    w_ffn_out_b = _to_bf16(w_ffn_out, D_FF // 4, ones(d))
    for l in range(depth):
        need_ctx = l < depth - 1
        n_out_tiles = n_lat_tiles + n_ctx_tiles if need_ctx else n_lat_tiles
        xr, gr, q, k, vt, gl = _in_proj(stream, mod[l], g_mix_pre[l], w_in_b, l, tables, geo)
        hf, hb = _rnn_branch(xr, conv_w_half[l], conv_b_half[l], w_gate[l], b_gate[l], lam[l], geo)
        o_stream = _attention(q, k, vt, attn_sink[l], geo, need_ctx)
        x_all = _merge(stream, hf, hb, gr, o_stream, gl, mod[l], g_mix_post[l],
                       w_o_rnn_b, w_o_attn_b, w_out_b, l, geo, n_out_tiles)
        x_all = _ffn(x_all, mod[l], g_ffn_pre[l], g_ffn_post[l],
                     w_ffn_in_b, w_ffn_out_b, l, geo, n_out_tiles)
        stream = (x_all, x_all, n_lat_tiles)
    return x_all[:n_lat_rows].reshape(bsz, seq_len, d)
```

```python
import functools
import math

import jax
import jax.numpy as jnp
from jax import lax
from jax.experimental import pallas as pl
from jax.experimental.pallas import tpu as pltpu

D_MODEL = 1024
HEAD_DIM = 128
N_Q_HEADS = 8
N_KV_HEADS = 2
Q_PER_KV = N_Q_HEADS // N_KV_HEADS
KV_WIDTH = N_KV_HEADS * HEAD_DIM
WINDOW = 128
GRID_W = 64
N_FREQ = HEAD_DIM // 4
ROPE_BASE = 10000.0
N_RNN_BLOCKS = 8
RNN_BLOCK_W = D_MODEL // N_RNN_BLOCKS
LRU_C = 8.0
CONV_W = 4
CONV_LEFT = 2
D_FF = 2816
EPS = 1e-6
NEG_INF = -1e30
MOD_CHUNKS = 6

LANES = 128
SUBLANES = 8
VMEM_LIMIT_BYTES = 56 * 1024 * 1024

CHUNK = 256
CONV_T = 16
SCAN_UNROLL = 8
MERGE_PARTS = 1
Q_BLOCK = 128
Q_SLOTS = 2
KEY_BLOCK = 128
BAND = Q_BLOCK + 2 * WINDOW
LOG2_E = math.log2(math.e)

BF16 = jnp.bfloat16
F32 = jnp.float32
F32_TINY = float(jnp.finfo(jnp.float32).tiny)


def _dot(a, b):
    return jnp.dot(a, b, preferred_element_type=F32)


def _sigmoid(x):
    return 0.5 * jnp.tanh(0.5 * x) + 0.5


def _gelu_tanh(x):
    c = math.sqrt(2.0 / math.pi)
    return 0.5 * x * (1.0 + jnp.tanh(c * (x + 0.044715 * (x * x * x))))


def _rms(x):
    return x * lax.rsqrt(jnp.mean(x * x, axis=-1, keepdims=True) + EPS)


def _resident(shape):
    nd = len(shape)
    return pl.BlockSpec(shape, lambda *_: (0,) * nd, pipeline_mode=pl.Buffered(1))


def _resident_layer(stacked, layer):
    tail = stacked.shape[1:]
    return pl.BlockSpec((None,) + tail, lambda *_: (layer,) + (0,) * len(tail),
                        pipeline_mode=pl.Buffered(1))


def _cast_kernel(w_ref, scale_ref, o_ref):
    o_ref[...] = (w_ref[...] * scale_ref[...]).astype(BF16)


def _to_bf16(w, row_block, col_scale):
    depth, rows, cols = w.shape
    spec = pl.BlockSpec((None, row_block, cols), lambda l, r: (l, r, 0))
    return pl.pallas_call(
        _cast_kernel,
        grid=(depth, rows // row_block),
        in_specs=[spec, pl.BlockSpec((1, cols), lambda l, r: (0, 0))],
        out_specs=spec,
        out_shape=jax.ShapeDtypeStruct(w.shape, BF16),
        compiler_params=pltpu.CompilerParams(
            dimension_semantics=("arbitrary", "arbitrary"),
            vmem_limit_bytes=VMEM_LIMIT_BYTES),
        name="cast_bf16",
    )(w, col_scale)


def _mod_kernel(c_ref, w_ref, b_ref, o_ref):
    c = c_ref[...]
    s = (c * _sigmoid(c)).astype(BF16)
    o_ref[...] = _dot(s, w_ref[...].astype(BF16)) + b_ref[...]


def _modulation(cs, w_mod, b_mod):
    n_layers, d, width = w_mod.shape
    tn = 1536
    return pl.pallas_call(
        _mod_kernel,
        grid=(n_layers, width // tn),
        in_specs=[
            pl.BlockSpec((SUBLANES, d), lambda l, j: (0, 0)),
            pl.BlockSpec((None, d, tn), lambda l, j: (l, 0, j)),
            pl.BlockSpec((None, 1, tn), lambda l, j: (l, 0, j)),
        ],
        out_specs=pl.BlockSpec((None, SUBLANES, tn), lambda l, j: (l, 0, j)),
        out_shape=jax.ShapeDtypeStruct((n_layers, SUBLANES, width), F32),
        compiler_params=pltpu.CompilerParams(
            dimension_semantics=("arbitrary", "arbitrary"),
            vmem_limit_bytes=VMEM_LIMIT_BYTES),
        name="modulation",
    )(cs, w_mod, b_mod.reshape(n_layers, 1, width))


def _rope(t, cos_ref, sin_ref, n_heads):
    cos = cos_ref[...]
    sin = sin_ref[...]
    lane = lax.broadcasted_iota(jnp.int32, cos.shape, 1)
    first = (lane & (2 * N_FREQ - 1)) < N_FREQ
    outs = []
    for h in range(n_heads):
        th = t[:, h * HEAD_DIM:(h + 1) * HEAD_DIM]
        partner = jnp.where(first,
                            pltpu.roll(th, HEAD_DIM - N_FREQ, axis=1),
                            pltpu.roll(th, N_FREQ, axis=1))
        outs.append(th * cos + partner * sin)
    return jnp.concatenate(outs, axis=-1)


def _stream_specs(stream, tm, d, n_lat_tiles):
    _, ctx, ctx_off = stream
    last_ctx = ctx.shape[0] // tm - 1
    lat_spec = pl.BlockSpec((tm, d), lambda i: (jnp.minimum(i, n_lat_tiles - 1), 0))
    ctx_spec = pl.BlockSpec(
        (tm, d), lambda i: (jnp.minimum(jnp.maximum(i - n_lat_tiles, 0) + ctx_off, last_ctx), 0))
    return lat_spec, ctx_spec


def _stream_tile(lat_ref, ctx_ref, n_lat_tiles):
    return jnp.where(pl.program_id(0) < n_lat_tiles, lat_ref[...], ctx_ref[...])


def _in_kernel(xl_ref, xc_ref, mod_ref, g_ref, w_ref, cq_ref, sq_ref, ck_ref, sk_ref,
               xr_ref, gr_ref, q_ref, k_ref, vt_ref, gl_ref, *, n_lat_tiles):
    d = D_MODEL
    x = _stream_tile(xl_ref, xc_ref, n_lat_tiles)
    h = (_rms(x) * g_ref[...]) * (1.0 + mod_ref[1:2, :]) + mod_ref[0:1, :]
    h = h.astype(BF16)
    xr = _dot(h, w_ref[:, 0:d])
    for n in range(N_RNN_BLOCKS):
        xr_ref[pl.ds(n, x.shape[0], stride=N_RNN_BLOCKS), :] = (
            xr[:, n * RNN_BLOCK_W:(n + 1) * RNN_BLOCK_W])
    gr_ref[...] = _dot(h, w_ref[:, d:2 * d]).astype(BF16)
    q = _dot(h, w_ref[:, 2 * d:3 * d])
    q_ref[...] = _rope(q, cq_ref, sq_ref, N_Q_HEADS).astype(BF16)
    k = _dot(h, w_ref[:, 3 * d:3 * d + KV_WIDTH])
    k_ref[...] = _rope(k, ck_ref, sk_ref, N_KV_HEADS).astype(BF16)
    v = _dot(h, w_ref[:, 3 * d + KV_WIDTH:3 * d + 2 * KV_WIDTH])
    for t in range(vt_ref.shape[0]):
        vt_ref[t] = v[t * KEY_BLOCK:(t + 1) * KEY_BLOCK, :].T.astype(BF16)
    gl_ref[...] = _dot(h, w_ref[:, 3 * d + 2 * KV_WIDTH:5 * d + 2 * KV_WIDTH]).astype(BF16)


def _in_proj(stream, mod_l, g_pre, w_in, layer, tables, geo):
    d = D_MODEL
    tm = geo["tm"]
    n_tiles = geo["n_lat_tiles"] + geo["n_ctx_tiles"]
    nt = n_tiles * tm
    row = lambda i: (i, 0)
    tab = pl.BlockSpec((tm, LANES), lambda i: (geo["table_block"](i), 0))
    return pl.pallas_call(
        functools.partial(_in_kernel, n_lat_tiles=geo["n_lat_tiles"]),
        grid=(n_tiles,),
        in_specs=[
            *_stream_specs(stream, tm, d, geo["n_lat_tiles"]),
            pl.BlockSpec((None, MOD_CHUNKS, d), lambda i: (geo["mod_row"](i), 0, 0)),
            _resident((1, d)),
            _resident_layer(w_in, layer),
            tab, tab, tab, tab,
        ],
        out_specs=[
            pl.BlockSpec((tm * N_RNN_BLOCKS, RNN_BLOCK_W), row),
            pl.BlockSpec((tm, d), row),
            pl.BlockSpec((tm, d), row),
            pl.BlockSpec((tm, KV_WIDTH), row),
            pl.BlockSpec((tm // KEY_BLOCK, KV_WIDTH, KEY_BLOCK), lambda i: (i, 0, 0)),
            pl.BlockSpec((tm, 2 * d), row),
        ],
        out_shape=[
            jax.ShapeDtypeStruct((nt * N_RNN_BLOCKS, RNN_BLOCK_W), F32),
            jax.ShapeDtypeStruct((nt, d), BF16),
            jax.ShapeDtypeStruct((nt, d), BF16),
            jax.ShapeDtypeStruct((nt, KV_WIDTH), BF16),
            jax.ShapeDtypeStruct((nt // KEY_BLOCK, KV_WIDTH, KEY_BLOCK), BF16),
            jax.ShapeDtypeStruct((nt, 2 * d), BF16),
        ],
        compiler_params=pltpu.CompilerParams(
            dimension_semantics=("arbitrary",),
            vmem_limit_bytes=VMEM_LIMIT_BYTES),
        name="in_proj",
    )(stream[0], stream[1], mod_l, g_pre.reshape(1, d), w_in, *tables)


def _rnn_kernel(xf_ref, xfp_ref, xfn_ref, xb_ref, xbp_ref, xbn_ref,
                cw_ref, cb_ref, wg_ref, bg_ref, lam_ref,
                hf_ref, hb_ref,
                xc_s, a_s, b_s, carry_s):
    j = pl.program_id(1)
    n_steps = pl.num_programs(1)
    nb, bw = N_RNN_BLOCKS, RNN_BLOCK_W
    n_in = CONV_T + CONV_W - 1
    right = CONV_W - 1 - CONV_LEFT

    @pl.when(j == 0)
    def _():
        carry_s[...] = jnp.zeros_like(carry_s)

    def coeffs(direction, x_ref, xp_ref, xn_ref, has_prev, has_next):
        def conv_piece(v, base):
            v = v.reshape(n_in, nb, bw)
            acc = cb_ref[...] + v[0:CONV_T] * cw_ref[0]
            for k in range(1, CONV_W):
                acc = acc + v[k:k + CONV_T] * cw_ref[k]
            xc_s[pl.ds(base, CONV_T * nb), :] = acc.reshape(CONV_T * nb, bw)

        halo_rows = xp_ref.shape[0]
        left = jnp.where(has_prev, xp_ref[halo_rows - CONV_LEFT * nb:halo_rows, :], 0.0)
        conv_piece(jnp.concatenate([left, x_ref[0:(n_in - CONV_LEFT) * nb, :]], axis=0), 0)
        tail = jnp.where(has_next, xn_ref[0:right * nb, :], 0.0)
        last = CHUNK - CONV_T
        conv_piece(jnp.concatenate(
            [x_ref[(last - CONV_LEFT) * nb:CHUNK * nb, :], tail], axis=0), last * nb)

        def conv_body(c, carry):
            base = pl.multiple_of(c * (CONV_T * nb), CONV_T * nb)
            conv_piece(x_ref[pl.ds(base - CONV_LEFT * nb, n_in * nb), :], base)
            return carry

        lax.fori_loop(1, CHUNK // CONV_T - 1, conv_body, 0)

        for n in range(nb):
            rows = pl.ds(n, CHUNK, stride=nb)
            lam = lam_ref[direction, n]
            c_half = (0.5 * LRU_C * LOG2_E) * (
                jnp.minimum(lam, 0.0) - jnp.log1p(jnp.exp(-jnp.abs(lam))))
            xh = xc_s[rows, :]
            z = _dot(xh.astype(BF16), wg_ref[direction, n]) + bg_ref[direction, n]
            a = jnp.exp2(c_half * jnp.tanh(z[:, 0:bw]) + c_half)
            gated = xh * jnp.tanh(z[:, bw:2 * bw]) + xh
            om = 1.0 - a * a
            a_s[direction, rows, :] = a
            b_s[direction, rows, :] = (om * lax.rsqrt(jnp.maximum(om, F32_TINY))) * gated

    coeffs(0, xf_ref, xfp_ref, xfn_ref, j >= 2, jnp.logical_and(j >= 1, j < n_steps - 1))
    coeffs(1, xb_ref, xbp_ref, xbn_ref, jnp.logical_and(j >= 1, j < n_steps - 1), j >= 2)

    def step(t, carry):
        hf, hb = carry
        tf = pl.multiple_of(t * nb, nb)
        tb = pl.multiple_of((CHUNK - 1 - t) * nb, nb)
        hf = a_s[0, pl.ds(tf, nb), :] * hf + b_s[0, pl.ds(tf, nb), :]
        hb = a_s[1, pl.ds(tb, nb), :] * hb + b_s[1, pl.ds(tb, nb), :]
        hf_ref[pl.ds(tf, nb), :] = hf
        hb_ref[pl.ds(tb, nb), :] = hb
        return hf, hb

    hf, hb = lax.fori_loop(0, CHUNK, step, (carry_s[0], carry_s[1]), unroll=SCAN_UNROLL)
    carry_s[0] = hf
    carry_s[1] = hb


def _rnn_branch(xr, conv_w, conv_b, w_gate, b_gate, lam, geo):
    nb, bw = N_RNN_BLOCKS, RNN_BLOCK_W
    bsz, n_lat = geo["batch"], geo["lat_chunks"]
    n_steps = n_lat + 1
    halo = SUBLANES
    n_halo = xr.shape[0] // (halo * nb)
    per = CHUNK // halo

    def fwd_blk(b, j):
        return jnp.where(j == 0, bsz * n_lat + b, b * n_lat + j - 1)

    def bwd_blk(b, j):
        return jnp.where(j == 0, bsz * n_lat + b, b * n_lat + n_lat - j)

    def chunk(blk):
        return pl.BlockSpec((CHUNK * nb, bw), lambda b, j: (blk(b, j), 0))

    def prev(blk):
        return pl.BlockSpec(
            (halo * nb, bw), lambda b, j: (jnp.maximum(blk(b, j) * per - 1, 0), 0))

    def nxt(blk):
        return pl.BlockSpec(
            (halo * nb, bw), lambda b, j: (jnp.minimum((blk(b, j) + 1) * per, n_halo - 1), 0))

    return pl.pallas_call(
        _rnn_kernel,
        grid=(bsz, n_steps),
        in_specs=[
            chunk(fwd_blk), prev(fwd_blk), nxt(fwd_blk),
            chunk(bwd_blk), prev(bwd_blk), nxt(bwd_blk),
            _resident((CONV_W, nb, bw)),
            _resident((nb, bw)),
            _resident(w_gate.shape),
            _resident(b_gate.shape),
            _resident(lam.shape),
        ],
        out_specs=[chunk(fwd_blk), chunk(bwd_blk)],
        out_shape=[jax.ShapeDtypeStruct(xr.shape, F32)] * 2,
        scratch_shapes=[
            pltpu.VMEM((CHUNK * nb, bw), F32),
            pltpu.VMEM((2, CHUNK * nb, bw), F32),
            pltpu.VMEM((2, CHUNK * nb, bw), F32),
            pltpu.VMEM((2, nb, bw), F32),
        ],
        compiler_params=pltpu.CompilerParams(
            dimension_semantics=("arbitrary", "arbitrary"),
            vmem_limit_bytes=VMEM_LIMIT_BYTES),
        name="rglru",
    )(xr, xr, xr, xr, xr, xr, conv_w.reshape(CONV_W, nb, bw), conv_b.reshape(nb, bw),
      w_gate, b_gate, lam)


def _stack_heads(q, g):
    base = g * Q_PER_KV * HEAD_DIM
    return jnp.concatenate(
        [q[:, base + h * HEAD_DIM: base + (h + 1) * HEAD_DIM] for h in range(Q_PER_KV)], axis=0)


def _nt_dot(a, b):
    return lax.dot_general(a, b, (((1,), (1,)), ((), ())), preferred_element_type=F32)


def _attend_units(sink_ref, units):
    folds = KEY_BLOCK // SUBLANES
    for u in units:
        n_q = u["q"].shape[0]
        width = Q_PER_KV * n_q
        qs = _stack_heads(u["q"], u["g"])
        sink = jnp.concatenate(
            [jnp.full((1, n_q), sink_ref[u["g"] * Q_PER_KV + h] * LOG2_E, F32)
             for h in range(Q_PER_KV)], axis=1)
        m8 = jnp.broadcast_to(sink, (SUBLANES, width))
        for j, (k, bias) in enumerate(u["key_blocks"]):
            s = _nt_dot(k, qs)
            if bias is not None:
                s = s + jnp.concatenate([bias] * Q_PER_KV, axis=1)
            u["s_s"][j * KEY_BLOCK:(j + 1) * KEY_BLOCK, :] = s
            m8 = jnp.maximum(m8, jnp.max(s.reshape(folds, SUBLANES, width), axis=0))
        u["sink"] = sink
        u["m"] = jnp.max(m8, axis=0, keepdims=True)
    for u in units:
        width = Q_PER_KV * u["q"].shape[0]
        l8 = jnp.zeros((SUBLANES, width), F32)
        for j in range(len(u["key_blocks"])):
            p = jnp.exp2(u["s_s"][j * KEY_BLOCK:(j + 1) * KEY_BLOCK, :] - u["m"])
            l8 = l8 + jnp.sum(p.reshape(folds, SUBLANES, width), axis=0)
            u["p_s"][j * KEY_BLOCK:(j + 1) * KEY_BLOCK, :] = p.astype(BF16)
        u["denom"] = jnp.sum(l8, axis=0, keepdims=True) + jnp.exp2(u["sink"] - u["m"])
    for u in units:
        n_q = u["q"].shape[0]
        n_keys = len(u["key_blocks"]) * KEY_BLOCK
        ot = _dot(u["vt"], u["p_s"][0:n_keys, :]) * (1.0 / u["denom"])
        for h in range(Q_PER_KV):
            u["o_store"](h, ot[:, h * n_q:(h + 1) * n_q].T)


def _lat_attn_kernel(sink_ref, q_ref, k_ref, vt_ref, kc_ref, vtc_ref, o_ref, s_s, p_s,
                     *, seq_len):
    n_blk = q_ref.shape[0] // Q_BLOCK
    band_blocks = BAND // KEY_BLOCK
    ctx_blocks = vtc_ref.shape[0]
    last_start = seq_len // KEY_BLOCK - band_blocks
    rel = (lax.broadcasted_iota(jnp.int32, (KEY_BLOCK, Q_BLOCK), 0)
           - lax.broadcasted_iota(jnp.int32, (KEY_BLOCK, Q_BLOCK), 1))

    def block_units(i, slot):
        units = []
        qb = pl.program_id(1) * n_blk + i
        jb0 = jnp.clip(qb - WINDOW // KEY_BLOCK, 0, last_start)
        biases = [jnp.where(jnp.abs(rel + (jb0 + t - qb) * KEY_BLOCK) <= WINDOW, 0.0, NEG_INF)
                  for t in range(band_blocks)]
        rows = pl.ds(pl.multiple_of(i * Q_BLOCK, Q_BLOCK), Q_BLOCK)
        q = q_ref[rows, :]
        for g in range(N_KV_HEADS):
            gs = slice(g * HEAD_DIM, (g + 1) * HEAD_DIM)
            key_blocks = [
                (k_ref[pl.ds(pl.multiple_of((jb0 + t) * KEY_BLOCK, KEY_BLOCK), KEY_BLOCK), gs],
                 biases[t]) for t in range(band_blocks)]
            key_blocks += [(kc_ref[t * KEY_BLOCK:(t + 1) * KEY_BLOCK, gs], None)
                           for t in range(ctx_blocks)]
            vt = jnp.concatenate([vt_ref[jb0 + t, gs, :] for t in range(band_blocks)]
                                 + [vtc_ref[t, gs, :] for t in range(ctx_blocks)], axis=1)

            def o_store(h, tile, g=g, rows=rows):
                c0 = (g * Q_PER_KV + h) * HEAD_DIM
                o_ref[rows, c0:c0 + HEAD_DIM] = tile.astype(BF16)

            units.append(dict(g=g, q=q, key_blocks=key_blocks, vt=vt,
                              s_s=s_s.at[slot, g], p_s=p_s.at[slot, g], o_store=o_store))
        return units

    def body(it, carry):
        units = []
        for slot in range(Q_SLOTS):
            units += block_units(it * Q_SLOTS + slot, slot)
        _attend_units(sink_ref, units)
        return carry

    lax.fori_loop(0, n_blk // Q_SLOTS, body, 0)


def _ctx_attn_kernel(sink_ref, q_ref, kc_ref, vtc_ref, o_ref, s_s, p_s):
    ctx_blocks = vtc_ref.shape[0]
    q = q_ref[...]
    units = []
    for g in range(N_KV_HEADS):
        gs = slice(g * HEAD_DIM, (g + 1) * HEAD_DIM)
        key_blocks = [(kc_ref[t * KEY_BLOCK:(t + 1) * KEY_BLOCK, gs], None)
                      for t in range(ctx_blocks)]
        vt = jnp.concatenate([vtc_ref[t, gs, :] for t in range(ctx_blocks)], axis=1)

        def o_store(h, tile, g=g):
            c0 = (g * Q_PER_KV + h) * HEAD_DIM
            o_ref[:, c0:c0 + HEAD_DIM] = tile.astype(BF16)

        units.append(dict(g=g, q=q, key_blocks=key_blocks, vt=vt,
                          s_s=s_s.at[g], p_s=p_s.at[g], o_store=o_store))
    _attend_units(sink_ref, units)


def _attention(q, k, vt, sink, geo, with_ctx_queries):
    nt, d = q.shape
    bsz, seq_len, n_ctx = geo["batch"], geo["seq"], geo["ctx"]
    q_sup = next(s for s in (1024, 512, 256, Q_BLOCK) if seq_len % s == 0)
    n_sup = seq_len // q_sup
    ctx_blk0 = bsz * seq_len // n_ctx
    smem = pl.BlockSpec(memory_space=pltpu.SMEM)
    ctx_keys = pl.BlockSpec((n_ctx, KV_WIDTH), lambda b, *_: (ctx_blk0 + b, 0))
    ctx_vals = pl.BlockSpec((n_ctx // KEY_BLOCK, KV_WIDTH, KEY_BLOCK),
                            lambda b, *_: (ctx_blk0 + b, 0, 0))
    o = pl.pallas_call(
        functools.partial(_lat_attn_kernel, seq_len=seq_len),
        grid=(bsz, n_sup),
        in_specs=[
            smem,
            pl.BlockSpec((q_sup, d), lambda b, i: (b * n_sup + i, 0)),
            pl.BlockSpec((seq_len, KV_WIDTH), lambda b, i: (b, 0)),
            pl.BlockSpec((seq_len // KEY_BLOCK, KV_WIDTH, KEY_BLOCK), lambda b, i: (b, 0, 0)),
            ctx_keys, ctx_vals,
        ],
        out_specs=pl.BlockSpec((q_sup, d), lambda b, i: (b * n_sup + i, 0)),
        out_shape=jax.ShapeDtypeStruct((bsz * seq_len, d), BF16),
        scratch_shapes=[
            pltpu.VMEM((Q_SLOTS, N_KV_HEADS, BAND + n_ctx, Q_PER_KV * Q_BLOCK), F32),
            pltpu.VMEM((Q_SLOTS, N_KV_HEADS, BAND + n_ctx, Q_PER_KV * Q_BLOCK), BF16),
        ],
        compiler_params=pltpu.CompilerParams(
            dimension_semantics=("arbitrary", "arbitrary"),
            vmem_limit_bytes=VMEM_LIMIT_BYTES),
        name="lat_attention",
    )(sink, q, k, vt, k, vt)
    if not with_ctx_queries:
        return (o, o, 0)
    o_ctx = pl.pallas_call(
        _ctx_attn_kernel,
        grid=(bsz,),
        in_specs=[
            smem,
            pl.BlockSpec((n_ctx, d), lambda b: (ctx_blk0 + b, 0)),
            ctx_keys, ctx_vals,
        ],
        out_specs=pl.BlockSpec((n_ctx, d), lambda b: (b, 0)),
        out_shape=jax.ShapeDtypeStruct((bsz * n_ctx, d), BF16),
        scratch_shapes=[
            pltpu.VMEM((N_KV_HEADS, n_ctx, Q_PER_KV * n_ctx), F32),
            pltpu.VMEM((N_KV_HEADS, n_ctx, Q_PER_KV * n_ctx), BF16),
        ],
        compiler_params=pltpu.CompilerParams(
            dimension_semantics=("arbitrary",),
            vmem_limit_bytes=VMEM_LIMIT_BYTES),
        name="ctx_attention",
    )(sink, q, k, vt)
    return (o, o_ctx, 0)


def _merge_kernel(xl_ref, xc_ref, hf_ref, hb_ref, gr_ref, ol_ref, oc_ref, gl_ref, mod_ref,
                  g_ref, wr_ref, wa_ref, wo_ref, out_ref, *, n_lat_tiles):
    d = D_MODEL
    nb = N_RNN_BLOCKS
    rows_per_part = xl_ref.shape[0] // MERGE_PARTS
    is_lat = pl.program_id(0) < n_lat_tiles
    c1 = math.sqrt(2.0 / math.pi)
    c2 = 0.044715 * c1
    parts = [slice(p * rows_per_part, (p + 1) * rows_per_part) for p in range(MERGE_PARTS)]

    def rnn_input(p):
        h = jnp.concatenate(
            [hf_ref[pl.ds(p * rows_per_part * nb + n, rows_per_part, stride=nb), :]
             + hb_ref[pl.ds(p * rows_per_part * nb + n, rows_per_part, stride=nb), :]
             for n in range(nb)], axis=-1)
        g = gr_ref[parts[p], :].astype(F32)
        hg = h * g
        return (hg * jnp.tanh(g * (c1 + c2 * (g * g))) + hg).astype(BF16)

    def gated_mix(p, y):
        r = parts[p]
        ya = _dot(y, wr_ref[...])
        yb = _dot(jnp.where(is_lat, ol_ref[r, :], oc_ref[r, :]), wa_ref[...])
        ta = jnp.tanh(gl_ref[r, 0:d].astype(F32))
        tb = jnp.tanh(gl_ref[r, d:2 * d].astype(F32))
        return ((ya * ta + ya) + (yb * tb + yb)).astype(BF16)

    def project(p, mix):
        r = parts[p]
        m = _dot(mix, wo_ref[...])
        x = jnp.where(is_lat, xl_ref[r, :], xc_ref[r, :])
        out_ref[r, :] = x + mod_ref[2:3, :] * (_rms(m) * g_ref[...])

    ys = [rnn_input(p) for p in range(MERGE_PARTS)]
    mixes = [gated_mix(p, ys[p]) for p in range(MERGE_PARTS)]
    for p in range(MERGE_PARTS):
        project(p, mixes[p])


def _merge(x_stream, hf, hb, gr, o_stream, gl, mod_l, g_post, w_o_rnn, w_o_attn, w_out,
           layer, geo, n_tiles):
    d = D_MODEL
    tm = geo["tm"]
    n_lat_tiles = geo["n_lat_tiles"]
    row = lambda i: (i, 0)
    tile = pl.BlockSpec((tm, d), row)
    slab = pl.BlockSpec((tm * N_RNN_BLOCKS, RNN_BLOCK_W), row)
    return pl.pallas_call(
        functools.partial(_merge_kernel, n_lat_tiles=n_lat_tiles),
        grid=(n_tiles,),
        in_specs=[
            *_stream_specs(x_stream, tm, d, n_lat_tiles), slab, slab, tile,
            *_stream_specs(o_stream, tm, d, n_lat_tiles),
            pl.BlockSpec((tm, 2 * d), row),
            pl.BlockSpec((None, MOD_CHUNKS, d), lambda i: (geo["mod_row"](i), 0, 0)),
            _resident((1, d)),
            _resident_layer(w_o_rnn, layer), _resident_layer(w_o_attn, layer),
            _resident_layer(w_out, layer),
        ],
        out_specs=tile,
        out_shape=jax.ShapeDtypeStruct((n_tiles * tm, d), F32),
        compiler_params=pltpu.CompilerParams(
            dimension_semantics=("arbitrary",),
            vmem_limit_bytes=VMEM_LIMIT_BYTES),
        name="merge",
    )(x_stream[0], x_stream[1], hf, hb, gr, o_stream[0], o_stream[1], gl, mod_l,
      g_post.reshape(1, d), w_o_rnn, w_o_attn, w_out)


def _ffn_kernel(x_ref, mod_ref, gpre_ref, gpost_ref, w1_ref, w2_ref, out_ref):
    x = x_ref[...]
    h = (_rms(x) * gpre_ref[...]) * (1.0 + mod_ref[4:5, :]) + mod_ref[3:4, :]
    h = h.astype(BF16)
    gate = _dot(h, w1_ref[:, 0:D_FF])
    up = _dot(h, w1_ref[:, D_FF:2 * D_FF])
    act = (gate * _sigmoid(gate) * up).astype(BF16)
    f = _dot(act, w2_ref[...])
    out_ref[...] = x + mod_ref[5:6, :] * (_rms(f) * gpost_ref[...])


def _ffn(x_all, mod_l, g_pre, g_post, w1, w2, layer, geo, n_tiles):
    d = x_all.shape[1]
    tm = geo["tm"]
    tile = pl.BlockSpec((tm, d), lambda i: (i, 0))
    return pl.pallas_call(
        _ffn_kernel,
        grid=(n_tiles,),
        in_specs=[
            tile,
            pl.BlockSpec((None, MOD_CHUNKS, d), lambda i: (geo["mod_row"](i), 0, 0)),
            _resident((1, d)), _resident((1, d)),
            _resident_layer(w1, layer), _resident_layer(w2, layer),
        ],
        out_specs=tile,
        out_shape=jax.ShapeDtypeStruct((n_tiles * tm, d), F32),
        compiler_params=pltpu.CompilerParams(
            dimension_semantics=("arbitrary",),
            vmem_limit_bytes=VMEM_LIMIT_BYTES),
        name="ffn",
    )(x_all, mod_l, g_pre.reshape(1, d), g_post.reshape(1, d), w1, w2)


def _rope_tables(seq_len, pad_rows):
    n_rows = seq_len // GRID_W
    inv = ROPE_BASE ** (-jnp.arange(N_FREQ, dtype=F32) / N_FREQ)
    ang = jnp.arange(max(n_rows, GRID_W), dtype=F32)[:, None] * inv[None, :]
    cos_u, sin_u = jnp.cos(ang), jnp.sin(ang)
    by_row = lambda t: jnp.repeat(t[:n_rows], GRID_W, axis=0)
    by_col = lambda t: jnp.tile(t[:GRID_W], (n_rows, 1))
    cos = jnp.concatenate([by_row(cos_u)] * 2 + [by_col(cos_u)] * 2, axis=-1)
    sin = jnp.concatenate(
        [-by_row(sin_u), by_row(sin_u), -by_col(sin_u), by_col(sin_u)], axis=-1)
    cos = jnp.concatenate([cos, jnp.ones((pad_rows, HEAD_DIM), F32)], axis=0)
    sin = jnp.concatenate([sin, jnp.zeros((pad_rows, HEAD_DIM), F32)], axis=0)
    scale = HEAD_DIM ** -0.5 * LOG2_E
    return cos * scale, sin * scale, cos, sin


def kernel(x, c, ctx, c_ctx, w_mod, b_mod, g_mix_pre, g_mix_post, g_ffn_pre, g_ffn_post, w_in, conv_w, conv_b, lru_wa, lru_ba, lru_wx, lru_bx, lru_lam, attn_sink, w_o_rnn, w_o_attn, w_out, w_ffn_in, w_ffn_out):
    bsz, seq_len, d = x.shape
    n_ctx = ctx.shape[1]
    depth = w_mod.shape[0]
    assert d == D_MODEL and n_ctx == CHUNK and seq_len % CHUNK == 0 and seq_len >= BAND
    assert bsz + 1 <= SUBLANES and seq_len % GRID_W == 0

    n_lat_rows, n_ctx_rows = bsz * seq_len, bsz * n_ctx
    tm = 512 if (n_ctx_rows % 512 == 0 and seq_len % 512 == 0) else CHUNK
    n_lat_tiles, n_ctx_tiles = n_lat_rows // tm, n_ctx_rows // tm
    tiles_per_batch = seq_len // tm
    geo = {
        "batch": bsz, "seq": seq_len, "ctx": n_ctx, "tm": tm,
        "lat_chunks": seq_len // CHUNK,
        "n_lat_tiles": n_lat_tiles, "n_ctx_tiles": n_ctx_tiles,
        "mod_row": lambda i: jnp.where(i < n_lat_tiles, i // tiles_per_batch, bsz),
        "table_block": lambda i: jnp.where(
            i < n_lat_tiles, i % tiles_per_batch, tiles_per_batch + i - n_lat_tiles),
    }

    cs = jnp.concatenate(
        [c, c_ctx[None, :], jnp.zeros((SUBLANES - bsz - 1, d), F32)], axis=0)
    mod = _modulation(cs, w_mod, b_mod).reshape(depth, SUBLANES, MOD_CHUNKS, d)

    tables = _rope_tables(seq_len, n_ctx_rows)

    w_gate = jnp.concatenate([lru_wa, lru_wx], axis=-1).astype(BF16)
    b_gate = 0.5 * jnp.concatenate(
        [lru_ba.reshape(depth, 2, N_RNN_BLOCKS, 1, RNN_BLOCK_W),
         lru_bx.reshape(depth, 2, N_RNN_BLOCKS, 1, RNN_BLOCK_W)], axis=-1)
    lam = lru_lam.reshape(depth, 2, N_RNN_BLOCKS, 1, RNN_BLOCK_W)
    conv_w_half, conv_b_half = 0.5 * conv_w, 0.5 * conv_b

    stream = (x.reshape(n_lat_rows, d), ctx.reshape(n_ctx_rows, d), 0)
    ones = lambda n: jnp.ones((1, n), F32)
    halves = lambda n: jnp.full((1, n), 0.5, F32)
    gl_cols = 2 * d
    w_in_b = _to_bf16(w_in, 256, jnp.concatenate(
        [ones(w_in.shape[-1] - gl_cols), halves(gl_cols)], axis=1))
    w_o_rnn_b = _to_bf16(w_o_rnn, 512, halves(d))
    w_o_attn_b = _to_bf16(w_o_attn, 512, ones(d))
    w_out_b = _to_bf16(w_out, 512, halves(d))
    w_ffn_in_b = _to_bf16(w_ffn_in, 256, ones(2 * D_FF))
    w_ffn_out_b = _to_bf16(w_ffn_out, D_FF // 4, ones(d))
    for l in range(depth):
        need_ctx = l < depth - 1
        n_out_tiles = n_lat_tiles + n_ctx_tiles if need_ctx else n_lat_tiles
        xr, gr, q, k, vt, gl = _in_proj(stream, mod[l], g_mix_pre[l], w_in_b, l, tables, geo)
        hf, hb = _rnn_branch(xr, conv_w_half[l], conv_b_half[l], w_gate[l], b_gate[l], lam[l], geo)
        o_stream = _attention(q, k, vt, attn_sink[l], geo, need_ctx)
        x_all = _merge(stream, hf, hb, gr, o_stream, gl, mod[l], g_mix_post[l],
                       w_o_rnn_b, w_o_attn_b, w_out_b, l, geo, n_out_tiles)
        x_all = _ffn(x_all, mod[l], g_ffn_pre[l], g_ffn_post[l],
                     w_ffn_in_b, w_ffn_out_b, l, geo, n_out_tiles)
        stream = (x_all, x_all, n_lat_tiles)
    return x_all[:n_lat_rows].reshape(bsz, seq_len, d)
```

```python
import functools
import math

import jax
import jax.numpy as jnp
from jax import lax
from jax.experimental import pallas as pl
from jax.experimental.pallas import tpu as pltpu

D_MODEL = 1024
HEAD_DIM = 128
N_Q_HEADS = 8
N_KV_HEADS = 2
Q_PER_KV = N_Q_HEADS // N_KV_HEADS
KV_WIDTH = N_KV_HEADS * HEAD_DIM
WINDOW = 128
GRID_W = 64
N_FREQ = HEAD_DIM // 4
ROPE_BASE = 10000.0
N_RNN_BLOCKS = 8
RNN_BLOCK_W = D_MODEL // N_RNN_BLOCKS
LRU_C = 8.0
CONV_W = 4
CONV_LEFT = 2
D_FF = 2816
EPS = 1e-6
NEG_INF = -1e30
MOD_CHUNKS = 6

LANES = 128
SUBLANES = 8
VMEM_LIMIT_BYTES = 56 * 1024 * 1024

CHUNK = 256
CONV_T = 16
SCAN_UNROLL = 8
MERGE_PARTS = 1
FFN_CHUNK = 768
FFN_SUBTILES = 2
Q_BLOCK = 128
Q_SLOTS = 2
KEY_BLOCK = 128
BAND = Q_BLOCK + 2 * WINDOW
LOG2_E = math.log2(math.e)

BF16 = jnp.bfloat16
F32 = jnp.float32
F32_TINY = float(jnp.finfo(jnp.float32).tiny)


def _dot(a, b):
    return jnp.dot(a, b, preferred_element_type=F32)


def _sigmoid(x):
    return 0.5 * jnp.tanh(0.5 * x) + 0.5


GELU_C1 = math.sqrt(2.0 / math.pi)
GELU_C2 = 0.044715 * GELU_C1


def _rms(x):
    return x * lax.rsqrt(jnp.mean(x * x, axis=-1, keepdims=True) + EPS)


def _resident(shape):
    nd = len(shape)
    return pl.BlockSpec(shape, lambda *_: (0,) * nd, pipeline_mode=pl.Buffered(1))


def _resident_layer(stacked, layer):
    tail = stacked.shape[1:]
    return pl.BlockSpec((None,) + tail, lambda *_: (layer,) + (0,) * len(tail),
                        pipeline_mode=pl.Buffered(1))


def _cast_kernel(w_ref, scale_ref, o_ref):
    o_ref[...] = (w_ref[...] * scale_ref[...]).astype(BF16)


def _to_bf16(w, row_block, col_scale):
    depth, rows, cols = w.shape
    spec = pl.BlockSpec((None, row_block, cols), lambda l, r: (l, r, 0))
    return pl.pallas_call(
        _cast_kernel,
        grid=(depth, rows // row_block),
        in_specs=[spec, pl.BlockSpec((1, cols), lambda l, r: (0, 0))],
        out_specs=spec,
        out_shape=jax.ShapeDtypeStruct(w.shape, BF16),
        compiler_params=pltpu.CompilerParams(
            dimension_semantics=("arbitrary", "arbitrary"),
            vmem_limit_bytes=VMEM_LIMIT_BYTES),
        name="cast_bf16",
    )(w, col_scale)


def _mod_kernel(c_ref, w_ref, b_ref, o_ref):
    c = c_ref[...]
    s = (c * _sigmoid(c)).astype(BF16)
    o_ref[...] = _dot(s, w_ref[...].astype(BF16)) + b_ref[...]


def _modulation(cs, w_mod, b_mod):
    n_layers, d, width = w_mod.shape
    tn = 1536
    return pl.pallas_call(
        _mod_kernel,
        grid=(n_layers, width // tn),
        in_specs=[
            pl.BlockSpec((SUBLANES, d), lambda l, j: (0, 0)),
            pl.BlockSpec((None, d, tn), lambda l, j: (l, 0, j)),
            pl.BlockSpec((None, 1, tn), lambda l, j: (l, 0, j)),
        ],
        out_specs=pl.BlockSpec((None, SUBLANES, tn), lambda l, j: (l, 0, j)),
        out_shape=jax.ShapeDtypeStruct((n_layers, SUBLANES, width), F32),
        compiler_params=pltpu.CompilerParams(
            dimension_semantics=("arbitrary", "arbitrary"),
            vmem_limit_bytes=VMEM_LIMIT_BYTES),
        name="modulation",
    )(cs, w_mod, b_mod.reshape(n_layers, 1, width))


def _rope(t, cos_ref, sin_ref, n_heads):
    cos = cos_ref[...]
    sin = sin_ref[...]
    lane = lax.broadcasted_iota(jnp.int32, cos.shape, 1)
    first = (lane & (2 * N_FREQ - 1)) < N_FREQ
    outs = []
    for h in range(n_heads):
        th = t[:, h * HEAD_DIM:(h + 1) * HEAD_DIM]
        partner = jnp.where(first,
                            pltpu.roll(th, HEAD_DIM - N_FREQ, axis=1),
                            pltpu.roll(th, N_FREQ, axis=1))
        outs.append(th * cos + partner * sin)
    return jnp.concatenate(outs, axis=-1)


def _stream_specs(stream, tm, d, n_lat_tiles):
    _, ctx, ctx_off = stream
    last_ctx = ctx.shape[0] // tm - 1
    lat_spec = pl.BlockSpec((tm, d), lambda i: (jnp.minimum(i, n_lat_tiles - 1), 0))
    ctx_spec = pl.BlockSpec(
        (tm, d), lambda i: (jnp.minimum(jnp.maximum(i - n_lat_tiles, 0) + ctx_off, last_ctx), 0))
    return lat_spec, ctx_spec


def _stream_tile(lat_ref, ctx_ref, n_lat_tiles):
    return jnp.where(pl.program_id(0) < n_lat_tiles, lat_ref[...], ctx_ref[...])


def _in_kernel(xl_ref, xc_ref, mod_ref, g_ref, w_ref, cq_ref, sq_ref, ck_ref, sk_ref,
               xr_ref, gr_ref, q_ref, k_ref, vt_ref, gl_ref, *, n_lat_tiles):
    d = D_MODEL
    x = _stream_tile(xl_ref, xc_ref, n_lat_tiles)
    h = (_rms(x) * g_ref[...]) * (1.0 + mod_ref[1:2, :]) + mod_ref[0:1, :]
    h = h.astype(BF16)
    xr = _dot(h, w_ref[:, 0:d])
    for n in range(N_RNN_BLOCKS):
        xr_ref[pl.ds(n, x.shape[0], stride=N_RNN_BLOCKS), :] = (
            xr[:, n * RNN_BLOCK_W:(n + 1) * RNN_BLOCK_W])
    g = _dot(h, w_ref[:, d:2 * d])
    gr_ref[...] = (g * jnp.tanh(g * (GELU_C1 + GELU_C2 * (g * g))) + g).astype(BF16)
    q = _dot(h, w_ref[:, 2 * d:3 * d])
    q_ref[...] = _rope(q, cq_ref, sq_ref, N_Q_HEADS).astype(BF16)
    k = _dot(h, w_ref[:, 3 * d:3 * d + KV_WIDTH])
    k_ref[...] = _rope(k, ck_ref, sk_ref, N_KV_HEADS).astype(BF16)
    v = _dot(h, w_ref[:, 3 * d + KV_WIDTH:3 * d + 2 * KV_WIDTH])
    for t in range(vt_ref.shape[0]):
        vt_ref[t] = v[t * KEY_BLOCK:(t + 1) * KEY_BLOCK, :].T.astype(BF16)
    gl_ref[...] = jnp.tanh(
        _dot(h, w_ref[:, 3 * d + 2 * KV_WIDTH:5 * d + 2 * KV_WIDTH])).astype(BF16)


def _in_proj(stream, mod_l, g_pre, w_in, layer, tables, geo):
    d = D_MODEL
    tm = geo["tm"]
    n_tiles = geo["n_lat_tiles"] + geo["n_ctx_tiles"]
    nt = n_tiles * tm
    row = lambda i: (i, 0)
    tab = pl.BlockSpec((tm, LANES), lambda i: (geo["table_block"](i), 0))
    return pl.pallas_call(
        functools.partial(_in_kernel, n_lat_tiles=geo["n_lat_tiles"]),
        grid=(n_tiles,),
        in_specs=[
            *_stream_specs(stream, tm, d, geo["n_lat_tiles"]),
            pl.BlockSpec((None, MOD_CHUNKS, d), lambda i: (geo["mod_row"](i), 0, 0)),
            _resident((1, d)),
            _resident_layer(w_in, layer),
            tab, tab, tab, tab,
        ],
        out_specs=[
            pl.BlockSpec((tm * N_RNN_BLOCKS, RNN_BLOCK_W), row),
            pl.BlockSpec((tm, d), row),
            pl.BlockSpec((tm, d), row),
            pl.BlockSpec((tm, KV_WIDTH), row),
            pl.BlockSpec((tm // KEY_BLOCK, KV_WIDTH, KEY_BLOCK), lambda i: (i, 0, 0)),
            pl.BlockSpec((tm, 2 * d), row),
        ],
        out_shape=[
            jax.ShapeDtypeStruct((nt * N_RNN_BLOCKS, RNN_BLOCK_W), F32),
            jax.ShapeDtypeStruct((nt, d), BF16),
            jax.ShapeDtypeStruct((nt, d), BF16),
            jax.ShapeDtypeStruct((nt, KV_WIDTH), BF16),
            jax.ShapeDtypeStruct((nt // KEY_BLOCK, KV_WIDTH, KEY_BLOCK), BF16),
            jax.ShapeDtypeStruct((nt, 2 * d), BF16),
        ],
        compiler_params=pltpu.CompilerParams(
            dimension_semantics=("arbitrary",),
            vmem_limit_bytes=VMEM_LIMIT_BYTES),
        name="in_proj",
    )(stream[0], stream[1], mod_l, g_pre.reshape(1, d), w_in, *tables)


def _rnn_kernel(xf_ref, xfp_ref, xfn_ref, xb_ref, xbp_ref, xbn_ref,
                cw_ref, cb_ref, wg_ref, bg_ref, lam_ref,
                hf_ref, hb_ref,
                xc_s, a_s, b_s, carry_s):
    j = pl.program_id(1)
    n_steps = pl.num_programs(1)
    nb, bw = N_RNN_BLOCKS, RNN_BLOCK_W
    n_in = CONV_T + CONV_W - 1
    right = CONV_W - 1 - CONV_LEFT

    @pl.when(j == 0)
    def _():
        carry_s[...] = jnp.zeros_like(carry_s)

    def coeffs(direction, x_ref, xp_ref, xn_ref, has_prev, has_next):
        def conv_piece(v, base):
            v = v.reshape(n_in, nb, bw)
            acc = cb_ref[...] + v[0:CONV_T] * cw_ref[0]
            for k in range(1, CONV_W):
                acc = acc + v[k:k + CONV_T] * cw_ref[k]
            xc_s[pl.ds(base, CONV_T * nb), :] = acc.reshape(CONV_T * nb, bw)

        halo_rows = xp_ref.shape[0]
        left = jnp.where(has_prev, xp_ref[halo_rows - CONV_LEFT * nb:halo_rows, :], 0.0)
        conv_piece(jnp.concatenate([left, x_ref[0:(n_in - CONV_LEFT) * nb, :]], axis=0), 0)
        tail = jnp.where(has_next, xn_ref[0:right * nb, :], 0.0)
        last = CHUNK - CONV_T
        conv_piece(jnp.concatenate(
            [x_ref[(last - CONV_LEFT) * nb:CHUNK * nb, :], tail], axis=0), last * nb)

        def conv_body(c, carry):
            base = pl.multiple_of(c * (CONV_T * nb), CONV_T * nb)
            conv_piece(x_ref[pl.ds(base - CONV_LEFT * nb, n_in * nb), :], base)
            return carry

        lax.fori_loop(1, CHUNK // CONV_T - 1, conv_body, 0)

        for n in range(nb):
            rows = pl.ds(n, CHUNK, stride=nb)
            lam = lam_ref[direction, n]
            c_half = (0.5 * LRU_C * LOG2_E) * (
                jnp.minimum(lam, 0.0) - jnp.log1p(jnp.exp(-jnp.abs(lam))))
            xh = xc_s[rows, :]
            z = _dot(xh.astype(BF16), wg_ref[direction, n]) + bg_ref[direction, n]
            a = jnp.exp2(c_half * jnp.tanh(z[:, 0:bw]) + c_half)
            gated = xh * jnp.tanh(z[:, bw:2 * bw]) + xh
            om = 1.0 - a * a
            a_s[direction, rows, :] = a
            b_s[direction, rows, :] = (om * lax.rsqrt(jnp.maximum(om, F32_TINY))) * gated

    coeffs(0, xf_ref, xfp_ref, xfn_ref, j >= 2, jnp.logical_and(j >= 1, j < n_steps - 1))
    coeffs(1, xb_ref, xbp_ref, xbn_ref, jnp.logical_and(j >= 1, j < n_steps - 1), j >= 2)

    def step(t, carry):
        hf, hb = carry
        tf = pl.multiple_of(t * nb, nb)
        tb = pl.multiple_of((CHUNK - 1 - t) * nb, nb)
        hf = a_s[0, pl.ds(tf, nb), :] * hf + b_s[0, pl.ds(tf, nb), :]
        hb = a_s[1, pl.ds(tb, nb), :] * hb + b_s[1, pl.ds(tb, nb), :]
        hf_ref[pl.ds(tf, nb), :] = hf
        hb_ref[pl.ds(tb, nb), :] = hb
        return hf, hb

    hf, hb = lax.fori_loop(0, CHUNK, step, (carry_s[0], carry_s[1]), unroll=SCAN_UNROLL)
    carry_s[0] = hf
    carry_s[1] = hb


def _rnn_branch(xr, conv_w, conv_b, w_gate, b_gate, lam, geo):
    nb, bw = N_RNN_BLOCKS, RNN_BLOCK_W
    bsz, n_lat = geo["batch"], geo["lat_chunks"]
    n_steps = n_lat + 1
    halo = SUBLANES
    n_halo = xr.shape[0] // (halo * nb)
    per = CHUNK // halo

    def fwd_blk(b, j):
        return jnp.where(j == 0, bsz * n_lat + b, b * n_lat + j - 1)

    def bwd_blk(b, j):
        return jnp.where(j == 0, bsz * n_lat + b, b * n_lat + n_lat - j)

    def chunk(blk):
        return pl.BlockSpec((CHUNK * nb, bw), lambda b, j: (blk(b, j), 0))

    def prev(blk):
        return pl.BlockSpec(
            (halo * nb, bw), lambda b, j: (jnp.maximum(blk(b, j) * per - 1, 0), 0))

    def nxt(blk):
        return pl.BlockSpec(
            (halo * nb, bw), lambda b, j: (jnp.minimum((blk(b, j) + 1) * per, n_halo - 1), 0))

    return pl.pallas_call(
        _rnn_kernel,
        grid=(bsz, n_steps),
        in_specs=[
            chunk(fwd_blk), prev(fwd_blk), nxt(fwd_blk),
            chunk(bwd_blk), prev(bwd_blk), nxt(bwd_blk),
            _resident((CONV_W, nb, bw)),
            _resident((nb, bw)),
            _resident(w_gate.shape),
            _resident(b_gate.shape),
            _resident(lam.shape),
        ],
        out_specs=[chunk(fwd_blk), chunk(bwd_blk)],
        out_shape=[jax.ShapeDtypeStruct(xr.shape, F32)] * 2,
        scratch_shapes=[
            pltpu.VMEM((CHUNK * nb, bw), F32),
            pltpu.VMEM((2, CHUNK * nb, bw), F32),
            pltpu.VMEM((2, CHUNK * nb, bw), F32),
            pltpu.VMEM((2, nb, bw), F32),
        ],
        compiler_params=pltpu.CompilerParams(
            dimension_semantics=("arbitrary", "arbitrary"),
            vmem_limit_bytes=VMEM_LIMIT_BYTES),
        name="rglru",
    )(xr, xr, xr, xr, xr, xr, conv_w.reshape(CONV_W, nb, bw), conv_b.reshape(nb, bw),
      w_gate, b_gate, lam)


def _stack_heads(q, g):
    base = g * Q_PER_KV * HEAD_DIM
    return jnp.concatenate(
        [q[:, base + h * HEAD_DIM: base + (h + 1) * HEAD_DIM] for h in range(Q_PER_KV)], axis=0)


def _nt_dot(a, b):
    return lax.dot_general(a, b, (((1,), (1,)), ((), ())), preferred_element_type=F32)


def _attend_units(sink_ref, units):
    folds = KEY_BLOCK // SUBLANES
    for u in units:
        n_q = u["q"].shape[0]
        width = Q_PER_KV * n_q
        qs = _stack_heads(u["q"], u["g"])
        sink = jnp.concatenate(
            [jnp.full((1, n_q), sink_ref[u["g"] * Q_PER_KV + h] * LOG2_E, F32)
             for h in range(Q_PER_KV)], axis=1)
        m8 = jnp.broadcast_to(sink, (SUBLANES, width))
        for j, (k, bias) in enumerate(u["key_blocks"]):
            s = _nt_dot(k, qs)
            if bias is not None:
                s = s + jnp.concatenate([bias] * Q_PER_KV, axis=1)
            u["s_s"][j * KEY_BLOCK:(j + 1) * KEY_BLOCK, :] = s
            m8 = jnp.maximum(m8, jnp.max(s.reshape(folds, SUBLANES, width), axis=0))
        u["sink"] = sink
        u["m"] = jnp.max(m8, axis=0, keepdims=True)
    for u in units:
        width = Q_PER_KV * u["q"].shape[0]
        l8 = jnp.zeros((SUBLANES, width), F32)
        for j in range(len(u["key_blocks"])):
            p = jnp.exp2(u["s_s"][j * KEY_BLOCK:(j + 1) * KEY_BLOCK, :] - u["m"])
            l8 = l8 + jnp.sum(p.reshape(folds, SUBLANES, width), axis=0)
            u["p_s"][j * KEY_BLOCK:(j + 1) * KEY_BLOCK, :] = p.astype(BF16)
        u["denom"] = jnp.sum(l8, axis=0, keepdims=True) + jnp.exp2(u["sink"] - u["m"])
    for u in units:
        n_q = u["q"].shape[0]
        n_keys = len(u["key_blocks"]) * KEY_BLOCK
        ot = _dot(u["vt"], u["p_s"][0:n_keys, :]) * (1.0 / u["denom"])
        for h in range(Q_PER_KV):
            u["o_store"](h, ot[:, h * n_q:(h + 1) * n_q].T)


def _lat_attn_kernel(sink_ref, q_ref, k_ref, vt_ref, kc_ref, vtc_ref, o_ref, s_s, p_s,
                     *, seq_len):
    n_blk = q_ref.shape[0] // Q_BLOCK
    band_blocks = BAND // KEY_BLOCK
    ctx_blocks = vtc_ref.shape[0]
    last_start = seq_len // KEY_BLOCK - band_blocks
    rel = (lax.broadcasted_iota(jnp.int32, (KEY_BLOCK, Q_BLOCK), 0)
           - lax.broadcasted_iota(jnp.int32, (KEY_BLOCK, Q_BLOCK), 1))

    def block_units(i, slot):
        units = []
        qb = pl.program_id(1) * n_blk + i
        jb0 = jnp.clip(qb - WINDOW // KEY_BLOCK, 0, last_start)
        biases = [jnp.where(jnp.abs(rel + (jb0 + t - qb) * KEY_BLOCK) <= WINDOW, 0.0, NEG_INF)
                  for t in range(band_blocks)]
        rows = pl.ds(pl.multiple_of(i * Q_BLOCK, Q_BLOCK), Q_BLOCK)
        q = q_ref[rows, :]
        for g in range(N_KV_HEADS):
            gs = slice(g * HEAD_DIM, (g + 1) * HEAD_DIM)
            key_blocks = [
                (k_ref[pl.ds(pl.multiple_of((jb0 + t) * KEY_BLOCK, KEY_BLOCK), KEY_BLOCK), gs],
                 biases[t]) for t in range(band_blocks)]
            key_blocks += [(kc_ref[t * KEY_BLOCK:(t + 1) * KEY_BLOCK, gs], None)
                           for t in range(ctx_blocks)]
            vt = jnp.concatenate([vt_ref[jb0 + t, gs, :] for t in range(band_blocks)]
                                 + [vtc_ref[t, gs, :] for t in range(ctx_blocks)], axis=1)

            def o_store(h, tile, g=g, rows=rows):
                c0 = (g * Q_PER_KV + h) * HEAD_DIM
                o_ref[rows, c0:c0 + HEAD_DIM] = tile.astype(BF16)

            units.append(dict(g=g, q=q, key_blocks=key_blocks, vt=vt,
                              s_s=s_s.at[slot, g], p_s=p_s.at[slot, g], o_store=o_store))
        return units

    def body(it, carry):
        units = []
        for slot in range(Q_SLOTS):
            units += block_units(it * Q_SLOTS + slot, slot)
        _attend_units(sink_ref, units)
        return carry

    lax.fori_loop(0, n_blk // Q_SLOTS, body, 0)


def _ctx_attn_kernel(sink_ref, q_ref, kc_ref, vtc_ref, o_ref, s_s, p_s):
    ctx_blocks = vtc_ref.shape[0]
    q = q_ref[...]
    units = []
    for g in range(N_KV_HEADS):
        gs = slice(g * HEAD_DIM, (g + 1) * HEAD_DIM)
        key_blocks = [(kc_ref[t * KEY_BLOCK:(t + 1) * KEY_BLOCK, gs], None)
                      for t in range(ctx_blocks)]
        vt = jnp.concatenate([vtc_ref[t, gs, :] for t in range(ctx_blocks)], axis=1)

        def o_store(h, tile, g=g):
            c0 = (g * Q_PER_KV + h) * HEAD_DIM
            o_ref[:, c0:c0 + HEAD_DIM] = tile.astype(BF16)

        units.append(dict(g=g, q=q, key_blocks=key_blocks, vt=vt,
                          s_s=s_s.at[g], p_s=p_s.at[g], o_store=o_store))
    _attend_units(sink_ref, units)


def _attention(q, k, vt, sink, geo, with_ctx_queries):
    nt, d = q.shape
    bsz, seq_len, n_ctx = geo["batch"], geo["seq"], geo["ctx"]
    q_sup = next(s for s in (1024, 512, 256, Q_BLOCK) if seq_len % s == 0)
    n_sup = seq_len // q_sup
    ctx_blk0 = bsz * seq_len // n_ctx
    smem = pl.BlockSpec(memory_space=pltpu.SMEM)
    ctx_keys = pl.BlockSpec((n_ctx, KV_WIDTH), lambda b, *_: (ctx_blk0 + b, 0))
    ctx_vals = pl.BlockSpec((n_ctx // KEY_BLOCK, KV_WIDTH, KEY_BLOCK),
                            lambda b, *_: (ctx_blk0 + b, 0, 0))
    o = pl.pallas_call(
        functools.partial(_lat_attn_kernel, seq_len=seq_len),
        grid=(bsz, n_sup),
        in_specs=[
            smem,
            pl.BlockSpec((q_sup, d), lambda b, i: (b * n_sup + i, 0)),
            pl.BlockSpec((seq_len, KV_WIDTH), lambda b, i: (b, 0)),
            pl.BlockSpec((seq_len // KEY_BLOCK, KV_WIDTH, KEY_BLOCK), lambda b, i: (b, 0, 0)),
            ctx_keys, ctx_vals,
        ],
        out_specs=pl.BlockSpec((q_sup, d), lambda b, i: (b * n_sup + i, 0)),
        out_shape=jax.ShapeDtypeStruct((bsz * seq_len, d), BF16),
        scratch_shapes=[
            pltpu.VMEM((Q_SLOTS, N_KV_HEADS, BAND + n_ctx, Q_PER_KV * Q_BLOCK), F32),
            pltpu.VMEM((Q_SLOTS, N_KV_HEADS, BAND + n_ctx, Q_PER_KV * Q_BLOCK), BF16),
        ],
        compiler_params=pltpu.CompilerParams(
            dimension_semantics=("arbitrary", "arbitrary"),
            vmem_limit_bytes=VMEM_LIMIT_BYTES),
        name="lat_attention",
    )(sink, q, k, vt, k, vt)
    if not with_ctx_queries:
        return (o, o, 0)
    o_ctx = pl.pallas_call(
        _ctx_attn_kernel,
        grid=(bsz,),
        in_specs=[
            smem,
            pl.BlockSpec((n_ctx, d), lambda b: (ctx_blk0 + b, 0)),
            ctx_keys, ctx_vals,
        ],
        out_specs=pl.BlockSpec((n_ctx, d), lambda b: (b, 0)),
        out_shape=jax.ShapeDtypeStruct((bsz * n_ctx, d), BF16),
        scratch_shapes=[
            pltpu.VMEM((N_KV_HEADS, n_ctx, Q_PER_KV * n_ctx), F32),
            pltpu.VMEM((N_KV_HEADS, n_ctx, Q_PER_KV * n_ctx), BF16),
        ],
        compiler_params=pltpu.CompilerParams(
            dimension_semantics=("arbitrary",),
            vmem_limit_bytes=VMEM_LIMIT_BYTES),
        name="ctx_attention",
    )(sink, q, k, vt)
    return (o, o_ctx, 0)


def _merge_kernel(xl_ref, xc_ref, hf_ref, hb_ref, gr_ref, ol_ref, oc_ref, gl_ref, mod_ref,
                  g_ref, wr_ref, wa_ref, wo_ref, out_ref, *, n_lat_tiles):
    d = D_MODEL
    nb = N_RNN_BLOCKS
    rows_per_part = xl_ref.shape[0] // MERGE_PARTS
    is_lat = pl.program_id(0) < n_lat_tiles
    parts =[slice(p * rows_per_part, (p + 1) * rows_per_part) for p in range(MERGE_PARTS)]

    def rnn_input(p):
        h = jnp.concatenate(
            [hf_ref[pl.ds(p * rows_per_part * nb + n, rows_per_part, stride=nb), :]
             + hb_ref[pl.ds(p * rows_per_part * nb + n, rows_per_part, stride=nb), :]
             for n in range(nb)], axis=-1)
        return (h * gr_ref[parts[p], :].astype(F32)).astype(BF16)

    def gated_mix(p, y):
        r = parts[p]
        ya = _dot(y, wr_ref[...])
        yb = _dot(jnp.where(is_lat, ol_ref[r, :], oc_ref[r, :]), wa_ref[...])
        ta = gl_ref[r, 0:d].astype(F32)
        tb = gl_ref[r, d:2 * d].astype(F32)
        return ((ya * ta + ya) + (yb * tb + yb)).astype(BF16)

    def project(p, mix):
        r = parts[p]
        m = _dot(mix, wo_ref[...])
        x = jnp.where(is_lat, xl_ref[r, :], xc_ref[r, :])
        out_ref[r, :] = x + mod_ref[2:3, :] * (_rms(m) * g_ref[...])

    ys = [rnn_input(p) for p in range(MERGE_PARTS)]
    mixes = [gated_mix(p, ys[p]) for p in range(MERGE_PARTS)]
    for p in range(MERGE_PARTS):
        project(p, mixes[p])


def _merge(x_stream, hf, hb, gr, o_stream, gl, mod_l, g_post, w_o_rnn, w_o_attn, w_out,
           layer, geo, n_tiles):
    d = D_MODEL
    tm = geo["tm"]
    n_lat_tiles = geo["n_lat_tiles"]
    row = lambda i: (i, 0)
    tile = pl.BlockSpec((tm, d), row)
    slab = pl.BlockSpec((tm * N_RNN_BLOCKS, RNN_BLOCK_W), row)
    return pl.pallas_call(
        functools.partial(_merge_kernel, n_lat_tiles=n_lat_tiles),
        grid=(n_tiles,),
        in_specs=[
            *_stream_specs(x_stream, tm, d, n_lat_tiles), slab, slab, tile,
            *_stream_specs(o_stream, tm, d, n_lat_tiles),
            pl.BlockSpec((tm, 2 * d), row),
            pl.BlockSpec((None, MOD_CHUNKS, d), lambda i: (geo["mod_row"](i), 0, 0)),
            _resident((1, d)),
            _resident_layer(w_o_rnn, layer), _resident_layer(w_o_attn, layer),
            _resident_layer(w_out, layer),
        ],
        out_specs=tile,
        out_shape=jax.ShapeDtypeStruct((n_tiles * tm, d), F32),
        compiler_params=pltpu.CompilerParams(
            dimension_semantics=("arbitrary",),
            vmem_limit_bytes=VMEM_LIMIT_BYTES),
        name="merge",
    )(x_stream[0], x_stream[1], hf, hb, gr, o_stream[0], o_stream[1], gl, mod_l,
      g_post.reshape(1, d), w_o_rnn, w_o_attn, w_out)


def _ffn_kernel(x_ref, mod_ref, gpre_ref, gpost_ref, w1_ref, w2_ref, out_ref):
    sub = x_ref.shape[0] // FFN_SUBTILES
    bounds = list(range(0, D_FF, FFN_CHUNK)) + [D_FF]
    n_chunks = len(bounds) - 1

    def normed(s):
        x = x_ref[s * sub:(s + 1) * sub, :]
        h = (_rms(x) * gpre_ref[...]) * (1.0 + mod_ref[4:5, :]) + mod_ref[3:4, :]
        return h.astype(BF16)

    def first_layer(h, c):
        lo, hi = bounds[c], bounds[c + 1]
        return _dot(h, w1_ref[:, lo:hi]), _dot(h, w1_ref[:, D_FF + lo:D_FF + hi])

    h_next = normed(0)
    for s in range(FFN_SUBTILES):
        h = h_next
        if s + 1 < FFN_SUBTILES:
            h_next = normed(s + 1)
        f = None
        nxt = first_layer(h, 0)
        for c in range(n_chunks):
            half_gate, up = nxt
            if c + 1 < n_chunks:
                nxt = first_layer(h, c + 1)
            act = ((half_gate * jnp.tanh(half_gate) + half_gate) * up).astype(BF16)
            part = _dot(act, w2_ref[bounds[c]:bounds[c + 1], :])
            f = part if f is None else f + part
        rows = slice(s * sub, (s + 1) * sub)
        out_ref[rows, :] = x_ref[rows, :] + mod_ref[5:6, :] * (_rms(f) * gpost_ref[...])


def _ffn(x_all, mod_l, g_pre, g_post, w1, w2, layer, geo, n_tiles):
    d = x_all.shape[1]
    tm = geo["tm"] * FFN_SUBTILES
    n_blocks = n_tiles // FFN_SUBTILES
    tile = pl.BlockSpec((tm, d), lambda i: (i, 0))
    return pl.pallas_call(
        _ffn_kernel,
        grid=(n_blocks,),
        in_specs=[
            tile,
            pl.BlockSpec((None, MOD_CHUNKS, d),
                         lambda i: (geo["mod_row"](i * FFN_SUBTILES), 0, 0)),
            _resident((1, d)), _resident((1, d)),
            _resident_layer(w1, layer), _resident_layer(w2, layer),
        ],
        out_specs=tile,
        out_shape=jax.ShapeDtypeStruct((n_blocks * tm, d), F32),
        compiler_params=pltpu.CompilerParams(
            dimension_semantics=("arbitrary",),
            vmem_limit_bytes=VMEM_LIMIT_BYTES),
        name="ffn",
    )(x_all, mod_l, g_pre.reshape(1, d), g_post.reshape(1, d), w1, w2)


def _rope_tables(seq_len, pad_rows):
    n_rows = seq_len // GRID_W
    inv = ROPE_BASE ** (-jnp.arange(N_FREQ, dtype=F32) / N_FREQ)
    ang = jnp.arange(max(n_rows, GRID_W), dtype=F32)[:, None] * inv[None, :]
    cos_u, sin_u = jnp.cos(ang), jnp.sin(ang)
    by_row = lambda t: jnp.repeat(t[:n_rows], GRID_W, axis=0)
    by_col = lambda t: jnp.tile(t[:GRID_W], (n_rows, 1))
    cos = jnp.concatenate([by_row(cos_u)] * 2 + [by_col(cos_u)] * 2, axis=-1)
    sin = jnp.concatenate(
        [-by_row(sin_u), by_row(sin_u), -by_col(sin_u), by_col(sin_u)], axis=-1)
    cos = jnp.concatenate([cos, jnp.ones((pad_rows, HEAD_DIM), F32)], axis=0)
    sin = jnp.concatenate([sin, jnp.zeros((pad_rows, HEAD_DIM), F32)], axis=0)
    scale = HEAD_DIM ** -0.5 * LOG2_E
    return cos * scale, sin * scale, cos, sin


def kernel(x, c, ctx, c_ctx, w_mod, b_mod, g_mix_pre, g_mix_post, g_ffn_pre, g_ffn_post, w_in, conv_w, conv_b, lru_wa, lru_ba, lru_wx, lru_bx, lru_lam, attn_sink, w_o_rnn, w_o_attn, w_out, w_ffn_in, w_ffn_out):
    bsz, seq_len, d = x.shape
    n_ctx = ctx.shape[1]
    depth = w_mod.shape[0]
    assert d == D_MODEL and n_ctx == CHUNK and seq_len % CHUNK == 0 and seq_len >= BAND
    assert bsz + 1 <= SUBLANES and seq_len % GRID_W == 0

    n_lat_rows, n_ctx_rows = bsz * seq_len, bsz * n_ctx
    big = 512 * FFN_SUBTILES
    tm = 512 if (n_ctx_rows % big == 0 and seq_len % big == 0) else CHUNK
    assert n_ctx_rows % (tm * FFN_SUBTILES) == 0 and seq_len % (tm * FFN_SUBTILES) == 0
    n_lat_tiles, n_ctx_tiles = n_lat_rows // tm, n_ctx_rows // tm
    tiles_per_batch = seq_len // tm
    geo = {
        "batch": bsz, "seq": seq_len, "ctx": n_ctx, "tm": tm,
        "lat_chunks": seq_len // CHUNK,
        "n_lat_tiles": n_lat_tiles, "n_ctx_tiles": n_ctx_tiles,
        "mod_row": lambda i: jnp.where(i < n_lat_tiles, i // tiles_per_batch, bsz),
        "table_block": lambda i: jnp.where(
            i < n_lat_tiles, i % tiles_per_batch, tiles_per_batch + i - n_lat_tiles),
    }

    cs = jnp.concatenate(
        [c, c_ctx[None, :], jnp.zeros((SUBLANES - bsz - 1, d), F32)], axis=0)
    mod = _modulation(cs, w_mod, b_mod).reshape(depth, SUBLANES, MOD_CHUNKS, d)

    tables = _rope_tables(seq_len, n_ctx_rows)

    w_gate = jnp.concatenate([lru_wa, lru_wx], axis=-1).astype(BF16)
    b_gate = 0.5 * jnp.concatenate(
        [lru_ba.reshape(depth, 2, N_RNN_BLOCKS, 1, RNN_BLOCK_W),
         lru_bx.reshape(depth, 2, N_RNN_BLOCKS, 1, RNN_BLOCK_W)], axis=-1)
    lam = lru_lam.reshape(depth, 2, N_RNN_BLOCKS, 1, RNN_BLOCK_W)
    conv_w_half, conv_b_half = 0.5 * conv_w, 0.5 * conv_b

    stream = (x.reshape(n_lat_rows, d), ctx.reshape(n_ctx_rows, d), 0)
    ones = lambda n: jnp.ones((1, n), F32)
    halves = lambda n: jnp.full((1, n), 0.5, F32)
    gl_cols = 2 * d
    w_in_b = _to_bf16(w_in, 256, jnp.concatenate(
        [ones(w_in.shape[-1] - gl_cols), halves(gl_cols)], axis=1))
    w_o_rnn_b = _to_bf16(w_o_rnn, 512, halves(d))
    w_o_attn_b = _to_bf16(w_o_attn, 512, ones(d))
    w_out_b = _to_bf16(w_out, 512, halves(d))
    w_ffn_in_b = _to_bf16(w_ffn_in, 256, jnp.concatenate([halves(D_FF), ones(D_FF)], axis=1))
    w_ffn_out_b = _to_bf16(w_ffn_out, D_FF // 4, ones(d))
    for l in range(depth):
        need_ctx = l < depth - 1
        n_out_tiles = n_lat_tiles + n_ctx_tiles if need_ctx else n_lat_tiles
        xr, gr, q, k, vt, gl = _in_proj(stream, mod[l], g_mix_pre[l], w_in_b, l, tables, geo)
        hf, hb = _rnn_branch(xr, conv_w_half[l], conv_b_half[l], w_gate[l], b_gate[l], lam[l], geo)
        o_stream = _attention(q, k, vt, attn_sink[l], geo, need_ctx)
        x_all = _merge(stream, hf, hb, gr, o_stream, gl, mod[l], g_mix_post[l],
                       w_o_rnn_b, w_o_attn_b, w_out_b, l, geo, n_out_tiles)
        x_all = _ffn(x_all, mod[l], g_ffn_pre[l], g_ffn_post[l],
                     w_ffn_in_b, w_ffn_out_b, l, geo, n_out_tiles)
        stream = (x_all, x_all, n_lat_tiles)
    return x_all[:n_lat_rows].reshape(bsz, seq_len, d)
```

```python
import functools
import math

import jax
import jax.numpy as jnp
from jax import lax
from jax.experimental import pallas as pl
from jax.experimental.pallas import tpu as pltpu

D_MODEL = 1024
HEAD_DIM = 128
N_Q_HEADS = 8
N_KV_HEADS = 2
Q_PER_KV = N_Q_HEADS // N_KV_HEADS
KV_WIDTH = N_KV_HEADS * HEAD_DIM
WINDOW = 128
GRID_W = 64
N_FREQ = HEAD_DIM // 4
ROPE_BASE = 10000.0
N_RNN_BLOCKS = 8
RNN_BLOCK_W = D_MODEL // N_RNN_BLOCKS
LRU_C = 8.0
CONV_W = 4
CONV_LEFT = 2
D_FF = 2816
EPS = 1e-6
NEG_INF = -1e30
MOD_CHUNKS = 6

LANES = 128
SUBLANES = 8
VMEM_LIMIT_BYTES = 56 * 1024 * 1024

CHUNK = 256
CONV_T = 16
SCAN_UNROLL = 8
MERGE_PARTS = 1
FFN_CHUNK = 768
FFN_SUBTILES = 2
Q_BLOCK = 128
Q_SLOTS = 2
KEY_BLOCK = 128
BAND = Q_BLOCK + 2 * WINDOW
LOG2_E = math.log2(math.e)

BF16 = jnp.bfloat16
F32 = jnp.float32
F32_TINY = float(jnp.finfo(jnp.float32).tiny)


def _dot(a, b):
    return jnp.dot(a, b, preferred_element_type=F32)


def _sigmoid(x):
    return 0.5 * jnp.tanh(0.5 * x) + 0.5


GELU_C1 = math.sqrt(2.0 / math.pi)
GELU_C2 = 0.044715 * GELU_C1


def _rms(x):
    return x * lax.rsqrt(jnp.mean(x * x, axis=-1, keepdims=True) + EPS)


def _resident(shape):
    nd = len(shape)
    return pl.BlockSpec(shape, lambda *_: (0,) * nd, pipeline_mode=pl.Buffered(1))


def _resident_layer(stacked, layer):
    tail = stacked.shape[1:]
    return pl.BlockSpec((None,) + tail, lambda *_: (layer,) + (0,) * len(tail),
                        pipeline_mode=pl.Buffered(1))


def _cast_kernel(w_ref, scale_ref, o_ref):
    o_ref[...] = (w_ref[...] * scale_ref[...]).astype(BF16)


def _to_bf16(w, row_block, col_scale):
    depth, rows, cols = w.shape
    spec = pl.BlockSpec((None, row_block, cols), lambda l, r: (l, r, 0))
    return pl.pallas_call(
        _cast_kernel,
        grid=(depth, rows // row_block),
        in_specs=[spec, pl.BlockSpec((1, cols), lambda l, r: (0, 0))],
        out_specs=spec,
        out_shape=jax.ShapeDtypeStruct(w.shape, BF16),
        compiler_params=pltpu.CompilerParams(
            dimension_semantics=("arbitrary", "arbitrary"),
            vmem_limit_bytes=VMEM_LIMIT_BYTES),
        name="cast_bf16",
    )(w, col_scale)


def _mod_kernel(c_ref, w_ref, b_ref, o_ref):
    c = c_ref[...]
    s = (c * _sigmoid(c)).astype(BF16)
    o_ref[...] = _dot(s, w_ref[...].astype(BF16)) + b_ref[...]


def _modulation(cs, w_mod, b_mod):
    n_layers, d, width = w_mod.shape
    tn = 1536
    return pl.pallas_call(
        _mod_kernel,
        grid=(n_layers, width // tn),
        in_specs=[
            pl.BlockSpec((SUBLANES, d), lambda l, j: (0, 0)),
            pl.BlockSpec((None, d, tn), lambda l, j: (l, 0, j)),
            pl.BlockSpec((None, 1, tn), lambda l, j: (l, 0, j)),
        ],
        out_specs=pl.BlockSpec((None, SUBLANES, tn), lambda l, j: (l, 0, j)),
        out_shape=jax.ShapeDtypeStruct((n_layers, SUBLANES, width), F32),
        compiler_params=pltpu.CompilerParams(
            dimension_semantics=("arbitrary", "arbitrary"),
            vmem_limit_bytes=VMEM_LIMIT_BYTES),
        name="modulation",
    )(cs, w_mod, b_mod.reshape(n_layers, 1, width))


def _rope(t, cos_ref, sin_ref, n_heads):
    cos = cos_ref[...]
    sin = sin_ref[...]
    lane = lax.broadcasted_iota(jnp.int32, cos.shape, 1)
    first = (lane & (2 * N_FREQ - 1)) < N_FREQ
    outs = []
    for h in range(n_heads):
        th = t[:, h * HEAD_DIM:(h + 1) * HEAD_DIM]
        partner = jnp.where(first,
                            pltpu.roll(th, HEAD_DIM - N_FREQ, axis=1),
                            pltpu.roll(th, N_FREQ, axis=1))
        outs.append(th * cos + partner * sin)
    return jnp.concatenate(outs, axis=-1)


def _stream_specs(stream, tm, d, n_lat_tiles):
    _, ctx, ctx_off = stream
    last_ctx = ctx.shape[0] // tm - 1
    lat_spec = pl.BlockSpec((tm, d), lambda i: (jnp.minimum(i, n_lat_tiles - 1), 0))
    ctx_spec = pl.BlockSpec(
        (tm, d), lambda i: (jnp.minimum(jnp.maximum(i - n_lat_tiles, 0) + ctx_off, last_ctx), 0))
    return lat_spec, ctx_spec


def _stream_tile(lat_ref, ctx_ref, n_lat_tiles):
    return jnp.where(pl.program_id(0) < n_lat_tiles, lat_ref[...], ctx_ref[...])


def _in_kernel(xl_ref, xc_ref, mod_ref, g_ref, w_ref, cq_ref, sq_ref, ck_ref, sk_ref,
               xr_ref, gr_ref, q_ref, k_ref, vt_ref, gl_ref, *, n_lat_tiles):
    d = D_MODEL
    x = _stream_tile(xl_ref, xc_ref, n_lat_tiles)
    h = (_rms(x) * g_ref[...]) * (1.0 + mod_ref[1:2, :]) + mod_ref[0:1, :]
    h = h.astype(BF16)
    xr = _dot(h, w_ref[:, 0:d])
    for n in range(N_RNN_BLOCKS):
        xr_ref[pl.ds(n, x.shape[0], stride=N_RNN_BLOCKS), :] = (
            xr[:, n * RNN_BLOCK_W:(n + 1) * RNN_BLOCK_W])
    gr_ref[...] = _dot(h, w_ref[:, d:2 * d]).astype(BF16)
    q = _dot(h, w_ref[:, 2 * d:3 * d])
    q_ref[...] = _rope(q, cq_ref, sq_ref, N_Q_HEADS).astype(BF16)
    k = _dot(h, w_ref[:, 3 * d:3 * d + KV_WIDTH])
    k_ref[...] = _rope(k, ck_ref, sk_ref, N_KV_HEADS).astype(BF16)
    v = _dot(h, w_ref[:, 3 * d + KV_WIDTH:3 * d + 2 * KV_WIDTH])
    for t in range(vt_ref.shape[0]):
        vt_ref[t] = v[t * KEY_BLOCK:(t + 1) * KEY_BLOCK, :].T.astype(BF16)
    gl_ref[...] = _dot(h, w_ref[:, 3 * d + 2 * KV_WIDTH:5 * d + 2 * KV_WIDTH]).astype(BF16)


def _in_proj(stream, mod_l, g_pre, w_in, layer, tables, geo):
    d = D_MODEL
    tm = geo["tm"]
    n_tiles = geo["n_lat_tiles"] + geo["n_ctx_tiles"]
    nt = n_tiles * tm
    row = lambda i: (i, 0)
    tab = pl.BlockSpec((tm, LANES), lambda i: (geo["table_block"](i), 0))
    return pl.pallas_call(
        functools.partial(_in_kernel, n_lat_tiles=geo["n_lat_tiles"]),
        grid=(n_tiles,),
        in_specs=[
            *_stream_specs(stream, tm, d, geo["n_lat_tiles"]),
            pl.BlockSpec((None, MOD_CHUNKS, d), lambda i: (geo["mod_row"](i), 0, 0)),
            _resident((1, d)),
            _resident_layer(w_in, layer),
            tab, tab, tab, tab,
        ],
        out_specs=[
            pl.BlockSpec((tm * N_RNN_BLOCKS, RNN_BLOCK_W), row),
            pl.BlockSpec((tm, d), row),
            pl.BlockSpec((tm, d), row),
            pl.BlockSpec((tm, KV_WIDTH), row),
            pl.BlockSpec((tm // KEY_BLOCK, KV_WIDTH, KEY_BLOCK), lambda i: (i, 0, 0)),
            pl.BlockSpec((tm, 2 * d), row),
        ],
        out_shape=[
            jax.ShapeDtypeStruct((nt * N_RNN_BLOCKS, RNN_BLOCK_W), F32),
            jax.ShapeDtypeStruct((nt, d), BF16),
            jax.ShapeDtypeStruct((nt, d), BF16),
            jax.ShapeDtypeStruct((nt, KV_WIDTH), BF16),
            jax.ShapeDtypeStruct((nt // KEY_BLOCK, KV_WIDTH, KEY_BLOCK), BF16),
            jax.ShapeDtypeStruct((nt, 2 * d), BF16),
        ],
        compiler_params=pltpu.CompilerParams(
            dimension_semantics=("arbitrary",),
            vmem_limit_bytes=VMEM_LIMIT_BYTES),
        name="in_proj",
    )(stream[0], stream[1], mod_l, g_pre.reshape(1, d), w_in, *tables)


def _rnn_kernel(xf_ref, xfp_ref, xfn_ref, xb_ref, xbp_ref, xbn_ref,
                cw_ref, cb_ref, wg_ref, bg_ref, lam_ref,
                hf_ref, hb_ref,
                xc_s, a_s, b_s, carry_s):
    j = pl.program_id(1)
    n_steps = pl.num_programs(1)
    nb, bw = N_RNN_BLOCKS, RNN_BLOCK_W
    n_in = CONV_T + CONV_W - 1
    right = CONV_W - 1 - CONV_LEFT

    @pl.when(j == 0)
    def _():
        carry_s[...] = jnp.zeros_like(carry_s)

    def conv(slot, x_ref, xp_ref, xn_ref, has_prev, has_next):
        def conv_piece(v, base):
            v = v.reshape(n_in, nb, bw)
            acc = cb_ref[...] + v[0:CONV_T] * cw_ref[0]
            for k in range(1, CONV_W):
                acc = acc + v[k:k + CONV_T] * cw_ref[k]
            xc_s[slot, pl.ds(base, CONV_T * nb), :] = acc.reshape(CONV_T * nb, bw)

        halo_rows = xp_ref.shape[0]
        left = jnp.where(has_prev, xp_ref[halo_rows - CONV_LEFT * nb:halo_rows, :], 0.0)
        conv_piece(jnp.concatenate([left, x_ref[0:(n_in - CONV_LEFT) * nb, :]], axis=0), 0)
        tail = jnp.where(has_next, xn_ref[0:right * nb, :], 0.0)
        last = CHUNK - CONV_T
        conv_piece(jnp.concatenate(
            [x_ref[(last - CONV_LEFT) * nb:CHUNK * nb, :], tail], axis=0), last * nb)

        def conv_body(c, carry):
            base = pl.multiple_of(c * (CONV_T * nb), CONV_T * nb)
            conv_piece(x_ref[pl.ds(base - CONV_LEFT * nb, n_in * nb), :], base)
            return carry

        lax.fori_loop(1, CHUNK // CONV_T - 1, conv_body, 0)

    fwd_chunk = j
    bwd_chunk = jnp.where(j == 0, 0, n_steps - j)
    mid = jnp.logical_and(j >= 1, j < n_steps - 1)

    @pl.when(2 * j <= n_steps)
    def _():
        conv(fwd_chunk, xf_ref, xfp_ref, xfn_ref, j >= 2, mid)

    @pl.when(jnp.logical_and(j >= 1, 2 * j < n_steps))
    def _():
        conv(bwd_chunk, xb_ref, xbp_ref, xbn_ref, mid, j >= 2)

    def coeffs(direction, slot):
        for n in range(nb):
            rows = pl.ds(n, CHUNK, stride=nb)
            lam = lam_ref[direction, n]
            c_half = (0.5 * LRU_C * LOG2_E) * (
                jnp.minimum(lam, 0.0) - jnp.log1p(jnp.exp(-jnp.abs(lam))))
            xh = xc_s[slot, rows, :]
            z = _dot(xh.astype(BF16), wg_ref[direction, n]) + bg_ref[direction, n]
            a = jnp.exp2(c_half * jnp.tanh(z[:, 0:bw]) + c_half)
            gated = xh * jnp.tanh(z[:, bw:2 * bw]) + xh
            om = 1.0 - a * a
            a_s[direction, rows, :] = a
            b_s[direction, rows, :] = (om * lax.rsqrt(jnp.maximum(om, F32_TINY))) * gated

    coeffs(0, fwd_chunk)
    coeffs(1, bwd_chunk)

    def step(t, carry):
        hf, hb = carry
        tf = pl.multiple_of(t * nb, nb)
        tb = pl.multiple_of((CHUNK - 1 - t) * nb, nb)
        hf = a_s[0, pl.ds(tf, nb), :] * hf + b_s[0, pl.ds(tf, nb), :]
        hb = a_s[1, pl.ds(tb, nb), :] * hb + b_s[1, pl.ds(tb, nb), :]
        hf_ref[pl.ds(tf, nb), :] = hf
        hb_ref[pl.ds(tb, nb), :] = hb
        return hf, hb

    hf, hb = lax.fori_loop(0, CHUNK, step, (carry_s[0], carry_s[1]), unroll=SCAN_UNROLL)
    carry_s[0] = hf
    carry_s[1] = hb


def _rnn_branch(xr, conv_w, conv_b, w_gate, b_gate, lam, geo):
    nb, bw = N_RNN_BLOCKS, RNN_BLOCK_W
    bsz, n_lat = geo["batch"], geo["lat_chunks"]
    n_steps = n_lat + 1
    halo = SUBLANES
    n_halo = xr.shape[0] // (halo * nb)
    per = CHUNK // halo

    def fwd_blk(b, j):
        return jnp.where(j == 0, bsz * n_lat + b, b * n_lat + j - 1)

    def bwd_blk(b, j):
        return jnp.where(j == 0, bsz * n_lat + b, b * n_lat + n_lat - j)

    def chunk(blk):
        return pl.BlockSpec((CHUNK * nb, bw), lambda b, j: (blk(b, j), 0))

    def prev(blk):
        return pl.BlockSpec(
            (halo * nb, bw), lambda b, j: (jnp.maximum(blk(b, j) * per - 1, 0), 0))

    def nxt(blk):
        return pl.BlockSpec(
            (halo * nb, bw), lambda b, j: (jnp.minimum((blk(b, j) + 1) * per, n_halo - 1), 0))

    return pl.pallas_call(
        _rnn_kernel,
        grid=(bsz, n_steps),
        in_specs=[
            chunk(fwd_blk), prev(fwd_blk), nxt(fwd_blk),
            chunk(bwd_blk), prev(bwd_blk), nxt(bwd_blk),
            _resident((CONV_W, nb, bw)),
            _resident((nb, bw)),
            _resident(w_gate.shape),
            _resident(b_gate.shape),
            _resident(lam.shape),
        ],
        out_specs=[chunk(fwd_blk), chunk(bwd_blk)],
        out_shape=[jax.ShapeDtypeStruct(xr.shape, F32)] * 2,
        scratch_shapes=[
            pltpu.VMEM((n_steps, CHUNK * nb, bw), F32),
            pltpu.VMEM((2, CHUNK * nb, bw), F32),
            pltpu.VMEM((2, CHUNK * nb, bw), F32),
            pltpu.VMEM((2, nb, bw), F32),
        ],
        compiler_params=pltpu.CompilerParams(
            dimension_semantics=("arbitrary", "arbitrary"),
            vmem_limit_bytes=VMEM_LIMIT_BYTES),
        name="rglru",
    )(xr, xr, xr, xr, xr, xr, conv_w.reshape(CONV_W, nb, bw), conv_b.reshape(nb, bw),
      w_gate, b_gate, lam)


def _stack_heads(q, g):
    base = g * Q_PER_KV * HEAD_DIM
    return jnp.concatenate(
        [q[:, base + h * HEAD_DIM: base + (h + 1) * HEAD_DIM] for h in range(Q_PER_KV)], axis=0)


def _nt_dot(a, b):
    return lax.dot_general(a, b, (((1,), (1,)), ((), ())), preferred_element_type=F32)


def _attend_units(sink_ref, units):
    folds = KEY_BLOCK // SUBLANES
    for u in units:
        n_q = u["q"].shape[0]
        width = Q_PER_KV * n_q
        qs = _stack_heads(u["q"], u["g"])
        sink = jnp.concatenate(
            [jnp.full((1, n_q), sink_ref[u["g"] * Q_PER_KV + h] * LOG2_E, F32)
             for h in range(Q_PER_KV)], axis=1)
        m8 = jnp.broadcast_to(sink, (SUBLANES, width))
        for j, (k, bias) in enumerate(u["key_blocks"]):
            s = _nt_dot(k, qs)
            if bias is not None:
                s = s + jnp.concatenate([bias] * Q_PER_KV, axis=1)
            u["s_s"][j * KEY_BLOCK:(j + 1) * KEY_BLOCK, :] = s
            m8 = jnp.maximum(m8, jnp.max(s.reshape(folds, SUBLANES, width), axis=0))
        u["sink"] = sink
        u["m"] = jnp.max(m8, axis=0, keepdims=True)
    for u in units:
        width = Q_PER_KV * u["q"].shape[0]
        l8 = jnp.zeros((SUBLANES, width), F32)
        for j in range(len(u["key_blocks"])):
            p = jnp.exp2(u["s_s"][j * KEY_BLOCK:(j + 1) * KEY_BLOCK, :] - u["m"])
            l8 = l8 + jnp.sum(p.reshape(folds, SUBLANES, width), axis=0)
            u["p_s"][j * KEY_BLOCK:(j + 1) * KEY_BLOCK, :] = p.astype(BF16)
        u["denom"] = jnp.sum(l8, axis=0, keepdims=True) + jnp.exp2(u["sink"] - u["m"])
    for u in units:
        n_q = u["q"].shape[0]
        n_keys = len(u["key_blocks"]) * KEY_BLOCK
        ot = _dot(u["vt"], u["p_s"][0:n_keys, :]) * (1.0 / u["denom"])
        for h in range(Q_PER_KV):
            u["o_store"](h, ot[:, h * n_q:(h + 1) * n_q].T)


def _lat_attn_kernel(sink_ref, q_ref, k_ref, vt_ref, kc_ref, vtc_ref, o_ref, s_s, p_s,
                     *, seq_len):
    n_blk = q_ref.shape[0] // Q_BLOCK
    band_blocks = BAND // KEY_BLOCK
    ctx_blocks = vtc_ref.shape[0]
    last_start = seq_len // KEY_BLOCK - band_blocks
    rel = (lax.broadcasted_iota(jnp.int32, (KEY_BLOCK, Q_BLOCK), 0)
           - lax.broadcasted_iota(jnp.int32, (KEY_BLOCK, Q_BLOCK), 1))

    def block_units(i, slot):
        units = []
        qb = pl.program_id(1) * n_blk + i
        jb0 = jnp.clip(qb - WINDOW // KEY_BLOCK, 0, last_start)
        biases = [jnp.where(jnp.abs(rel + (jb0 + t - qb) * KEY_BLOCK) <= WINDOW, 0.0, NEG_INF)
                  for t in range(band_blocks)]
        rows = pl.ds(pl.multiple_of(i * Q_BLOCK, Q_BLOCK), Q_BLOCK)
        q = q_ref[rows, :]
        for g in range(N_KV_HEADS):
            gs = slice(g * HEAD_DIM, (g + 1) * HEAD_DIM)
            key_blocks = [
                (k_ref[pl.ds(pl.multiple_of((jb0 + t) * KEY_BLOCK, KEY_BLOCK), KEY_BLOCK), gs],
                 biases[t]) for t in range(band_blocks)]
            key_blocks += [(kc_ref[t * KEY_BLOCK:(t + 1) * KEY_BLOCK, gs], None)
                           for t in range(ctx_blocks)]
            vt = jnp.concatenate([vt_ref[jb0 + t, gs, :] for t in range(band_blocks)]
                                 + [vtc_ref[t, gs, :] for t in range(ctx_blocks)], axis=1)

            def o_store(h, tile, g=g, rows=rows):
                c0 = (g * Q_PER_KV + h) * HEAD_DIM
                o_ref[rows, c0:c0 + HEAD_DIM] = tile.astype(BF16)

            units.append(dict(g=g, q=q, key_blocks=key_blocks, vt=vt,
                              s_s=s_s.at[slot, g], p_s=p_s.at[slot, g], o_store=o_store))
        return units

    def body(it, carry):
        units = []
        for slot in range(Q_SLOTS):
            units += block_units(it * Q_SLOTS + slot, slot)
        _attend_units(sink_ref, units)
        return carry

    lax.fori_loop(0, n_blk // Q_SLOTS, body, 0)


def _ctx_attn_kernel(sink_ref, q_ref, kc_ref, vtc_ref, o_ref, s_s, p_s):
    ctx_blocks = vtc_ref.shape[0]
    q = q_ref[...]
    units = []
    for g in range(N_KV_HEADS):
        gs = slice(g * HEAD_DIM, (g + 1) * HEAD_DIM)
        key_blocks = [(kc_ref[t * KEY_BLOCK:(t + 1) * KEY_BLOCK, gs], None)
                      for t in range(ctx_blocks)]
        vt = jnp.concatenate([vtc_ref[t, gs, :] for t in range(ctx_blocks)], axis=1)

        def o_store(h, tile, g=g):
            c0 = (g * Q_PER_KV + h) * HEAD_DIM
            o_ref[:, c0:c0 + HEAD_DIM] = tile.astype(BF16)

        units.append(dict(g=g, q=q, key_blocks=key_blocks, vt=vt,
                          s_s=s_s.at[g], p_s=p_s.at[g], o_store=o_store))
    _attend_units(sink_ref, units)


def _attention(q, k, vt, sink, geo, with_ctx_queries):
    nt, d = q.shape
    bsz, seq_len, n_ctx = geo["batch"], geo["seq"], geo["ctx"]
    q_sup = next(s for s in (1024, 512, 256, Q_BLOCK) if seq_len % s == 0)
    n_sup = seq_len // q_sup
    ctx_blk0 = bsz * seq_len // n_ctx
    smem = pl.BlockSpec(memory_space=pltpu.SMEM)
    ctx_keys = pl.BlockSpec((n_ctx, KV_WIDTH), lambda b, *_: (ctx_blk0 + b, 0))
    ctx_vals = pl.BlockSpec((n_ctx // KEY_BLOCK, KV_WIDTH, KEY_BLOCK),
                            lambda b, *_: (ctx_blk0 + b, 0, 0))
    o = pl.pallas_call(
        functools.partial(_lat_attn_kernel, seq_len=seq_len),
        grid=(bsz, n_sup),
        in_specs=[
            smem,
            pl.BlockSpec((q_sup, d), lambda b, i: (b * n_sup + i, 0)),
            pl.BlockSpec((seq_len, KV_WIDTH), lambda b, i: (b, 0)),
            pl.BlockSpec((seq_len // KEY_BLOCK, KV_WIDTH, KEY_BLOCK), lambda b, i: (b, 0, 0)),
            ctx_keys, ctx_vals,
        ],
        out_specs=pl.BlockSpec((q_sup, d), lambda b, i: (b * n_sup + i, 0)),
        out_shape=jax.ShapeDtypeStruct((bsz * seq_len, d), BF16),
        scratch_shapes=[
            pltpu.VMEM((Q_SLOTS, N_KV_HEADS, BAND + n_ctx, Q_PER_KV * Q_BLOCK), F32),
            pltpu.VMEM((Q_SLOTS, N_KV_HEADS, BAND + n_ctx, Q_PER_KV * Q_BLOCK), BF16),
        ],
        compiler_params=pltpu.CompilerParams(
            dimension_semantics=("arbitrary", "arbitrary"),
            vmem_limit_bytes=VMEM_LIMIT_BYTES),
        name="lat_attention",
    )(sink, q, k, vt, k, vt)
    if not with_ctx_queries:
        return (o, o, 0)
    o_ctx = pl.pallas_call(
        _ctx_attn_kernel,
        grid=(bsz,),
        in_specs=[
            smem,
            pl.BlockSpec((n_ctx, d), lambda b: (ctx_blk0 + b, 0)),
            ctx_keys, ctx_vals,
        ],
        out_specs=pl.BlockSpec((n_ctx, d), lambda b: (b, 0)),
        out_shape=jax.ShapeDtypeStruct((bsz * n_ctx, d), BF16),
        scratch_shapes=[
            pltpu.VMEM((N_KV_HEADS, n_ctx, Q_PER_KV * n_ctx), F32),
            pltpu.VMEM((N_KV_HEADS, n_ctx, Q_PER_KV * n_ctx), BF16),
        ],
        compiler_params=pltpu.CompilerParams(
            dimension_semantics=("arbitrary",),
            vmem_limit_bytes=VMEM_LIMIT_BYTES),
        name="ctx_attention",
    )(sink, q, k, vt)
    return (o, o_ctx, 0)


def _merge_kernel(xl_ref, xc_ref, hf_ref, hb_ref, gr_ref, ol_ref, oc_ref, gl_ref, mod_ref,
                  g_ref, wr_ref, wa_ref, wo_ref, out_ref, *, n_lat_tiles):
    d = D_MODEL
    nb = N_RNN_BLOCKS
    rows_per_part = xl_ref.shape[0] // MERGE_PARTS
    is_lat = pl.program_id(0) < n_lat_tiles
    parts =[slice(p * rows_per_part, (p + 1) * rows_per_part) for p in range(MERGE_PARTS)]

    def rnn_input(p):
        h = jnp.concatenate(
            [hf_ref[pl.ds(p * rows_per_part * nb + n, rows_per_part, stride=nb), :]
             + hb_ref[pl.ds(p * rows_per_part * nb + n, rows_per_part, stride=nb), :]
             for n in range(nb)], axis=-1)
        g = gr_ref[parts[p], :].astype(F32)
        hg = h * g
        return (hg * jnp.tanh(g * (GELU_C1 + GELU_C2 * (g * g))) + hg).astype(BF16)

    def gated_mix(p, y):
        r = parts[p]
        ya = _dot(y, wr_ref[...])
        yb = _dot(jnp.where(is_lat, ol_ref[r, :], oc_ref[r, :]), wa_ref[...])
        ta = jnp.tanh(gl_ref[r, 0:d].astype(F32))
        tb = jnp.tanh(gl_ref[r, d:2 * d].astype(F32))
        return ((ya * ta + ya) + (yb * tb + yb)).astype(BF16)

    def project(p, mix):
        r = parts[p]
        m = _dot(mix, wo_ref[...])
        x = jnp.where(is_lat, xl_ref[r, :], xc_ref[r, :])
        out_ref[r, :] = x + mod_ref[2:3, :] * (_rms(m) * g_ref[...])

    ys = [rnn_input(p) for p in range(MERGE_PARTS)]
    mixes = [gated_mix(p, ys[p]) for p in range(MERGE_PARTS)]
    for p in range(MERGE_PARTS):
        project(p, mixes[p])


def _merge(x_stream, hf, hb, gr, o_stream, gl, mod_l, g_post, w_o_rnn, w_o_attn, w_out,
           layer, geo, n_tiles):
    d = D_MODEL
    tm = geo["tm"]
    n_lat_tiles = geo["n_lat_tiles"]
    row = lambda i: (i, 0)
    tile = pl.BlockSpec((tm, d), row)
    slab = pl.BlockSpec((tm * N_RNN_BLOCKS, RNN_BLOCK_W), row)
    return pl.pallas_call(
        functools.partial(_merge_kernel, n_lat_tiles=n_lat_tiles),
        grid=(n_tiles,),
        in_specs=[
            *_stream_specs(x_stream, tm, d, n_lat_tiles), slab, slab, tile,
            *_stream_specs(o_stream, tm, d, n_lat_tiles),
            pl.BlockSpec((tm, 2 * d), row),
            pl.BlockSpec((None, MOD_CHUNKS, d), lambda i: (geo["mod_row"](i), 0, 0)),
            _resident((1, d)),
            _resident_layer(w_o_rnn, layer), _resident_layer(w_o_attn, layer),
            _resident_layer(w_out, layer),
        ],
        out_specs=tile,
        out_shape=jax.ShapeDtypeStruct((n_tiles * tm, d), F32),
        compiler_params=pltpu.CompilerParams(
            dimension_semantics=("arbitrary",),
            vmem_limit_bytes=VMEM_LIMIT_BYTES),
        name="merge",
    )(x_stream[0], x_stream[1], hf, hb, gr, o_stream[0], o_stream[1], gl, mod_l,
      g_post.reshape(1, d), w_o_rnn, w_o_attn, w_out)


def _ffn_kernel(x_ref, mod_ref, gpre_ref, gpost_ref, w1_ref, w2_ref, out_ref):
    sub = x_ref.shape[0] // FFN_SUBTILES
    bounds = list(range(0, D_FF, FFN_CHUNK)) + [D_FF]
    n_chunks = len(bounds) - 1

    def normed(s):
        x = x_ref[s * sub:(s + 1) * sub, :]
        h = (_rms(x) * gpre_ref[...]) * (1.0 + mod_ref[4:5, :]) + mod_ref[3:4, :]
        return h.astype(BF16)

    def first_layer(h, c):
        lo, hi = bounds[c], bounds[c + 1]
        return _dot(h, w1_ref[:, lo:hi]), _dot(h, w1_ref[:, D_FF + lo:D_FF + hi])

    h_next = normed(0)
    for s in range(FFN_SUBTILES):
        h = h_next
        if s + 1 < FFN_SUBTILES:
            h_next = normed(s + 1)
        f = None
        nxt = first_layer(h, 0)
        for c in range(n_chunks):
            half_gate, up = nxt
            if c + 1 < n_chunks:
                nxt = first_layer(h, c + 1)
            act = ((half_gate * jnp.tanh(half_gate) + half_gate) * up).astype(BF16)
            part = _dot(act, w2_ref[bounds[c]:bounds[c + 1], :])
            f = part if f is None else f + part
        rows = slice(s * sub, (s + 1) * sub)
        out_ref[rows, :] = x_ref[rows, :] + mod_ref[5:6, :] * (_rms(f) * gpost_ref[...])


def _ffn(x_all, mod_l, g_pre, g_post, w1, w2, layer, geo, n_tiles):
    d = x_all.shape[1]
    tm = geo["tm"] * FFN_SUBTILES
    n_blocks = n_tiles // FFN_SUBTILES
    tile = pl.BlockSpec((tm, d), lambda i: (i, 0))
    return pl.pallas_call(
        _ffn_kernel,
        grid=(n_blocks,),
        in_specs=[
            tile,
            pl.BlockSpec((None, MOD_CHUNKS, d),
                         lambda i: (geo["mod_row"](i * FFN_SUBTILES), 0, 0)),
            _resident((1, d)), _resident((1, d)),
            _resident_layer(w1, layer), _resident_layer(w2, layer),
        ],
        out_specs=tile,
        out_shape=jax.ShapeDtypeStruct((n_blocks * tm, d), F32),
        compiler_params=pltpu.CompilerParams(
            dimension_semantics=("arbitrary",),
            vmem_limit_bytes=VMEM_LIMIT_BYTES),
        name="ffn",
    )(x_all, mod_l, g_pre.reshape(1, d), g_post.reshape(1, d), w1, w2)


def _rope_tables(seq_len, pad_rows):
    n_rows = seq_len // GRID_W
    inv = ROPE_BASE ** (-jnp.arange(N_FREQ, dtype=F32) / N_FREQ)
    ang = jnp.arange(max(n_rows, GRID_W), dtype=F32)[:, None] * inv[None, :]
    cos_u, sin_u = jnp.cos(ang), jnp.sin(ang)
    by_row = lambda t: jnp.repeat(t[:n_rows], GRID_W, axis=0)
    by_col = lambda t: jnp.tile(t[:GRID_W], (n_rows, 1))
    cos = jnp.concatenate([by_row(cos_u)] * 2 + [by_col(cos_u)] * 2, axis=-1)
    sin = jnp.concatenate(
        [-by_row(sin_u), by_row(sin_u), -by_col(sin_u), by_col(sin_u)], axis=-1)
    cos = jnp.concatenate([cos, jnp.ones((pad_rows, HEAD_DIM), F32)], axis=0)
    sin = jnp.concatenate([sin, jnp.zeros((pad_rows, HEAD_DIM), F32)], axis=0)
    scale = HEAD_DIM ** -0.5 * LOG2_E
    return cos * scale, sin * scale, cos, sin


def kernel(x, c, ctx, c_ctx, w_mod, b_mod, g_mix_pre, g_mix_post, g_ffn_pre, g_ffn_post, w_in, conv_w, conv_b, lru_wa, lru_ba, lru_wx, lru_bx, lru_lam, attn_sink, w_o_rnn, w_o_attn, w_out, w_ffn_in, w_ffn_out):
    bsz, seq_len, d = x.shape
    n_ctx = ctx.shape[1]
    depth = w_mod.shape[0]
    assert d == D_MODEL and n_ctx == CHUNK and seq_len % CHUNK == 0 and seq_len >= BAND
    assert bsz + 1 <= SUBLANES and seq_len % GRID_W == 0

    n_lat_rows, n_ctx_rows = bsz * seq_len, bsz * n_ctx
    big = 512 * FFN_SUBTILES
    tm = 512 if (n_ctx_rows % big == 0 and seq_len % big == 0) else CHUNK
    assert n_ctx_rows % (tm * FFN_SUBTILES) == 0 and seq_len % (tm * FFN_SUBTILES) == 0
    n_lat_tiles, n_ctx_tiles = n_lat_rows // tm, n_ctx_rows // tm
    tiles_per_batch = seq_len // tm
    geo = {
        "batch": bsz, "seq": seq_len, "ctx": n_ctx, "tm": tm,
        "lat_chunks": seq_len // CHUNK,
        "n_lat_tiles": n_lat_tiles, "n_ctx_tiles": n_ctx_tiles,
        "mod_row": lambda i: jnp.where(i < n_lat_tiles, i // tiles_per_batch, bsz),
        "table_block": lambda i: jnp.where(
            i < n_lat_tiles, i % tiles_per_batch, tiles_per_batch + i - n_lat_tiles),
    }

    cs = jnp.concatenate(
        [c, c_ctx[None, :], jnp.zeros((SUBLANES - bsz - 1, d), F32)], axis=0)
    mod = _modulation(cs, w_mod, b_mod).reshape(depth, SUBLANES, MOD_CHUNKS, d)

    tables = _rope_tables(seq_len, n_ctx_rows)

    w_gate = jnp.concatenate([lru_wa, lru_wx], axis=-1).astype(BF16)
    b_gate = 0.5 * jnp.concatenate(
        [lru_ba.reshape(depth, 2, N_RNN_BLOCKS, 1, RNN_BLOCK_W),
         lru_bx.reshape(depth, 2, N_RNN_BLOCKS, 1, RNN_BLOCK_W)], axis=-1)
    lam = lru_lam.reshape(depth, 2, N_RNN_BLOCKS, 1, RNN_BLOCK_W)
    conv_w_half, conv_b_half = 0.5 * conv_w, 0.5 * conv_b

    stream = (x.reshape(n_lat_rows, d), ctx.reshape(n_ctx_rows, d), 0)
    ones = lambda n: jnp.ones((1, n), F32)
    halves = lambda n: jnp.full((1, n), 0.5, F32)
    gl_cols = 2 * d
    w_in_b = _to_bf16(w_in, 256, jnp.concatenate(
        [ones(w_in.shape[-1] - gl_cols), halves(gl_cols)], axis=1))
    w_o_rnn_b = _to_bf16(w_o_rnn, 512, halves(d))
    w_o_attn_b = _to_bf16(w_o_attn, 512, ones(d))
    w_out_b = _to_bf16(w_out, 512, halves(d))
    w_ffn_in_b = _to_bf16(w_ffn_in, 256, jnp.concatenate([halves(D_FF), ones(D_FF)], axis=1))
    w_ffn_out_b = _to_bf16(w_ffn_out, D_FF // 4, ones(d))
    for l in range(depth):
        need_ctx = l < depth - 1
        n_out_tiles = n_lat_tiles + n_ctx_tiles if need_ctx else n_lat_tiles
        xr, gr, q, k, vt, gl = _in_proj(stream, mod[l], g_mix_pre[l], w_in_b, l, tables, geo)
        hf, hb = _rnn_branch(xr, conv_w_half[l], conv_b_half[l], w_gate[l], b_gate[l], lam[l], geo)
        o_stream = _attention(q, k, vt, attn_sink[l], geo, need_ctx)
        x_all = _merge(stream, hf, hb, gr, o_stream, gl, mod[l], g_mix_post[l],
                       w_o_rnn_b, w_o_attn_b, w_out_b, l, geo, n_out_tiles)
        x_all = _ffn(x_all, mod[l], g_ffn_pre[l], g_ffn_post[l],
                     w_ffn_in_b, w_ffn_out_b, l, geo, n_out_tiles)
        stream = (x_all, x_all, n_lat_tiles)
    return x_all[:n_lat_rows].reshape(bsz, seq_len, d)
```

```python
import functools
import math

import jax
import jax.numpy as jnp
from jax import lax
from jax.experimental import pallas as pl
from jax.experimental.pallas import tpu as pltpu

D_MODEL = 1024
HEAD_DIM = 128
N_Q_HEADS = 8
N_KV_HEADS = 2
Q_PER_KV = N_Q_HEADS // N_KV_HEADS
KV_WIDTH = N_KV_HEADS * HEAD_DIM
WINDOW = 128
GRID_W = 64
N_FREQ = HEAD_DIM // 4
ROPE_BASE = 10000.0
N_RNN_BLOCKS = 8
RNN_BLOCK_W = D_MODEL // N_RNN_BLOCKS
LRU_C = 8.0
CONV_W = 4
CONV_LEFT = 2
D_FF = 2816
EPS = 1e-6
NEG_INF = -1e30
MOD_CHUNKS = 6

LANES = 128
SUBLANES = 8
VMEM_LIMIT_BYTES = 56 * 1024 * 1024

CHUNK = 256
CONV_T = 16
SCAN_UNROLL = 8
MERGE_PARTS = 1
FFN_CHUNK = 768
FFN_SUBTILES = 2
Q_BLOCK = 128
Q_SLOTS = 2
KEY_BLOCK = 128
BAND = Q_BLOCK + 2 * WINDOW
LOG2_E = math.log2(math.e)

BF16 = jnp.bfloat16
F32 = jnp.float32
F32_TINY = float(jnp.finfo(jnp.float32).tiny)


def _dot(a, b):
    return jnp.dot(a, b, preferred_element_type=F32)


def _sigmoid(x):
    return 0.5 * jnp.tanh(0.5 * x) + 0.5


GELU_C1 = math.sqrt(2.0 / math.pi)
GELU_C2 = 0.044715 * GELU_C1


def _rms(x):
    return x * lax.rsqrt(jnp.mean(x * x, axis=-1, keepdims=True) + EPS)


def _resident(shape):
    nd = len(shape)
    return pl.BlockSpec(shape, lambda *_: (0,) * nd, pipeline_mode=pl.Buffered(1))


def _resident_layer(stacked, layer):
    tail = stacked.shape[1:]
    return pl.BlockSpec((None,) + tail, lambda *_: (layer,) + (0,) * len(tail),
                        pipeline_mode=pl.Buffered(1))


def _cast_kernel(w_ref, scale_ref, o_ref):
    o_ref[...] = (w_ref[...] * scale_ref[...]).astype(BF16)


def _to_bf16(w, row_block, col_scale):
    depth, rows, cols = w.shape
    spec = pl.BlockSpec((None, row_block, cols), lambda l, r: (l, r, 0))
    return pl.pallas_call(
        _cast_kernel,
        grid=(depth, rows // row_block),
        in_specs=[spec, pl.BlockSpec((1, cols), lambda l, r: (0, 0))],
        out_specs=spec,
        out_shape=jax.ShapeDtypeStruct(w.shape, BF16),
        compiler_params=pltpu.CompilerParams(
            dimension_semantics=("arbitrary", "arbitrary"),
            vmem_limit_bytes=VMEM_LIMIT_BYTES),
        name="cast_bf16",
    )(w, col_scale)


def _mod_kernel(c_ref, w_ref, b_ref, o_ref):
    c = c_ref[...]
    s = (c * _sigmoid(c)).astype(BF16)
    o_ref[...] = _dot(s, w_ref[...].astype(BF16)) + b_ref[...]


def _modulation(cs, w_mod, b_mod):
    n_layers, d, width = w_mod.shape
    tn = 1536
    return pl.pallas_call(
        _mod_kernel,
        grid=(n_layers, width // tn),
        in_specs=[
            pl.BlockSpec((SUBLANES, d), lambda l, j: (0, 0)),
            pl.BlockSpec((None, d, tn), lambda l, j: (l, 0, j)),
            pl.BlockSpec((None, 1, tn), lambda l, j: (l, 0, j)),
        ],
        out_specs=pl.BlockSpec((None, SUBLANES, tn), lambda l, j: (l, 0, j)),
        out_shape=jax.ShapeDtypeStruct((n_layers, SUBLANES, width), F32),
        compiler_params=pltpu.CompilerParams(
            dimension_semantics=("arbitrary", "arbitrary"),
            vmem_limit_bytes=VMEM_LIMIT_BYTES),
        name="modulation",
    )(cs, w_mod, b_mod.reshape(n_layers, 1, width))


def _rope(t, cos_ref, sin_ref, n_heads):
    cos = cos_ref[...]
    sin = sin_ref[...]
    lane = lax.broadcasted_iota(jnp.int32, cos.shape, 1)
    first = (lane & (2 * N_FREQ - 1)) < N_FREQ
    outs = []
    for h in range(n_heads):
        th = t[:, h * HEAD_DIM:(h + 1) * HEAD_DIM]
        partner = jnp.where(first,
                            pltpu.roll(th, HEAD_DIM - N_FREQ, axis=1),
                            pltpu.roll(th, N_FREQ, axis=1))
        outs.append(th * cos + partner * sin)
    return jnp.concatenate(outs, axis=-1)


def _stream_specs(stream, tm, d, n_lat_tiles):
    _, ctx, ctx_off = stream
    last_ctx = ctx.shape[0] // tm - 1
    lat_spec = pl.BlockSpec((tm, d), lambda i: (jnp.minimum(i, n_lat_tiles - 1), 0))
    ctx_spec = pl.BlockSpec(
        (tm, d), lambda i: (jnp.minimum(jnp.maximum(i - n_lat_tiles, 0) + ctx_off, last_ctx), 0))
    return lat_spec, ctx_spec


def _stream_tile(lat_ref, ctx_ref, n_lat_tiles):
    return jnp.where(pl.program_id(0) < n_lat_tiles, lat_ref[...], ctx_ref[...])


def _in_kernel(xl_ref, xc_ref, mod_ref, g_ref, w_ref, cq_ref, sq_ref, ck_ref, sk_ref,
               xr_ref, gr_ref, q_ref, k_ref, vt_ref, gl_ref, *, n_lat_tiles):
    d = D_MODEL
    x = _stream_tile(xl_ref, xc_ref, n_lat_tiles)
    h = (_rms(x) * g_ref[...]) * (1.0 + mod_ref[1:2, :]) + mod_ref[0:1, :]
    h = h.astype(BF16)
    xr = _dot(h, w_ref[:, 0:d])
    for n in range(N_RNN_BLOCKS):
        xr_ref[pl.ds(n, x.shape[0], stride=N_RNN_BLOCKS), :] = (
            xr[:, n * RNN_BLOCK_W:(n + 1) * RNN_BLOCK_W])
    gr_ref[...] = _dot(h, w_ref[:, d:2 * d]).astype(BF16)
    q = _dot(h, w_ref[:, 2 * d:3 * d])
    q_ref[...] = _rope(q, cq_ref, sq_ref, N_Q_HEADS).astype(BF16)
    k = _dot(h, w_ref[:, 3 * d:3 * d + KV_WIDTH])
    k_ref[...] = _rope(k, ck_ref, sk_ref, N_KV_HEADS).astype(BF16)
    v = _dot(h, w_ref[:, 3 * d + KV_WIDTH:3 * d + 2 * KV_WIDTH])
    for t in range(vt_ref.shape[0]):
        vt_ref[t] = v[t * KEY_BLOCK:(t + 1) * KEY_BLOCK, :].T.astype(BF16)
    gl_ref[...] = _dot(h, w_ref[:, 3 * d + 2 * KV_WIDTH:5 * d + 2 * KV_WIDTH]).astype(BF16)


def _in_proj(stream, mod_l, g_pre, w_in, layer, tables, geo):
    d = D_MODEL
    tm = geo["tm"]
    n_tiles = geo["n_lat_tiles"] + geo["n_ctx_tiles"]
    nt = n_tiles * tm
    row = lambda i: (i, 0)
    tab = pl.BlockSpec((tm, LANES), lambda i: (geo["table_block"](i), 0))
    return pl.pallas_call(
        functools.partial(_in_kernel, n_lat_tiles=geo["n_lat_tiles"]),
        grid=(n_tiles,),
        in_specs=[
            *_stream_specs(stream, tm, d, geo["n_lat_tiles"]),
            pl.BlockSpec((None, MOD_CHUNKS, d), lambda i: (geo["mod_row"](i), 0, 0)),
            _resident((1, d)),
            _resident_layer(w_in, layer),
            tab, tab, tab, tab,
        ],
        out_specs=[
            pl.BlockSpec((tm * N_RNN_BLOCKS, RNN_BLOCK_W), row),
            pl.BlockSpec((tm, d), row),
            pl.BlockSpec((tm, d), row),
            pl.BlockSpec((tm, KV_WIDTH), row),
            pl.BlockSpec((tm // KEY_BLOCK, KV_WIDTH, KEY_BLOCK), lambda i: (i, 0, 0)),
            pl.BlockSpec((tm, 2 * d), row),
        ],
        out_shape=[
            jax.ShapeDtypeStruct((nt * N_RNN_BLOCKS, RNN_BLOCK_W), F32),
            jax.ShapeDtypeStruct((nt, d), BF16),
            jax.ShapeDtypeStruct((nt, d), BF16),
            jax.ShapeDtypeStruct((nt, KV_WIDTH), BF16),
            jax.ShapeDtypeStruct((nt // KEY_BLOCK, KV_WIDTH, KEY_BLOCK), BF16),
            jax.ShapeDtypeStruct((nt, 2 * d), BF16),
        ],
        compiler_params=pltpu.CompilerParams(
            dimension_semantics=("arbitrary",),
            vmem_limit_bytes=VMEM_LIMIT_BYTES),
        name="in_proj",
    )(stream[0], stream[1], mod_l, g_pre.reshape(1, d), w_in, *tables)


def _rnn_kernel(xf_ref, xfp_ref, xfn_ref, xb_ref, xbp_ref, xbn_ref,
                cw_ref, cb_ref, wg_ref, bg_ref, lam_ref,
                hf_ref, hb_ref,
                xc_s, a_s, b_s, carry_s, *, n_steps):
    j = pl.program_id(1)
    nb, bw = N_RNN_BLOCKS, RNN_BLOCK_W
    n_in = CONV_T + CONV_W - 1
    right = CONV_W - 1 - CONV_LEFT

    @pl.when(j == 0)
    def _():
        carry_s[...] = jnp.zeros_like(carry_s)

    def conv(slot, x_ref, xp_ref, xn_ref, has_prev, has_next):
        def conv_piece(v, base):
            v = v.reshape(n_in, nb, bw)
            acc = cb_ref[...] + v[0:CONV_T] * cw_ref[0]
            for k in range(1, CONV_W):
                acc = acc + v[k:k + CONV_T] * cw_ref[k]
            xc_s[slot, pl.ds(base, CONV_T * nb), :] = acc.reshape(CONV_T * nb, bw)

        halo_rows = xp_ref.shape[0]
        left = jnp.where(has_prev, xp_ref[halo_rows - CONV_LEFT * nb:halo_rows, :], 0.0)
        conv_piece(jnp.concatenate([left, x_ref[0:(n_in - CONV_LEFT) * nb, :]], axis=0), 0)
        tail = jnp.where(has_next, xn_ref[0:right * nb, :], 0.0)
        last = CHUNK - CONV_T
        conv_piece(jnp.concatenate(
            [x_ref[(last - CONV_LEFT) * nb:CHUNK * nb, :], tail], axis=0), last * nb)

        def conv_body(c, carry):
            base = pl.multiple_of(c * (CONV_T * nb), CONV_T * nb)
            conv_piece(x_ref[pl.ds(base - CONV_LEFT * nb, n_in * nb), :], base)
            return carry

        lax.fori_loop(1, CHUNK // CONV_T - 1, conv_body, 0)

    fwd_chunk = j
    bwd_chunk = jnp.where(j == 0, 0, n_steps - j)
    mid = jnp.logical_and(j >= 1, j < n_steps - 1)

    @pl.when(2 * j <= n_steps)
    def _():
        conv(fwd_chunk, xf_ref, xfp_ref, xfn_ref, j >= 2, mid)

    @pl.when(jnp.logical_and(j >= 1, 2 * j < n_steps))
    def _():
        conv(bwd_chunk, xb_ref, xbp_ref, xbn_ref, mid, j >= 2)

    def gate_unit(parity, direction, n):
        rows = pl.ds(n, CHUNK, stride=nb)
        lam = lam_ref[direction, n]
        c_half = (0.5 * LRU_C * LOG2_E) * (
            jnp.minimum(lam, 0.0) - jnp.log1p(jnp.exp(-jnp.abs(lam))))
        xh = xc_s[bwd_chunk if direction else fwd_chunk, rows, :]
        z = _dot(xh.astype(BF16), wg_ref[direction, n]) + bg_ref[direction, n]
        a = jnp.exp2(c_half * jnp.tanh(z[:, 0:bw]) + c_half)
        gated = xh * jnp.tanh(z[:, bw:2 * bw]) + xh
        om = 1.0 - a * a
        a_s[parity, direction, rows, :] = a
        b_s[parity, direction, rows, :] = (om * lax.rsqrt(jnp.maximum(om, F32_TINY))) * gated

    def scan_steps(parity, t0, t1, carry):
        hf, hb = carry
        for t in range(t0, t1):
            tf, tb = t * nb, (CHUNK - 1 - t) * nb
            hf = a_s[parity, 0, tf:tf + nb, :] * hf + b_s[parity, 0, tf:tf + nb, :]
            hb = a_s[parity, 1, tb:tb + nb, :] * hb + b_s[parity, 1, tb:tb + nb, :]
            hf_ref[tf:tf + nb, :] = hf
            hb_ref[tb:tb + nb, :] = hb
        return hf, hb

    def work(parity, do_gates, do_scan):
        units = [(direction, n) for direction in range(2) for n in range(nb)]
        per_unit = CHUNK // len(units)
        carry = (carry_s[0], carry_s[1]) if do_scan else None
        for idx, (direction, n) in enumerate(units):
            if do_gates:
                gate_unit(parity, direction, n)
            if do_scan:
                carry = scan_steps(1 - parity, idx * per_unit, (idx + 1) * per_unit, carry)
        if do_scan:
            carry_s[0], carry_s[1] = carry

    last_parity = (n_steps - 1) % 2
    pl.when(j == 0)(lambda: work(0, True, False))
    pl.when(j == n_steps)(lambda: work(1 - last_parity, False, True))
    inner = jnp.logical_and(j >= 1, j < n_steps)
    pl.when(jnp.logical_and(inner, j % 2 == 0))(lambda: work(0, True, True))
    pl.when(jnp.logical_and(inner, j % 2 == 1))(lambda: work(1, True, True))


def _rnn_branch(xr, conv_w, conv_b, w_gate, b_gate, lam, geo):
    nb, bw = N_RNN_BLOCKS, RNN_BLOCK_W
    bsz, n_lat = geo["batch"], geo["lat_chunks"]
    n_steps = n_lat + 1
    halo = SUBLANES
    n_halo = xr.shape[0] // (halo * nb)
    per = CHUNK // halo

    def fwd_blk(b, s):
        return jnp.where(s == 0, bsz * n_lat + b, b * n_lat + s - 1)

    def bwd_blk(b, s):
        return jnp.where(s == 0, bsz * n_lat + b, b * n_lat + n_lat - s)

    in_step = lambda j: jnp.minimum(j, n_steps - 1)
    out_step = lambda j: jnp.maximum(j - 1, 0)

    def chunk(blk, step):
        return pl.BlockSpec((CHUNK * nb, bw), lambda b, j: (blk(b, step(j)), 0))

    def prev(blk):
        return pl.BlockSpec(
            (halo * nb, bw), lambda b, j: (jnp.maximum(blk(b, in_step(j)) * per - 1, 0), 0))

    def nxt(blk):
        return pl.BlockSpec(
            (halo * nb, bw),
            lambda b, j: (jnp.minimum((blk(b, in_step(j)) + 1) * per, n_halo - 1), 0))

    return pl.pallas_call(
        functools.partial(_rnn_kernel, n_steps=n_steps),
        grid=(bsz, n_steps + 1),
        in_specs=[
            chunk(fwd_blk, in_step), prev(fwd_blk), nxt(fwd_blk),
            chunk(bwd_blk, in_step), prev(bwd_blk), nxt(bwd_blk),
            _resident((CONV_W, nb, bw)),
            _resident((nb, bw)),
            _resident(w_gate.shape),
            _resident(b_gate.shape),
            _resident(lam.shape),
        ],
        out_specs=[chunk(fwd_blk, out_step), chunk(bwd_blk, out_step)],
        out_shape=[jax.ShapeDtypeStruct(xr.shape, F32)] * 2,
        scratch_shapes=[
            pltpu.VMEM((n_steps, CHUNK * nb, bw), F32),
            pltpu.VMEM((2, 2, CHUNK * nb, bw), F32),
            pltpu.VMEM((2, 2, CHUNK * nb, bw), F32),
            pltpu.VMEM((2, nb, bw), F32),
        ],
        compiler_params=pltpu.CompilerParams(
            dimension_semantics=("arbitrary", "arbitrary"),
            vmem_limit_bytes=VMEM_LIMIT_BYTES),
        name="rglru",
    )(xr, xr, xr, xr, xr, xr, conv_w.reshape(CONV_W, nb, bw), conv_b.reshape(nb, bw),
      w_gate, b_gate, lam)


def _stack_heads(q, g):
    base = g * Q_PER_KV * HEAD_DIM
    return jnp.concatenate(
        [q[:, base + h * HEAD_DIM: base + (h + 1) * HEAD_DIM] for h in range(Q_PER_KV)], axis=0)


def _nt_dot(a, b):
    return lax.dot_general(a, b, (((1,), (1,)), ((), ())), preferred_element_type=F32)


def _attend_units(sink_ref, units):
    folds = KEY_BLOCK // SUBLANES
    for u in units:
        n_q = u["q"].shape[0]
        width = Q_PER_KV * n_q
        qs = _stack_heads(u["q"], u["g"])
        sink = jnp.concatenate(
            [jnp.full((1, n_q), sink_ref[u["g"] * Q_PER_KV + h] * LOG2_E, F32)
             for h in range(Q_PER_KV)], axis=1)
        m8 = jnp.broadcast_to(sink, (SUBLANES, width))
        for j, (k, bias) in enumerate(u["key_blocks"]):
            s = _nt_dot(k, qs)
            if bias is not None:
                s = s + jnp.concatenate([bias] * Q_PER_KV, axis=1)
            u["s_s"][j * KEY_BLOCK:(j + 1) * KEY_BLOCK, :] = s
            m8 = jnp.maximum(m8, jnp.max(s.reshape(folds, SUBLANES, width), axis=0))
        u["sink"] = sink
        u["m"] = jnp.max(m8, axis=0, keepdims=True)
    for u in units:
        width = Q_PER_KV * u["q"].shape[0]
        l8 = jnp.zeros((SUBLANES, width), F32)
        for j in range(len(u["key_blocks"])):
            p = jnp.exp2(u["s_s"][j * KEY_BLOCK:(j + 1) * KEY_BLOCK, :] - u["m"])
            l8 = l8 + jnp.sum(p.reshape(folds, SUBLANES, width), axis=0)
            u["p_s"][j * KEY_BLOCK:(j + 1) * KEY_BLOCK, :] = p.astype(BF16)
        u["denom"] = jnp.sum(l8, axis=0, keepdims=True) + jnp.exp2(u["sink"] - u["m"])
    for u in units:
        n_q = u["q"].shape[0]
        n_keys = len(u["key_blocks"]) * KEY_BLOCK
        ot = _dot(u["vt"], u["p_s"][0:n_keys, :]) * (1.0 / u["denom"])
        for h in range(Q_PER_KV):
            u["o_store"](h, ot[:, h * n_q:(h + 1) * n_q].T)


def _lat_attn_kernel(sink_ref, q_ref, k_ref, vt_ref, kc_ref, vtc_ref, o_ref, s_s, p_s,
                     *, seq_len):
    n_blk = q_ref.shape[0] // Q_BLOCK
    band_blocks = BAND // KEY_BLOCK
    ctx_blocks = vtc_ref.shape[0]
    last_start = seq_len // KEY_BLOCK - band_blocks
    rel = (lax.broadcasted_iota(jnp.int32, (KEY_BLOCK, Q_BLOCK), 0)
           - lax.broadcasted_iota(jnp.int32, (KEY_BLOCK, Q_BLOCK), 1))

    def block_units(i, slot):
        units = []
        qb = pl.program_id(1) * n_blk + i
        jb0 = jnp.clip(qb - WINDOW // KEY_BLOCK, 0, last_start)
        biases = [jnp.where(jnp.abs(rel + (jb0 + t - qb) * KEY_BLOCK) <= WINDOW, 0.0, NEG_INF)
                  for t in range(band_blocks)]
        rows = pl.ds(pl.multiple_of(i * Q_BLOCK, Q_BLOCK), Q_BLOCK)
        q = q_ref[rows, :]
        for g in range(N_KV_HEADS):
            gs = slice(g * HEAD_DIM, (g + 1) * HEAD_DIM)
            key_blocks = [
                (k_ref[pl.ds(pl.multiple_of((jb0 + t) * KEY_BLOCK, KEY_BLOCK), KEY_BLOCK), gs],
                 biases[t]) for t in range(band_blocks)]
            key_blocks += [(kc_ref[t * KEY_BLOCK:(t + 1) * KEY_BLOCK, gs], None)
                           for t in range(ctx_blocks)]
            vt = jnp.concatenate([vt_ref[jb0 + t, gs, :] for t in range(band_blocks)]
                                 + [vtc_ref[t, gs, :] for t in range(ctx_blocks)], axis=1)

            def o_store(h, tile, g=g, rows=rows):
                c0 = (g * Q_PER_KV + h) * HEAD_DIM
                o_ref[rows, c0:c0 + HEAD_DIM] = tile.astype(BF16)

            units.append(dict(g=g, q=q, key_blocks=key_blocks, vt=vt,
                              s_s=s_s.at[slot, g], p_s=p_s.at[slot, g], o_store=o_store))
        return units

    def body(it, carry):
        units = []
        for slot in range(Q_SLOTS):
            units += block_units(it * Q_SLOTS + slot, slot)
        _attend_units(sink_ref, units)
        return carry

    lax.fori_loop(0, n_blk // Q_SLOTS, body, 0)


def _ctx_attn_kernel(sink_ref, q_ref, kc_ref, vtc_ref, o_ref, s_s, p_s):
    ctx_blocks = vtc_ref.shape[0]
    q = q_ref[...]
    units = []
    for g in range(N_KV_HEADS):
        gs = slice(g * HEAD_DIM, (g + 1) * HEAD_DIM)
        key_blocks = [(kc_ref[t * KEY_BLOCK:(t + 1) * KEY_BLOCK, gs], None)
                      for t in range(ctx_blocks)]
        vt = jnp.concatenate([vtc_ref[t, gs, :] for t in range(ctx_blocks)], axis=1)

        def o_store(h, tile, g=g):
            c0 = (g * Q_PER_KV + h) * HEAD_DIM
            o_ref[:, c0:c0 + HEAD_DIM] = tile.astype(BF16)

        units.append(dict(g=g, q=q, key_blocks=key_blocks, vt=vt,
                          s_s=s_s.at[g], p_s=p_s.at[g], o_store=o_store))
    _attend_units(sink_ref, units)


def _attention(q, k, vt, sink, geo, with_ctx_queries):
    nt, d = q.shape
    bsz, seq_len, n_ctx = geo["batch"], geo["seq"], geo["ctx"]
    q_sup = next(s for s in (1024, 512, 256, Q_BLOCK) if seq_len % s == 0)
    n_sup = seq_len // q_sup
    ctx_blk0 = bsz * seq_len // n_ctx
    smem = pl.BlockSpec(memory_space=pltpu.SMEM)
    ctx_keys = pl.BlockSpec((n_ctx, KV_WIDTH), lambda b, *_: (ctx_blk0 + b, 0))
    ctx_vals = pl.BlockSpec((n_ctx // KEY_BLOCK, KV_WIDTH, KEY_BLOCK),
                            lambda b, *_: (ctx_blk0 + b, 0, 0))
    o = pl.pallas_call(
        functools.partial(_lat_attn_kernel, seq_len=seq_len),
        grid=(bsz, n_sup),
        in_specs=[
            smem,
            pl.BlockSpec((q_sup, d), lambda b, i: (b * n_sup + i, 0)),
            pl.BlockSpec((seq_len, KV_WIDTH), lambda b, i: (b, 0)),
            pl.BlockSpec((seq_len // KEY_BLOCK, KV_WIDTH, KEY_BLOCK), lambda b, i: (b, 0, 0)),
            ctx_keys, ctx_vals,
        ],
        out_specs=pl.BlockSpec((q_sup, d), lambda b, i: (b * n_sup + i, 0)),
        out_shape=jax.ShapeDtypeStruct((bsz * seq_len, d), BF16),
        scratch_shapes=[
            pltpu.VMEM((Q_SLOTS, N_KV_HEADS, BAND + n_ctx, Q_PER_KV * Q_BLOCK), F32),
            pltpu.VMEM((Q_SLOTS, N_KV_HEADS, BAND + n_ctx, Q_PER_KV * Q_BLOCK), BF16),
        ],
        compiler_params=pltpu.CompilerParams(
            dimension_semantics=("arbitrary", "arbitrary"),
            vmem_limit_bytes=VMEM_LIMIT_BYTES),
        name="lat_attention",
    )(sink, q, k, vt, k, vt)
    if not with_ctx_queries:
        return (o, o, 0)
    o_ctx = pl.pallas_call(
        _ctx_attn_kernel,
        grid=(bsz,),
        in_specs=[
            smem,
            pl.BlockSpec((n_ctx, d), lambda b: (ctx_blk0 + b, 0)),
            ctx_keys, ctx_vals,
        ],
        out_specs=pl.BlockSpec((n_ctx, d), lambda b: (b, 0)),
        out_shape=jax.ShapeDtypeStruct((bsz * n_ctx, d), BF16),
        scratch_shapes=[
            pltpu.VMEM((N_KV_HEADS, n_ctx, Q_PER_KV * n_ctx), F32),
            pltpu.VMEM((N_KV_HEADS, n_ctx, Q_PER_KV * n_ctx), BF16),
        ],
        compiler_params=pltpu.CompilerParams(
            dimension_semantics=("arbitrary",),
            vmem_limit_bytes=VMEM_LIMIT_BYTES),
        name="ctx_attention",
    )(sink, q, k, vt)
    return (o, o_ctx, 0)


def _merge_kernel(xl_ref, xc_ref, hf_ref, hb_ref, gr_ref, ol_ref, oc_ref, gl_ref, mod_ref,
                  g_ref, wr_ref, wa_ref, wo_ref, out_ref, *, n_lat_tiles):
    d = D_MODEL
    nb = N_RNN_BLOCKS
    rows_per_part = xl_ref.shape[0] // MERGE_PARTS
    is_lat = pl.program_id(0) < n_lat_tiles
    parts =[slice(p * rows_per_part, (p + 1) * rows_per_part) for p in range(MERGE_PARTS)]

    def rnn_input(p):
        h = jnp.concatenate(
            [hf_ref[pl.ds(p * rows_per_part * nb + n, rows_per_part, stride=nb), :]
             + hb_ref[pl.ds(p * rows_per_part * nb + n, rows_per_part, stride=nb), :]
             for n in range(nb)], axis=-1)
        g = gr_ref[parts[p], :].astype(F32)
        hg = h * g
        return (hg * jnp.tanh(g * (GELU_C1 + GELU_C2 * (g * g))) + hg).astype(BF16)

    def gated_mix(p, y):
        r = parts[p]
        ya = _dot(y, wr_ref[...])
        yb = _dot(jnp.where(is_lat, ol_ref[r, :], oc_ref[r, :]), wa_ref[...])
        ta = jnp.tanh(gl_ref[r, 0:d].astype(F32))
        tb = jnp.tanh(gl_ref[r, d:2 * d].astype(F32))
        return ((ya * ta + ya) + (yb * tb + yb)).astype(BF16)

    def project(p, mix):
        r = parts[p]
        m = _dot(mix, wo_ref[...])
        x = jnp.where(is_lat, xl_ref[r, :], xc_ref[r, :])
        out_ref[r, :] = x + mod_ref[2:3, :] * (_rms(m) * g_ref[...])

    ys = [rnn_input(p) for p in range(MERGE_PARTS)]
    mixes = [gated_mix(p, ys[p]) for p in range(MERGE_PARTS)]
    for p in range(MERGE_PARTS):
        project(p, mixes[p])


def _merge(x_stream, hf, hb, gr, o_stream, gl, mod_l, g_post, w_o_rnn, w_o_attn, w_out,
           layer, geo, n_tiles):
    d = D_MODEL
    tm = geo["tm"]
    n_lat_tiles = geo["n_lat_tiles"]
    row = lambda i: (i, 0)
    tile = pl.BlockSpec((tm, d), row)
    slab = pl.BlockSpec((tm * N_RNN_BLOCKS, RNN_BLOCK_W), row)
    return pl.pallas_call(
        functools.partial(_merge_kernel, n_lat_tiles=n_lat_tiles),
        grid=(n_tiles,),
        in_specs=[
            *_stream_specs(x_stream, tm, d, n_lat_tiles), slab, slab, tile,
            *_stream_specs(o_stream, tm, d, n_lat_tiles),
            pl.BlockSpec((tm, 2 * d), row),
            pl.BlockSpec((None, MOD_CHUNKS, d), lambda i: (geo["mod_row"](i), 0, 0)),
            _resident((1, d)),
            _resident_layer(w_o_rnn, layer), _resident_layer(w_o_attn, layer),
            _resident_layer(w_out, layer),
        ],
        out_specs=tile,
        out_shape=jax.ShapeDtypeStruct((n_tiles * tm, d), F32),
        compiler_params=pltpu.CompilerParams(
            dimension_semantics=("arbitrary",),
            vmem_limit_bytes=VMEM_LIMIT_BYTES),
        name="merge",
    )(x_stream[0], x_stream[1], hf, hb, gr, o_stream[0], o_stream[1], gl, mod_l,
      g_post.reshape(1, d), w_o_rnn, w_o_attn, w_out)


def _ffn_kernel(x_ref, mod_ref, gpre_ref, gpost_ref, w1_ref, w2_ref, out_ref):
    sub = x_ref.shape[0] // FFN_SUBTILES
    bounds = list(range(0, D_FF, FFN_CHUNK)) + [D_FF]
    n_chunks = len(bounds) - 1

    def normed(s):
        x = x_ref[s * sub:(s + 1) * sub, :]
        h = (_rms(x) * gpre_ref[...]) * (1.0 + mod_ref[4:5, :]) + mod_ref[3:4, :]
        return h.astype(BF16)

    def first_layer(h, c):
        lo, hi = bounds[c], bounds[c + 1]
        return _dot(h, w1_ref[:, lo:hi]), _dot(h, w1_ref[:, D_FF + lo:D_FF + hi])

    h_next = normed(0)
    for s in range(FFN_SUBTILES):
        h = h_next
        if s + 1 < FFN_SUBTILES:
            h_next = normed(s + 1)
        f = None
        nxt = first_layer(h, 0)
        for c in range(n_chunks):
            half_gate, up = nxt
            if c + 1 < n_chunks:
                nxt = first_layer(h, c + 1)
            act = ((half_gate * jnp.tanh(half_gate) + half_gate) * up).astype(BF16)
            part = _dot(act, w2_ref[bounds[c]:bounds[c + 1], :])
            f = part if f is None else f + part
        rows = slice(s * sub, (s + 1) * sub)
        out_ref[rows, :] = x_ref[rows, :] + mod_ref[5:6, :] * (_rms(f) * gpost_ref[...])


def _ffn(x_all, mod_l, g_pre, g_post, w1, w2, layer, geo, n_tiles):
    d = x_all.shape[1]
    tm = geo["tm"] * FFN_SUBTILES
    n_blocks = n_tiles // FFN_SUBTILES
    tile = pl.BlockSpec((tm, d), lambda i: (i, 0))
    return pl.pallas_call(
        _ffn_kernel,
        grid=(n_blocks,),
        in_specs=[
            tile,
            pl.BlockSpec((None, MOD_CHUNKS, d),
                         lambda i: (geo["mod_row"](i * FFN_SUBTILES), 0, 0)),
            _resident((1, d)), _resident((1, d)),
            _resident_layer(w1, layer), _resident_layer(w2, layer),
        ],
        out_specs=tile,
        out_shape=jax.ShapeDtypeStruct((n_blocks * tm, d), F32),
        compiler_params=pltpu.CompilerParams(
            dimension_semantics=("arbitrary",),
            vmem_limit_bytes=VMEM_LIMIT_BYTES),
        name="ffn",
    )(x_all, mod_l, g_pre.reshape(1, d), g_post.reshape(1, d), w1, w2)


def _rope_tables(seq_len, pad_rows):
    n_rows = seq_len // GRID_W
    inv = ROPE_BASE ** (-jnp.arange(N_FREQ, dtype=F32) / N_FREQ)
    ang = jnp.arange(max(n_rows, GRID_W), dtype=F32)[:, None] * inv[None, :]
    cos_u, sin_u = jnp.cos(ang), jnp.sin(ang)
    by_row = lambda t: jnp.repeat(t[:n_rows], GRID_W, axis=0)
    by_col = lambda t: jnp.tile(t[:GRID_W], (n_rows, 1))
    cos = jnp.concatenate([by_row(cos_u)] * 2 + [by_col(cos_u)] * 2, axis=-1)
    sin = jnp.concatenate(
        [-by_row(sin_u), by_row(sin_u), -by_col(sin_u), by_col(sin_u)], axis=-1)
    cos = jnp.concatenate([cos, jnp.ones((pad_rows, HEAD_DIM), F32)], axis=0)
    sin = jnp.concatenate([sin, jnp.zeros((pad_rows, HEAD_DIM), F32)], axis=0)
    scale = HEAD_DIM ** -0.5 * LOG2_E
    return cos * scale, sin * scale, cos, sin


def kernel(x, c, ctx, c_ctx, w_mod, b_mod, g_mix_pre, g_mix_post, g_ffn_pre, g_ffn_post, w_in, conv_w, conv_b, lru_wa, lru_ba, lru_wx, lru_bx, lru_lam, attn_sink, w_o_rnn, w_o_attn, w_out, w_ffn_in, w_ffn_out):
    bsz, seq_len, d = x.shape
    n_ctx = ctx.shape[1]
    depth = w_mod.shape[0]
    assert d == D_MODEL and n_ctx == CHUNK and seq_len % CHUNK == 0 and seq_len >= BAND
    assert bsz + 1 <= SUBLANES and seq_len % GRID_W == 0

    n_lat_rows, n_ctx_rows = bsz * seq_len, bsz * n_ctx
    big = 512 * FFN_SUBTILES
    tm = 512 if (n_ctx_rows % big == 0 and seq_len % big == 0) else CHUNK
    assert n_ctx_rows % (tm * FFN_SUBTILES) == 0 and seq_len % (tm * FFN_SUBTILES) == 0
    n_lat_tiles, n_ctx_tiles = n_lat_rows // tm, n_ctx_rows // tm
    tiles_per_batch = seq_len // tm
    geo = {
        "batch": bsz, "seq": seq_len, "ctx": n_ctx, "tm": tm,
        "lat_chunks": seq_len // CHUNK,
        "n_lat_tiles": n_lat_tiles, "n_ctx_tiles": n_ctx_tiles,
        "mod_row": lambda i: jnp.where(i < n_lat_tiles, i // tiles_per_batch, bsz),
        "table_block": lambda i: jnp.where(
            i < n_lat_tiles, i % tiles_per_batch, tiles_per_batch + i - n_lat_tiles),
    }

    cs = jnp.concatenate(
        [c, c_ctx[None, :], jnp.zeros((SUBLANES - bsz - 1, d), F32)], axis=0)
    mod = _modulation(cs, w_mod, b_mod).reshape(depth, SUBLANES, MOD_CHUNKS, d)

    tables = _rope_tables(seq_len, n_ctx_rows)

    w_gate = jnp.concatenate([lru_wa, lru_wx], axis=-1).astype(BF16)
    b_gate = 0.5 * jnp.concatenate(
        [lru_ba.reshape(depth, 2, N_RNN_BLOCKS, 1, RNN_BLOCK_W),
         lru_bx.reshape(depth, 2, N_RNN_BLOCKS, 1, RNN_BLOCK_W)], axis=-1)
    lam = lru_lam.reshape(depth, 2, N_RNN_BLOCKS, 1, RNN_BLOCK_W)
    conv_w_half, conv_b_half = 0.5 * conv_w, 0.5 * conv_b

    stream = (x.reshape(n_lat_rows, d), ctx.reshape(n_ctx_rows, d), 0)
    ones = lambda n: jnp.ones((1, n), F32)
    halves = lambda n: jnp.full((1, n), 0.5, F32)
    gl_cols = 2 * d
    w_in_b = _to_bf16(w_in, 256, jnp.concatenate(
        [ones(w_in.shape[-1] - gl_cols), halves(gl_cols)], axis=1))
    w_o_rnn_b = _to_bf16(w_o_rnn, 512, halves(d))
    w_o_attn_b = _to_bf16(w_o_attn, 512, ones(d))
    w_out_b = _to_bf16(w_out, 512, halves(d))
    w_ffn_in_b = _to_bf16(w_ffn_in, 256, jnp.concatenate([halves(D_FF), ones(D_FF)], axis=1))
    w_ffn_out_b = _to_bf16(w_ffn_out, D_FF // 4, ones(d))
    for l in range(depth):
        need_ctx = l < depth - 1
        n_out_tiles = n_lat_tiles + n_ctx_tiles if need_ctx else n_lat_tiles
        xr, gr, q, k, vt, gl = _in_proj(stream, mod[l], g_mix_pre[l], w_in_b, l, tables, geo)
        hf, hb = _rnn_branch(xr, conv_w_half[l], conv_b_half[l], w_gate[l], b_gate[l], lam[l], geo)
        o_stream = _attention(q, k, vt, attn_sink[l], geo, need_ctx)
        x_all = _merge(stream, hf, hb, gr, o_stream, gl, mod[l], g_mix_post[l],
                       w_o_rnn_b, w_o_attn_b, w_out_b, l, geo, n_out_tiles)
        x_all = _ffn(x_all, mod[l], g_ffn_pre[l], g_ffn_post[l],
                     w_ffn_in_b, w_ffn_out_b, l, geo, n_out_tiles)
        stream = (x_all, x_all, n_lat_tiles)
    return x_all[:n_lat_rows].reshape(bsz, seq_len, d)
```

```python
import functools
import math

import jax
import jax.numpy as jnp
from jax import lax
from jax.experimental import pallas as pl
from jax.experimental.pallas import tpu as pltpu

D_MODEL = 1024
HEAD_DIM = 128
N_Q_HEADS = 8
N_KV_HEADS = 2
Q_PER_KV = N_Q_HEADS // N_KV_HEADS
KV_WIDTH = N_KV_HEADS * HEAD_DIM
WINDOW = 128
GRID_W = 64
N_FREQ = HEAD_DIM // 4
ROPE_BASE = 10000.0
N_RNN_BLOCKS = 8
RNN_BLOCK_W = D_MODEL // N_RNN_BLOCKS
LRU_C = 8.0
CONV_W = 4
CONV_LEFT = 2
D_FF = 2816
EPS = 1e-6
NEG_INF = -1e30
MOD_CHUNKS = 6

LANES = 128
SUBLANES = 8
VMEM_LIMIT_BYTES = 56 * 1024 * 1024

CHUNK = 256
CONV_T = 16
SCAN_UNROLL = 8
MERGE_PARTS = 1
FFN_CHUNK = 768
FFN_SUBTILES = 2
Q_BLOCK = 128
Q_SLOTS = 4
KEY_BLOCK = 128
BAND = Q_BLOCK + 2 * WINDOW
LOG2_E = math.log2(math.e)

BF16 = jnp.bfloat16
F32 = jnp.float32
F32_TINY = float(jnp.finfo(jnp.float32).tiny)


def _dot(a, b):
    return jnp.dot(a, b, preferred_element_type=F32)


def _sigmoid(x):
    return 0.5 * jnp.tanh(0.5 * x) + 0.5


GELU_C1 = math.sqrt(2.0 / math.pi)
GELU_C2 = 0.044715 * GELU_C1


def _rms(x):
    return x * lax.rsqrt(jnp.mean(x * x, axis=-1, keepdims=True) + EPS)


def _resident(shape):
    nd = len(shape)
    return pl.BlockSpec(shape, lambda *_: (0,) * nd, pipeline_mode=pl.Buffered(1))


def _resident_layer(stacked, layer):
    tail = stacked.shape[1:]
    return pl.BlockSpec((None,) + tail, lambda *_: (layer,) + (0,) * len(tail),
                        pipeline_mode=pl.Buffered(1))


def _cast_kernel(w_ref, scale_ref, o_ref):
    o_ref[...] = (w_ref[...] * scale_ref[...]).astype(BF16)


def _to_bf16(w, row_block, col_scale):
    depth, rows, cols = w.shape
    spec = pl.BlockSpec((None, row_block, cols), lambda l, r: (l, r, 0))
    return pl.pallas_call(
        _cast_kernel,
        grid=(depth, rows // row_block),
        in_specs=[spec, pl.BlockSpec((1, cols), lambda l, r: (0, 0))],
        out_specs=spec,
        out_shape=jax.ShapeDtypeStruct(w.shape, BF16),
        compiler_params=pltpu.CompilerParams(
            dimension_semantics=("arbitrary", "arbitrary"),
            vmem_limit_bytes=VMEM_LIMIT_BYTES),
        name="cast_bf16",
    )(w, col_scale)


def _mod_kernel(c_ref, w_ref, b_ref, o_ref):
    c = c_ref[...]
    s = (c * _sigmoid(c)).astype(BF16)
    o_ref[...] = _dot(s, w_ref[...].astype(BF16)) + b_ref[...]


def _modulation(cs, w_mod, b_mod):
    n_layers, d, width = w_mod.shape
    tn = 1536
    return pl.pallas_call(
        _mod_kernel,
        grid=(n_layers, width // tn),
        in_specs=[
            pl.BlockSpec((SUBLANES, d), lambda l, j: (0, 0)),
            pl.BlockSpec((None, d, tn), lambda l, j: (l, 0, j)),
            pl.BlockSpec((None, 1, tn), lambda l, j: (l, 0, j)),
        ],
        out_specs=pl.BlockSpec((None, SUBLANES, tn), lambda l, j: (l, 0, j)),
        out_shape=jax.ShapeDtypeStruct((n_layers, SUBLANES, width), F32),
        compiler_params=pltpu.CompilerParams(
            dimension_semantics=("arbitrary", "arbitrary"),
            vmem_limit_bytes=VMEM_LIMIT_BYTES),
        name="modulation",
    )(cs, w_mod, b_mod.reshape(n_layers, 1, width))


def _rope(t, cos_ref, sin_ref, n_heads):
    cos = cos_ref[...]
    sin = sin_ref[...]
    lane = lax.broadcasted_iota(jnp.int32, cos.shape, 1)
    first = (lane & (2 * N_FREQ - 1)) < N_FREQ
    outs = []
    for h in range(n_heads):
        th = t[:, h * HEAD_DIM:(h + 1) * HEAD_DIM]
        partner = jnp.where(first,
                            pltpu.roll(th, HEAD_DIM - N_FREQ, axis=1),
                            pltpu.roll(th, N_FREQ, axis=1))
        outs.append(th * cos + partner * sin)
    return jnp.concatenate(outs, axis=-1)


def _stream_specs(stream, tm, d, n_lat_tiles, tile_of_step=lambda i: i):
    _, ctx, ctx_off = stream
    last_ctx = ctx.shape[0] // tm - 1
    lat_spec = pl.BlockSpec(
        (tm, d), lambda i: (jnp.minimum(tile_of_step(i), n_lat_tiles - 1), 0))
    ctx_spec = pl.BlockSpec(
        (tm, d),
        lambda i: (jnp.minimum(jnp.maximum(tile_of_step(i) - n_lat_tiles, 0) + ctx_off, last_ctx),
                   0))
    return lat_spec, ctx_spec


def _stream_tile(lat_ref, ctx_ref, n_lat_tiles, tile=None):
    tile = pl.program_id(0) if tile is None else tile
    return jnp.where(tile < n_lat_tiles, lat_ref[...], ctx_ref[...])


def _in_kernel(x0_ref, mod0_ref, xl_ref, xc_ref, mod_ref, g_ref, w_ref,
               cq_ref, sq_ref, ck_ref, sk_ref,
               xr_ref, gr_ref, q_ref, k_ref, vt_ref, gl_ref, h_s, *, n_lat_tiles, n_tiles):
    d = D_MODEL
    i = pl.program_id(0)

    def normed(x, m_ref):
        return ((_rms(x) * g_ref[...]) * (1.0 + m_ref[1:2, :]) + m_ref[0:1, :]).astype(BF16)

    @pl.when(i == 0)
    def _():
        h_s[0] = normed(x0_ref[...], mod0_ref)

    def project(slot):
        nxt = jnp.minimum(i + 1, n_tiles - 1)
        h_s[1 - slot] = normed(_stream_tile(xl_ref, xc_ref, n_lat_tiles, nxt), mod_ref)
        h = h_s[slot]
        xr = _dot(h, w_ref[:, 0:d])
        for n in range(N_RNN_BLOCKS):
            xr_ref[pl.ds(n, h.shape[0], stride=N_RNN_BLOCKS), :] = (
                xr[:, n * RNN_BLOCK_W:(n + 1) * RNN_BLOCK_W])
        gr_ref[...] = _dot(h, w_ref[:, d:2 * d]).astype(BF16)
        q = _dot(h, w_ref[:, 2 * d:3 * d])
        q_ref[...] = _rope(q, cq_ref, sq_ref, N_Q_HEADS).astype(BF16)
        k = _dot(h, w_ref[:, 3 * d:3 * d + KV_WIDTH])
        k_ref[...] = _rope(k, ck_ref, sk_ref, N_KV_HEADS).astype(BF16)
        v = _dot(h, w_ref[:, 3 * d + KV_WIDTH:3 * d + 2 * KV_WIDTH])
        for t in range(vt_ref.shape[0]):
            vt_ref[t] = v[t * KEY_BLOCK:(t + 1) * KEY_BLOCK, :].T.astype(BF16)
        gl_ref[...] = _dot(h, w_ref[:, 3 * d + 2 * KV_WIDTH:5 * d + 2 * KV_WIDTH]).astype(BF16)

    pl.when(i % 2 == 0)(lambda: project(0))
    pl.when(i % 2 == 1)(lambda: project(1))


def _in_proj(stream, mod_l, g_pre, w_in, layer, tables, geo):
    d = D_MODEL
    tm = geo["tm"]
    n_tiles = geo["n_lat_tiles"] + geo["n_ctx_tiles"]
    nt = n_tiles * tm
    row = lambda i: (i, 0)
    tab = pl.BlockSpec((tm, LANES), lambda i: (geo["table_block"](i), 0))
    next_tile = lambda i: jnp.minimum(i + 1, n_tiles - 1)
    return pl.pallas_call(
        functools.partial(_in_kernel, n_lat_tiles=geo["n_lat_tiles"], n_tiles=n_tiles),
        grid=(n_tiles,),
        in_specs=[
            pl.BlockSpec((tm, d), lambda i: (0, 0), pipeline_mode=pl.Buffered(1)),
            pl.BlockSpec((None, MOD_CHUNKS, d), lambda i: (geo["mod_row"](0), 0, 0)),
            *_stream_specs(stream, tm, d, geo["n_lat_tiles"], next_tile),
            pl.BlockSpec((None, MOD_CHUNKS, d), lambda i: (geo["mod_row"](next_tile(i)), 0, 0)),
            _resident((1, d)),
            _resident_layer(w_in, layer),
            tab, tab, tab, tab,
        ],
        out_specs=[
            pl.BlockSpec((tm * N_RNN_BLOCKS, RNN_BLOCK_W), row),
            pl.BlockSpec((tm, d), row),
            pl.BlockSpec((tm, d), row),
            pl.BlockSpec((tm, KV_WIDTH), row),
            pl.BlockSpec((tm // KEY_BLOCK, KV_WIDTH, KEY_BLOCK), lambda i: (i, 0, 0)),
            pl.BlockSpec((tm, 2 * d), row),
        ],
        out_shape=[
            jax.ShapeDtypeStruct((nt * N_RNN_BLOCKS, RNN_BLOCK_W), F32),
            jax.ShapeDtypeStruct((nt, d), BF16),
            jax.ShapeDtypeStruct((nt, d), BF16),
            jax.ShapeDtypeStruct((nt, KV_WIDTH), BF16),
            jax.ShapeDtypeStruct((nt // KEY_BLOCK, KV_WIDTH, KEY_BLOCK), BF16),
            jax.ShapeDtypeStruct((nt, 2 * d), BF16),
        ],
        scratch_shapes=[pltpu.VMEM((2, tm, d), BF16)],
        compiler_params=pltpu.CompilerParams(
            dimension_semantics=("arbitrary",),
            vmem_limit_bytes=VMEM_LIMIT_BYTES),
        name="in_proj",
    )(stream[0], mod_l, stream[0], stream[1], mod_l, g_pre.reshape(1, d), w_in, *tables)


def _rnn_kernel(xf_ref, xfp_ref, xfn_ref, xb_ref, xbp_ref, xbn_ref,
                cw_ref, cb_ref, wg_ref, bg_ref, lam_ref,
                hf_ref, hb_ref,
                xc_s, a_s, b_s, carry_s, *, n_steps):
    j = pl.program_id(1)
    nb, bw = N_RNN_BLOCKS, RNN_BLOCK_W
    n_in = CONV_T + CONV_W - 1
    right = CONV_W - 1 - CONV_LEFT

    @pl.when(j == 0)
    def _():
        carry_s[...] = jnp.zeros_like(carry_s)

    def conv(slot, x_ref, xp_ref, xn_ref, has_prev, has_next):
        def conv_piece(v, base):
            v = v.reshape(n_in, nb, bw)
            acc = cb_ref[...] + v[0:CONV_T] * cw_ref[0]
            for k in range(1, CONV_W):
                acc = acc + v[k:k + CONV_T] * cw_ref[k]
            xc_s[slot, pl.ds(base, CONV_T * nb), :] = acc.reshape(CONV_T * nb, bw)

        halo_rows = xp_ref.shape[0]
        left = jnp.where(has_prev, xp_ref[halo_rows - CONV_LEFT * nb:halo_rows, :], 0.0)
        conv_piece(jnp.concatenate([left, x_ref[0:(n_in - CONV_LEFT) * nb, :]], axis=0), 0)
        tail = jnp.where(has_next, xn_ref[0:right * nb, :], 0.0)
        last = CHUNK - CONV_T
        conv_piece(jnp.concatenate(
            [x_ref[(last - CONV_LEFT) * nb:CHUNK * nb, :], tail], axis=0), last * nb)

        def conv_body(c, carry):
            base = pl.multiple_of(c * (CONV_T * nb), CONV_T * nb)
            conv_piece(x_ref[pl.ds(base - CONV_LEFT * nb, n_in * nb), :], base)
            return carry

        lax.fori_loop(1, CHUNK // CONV_T - 1, conv_body, 0)

    fwd_chunk = j
    bwd_chunk = jnp.where(j == 0, 0, n_steps - j)
    mid = jnp.logical_and(j >= 1, j < n_steps - 1)

    @pl.when(2 * j <= n_steps)
    def _():
        conv(fwd_chunk, xf_ref, xfp_ref, xfn_ref, j >= 2, mid)

    @pl.when(jnp.logical_and(j >= 1, 2 * j < n_steps))
    def _():
        conv(bwd_chunk, xb_ref, xbp_ref, xbn_ref, mid, j >= 2)

    def gate_unit(parity, direction, n):
        rows = pl.ds(n, CHUNK, stride=nb)
        lam = lam_ref[direction, n]
        c_half = (0.5 * LRU_C * LOG2_E) * (
            jnp.minimum(lam, 0.0) - jnp.log1p(jnp.exp(-jnp.abs(lam))))
        xh = xc_s[bwd_chunk if direction else fwd_chunk, rows, :]
        z = _dot(xh.astype(BF16), wg_ref[direction, n]) + bg_ref[direction, n]
        a = jnp.exp2(c_half * jnp.tanh(z[:, 0:bw]) + c_half)
        gated = xh * jnp.tanh(z[:, bw:2 * bw]) + xh
        om = 1.0 - a * a
        a_s[parity, direction, rows, :] = a
        b_s[parity, direction, rows, :] = (om * lax.rsqrt(jnp.maximum(om, F32_TINY))) * gated

    def scan_steps(parity, t0, t1, carry):
        hf, hb = carry
        for t in range(t0, t1):
            tf, tb = t * nb, (CHUNK - 1 - t) * nb
            hf = a_s[parity, 0, tf:tf + nb, :] * hf + b_s[parity, 0, tf:tf + nb, :]
            hb = a_s[parity, 1, tb:tb + nb, :] * hb + b_s[parity, 1, tb:tb + nb, :]
            hf_ref[tf:tf + nb, :] = hf
            hb_ref[tb:tb + nb, :] = hb
        return hf, hb

    def work(parity, do_gates, do_scan):
        units = [(direction, n) for direction in range(2) for n in range(nb)]
        per_unit = CHUNK // len(units)
        carry = (carry_s[0], carry_s[1]) if do_scan else None
        for idx, (direction, n) in enumerate(units):
            if do_gates:
                gate_unit(parity, direction, n)
            if do_scan:
                carry = scan_steps(1 - parity, idx * per_unit, (idx + 1) * per_unit, carry)
        if do_scan:
            carry_s[0], carry_s[1] = carry

    last_parity = (n_steps - 1) % 2
    pl.when(j == 0)(lambda: work(0, True, False))
    pl.when(j == n_steps)(lambda: work(1 - last_parity, False, True))
    inner = jnp.logical_and(j >= 1, j < n_steps)
    pl.when(jnp.logical_and(inner, j % 2 == 0))(lambda: work(0, True, True))
    pl.when(jnp.logical_and(inner, j % 2 == 1))(lambda: work(1, True, True))


def _rnn_branch(xr, conv_w, conv_b, w_gate, b_gate, lam, geo):
    nb, bw = N_RNN_BLOCKS, RNN_BLOCK_W
    bsz, n_lat = geo["batch"], geo["lat_chunks"]
    n_steps = n_lat + 1
    halo = SUBLANES
    n_halo = xr.shape[0] // (halo * nb)
    per = CHUNK // halo

    def fwd_blk(b, s):
        return jnp.where(s == 0, bsz * n_lat + b, b * n_lat + s - 1)

    def bwd_blk(b, s):
        return jnp.where(s == 0, bsz * n_lat + b, b * n_lat + n_lat - s)

    in_step = lambda j: jnp.minimum(j, n_steps - 1)
    out_step = lambda j: jnp.maximum(j - 1, 0)

    def chunk(blk, step):
        return pl.BlockSpec((CHUNK * nb, bw), lambda b, j: (blk(b, step(j)), 0))

    def prev(blk):
        return pl.BlockSpec(
            (halo * nb, bw), lambda b, j: (jnp.maximum(blk(b, in_step(j)) * per - 1, 0), 0))

    def nxt(blk):
        return pl.BlockSpec(
            (halo * nb, bw),
            lambda b, j: (jnp.minimum((blk(b, in_step(j)) + 1) * per, n_halo - 1), 0))

    return pl.pallas_call(
        functools.partial(_rnn_kernel, n_steps=n_steps),
        grid=(bsz, n_steps + 1),
        in_specs=[
            chunk(fwd_blk, in_step), prev(fwd_blk), nxt(fwd_blk),
            chunk(bwd_blk, in_step), prev(bwd_blk), nxt(bwd_blk),
            _resident((CONV_W, nb, bw)),
            _resident((nb, bw)),
            _resident(w_gate.shape),
            _resident(b_gate.shape),
            _resident(lam.shape),
        ],
        out_specs=[chunk(fwd_blk, out_step), chunk(bwd_blk, out_step)],
        out_shape=[jax.ShapeDtypeStruct(xr.shape, F32)] * 2,
        scratch_shapes=[
            pltpu.VMEM((n_steps, CHUNK * nb, bw), F32),
            pltpu.VMEM((2, 2, CHUNK * nb, bw), F32),
            pltpu.VMEM((2, 2, CHUNK * nb, bw), F32),
            pltpu.VMEM((2, nb, bw), F32),
        ],
        compiler_params=pltpu.CompilerParams(
            dimension_semantics=("arbitrary", "arbitrary"),
            vmem_limit_bytes=VMEM_LIMIT_BYTES),
        name="rglru",
    )(xr, xr, xr, xr, xr, xr, conv_w.reshape(CONV_W, nb, bw), conv_b.reshape(nb, bw),
      w_gate, b_gate, lam)


def _stack_heads(q, g):
    base = g * Q_PER_KV * HEAD_DIM
    return jnp.concatenate(
        [q[:, base + h * HEAD_DIM: base + (h + 1) * HEAD_DIM] for h in range(Q_PER_KV)], axis=0)


def _nt_dot(a, b):
    return lax.dot_general(a, b, (((1,), (1,)), ((), ())), preferred_element_type=F32)


def _attend_units(sink_ref, units):
    folds = KEY_BLOCK // SUBLANES
    for u in units:
        n_q = u["q"].shape[0]
        width = Q_PER_KV * n_q
        qs = _stack_heads(u["q"], u["g"])
        sink = jnp.concatenate(
            [jnp.full((1, n_q), sink_ref[u["g"] * Q_PER_KV + h] * LOG2_E, F32)
             for h in range(Q_PER_KV)], axis=1)
        m8 = jnp.broadcast_to(sink, (SUBLANES, width))
        for j, (k, bias) in enumerate(u["key_blocks"]):
            s = _nt_dot(k, qs)
            if bias is not None:
                s = s + jnp.concatenate([bias] * Q_PER_KV, axis=1)
            u["s_s"][j * KEY_BLOCK:(j + 1) * KEY_BLOCK, :] = s
            m8 = jnp.maximum(m8, jnp.max(s.reshape(folds, SUBLANES, width), axis=0))
        u["sink"] = sink
        u["m"] = jnp.max(m8, axis=0, keepdims=True)
    for u in units:
        width = Q_PER_KV * u["q"].shape[0]
        l8 = jnp.zeros((SUBLANES, width), F32)
        for j in range(len(u["key_blocks"])):
            p = jnp.exp2(u["s_s"][j * KEY_BLOCK:(j + 1) * KEY_BLOCK, :] - u["m"])
            l8 = l8 + jnp.sum(p.reshape(folds, SUBLANES, width), axis=0)
            u["p_s"][j * KEY_BLOCK:(j + 1) * KEY_BLOCK, :] = p.astype(BF16)
        u["denom"] = jnp.sum(l8, axis=0, keepdims=True) + jnp.exp2(u["sink"] - u["m"])
    for u in units:
        n_q = u["q"].shape[0]
        n_keys = len(u["key_blocks"]) * KEY_BLOCK
        ot = _dot(u["vt"], u["p_s"][0:n_keys, :]) * (1.0 / u["denom"])
        for h in range(Q_PER_KV):
            u["o_store"](h, ot[:, h * n_q:(h + 1) * n_q].T)


def _lat_attn_kernel(sink_ref, q_ref, k_ref, vt_ref, kc_ref, vtc_ref, o_ref, s_s, p_s,
                     *, seq_len):
    n_blk = q_ref.shape[0] // Q_BLOCK
    band_blocks = BAND // KEY_BLOCK
    ctx_blocks = vtc_ref.shape[0]
    last_start = seq_len // KEY_BLOCK - band_blocks
    rel = (lax.broadcasted_iota(jnp.int32, (KEY_BLOCK, Q_BLOCK), 0)
           - lax.broadcasted_iota(jnp.int32, (KEY_BLOCK, Q_BLOCK), 1))

    def block_units(i, slot):
        units = []
        qb = pl.program_id(1) * n_blk + i
        jb0 = jnp.clip(qb - WINDOW // KEY_BLOCK, 0, last_start)
        biases = [jnp.where(jnp.abs(rel + (jb0 + t - qb) * KEY_BLOCK) <= WINDOW, 0.0, NEG_INF)
                  for t in range(band_blocks)]
        rows = pl.ds(pl.multiple_of(i * Q_BLOCK, Q_BLOCK), Q_BLOCK)
        q = q_ref[rows, :]
        for g in range(N_KV_HEADS):
            gs = slice(g * HEAD_DIM, (g + 1) * HEAD_DIM)
            key_blocks = [
                (k_ref[pl.ds(pl.multiple_of((jb0 + t) * KEY_BLOCK, KEY_BLOCK), KEY_BLOCK), gs],
                 biases[t]) for t in range(band_blocks)]
            key_blocks += [(kc_ref[t * KEY_BLOCK:(t + 1) * KEY_BLOCK, gs], None)
                           for t in range(ctx_blocks)]
            vt = jnp.concatenate([vt_ref[jb0 + t, gs, :] for t in range(band_blocks)]
                                 + [vtc_ref[t, gs, :] for t in range(ctx_blocks)], axis=1)

            def o_store(h, tile, g=g, rows=rows):
                c0 = (g * Q_PER_KV + h) * HEAD_DIM
                o_ref[rows, c0:c0 + HEAD_DIM] = tile.astype(BF16)

            units.append(dict(g=g, q=q, key_blocks=key_blocks, vt=vt,
                              s_s=s_s.at[slot, g], p_s=p_s.at[slot, g], o_store=o_store))
        return units

    def body(it, carry):
        units = []
        for slot in range(Q_SLOTS):
            units += block_units(it * Q_SLOTS + slot, slot)
        _attend_units(sink_ref, units)
        return carry

    lax.fori_loop(0, n_blk // Q_SLOTS, body, 0)


def _ctx_attn_kernel(sink_ref, q_ref, kc_ref, vtc_ref, o_ref, s_s, p_s):
    ctx_blocks = vtc_ref.shape[0]
    q = q_ref[...]
    units = []
    for g in range(N_KV_HEADS):
        gs = slice(g * HEAD_DIM, (g + 1) * HEAD_DIM)
        key_blocks = [(kc_ref[t * KEY_BLOCK:(t + 1) * KEY_BLOCK, gs], None)
                      for t in range(ctx_blocks)]
        vt = jnp.concatenate([vtc_ref[t, gs, :] for t in range(ctx_blocks)], axis=1)

        def o_store(h, tile, g=g):
            c0 = (g * Q_PER_KV + h) * HEAD_DIM
            o_ref[:, c0:c0 + HEAD_DIM] = tile.astype(BF16)

        units.append(dict(g=g, q=q, key_blocks=key_blocks, vt=vt,
                          s_s=s_s.at[g], p_s=p_s.at[g], o_store=o_store))
    _attend_units(sink_ref, units)


def _attention(q, k, vt, sink, geo, with_ctx_queries):
    nt, d = q.shape
    bsz, seq_len, n_ctx = geo["batch"], geo["seq"], geo["ctx"]
    q_sup = next(s for s in (1024, 512, 256, Q_BLOCK) if seq_len % s == 0)
    n_sup = seq_len // q_sup
    ctx_blk0 = bsz * seq_len // n_ctx
    smem = pl.BlockSpec(memory_space=pltpu.SMEM)
    ctx_keys = pl.BlockSpec((n_ctx, KV_WIDTH), lambda b, *_: (ctx_blk0 + b, 0))
    ctx_vals = pl.BlockSpec((n_ctx // KEY_BLOCK, KV_WIDTH, KEY_BLOCK),
                            lambda b, *_: (ctx_blk0 + b, 0, 0))
    o = pl.pallas_call(
        functools.partial(_lat_attn_kernel, seq_len=seq_len),
        grid=(bsz, n_sup),
        in_specs=[
            smem,
            pl.BlockSpec((q_sup, d), lambda b, i: (b * n_sup + i, 0)),
            pl.BlockSpec((seq_len, KV_WIDTH), lambda b, i: (b, 0)),
            pl.BlockSpec((seq_len // KEY_BLOCK, KV_WIDTH, KEY_BLOCK), lambda b, i: (b, 0, 0)),
            ctx_keys, ctx_vals,
        ],
        out_specs=pl.BlockSpec((q_sup, d), lambda b, i: (b * n_sup + i, 0)),
        out_shape=jax.ShapeDtypeStruct((bsz * seq_len, d), BF16),
        scratch_shapes=[
            pltpu.VMEM((Q_SLOTS, N_KV_HEADS, BAND + n_ctx, Q_PER_KV * Q_BLOCK), F32),
            pltpu.VMEM((Q_SLOTS, N_KV_HEADS, BAND + n_ctx, Q_PER_KV * Q_BLOCK), BF16),
        ],
        compiler_params=pltpu.CompilerParams(
            dimension_semantics=("arbitrary", "arbitrary"),
            vmem_limit_bytes=VMEM_LIMIT_BYTES),
        name="lat_attention",
    )(sink, q, k, vt, k, vt)
    if not with_ctx_queries:
        return (o, o, 0)
    o_ctx = pl.pallas_call(
        _ctx_attn_kernel,
        grid=(bsz,),
        in_specs=[
            smem,
            pl.BlockSpec((n_ctx, d), lambda b: (ctx_blk0 + b, 0)),
            ctx_keys, ctx_vals,
        ],
        out_specs=pl.BlockSpec((n_ctx, d), lambda b: (b, 0)),
        out_shape=jax.ShapeDtypeStruct((bsz * n_ctx, d), BF16),
        scratch_shapes=[
            pltpu.VMEM((N_KV_HEADS, n_ctx, Q_PER_KV * n_ctx), F32),
            pltpu.VMEM((N_KV_HEADS, n_ctx, Q_PER_KV * n_ctx), BF16),
        ],
        compiler_params=pltpu.CompilerParams(
            dimension_semantics=("arbitrary",),
            vmem_limit_bytes=VMEM_LIMIT_BYTES),
        name="ctx_attention",
    )(sink, q, k, vt)
    return (o, o_ctx, 0)


def _merge_kernel(xl_ref, xc_ref, hf_ref, hb_ref, gr_ref, ol_ref, oc_ref, gl_ref, mod_ref,
                  g_ref, wr_ref, wa_ref, wo_ref, out_ref, *, n_lat_tiles):
    d = D_MODEL
    nb = N_RNN_BLOCKS
    rows_per_part = xl_ref.shape[0] // MERGE_PARTS
    is_lat = pl.program_id(0) < n_lat_tiles
    parts =[slice(p * rows_per_part, (p + 1) * rows_per_part) for p in range(MERGE_PARTS)]

    def rnn_input(p):
        h = jnp.concatenate(
            [hf_ref[pl.ds(p * rows_per_part * nb + n, rows_per_part, stride=nb), :]
             + hb_ref[pl.ds(p * rows_per_part * nb + n, rows_per_part, stride=nb), :]
             for n in range(nb)], axis=-1)
        g = gr_ref[parts[p], :].astype(F32)
        hg = h * g
        return (hg * jnp.tanh(g * (GELU_C1 + GELU_C2 * (g * g))) + hg).astype(BF16)

    def gated_mix(p, y):
        r = parts[p]
        ya = _dot(y, wr_ref[...])
        yb = _dot(jnp.where(is_lat, ol_ref[r, :], oc_ref[r, :]), wa_ref[...])
        ta = jnp.tanh(gl_ref[r, 0:d].astype(F32))
        tb = jnp.tanh(gl_ref[r, d:2 * d].astype(F32))
        return ((ya * ta + ya) + (yb * tb + yb)).astype(BF16)

    def project(p, mix):
        r = parts[p]
        m = _dot(mix, wo_ref[...])
        x = jnp.where(is_lat, xl_ref[r, :], xc_ref[r, :])
        out_ref[r, :] = x + mod_ref[2:3, :] * (_rms(m) * g_ref[...])

    ys = [rnn_input(p) for p in range(MERGE_PARTS)]
    mixes = [gated_mix(p, ys[p]) for p in range(MERGE_PARTS)]
    for p in range(MERGE_PARTS):
        project(p, mixes[p])


def _merge(x_stream, hf, hb, gr, o_stream, gl, mod_l, g_post, w_o_rnn, w_o_attn, w_out,
           layer, geo, n_tiles):
    d = D_MODEL
    tm = geo["tm"]
    n_lat_tiles = geo["n_lat_tiles"]
    row = lambda i: (i, 0)
    tile = pl.BlockSpec((tm, d), row)
    slab = pl.BlockSpec((tm * N_RNN_BLOCKS, RNN_BLOCK_W), row)
    return pl.pallas_call(
        functools.partial(_merge_kernel, n_lat_tiles=n_lat_tiles),
        grid=(n_tiles,),
        in_specs=[
            *_stream_specs(x_stream, tm, d, n_lat_tiles), slab, slab, tile,
            *_stream_specs(o_stream, tm, d, n_lat_tiles),
            pl.BlockSpec((tm, 2 * d), row),
            pl.BlockSpec((None, MOD_CHUNKS, d), lambda i: (geo["mod_row"](i), 0, 0)),
            _resident((1, d)),
            _resident_layer(w_o_rnn, layer), _resident_layer(w_o_attn, layer),
            _resident_layer(w_out, layer),
        ],
        out_specs=tile,
        out_shape=jax.ShapeDtypeStruct((n_tiles * tm, d), F32),
        compiler_params=pltpu.CompilerParams(
            dimension_semantics=("arbitrary",),
            vmem_limit_bytes=VMEM_LIMIT_BYTES),
        name="merge",
    )(x_stream[0], x_stream[1], hf, hb, gr, o_stream[0], o_stream[1], gl, mod_l,
      g_post.reshape(1, d), w_o_rnn, w_o_attn, w_out)


def _ffn_kernel(x_ref, mod_ref, gpre_ref, gpost_ref, w1_ref, w2_ref, out_ref):
    sub = x_ref.shape[0] // FFN_SUBTILES
    bounds = list(range(0, D_FF, FFN_CHUNK)) + [D_FF]
    n_chunks = len(bounds) - 1

    def normed(s):
        x = x_ref[s * sub:(s + 1) * sub, :]
        h = (_rms(x) * gpre_ref[...]) * (1.0 + mod_ref[4:5, :]) + mod_ref[3:4, :]
        return h.astype(BF16)

    def first_layer(h, c):
        lo, hi = bounds[c], bounds[c + 1]
        return _dot(h, w1_ref[:, lo:hi]), _dot(h, w1_ref[:, D_FF + lo:D_FF + hi])

    h_next = normed(0)
    for s in range(FFN_SUBTILES):
        h = h_next
        if s + 1 < FFN_SUBTILES:
            h_next = normed(s + 1)
        f = None
        nxt = first_layer(h, 0)
        for c in range(n_chunks):
            half_gate, up = nxt
            if c + 1 < n_chunks:
                nxt = first_layer(h, c + 1)
            act = ((half_gate * jnp.tanh(half_gate) + half_gate) * up).astype(BF16)
            part = _dot(act, w2_ref[bounds[c]:bounds[c + 1], :])
            f = part if f is None else f + part
        rows = slice(s * sub, (s + 1) * sub)
        out_ref[rows, :] = x_ref[rows, :] + mod_ref[5:6, :] * (_rms(f) * gpost_ref[...])


def _ffn(x_all, mod_l, g_pre, g_post, w1, w2, layer, geo, n_tiles):
    d = x_all.shape[1]
    tm = geo["tm"] * FFN_SUBTILES
    n_blocks = n_tiles // FFN_SUBTILES
    tile = pl.BlockSpec((tm, d), lambda i: (i, 0))
    return pl.pallas_call(
        _ffn_kernel,
        grid=(n_blocks,),
        in_specs=[
            tile,
            pl.BlockSpec((None, MOD_CHUNKS, d),
                         lambda i: (geo["mod_row"](i * FFN_SUBTILES), 0, 0)),
            _resident((1, d)), _resident((1, d)),
            _resident_layer(w1, layer), _resident_layer(w2, layer),
        ],
        out_specs=tile,
        out_shape=jax.ShapeDtypeStruct((n_blocks * tm, d), F32),
        compiler_params=pltpu.CompilerParams(
            dimension_semantics=("arbitrary",),
            vmem_limit_bytes=VMEM_LIMIT_BYTES),
        name="ffn",
    )(x_all, mod_l, g_pre.reshape(1, d), g_post.reshape(1, d), w1, w2)


def _rope_tables(seq_len, pad_rows):
    n_rows = seq_len // GRID_W
    inv = ROPE_BASE ** (-jnp.arange(N_FREQ, dtype=F32) / N_FREQ)
    ang = jnp.arange(max(n_rows, GRID_W), dtype=F32)[:, None] * inv[None, :]
    cos_u, sin_u = jnp.cos(ang), jnp.sin(ang)
    by_row = lambda t: jnp.repeat(t[:n_rows], GRID_W, axis=0)
    by_col = lambda t: jnp.tile(t[:GRID_W], (n_rows, 1))
    cos = jnp.concatenate([by_row(cos_u)] * 2 + [by_col(cos_u)] * 2, axis=-1)
    sin = jnp.concatenate(
        [-by_row(sin_u), by_row(sin_u), -by_col(sin_u), by_col(sin_u)], axis=-1)
    cos = jnp.concatenate([cos, jnp.ones((pad_rows, HEAD_DIM), F32)], axis=0)
    sin = jnp.concatenate([sin, jnp.zeros((pad_rows, HEAD_DIM), F32)], axis=0)
    scale = HEAD_DIM ** -0.5 * LOG2_E
    return cos * scale, sin * scale, cos, sin


def kernel(x, c, ctx, c_ctx, w_mod, b_mod, g_mix_pre, g_mix_post, g_ffn_pre, g_ffn_post, w_in, conv_w, conv_b, lru_wa, lru_ba, lru_wx, lru_bx, lru_lam, attn_sink, w_o_rnn, w_o_attn, w_out, w_ffn_in, w_ffn_out):
    bsz, seq_len, d = x.shape
    n_ctx = ctx.shape[1]
    depth = w_mod.shape[0]
    assert d == D_MODEL and n_ctx == CHUNK and seq_len % CHUNK == 0 and seq_len >= BAND
    assert bsz + 1 <= SUBLANES and seq_len % GRID_W == 0

    n_lat_rows, n_ctx_rows = bsz * seq_len, bsz * n_ctx
    big = 512 * FFN_SUBTILES
    tm = 512 if (n_ctx_rows % big == 0 and seq_len % big == 0) else CHUNK
    assert n_ctx_rows % (tm * FFN_SUBTILES) == 0 and seq_len % (tm * FFN_SUBTILES) == 0
    n_lat_tiles, n_ctx_tiles = n_lat_rows // tm, n_ctx_rows // tm
    tiles_per_batch = seq_len // tm
    geo = {
        "batch": bsz, "seq": seq_len, "ctx": n_ctx, "tm": tm,
        "lat_chunks": seq_len // CHUNK,
        "n_lat_tiles": n_lat_tiles, "n_ctx_tiles": n_ctx_tiles,
        "mod_row": lambda i: jnp.where(i < n_lat_tiles, i // tiles_per_batch, bsz),
        "table_block": lambda i: jnp.where(
            i < n_lat_tiles, i % tiles_per_batch, tiles_per_batch + i - n_lat_tiles),
    }

    cs = jnp.concatenate(
        [c, c_ctx[None, :], jnp.zeros((SUBLANES - bsz - 1, d), F32)], axis=0)
    mod = _modulation(cs, w_mod, b_mod).reshape(depth, SUBLANES, MOD_CHUNKS, d)

    tables = _rope_tables(seq_len, n_ctx_rows)

    w_gate = jnp.concatenate([lru_wa, lru_wx], axis=-1).astype(BF16)
    b_gate = 0.5 * jnp.concatenate(
        [lru_ba.reshape(depth, 2, N_RNN_BLOCKS, 1, RNN_BLOCK_W),
         lru_bx.reshape(depth, 2, N_RNN_BLOCKS, 1, RNN_BLOCK_W)], axis=-1)
    lam = lru_lam.reshape(depth, 2, N_RNN_BLOCKS, 1, RNN_BLOCK_W)
    conv_w_half, conv_b_half = 0.5 * conv_w, 0.5 * conv_b

    stream = (x.reshape(n_lat_rows, d), ctx.reshape(n_ctx_rows, d), 0)
    ones = lambda n: jnp.ones((1, n), F32)
    halves = lambda n: jnp.full((1, n), 0.5, F32)
    gl_cols = 2 * d
    w_in_b = _to_bf16(w_in, 256, jnp.concatenate(
        [ones(w_in.shape[-1] - gl_cols), halves(gl_cols)], axis=1))
    w_o_rnn_b = _to_bf16(w_o_rnn, 512, halves(d))
    w_o_attn_b = _to_bf16(w_o_attn, 512, ones(d))
    w_out_b = _to_bf16(w_out, 512, halves(d))
    w_ffn_in_b = _to_bf16(w_ffn_in, 256, jnp.concatenate([halves(D_FF), ones(D_FF)], axis=1))
    w_ffn_out_b = _to_bf16(w_ffn_out, D_FF // 4, ones(d))
    for l in range(depth):
        need_ctx = l < depth - 1
        n_out_tiles = n_lat_tiles + n_ctx_tiles if need_ctx else n_lat_tiles
        xr, gr, q, k, vt, gl = _in_proj(stream, mod[l], g_mix_pre[l], w_in_b, l, tables, geo)
        hf, hb = _rnn_branch(xr, conv_w_half[l], conv_b_half[l], w_gate[l], b_gate[l], lam[l], geo)
        o_stream = _attention(q, k, vt, attn_sink[l], geo, need_ctx)
        x_all = _merge(stream, hf, hb, gr, o_stream, gl, mod[l], g_mix_post[l],
                       w_o_rnn_b, w_o_attn_b, w_out_b, l, geo, n_out_tiles)
        x_all = _ffn(x_all, mod[l], g_ffn_pre[l], g_ffn_post[l],
                     w_ffn_in_b, w_ffn_out_b, l, geo, n_out_tiles)
        stream = (x_all, x_all, n_lat_tiles)
    return x_all[:n_lat_rows].reshape(bsz, seq_len, d)
```

```python
import functools
import math

import jax
import jax.numpy as jnp
import numpy as np
from jax import lax
from jax.experimental import pallas as pl
from jax.experimental.pallas import tpu as pltpu

D_MODEL = 1024
HEAD_DIM = 128
N_Q_HEADS = 8
N_KV_HEADS = 2
Q_PER_KV = N_Q_HEADS // N_KV_HEADS
KV_WIDTH = N_KV_HEADS * HEAD_DIM
WINDOW = 128
GRID_W = 64
N_FREQ = HEAD_DIM // 4
ROPE_BASE = 10000.0
N_RNN_BLOCKS = 8
RNN_BLOCK_W = D_MODEL // N_RNN_BLOCKS
LRU_C = 8.0
CONV_W = 4
CONV_LEFT = 2
D_FF = 2816
EPS = 1e-6
NEG_INF = -1e30
MOD_CHUNKS = 6

LANES = 128
SUBLANES = 8
VMEM_LIMIT_BYTES = 56 * 1024 * 1024

CHUNK = 256
CONV_T = 16
SCAN_UNROLL = 8
MERGE_PARTS = 1
FFN_CHUNK = 768
FFN_SUBTILES = 2
Q_BLOCK = 128
Q_SLOTS = 4
KEY_BLOCK = 128
BAND = Q_BLOCK + 2 * WINDOW
LOG2_E = math.log2(math.e)

BF16 = jnp.bfloat16
F32 = jnp.float32
F32_TINY = float(jnp.finfo(jnp.float32).tiny)


def _dot(a, b):
    return jnp.dot(a, b, preferred_element_type=F32)


def _sigmoid(x):
    return 0.5 * jnp.tanh(0.5 * x) + 0.5


GELU_C1 = math.sqrt(2.0 / math.pi)
GELU_C2 = 0.044715 * GELU_C1


def _rms(x):
    return x * lax.rsqrt(jnp.mean(x * x, axis=-1, keepdims=True) + EPS)


def _resident(shape):
    nd = len(shape)
    return pl.BlockSpec(shape, lambda *_: (0,) * nd, pipeline_mode=pl.Buffered(1))


def _resident_layer(stacked, layer):
    tail = stacked.shape[1:]
    return pl.BlockSpec((None,) + tail, lambda *_: (layer,) + (0,) * len(tail),
                        pipeline_mode=pl.Buffered(1))


def _cast_kernel(w_ref, scale_ref, o_ref):
    o_ref[...] = (w_ref[...] * scale_ref[...]).astype(BF16)


def _to_bf16(w, row_block, col_scale):
    depth, rows, cols = w.shape
    spec = pl.BlockSpec((None, row_block, cols), lambda l, r: (l, r, 0))
    return pl.pallas_call(
        _cast_kernel,
        grid=(depth, rows // row_block),
        in_specs=[spec, pl.BlockSpec((1, cols), lambda l, r: (0, 0))],
        out_specs=spec,
        out_shape=jax.ShapeDtypeStruct(w.shape, BF16),
        compiler_params=pltpu.CompilerParams(
            dimension_semantics=("arbitrary", "arbitrary"),
            vmem_limit_bytes=VMEM_LIMIT_BYTES),
        name="cast_bf16",
    )(w, col_scale)


def _mod_kernel(c_ref, w_ref, b_ref, o_ref):
    c = c_ref[...]
    s = (c * _sigmoid(c)).astype(BF16)
    o_ref[...] = _dot(s, w_ref[...].astype(BF16)) + b_ref[...]


def _modulation(cs, w_mod, b_mod):
    n_layers, d, width = w_mod.shape
    tn = 1536
    return pl.pallas_call(
        _mod_kernel,
        grid=(n_layers, width // tn),
        in_specs=[
            pl.BlockSpec((SUBLANES, d), lambda l, j: (0, 0)),
            pl.BlockSpec((None, d, tn), lambda l, j: (l, 0, j)),
            pl.BlockSpec((None, 1, tn), lambda l, j: (l, 0, j)),
        ],
        out_specs=pl.BlockSpec((None, SUBLANES, tn), lambda l, j: (l, 0, j)),
        out_shape=jax.ShapeDtypeStruct((n_layers, SUBLANES, width), F32),
        compiler_params=pltpu.CompilerParams(
            dimension_semantics=("arbitrary", "arbitrary"),
            vmem_limit_bytes=VMEM_LIMIT_BYTES),
        name="modulation",
    )(cs, w_mod, b_mod.reshape(n_layers, 1, width))


def _rope(t, cos_ref, sin_ref, n_heads):
    cos = cos_ref[...]
    sin = sin_ref[...]
    lane = lax.broadcasted_iota(jnp.int32, cos.shape, 1)
    first = (lane & (2 * N_FREQ - 1)) < N_FREQ
    outs = []
    for h in range(n_heads):
        th = t[:, h * HEAD_DIM:(h + 1) * HEAD_DIM]
        partner = jnp.where(first,
                            pltpu.roll(th, HEAD_DIM - N_FREQ, axis=1),
                            pltpu.roll(th, N_FREQ, axis=1))
        outs.append(th * cos + partner * sin)
    return jnp.concatenate(outs, axis=-1)


def _stream_specs(stream, tm, d, n_lat_tiles):
    _, ctx, ctx_off = stream
    last_ctx = ctx.shape[0] // tm - 1
    lat_spec = pl.BlockSpec((tm, d), lambda i: (jnp.minimum(i, n_lat_tiles - 1), 0))
    ctx_spec = pl.BlockSpec(
        (tm, d), lambda i: (jnp.minimum(jnp.maximum(i - n_lat_tiles, 0) + ctx_off, last_ctx), 0))
    return lat_spec, ctx_spec


def _mod_spec(geo, layer, tiles_per_step=1):
    return pl.BlockSpec(
        (None, None, MOD_CHUNKS, D_MODEL),
        lambda i: (layer, geo["mod_row"](i * tiles_per_step), 0, 0))


def _stream_tile(lat_ref, ctx_ref, n_lat_tiles):
    return jnp.where(pl.program_id(0) < n_lat_tiles, lat_ref[...], ctx_ref[...])


def _in_kernel(xl_ref, xc_ref, mod_ref, g_ref, w_ref, cq_ref, sq_ref, ck_ref, sk_ref,
               xr_ref, gr_ref, q_ref, k_ref, vt_ref, gl_ref, *, n_lat_tiles):
    d = D_MODEL
    x = _stream_tile(xl_ref, xc_ref, n_lat_tiles)
    h = (_rms(x) * g_ref[...]) * (1.0 + mod_ref[1:2, :]) + mod_ref[0:1, :]
    h = h.astype(BF16)
    xr = _dot(h, w_ref[:, 0:d])
    for n in range(N_RNN_BLOCKS):
        xr_ref[pl.ds(n, x.shape[0], stride=N_RNN_BLOCKS), :] = (
            xr[:, n * RNN_BLOCK_W:(n + 1) * RNN_BLOCK_W])
    gr_ref[...] = _dot(h, w_ref[:, d:2 * d]).astype(BF16)
    q = _dot(h, w_ref[:, 2 * d:3 * d])
    q_ref[...] = _rope(q, cq_ref, sq_ref, N_Q_HEADS).astype(BF16)
    k = _dot(h, w_ref[:, 3 * d:3 * d + KV_WIDTH])
    k_ref[...] = _rope(k, ck_ref, sk_ref, N_KV_HEADS).astype(BF16)
    v = _dot(h, w_ref[:, 3 * d + KV_WIDTH:3 * d + 2 * KV_WIDTH])
    for t in range(vt_ref.shape[0]):
        vt_ref[t] = v[t * KEY_BLOCK:(t + 1) * KEY_BLOCK, :].T.astype(BF16)
    gl_ref[...] = _dot(h, w_ref[:, 3 * d + 2 * KV_WIDTH:5 * d + 2 * KV_WIDTH]).astype(BF16)


def _in_proj(stream, mod, g_pre, w_in, layer, tables, geo):
    d = D_MODEL
    tm = geo["tm"]
    n_tiles = geo["n_lat_tiles"] + geo["n_ctx_tiles"]
    nt = n_tiles * tm
    row = lambda i: (i, 0)
    tab = pl.BlockSpec((tm, LANES), lambda i: (geo["table_block"](i), 0))
    return pl.pallas_call(
        functools.partial(_in_kernel, n_lat_tiles=geo["n_lat_tiles"]),
        grid=(n_tiles,),
        in_specs=[
            *_stream_specs(stream, tm, d, geo["n_lat_tiles"]),
            _mod_spec(geo, layer),
            _resident_layer(g_pre, layer),
            _resident_layer(w_in, layer),
            tab, tab, tab, tab,
        ],
        out_specs=[
            pl.BlockSpec((tm * N_RNN_BLOCKS, RNN_BLOCK_W), row),
            pl.BlockSpec((tm, d), row),
            pl.BlockSpec((tm, d), row),
            pl.BlockSpec((tm, KV_WIDTH), row),
            pl.BlockSpec((tm // KEY_BLOCK, KV_WIDTH, KEY_BLOCK), lambda i: (i, 0, 0)),
            pl.BlockSpec((tm, 2 * d), row),
        ],
        out_shape=[
            jax.ShapeDtypeStruct((nt * N_RNN_BLOCKS, RNN_BLOCK_W), F32),
            jax.ShapeDtypeStruct((nt, d), BF16),
            jax.ShapeDtypeStruct((nt, d), BF16),
            jax.ShapeDtypeStruct((nt, KV_WIDTH), BF16),
            jax.ShapeDtypeStruct((nt // KEY_BLOCK, KV_WIDTH, KEY_BLOCK), BF16),
            jax.ShapeDtypeStruct((nt, 2 * d), BF16),
        ],
        compiler_params=pltpu.CompilerParams(
            dimension_semantics=("arbitrary",),
            vmem_limit_bytes=VMEM_LIMIT_BYTES),
        name="in_proj",
    )(stream[0], stream[1], mod, g_pre, w_in, *tables)


def _rnn_kernel(xf_ref, xfp_ref, xfn_ref, xb_ref, xbp_ref, xbn_ref,
                cw_ref, cb_ref, wg_ref, bg_ref, lam_ref,
                hf_ref, hb_ref,
                xc_s, a_s, b_s, carry_s, *, n_steps):
    j = pl.program_id(1)
    nb, bw = N_RNN_BLOCKS, RNN_BLOCK_W
    n_in = CONV_T + CONV_W - 1
    right = CONV_W - 1 - CONV_LEFT

    @pl.when(j == 0)
    def _():
        carry_s[...] = jnp.zeros_like(carry_s)

    def conv(slot, x_ref, xp_ref, xn_ref, has_prev, has_next):
        def conv_piece(v, base):
            v = v.reshape(n_in, nb, bw)
            acc = cb_ref[...] + v[0:CONV_T] * cw_ref[0]
            for k in range(1, CONV_W):
                acc = acc + v[k:k + CONV_T] * cw_ref[k]
            xc_s[slot, pl.ds(base, CONV_T * nb), :] = acc.reshape(CONV_T * nb, bw)

        halo_rows = xp_ref.shape[0]
        left = jnp.where(has_prev, xp_ref[halo_rows - CONV_LEFT * nb:halo_rows, :], 0.0)
        conv_piece(jnp.concatenate([left, x_ref[0:(n_in - CONV_LEFT) * nb, :]], axis=0), 0)
        tail = jnp.where(has_next, xn_ref[0:right * nb, :], 0.0)
        last = CHUNK - CONV_T
        conv_piece(jnp.concatenate(
            [x_ref[(last - CONV_LEFT) * nb:CHUNK * nb, :], tail], axis=0), last * nb)

        def conv_body(c, carry):
            base = pl.multiple_of(c * (CONV_T * nb), CONV_T * nb)
            conv_piece(x_ref[pl.ds(base - CONV_LEFT * nb, n_in * nb), :], base)
            return carry

        lax.fori_loop(1, CHUNK // CONV_T - 1, conv_body, 0)

    fwd_chunk = j
    bwd_chunk = jnp.where(j == 0, 0, n_steps - j)
    mid = jnp.logical_and(j >= 1, j < n_steps - 1)

    @pl.when(2 * j <= n_steps)
    def _():
        conv(fwd_chunk, xf_ref, xfp_ref, xfn_ref, j >= 2, mid)

    @pl.when(jnp.logical_and(j >= 1, 2 * j < n_steps))
    def _():
        conv(bwd_chunk, xb_ref, xbp_ref, xbn_ref, mid, j >= 2)

    def gate_unit(parity, direction, n):
        rows = pl.ds(n, CHUNK, stride=nb)
        lam = lam_ref[direction, n]
        c_half = (0.5 * LRU_C * LOG2_E) * (
            jnp.minimum(lam, 0.0) - jnp.log1p(jnp.exp(-jnp.abs(lam))))
        xh = xc_s[bwd_chunk if direction else fwd_chunk, rows, :]
        z = _dot(xh.astype(BF16), wg_ref[direction, n]) + bg_ref[direction, n]
        a = jnp.exp2(c_half * jnp.tanh(z[:, 0:bw]) + c_half)
        gated = xh * jnp.tanh(z[:, bw:2 * bw]) + xh
        om = 1.0 - a * a
        a_s[parity, direction, rows, :] = a
        b_s[parity, direction, rows, :] = (om * lax.rsqrt(jnp.maximum(om, F32_TINY))) * gated

    def scan_steps(parity, t0, t1, carry):
        hf, hb = carry
        for t in range(t0, t1):
            tf, tb = t * nb, (CHUNK - 1 - t) * nb
            hf = a_s[parity, 0, tf:tf + nb, :] * hf + b_s[parity, 0, tf:tf + nb, :]
            hb = a_s[parity, 1, tb:tb + nb, :] * hb + b_s[parity, 1, tb:tb + nb, :]
            hf_ref[tf:tf + nb, :] = hf
            hb_ref[tb:tb + nb, :] = hb
        return hf, hb

    def work(parity, do_gates, do_scan):
        units = [(direction, n) for direction in range(2) for n in range(nb)]
        per_unit = CHUNK // len(units)
        carry = (carry_s[0], carry_s[1]) if do_scan else None
        for idx, (direction, n) in enumerate(units):
            if do_gates:
                gate_unit(parity, direction, n)
            if do_scan:
                carry = scan_steps(1 - parity, idx * per_unit, (idx + 1) * per_unit, carry)
        if do_scan:
            carry_s[0], carry_s[1] = carry

    last_parity = (n_steps - 1) % 2
    pl.when(j == 0)(lambda: work(0, True, False))
    pl.when(j == n_steps)(lambda: work(1 - last_parity, False, True))
    inner = jnp.logical_and(j >= 1, j < n_steps)
    pl.when(jnp.logical_and(inner, j % 2 == 0))(lambda: work(0, True, True))
    pl.when(jnp.logical_and(inner, j % 2 == 1))(lambda: work(1, True, True))


def _rnn_branch(xr, conv_w, conv_b, w_gate, b_gate, lam, layer, geo):
    nb, bw = N_RNN_BLOCKS, RNN_BLOCK_W
    bsz, n_lat = geo["batch"], geo["lat_chunks"]
    n_steps = n_lat + 1
    halo = SUBLANES
    n_halo = xr.shape[0] // (halo * nb)
    per = CHUNK // halo

    def fwd_blk(b, s):
        return jnp.where(s == 0, bsz * n_lat + b, b * n_lat + s - 1)

    def bwd_blk(b, s):
        return jnp.where(s == 0, bsz * n_lat + b, b * n_lat + n_lat - s)

    in_step = lambda j: jnp.minimum(j, n_steps - 1)
    out_step = lambda j: jnp.maximum(j - 1, 0)

    def chunk(blk, step):
        return pl.BlockSpec((CHUNK * nb, bw), lambda b, j: (blk(b, step(j)), 0))

    def prev(blk):
        return pl.BlockSpec(
            (halo * nb, bw), lambda b, j: (jnp.maximum(blk(b, in_step(j)) * per - 1, 0), 0))

    def nxt(blk):
        return pl.BlockSpec(
            (halo * nb, bw),
            lambda b, j: (jnp.minimum((blk(b, in_step(j)) + 1) * per, n_halo - 1), 0))

    return pl.pallas_call(
        functools.partial(_rnn_kernel, n_steps=n_steps),
        grid=(bsz, n_steps + 1),
        in_specs=[
            chunk(fwd_blk, in_step), prev(fwd_blk), nxt(fwd_blk),
            chunk(bwd_blk, in_step), prev(bwd_blk), nxt(bwd_blk),
            _resident_layer(conv_w, layer),
            _resident_layer(conv_b, layer),
            _resident_layer(w_gate, layer),
            _resident_layer(b_gate, layer),
            _resident_layer(lam, layer),
        ],
        out_specs=[chunk(fwd_blk, out_step), chunk(bwd_blk, out_step)],
        out_shape=[jax.ShapeDtypeStruct(xr.shape, F32)] * 2,
        scratch_shapes=[
            pltpu.VMEM((n_steps, CHUNK * nb, bw), F32),
            pltpu.VMEM((2, 2, CHUNK * nb, bw), F32),
            pltpu.VMEM((2, 2, CHUNK * nb, bw), F32),
            pltpu.VMEM((2, nb, bw), F32),
        ],
        compiler_params=pltpu.CompilerParams(
            dimension_semantics=("arbitrary", "arbitrary"),
            vmem_limit_bytes=VMEM_LIMIT_BYTES),
        name="rglru",
    )(xr, xr, xr, xr, xr, xr, conv_w, conv_b, w_gate, b_gate, lam)


def _stack_heads(q, g):
    base = g * Q_PER_KV * HEAD_DIM
    return jnp.concatenate(
        [q[:, base + h * HEAD_DIM: base + (h + 1) * HEAD_DIM] for h in range(Q_PER_KV)], axis=0)


def _nt_dot(a, b):
    return lax.dot_general(a, b, (((1,), (1,)), ((), ())), preferred_element_type=F32)


def _attend_units(sink_ref, layer, units):
    folds = KEY_BLOCK // SUBLANES
    for u in units:
        n_q = u["q"].shape[0]
        width = Q_PER_KV * n_q
        qs = _stack_heads(u["q"], u["g"])
        sink = jnp.concatenate(
            [jnp.full((1, n_q), sink_ref[layer, u["g"] * Q_PER_KV + h] * LOG2_E, F32)
             for h in range(Q_PER_KV)], axis=1)
        m8 = jnp.broadcast_to(sink, (SUBLANES, width))
        for j, (k, bias) in enumerate(u["key_blocks"]):
            s = _nt_dot(k, qs)
            if bias is not None:
                s = s + jnp.concatenate([bias] * Q_PER_KV, axis=1)
            u["s_s"][j * KEY_BLOCK:(j + 1) * KEY_BLOCK, :] = s
            m8 = jnp.maximum(m8, jnp.max(s.reshape(folds, SUBLANES, width), axis=0))
        u["sink"] = sink
        u["m"] = jnp.max(m8, axis=0, keepdims=True)
    for u in units:
        width = Q_PER_KV * u["q"].shape[0]
        l8 = jnp.zeros((SUBLANES, width), F32)
        for j in range(len(u["key_blocks"])):
            p = jnp.exp2(u["s_s"][j * KEY_BLOCK:(j + 1) * KEY_BLOCK, :] - u["m"])
            l8 = l8 + jnp.sum(p.reshape(folds, SUBLANES, width), axis=0)
            u["p_s"][j * KEY_BLOCK:(j + 1) * KEY_BLOCK, :] = p.astype(BF16)
        u["denom"] = jnp.sum(l8, axis=0, keepdims=True) + jnp.exp2(u["sink"] - u["m"])
    for u in units:
        n_q = u["q"].shape[0]
        n_keys = len(u["key_blocks"]) * KEY_BLOCK
        ot = _dot(u["vt"], u["p_s"][0:n_keys, :]) * (1.0 / u["denom"])
        for h in range(Q_PER_KV):
            u["o_store"](h, ot[:, h * n_q:(h + 1) * n_q].T)


def _lat_attn_kernel(sink_ref, q_ref, k_ref, vt_ref, kc_ref, vtc_ref, o_ref, s_s, p_s,
                     *, seq_len, layer):
    n_blk = q_ref.shape[0] // Q_BLOCK
    band_blocks = BAND // KEY_BLOCK
    ctx_blocks = vtc_ref.shape[0]
    last_start = seq_len // KEY_BLOCK - band_blocks
    rel = (lax.broadcasted_iota(jnp.int32, (KEY_BLOCK, Q_BLOCK), 0)
           - lax.broadcasted_iota(jnp.int32, (KEY_BLOCK, Q_BLOCK), 1))

    def block_units(i, slot):
        units = []
        qb = pl.program_id(1) * n_blk + i
        jb0 = jnp.clip(qb - WINDOW // KEY_BLOCK, 0, last_start)
        biases = [jnp.where(jnp.abs(rel + (jb0 + t - qb) * KEY_BLOCK) <= WINDOW, 0.0, NEG_INF)
                  for t in range(band_blocks)]
        rows = pl.ds(pl.multiple_of(i * Q_BLOCK, Q_BLOCK), Q_BLOCK)
        q = q_ref[rows, :]
        for g in range(N_KV_HEADS):
            gs = slice(g * HEAD_DIM, (g + 1) * HEAD_DIM)
            key_blocks = [
                (k_ref[pl.ds(pl.multiple_of((jb0 + t) * KEY_BLOCK, KEY_BLOCK), KEY_BLOCK), gs],
                 biases[t]) for t in range(band_blocks)]
            key_blocks += [(kc_ref[t * KEY_BLOCK:(t + 1) * KEY_BLOCK, gs], None)
                           for t in range(ctx_blocks)]
            vt = jnp.concatenate([vt_ref[jb0 + t, gs, :] for t in range(band_blocks)]
                                 + [vtc_ref[t, gs, :] for t in range(ctx_blocks)], axis=1)

            def o_store(h, tile, g=g, rows=rows):
                c0 = (g * Q_PER_KV + h) * HEAD_DIM
                o_ref[rows, c0:c0 + HEAD_DIM] = tile.astype(BF16)

            units.append(dict(g=g, q=q, key_blocks=key_blocks, vt=vt,
                              s_s=s_s.at[slot, g], p_s=p_s.at[slot, g], o_store=o_store))
        return units

    def body(it, carry):
        units = []
        for slot in range(Q_SLOTS):
            units += block_units(it * Q_SLOTS + slot, slot)
        _attend_units(sink_ref, layer, units)
        return carry

    lax.fori_loop(0, n_blk // Q_SLOTS, body, 0)


def _ctx_attn_kernel(sink_ref, q_ref, kc_ref, vtc_ref, o_ref, s_s, p_s, *, layer):
    ctx_blocks = vtc_ref.shape[0]
    q = q_ref[...]
    units = []
    for g in range(N_KV_HEADS):
        gs = slice(g * HEAD_DIM, (g + 1) * HEAD_DIM)
        key_blocks = [(kc_ref[t * KEY_BLOCK:(t + 1) * KEY_BLOCK, gs], None)
                      for t in range(ctx_blocks)]
        vt = jnp.concatenate([vtc_ref[t, gs, :] for t in range(ctx_blocks)], axis=1)

        def o_store(h, tile, g=g):
            c0 = (g * Q_PER_KV + h) * HEAD_DIM
            o_ref[:, c0:c0 + HEAD_DIM] = tile.astype(BF16)

        units.append(dict(g=g, q=q, key_blocks=key_blocks, vt=vt,
                          s_s=s_s.at[g], p_s=p_s.at[g], o_store=o_store))
    _attend_units(sink_ref, layer, units)


def _attention(q, k, vt, sink, layer, geo, with_ctx_queries):
    nt, d = q.shape
    bsz, seq_len, n_ctx = geo["batch"], geo["seq"], geo["ctx"]
    q_sup = next(s for s in (1024, 512, 256, Q_BLOCK) if seq_len % s == 0)
    n_sup = seq_len // q_sup
    ctx_blk0 = bsz * seq_len // n_ctx
    smem = pl.BlockSpec(memory_space=pltpu.SMEM)
    ctx_keys = pl.BlockSpec((n_ctx, KV_WIDTH), lambda b, *_: (ctx_blk0 + b, 0))
    ctx_vals = pl.BlockSpec((n_ctx // KEY_BLOCK, KV_WIDTH, KEY_BLOCK),
                            lambda b, *_: (ctx_blk0 + b, 0, 0))
    o = pl.pallas_call(
        functools.partial(_lat_attn_kernel, seq_len=seq_len, layer=layer),
        grid=(bsz, n_sup),
        in_specs=[
            smem,
            pl.BlockSpec((q_sup, d), lambda b, i: (b * n_sup + i, 0)),
            pl.BlockSpec((seq_len, KV_WIDTH), lambda b, i: (b, 0)),
            pl.BlockSpec((seq_len // KEY_BLOCK, KV_WIDTH, KEY_BLOCK), lambda b, i: (b, 0, 0)),
            ctx_keys, ctx_vals,
        ],
        out_specs=pl.BlockSpec((q_sup, d), lambda b, i: (b * n_sup + i, 0)),
        out_shape=jax.ShapeDtypeStruct((bsz * seq_len, d), BF16),
        scratch_shapes=[
            pltpu.VMEM((Q_SLOTS, N_KV_HEADS, BAND + n_ctx, Q_PER_KV * Q_BLOCK), F32),
            pltpu.VMEM((Q_SLOTS, N_KV_HEADS, BAND + n_ctx, Q_PER_KV * Q_BLOCK), BF16),
        ],
        compiler_params=pltpu.CompilerParams(
            dimension_semantics=("arbitrary", "arbitrary"),
            vmem_limit_bytes=VMEM_LIMIT_BYTES),
        name="lat_attention",
    )(sink, q, k, vt, k, vt)
    if not with_ctx_queries:
        return (o, o, 0)
    o_ctx = pl.pallas_call(
        functools.partial(_ctx_attn_kernel, layer=layer),
        grid=(bsz,),
        in_specs=[
            smem,
            pl.BlockSpec((n_ctx, d), lambda b: (ctx_blk0 + b, 0)),
            ctx_keys, ctx_vals,
        ],
        out_specs=pl.BlockSpec((n_ctx, d), lambda b: (b, 0)),
        out_shape=jax.ShapeDtypeStruct((bsz * n_ctx, d), BF16),
        scratch_shapes=[
            pltpu.VMEM((N_KV_HEADS, n_ctx, Q_PER_KV * n_ctx), F32),
            pltpu.VMEM((N_KV_HEADS, n_ctx, Q_PER_KV * n_ctx), BF16),
        ],
        compiler_params=pltpu.CompilerParams(
            dimension_semantics=("arbitrary",),
            vmem_limit_bytes=VMEM_LIMIT_BYTES),
        name="ctx_attention",
    )(sink, q, k, vt)
    return (o, o_ctx, 0)


def _merge_kernel(xl_ref, xc_ref, hf_ref, hb_ref, gr_ref, ol_ref, oc_ref, gl_ref, mod_ref,
                  g_ref, wr_ref, wa_ref, wo_ref, out_ref, *, n_lat_tiles):
    d = D_MODEL
    nb = N_RNN_BLOCKS
    rows_per_part = xl_ref.shape[0] // MERGE_PARTS
    is_lat = pl.program_id(0) < n_lat_tiles
    parts =[slice(p * rows_per_part, (p + 1) * rows_per_part) for p in range(MERGE_PARTS)]

    def rnn_input(p):
        h = jnp.concatenate(
            [hf_ref[pl.ds(p * rows_per_part * nb + n, rows_per_part, stride=nb), :]
             + hb_ref[pl.ds(p * rows_per_part * nb + n, rows_per_part, stride=nb), :]
             for n in range(nb)], axis=-1)
        g = gr_ref[parts[p], :].astype(F32)
        hg = h * g
        return (hg * jnp.tanh(g * (GELU_C1 + GELU_C2 * (g * g))) + hg).astype(BF16)

    def gated_mix(p, y):
        r = parts[p]
        ya = _dot(y, wr_ref[...])
        yb = _dot(jnp.where(is_lat, ol_ref[r, :], oc_ref[r, :]), wa_ref[...])
        ta = jnp.tanh(gl_ref[r, 0:d].astype(F32))
        tb = jnp.tanh(gl_ref[r, d:2 * d].astype(F32))
        return ((ya * ta + ya) + (yb * tb + yb)).astype(BF16)

    def project(p, mix):
        r = parts[p]
        m = _dot(mix, wo_ref[...])
        x = jnp.where(is_lat, xl_ref[r, :], xc_ref[r, :])
        out_ref[r, :] = x + mod_ref[2:3, :] * (_rms(m) * g_ref[...])

    ys = [rnn_input(p) for p in range(MERGE_PARTS)]
    mixes = [gated_mix(p, ys[p]) for p in range(MERGE_PARTS)]
    for p in range(MERGE_PARTS):
        project(p, mixes[p])


def _merge(x_stream, hf, hb, gr, o_stream, gl, mod, g_post, w_o_rnn, w_o_attn, w_out,
           layer, geo, n_tiles):
    d = D_MODEL
    tm = geo["tm"]
    n_lat_tiles = geo["n_lat_tiles"]
    row = lambda i: (i, 0)
    tile = pl.BlockSpec((tm, d), row)
    slab = pl.BlockSpec((tm * N_RNN_BLOCKS, RNN_BLOCK_W), row)
    return pl.pallas_call(
        functools.partial(_merge_kernel, n_lat_tiles=n_lat_tiles),
        grid=(n_tiles,),
        in_specs=[
            *_stream_specs(x_stream, tm, d, n_lat_tiles), slab, slab, tile,
            *_stream_specs(o_stream, tm, d, n_lat_tiles),
            pl.BlockSpec((tm, 2 * d), row),
            _mod_spec(geo, layer),
            _resident_layer(g_post, layer),
            _resident_layer(w_o_rnn, layer), _resident_layer(w_o_attn, layer),
            _resident_layer(w_out, layer),
        ],
        out_specs=tile,
        out_shape=jax.ShapeDtypeStruct((n_tiles * tm, d), F32),
        compiler_params=pltpu.CompilerParams(
            dimension_semantics=("arbitrary",),
            vmem_limit_bytes=VMEM_LIMIT_BYTES),
        name="merge",
    )(x_stream[0], x_stream[1], hf, hb, gr, o_stream[0], o_stream[1], gl, mod,
      g_post, w_o_rnn, w_o_attn, w_out)


def _ffn_kernel(x_ref, mod_ref, gpre_ref, gpost_ref, w1_ref, w2_ref, out_ref):
    sub = x_ref.shape[0] // FFN_SUBTILES
    bounds = list(range(0, D_FF, FFN_CHUNK)) + [D_FF]
    n_chunks = len(bounds) - 1

    def normed(s):
        x = x_ref[s * sub:(s + 1) * sub, :]
        h = (_rms(x) * gpre_ref[...]) * (1.0 + mod_ref[4:5, :]) + mod_ref[3:4, :]
        return h.astype(BF16)

    def first_layer(h, c):
        lo, hi = bounds[c], bounds[c + 1]
        return _dot(h, w1_ref[:, lo:hi]), _dot(h, w1_ref[:, D_FF + lo:D_FF + hi])

    h_next = normed(0)
    for s in range(FFN_SUBTILES):
        h = h_next
        if s + 1 < FFN_SUBTILES:
            h_next = normed(s + 1)
        f = None
        nxt = first_layer(h, 0)
        for c in range(n_chunks):
            half_gate, up = nxt
            if c + 1 < n_chunks:
                nxt = first_layer(h, c + 1)
            act = ((half_gate * jnp.tanh(half_gate) + half_gate) * up).astype(BF16)
            part = _dot(act, w2_ref[bounds[c]:bounds[c + 1], :])
            f = part if f is None else f + part
        rows = slice(s * sub, (s + 1) * sub)
        out_ref[rows, :] = x_ref[rows, :] + mod_ref[5:6, :] * (_rms(f) * gpost_ref[...])


def _ffn(x_all, mod, g_pre, g_post, w1, w2, layer, geo, n_tiles):
    d = x_all.shape[1]
    tm = geo["tm"] * FFN_SUBTILES
    n_blocks = n_tiles // FFN_SUBTILES
    tile = pl.BlockSpec((tm, d), lambda i: (i, 0))
    return pl.pallas_call(
        _ffn_kernel,
        grid=(n_blocks,),
        in_specs=[
            tile,
            _mod_spec(geo, layer, FFN_SUBTILES),
            _resident_layer(g_pre, layer), _resident_layer(g_post, layer),
            _resident_layer(w1, layer), _resident_layer(w2, layer),
        ],
        out_specs=tile,
        out_shape=jax.ShapeDtypeStruct((n_blocks * tm, d), F32),
        compiler_params=pltpu.CompilerParams(
            dimension_semantics=("arbitrary",),
            vmem_limit_bytes=VMEM_LIMIT_BYTES),
        name="ffn",
    )(x_all, mod, g_pre, g_post, w1, w2)


def _rope_tables(seq_len, pad_rows):
    f32 = np.float32
    n_rows = seq_len // GRID_W
    inv = f32(ROPE_BASE) ** (-np.arange(N_FREQ, dtype=f32) / f32(N_FREQ))
    ang = np.arange(max(n_rows, GRID_W), dtype=f32)[:, None] * inv[None, :]
    cos_u, sin_u = np.cos(ang), np.sin(ang)
    by_row = lambda t: np.repeat(t[:n_rows], GRID_W, axis=0)
    by_col = lambda t: np.tile(t[:GRID_W], (n_rows, 1))
    cos = np.concatenate([by_row(cos_u)] * 2 + [by_col(cos_u)] * 2, axis=-1)
    sin = np.concatenate(
        [-by_row(sin_u), by_row(sin_u), -by_col(sin_u), by_col(sin_u)], axis=-1)
    cos = np.concatenate([cos, np.ones((pad_rows, HEAD_DIM), f32)], axis=0)
    sin = np.concatenate([sin, np.zeros((pad_rows, HEAD_DIM), f32)], axis=0)
    scale = f32(HEAD_DIM ** -0.5 * LOG2_E)
    return tuple(jnp.asarray(t.astype(f32)) for t in (cos * scale, sin * scale, cos, sin))


def kernel(x, c, ctx, c_ctx, w_mod, b_mod, g_mix_pre, g_mix_post, g_ffn_pre, g_ffn_post, w_in, conv_w, conv_b, lru_wa, lru_ba, lru_wx, lru_bx, lru_lam, attn_sink, w_o_rnn, w_o_attn, w_out, w_ffn_in, w_ffn_out):
    bsz, seq_len, d = x.shape
    n_ctx = ctx.shape[1]
    depth = w_mod.shape[0]
    assert d == D_MODEL and n_ctx == CHUNK and seq_len % CHUNK == 0 and seq_len >= BAND
    assert bsz + 1 <= SUBLANES and seq_len % GRID_W == 0

    n_lat_rows, n_ctx_rows = bsz * seq_len, bsz * n_ctx
    big = 512 * FFN_SUBTILES
    tm = 512 if (n_ctx_rows % big == 0 and seq_len % big == 0) else CHUNK
    assert n_ctx_rows % (tm * FFN_SUBTILES) == 0 and seq_len % (tm * FFN_SUBTILES) == 0
    n_lat_tiles, n_ctx_tiles = n_lat_rows // tm, n_ctx_rows // tm
    tiles_per_batch = seq_len // tm
    geo = {
        "batch": bsz, "seq": seq_len, "ctx": n_ctx, "tm": tm,
        "lat_chunks": seq_len // CHUNK,
        "n_lat_tiles": n_lat_tiles, "n_ctx_tiles": n_ctx_tiles,
        "mod_row": lambda i: jnp.where(i < n_lat_tiles, i // tiles_per_batch, bsz),
        "table_block": lambda i: jnp.where(
            i < n_lat_tiles, i % tiles_per_batch, tiles_per_batch + i - n_lat_tiles),
    }

    cs = jnp.concatenate(
        [c, c_ctx[None, :], jnp.zeros((SUBLANES - bsz - 1, d), F32)], axis=0)
    mod = _modulation(cs, w_mod, b_mod).reshape(depth, SUBLANES, MOD_CHUNKS, d)

    tables = _rope_tables(seq_len, n_ctx_rows)

    w_gate = jnp.concatenate([lru_wa, lru_wx], axis=-1).astype(BF16)
    b_gate = 0.5 * jnp.concatenate(
        [lru_ba.reshape(depth, 2, N_RNN_BLOCKS, 1, RNN_BLOCK_W),
         lru_bx.reshape(depth, 2, N_RNN_BLOCKS, 1, RNN_BLOCK_W)], axis=-1)
    lam = lru_lam.reshape(depth, 2, N_RNN_BLOCKS, 1, RNN_BLOCK_W)
    conv_w_half = (0.5 * conv_w).reshape(depth, CONV_W, N_RNN_BLOCKS, RNN_BLOCK_W)
    conv_b_half = (0.5 * conv_b).reshape(depth, N_RNN_BLOCKS, RNN_BLOCK_W)
    g_mix_pre, g_mix_post, g_ffn_pre, g_ffn_post = (
        g.reshape(depth, 1, d) for g in (g_mix_pre, g_mix_post, g_ffn_pre, g_ffn_post))

    stream = (x.reshape(n_lat_rows, d), ctx.reshape(n_ctx_rows, d), 0)
    ones = lambda n: jnp.ones((1, n), F32)
    halves = lambda n: jnp.full((1, n), 0.5, F32)
    gl_cols = 2 * d
    w_in_b = _to_bf16(w_in, 256, jnp.concatenate(
        [ones(w_in.shape[-1] - gl_cols), halves(gl_cols)], axis=1))
    w_o_rnn_b = _to_bf16(w_o_rnn, 512, halves(d))
    w_o_attn_b = _to_bf16(w_o_attn, 512, ones(d))
    w_out_b = _to_bf16(w_out, 512, halves(d))
    w_ffn_in_b = _to_bf16(w_ffn_in, 256, jnp.concatenate([halves(D_FF), ones(D_FF)], axis=1))
    w_ffn_out_b = _to_bf16(w_ffn_out, D_FF // 4, ones(d))
    for l in range(depth):
        need_ctx = l < depth - 1
        n_out_tiles = n_lat_tiles + n_ctx_tiles if need_ctx else n_lat_tiles
        xr, gr, q, k, vt, gl = _in_proj(stream, mod, g_mix_pre, w_in_b, l, tables, geo)
        hf, hb = _rnn_branch(xr, conv_w_half, conv_b_half, w_gate, b_gate, lam, l, geo)
        o_stream = _attention(q, k, vt, attn_sink, l, geo, need_ctx)
        x_all = _merge(stream, hf, hb, gr, o_stream, gl, mod, g_mix_post,
                       w_o_rnn_b, w_o_attn_b, w_out_b, l, geo, n_out_tiles)
        x_all = _ffn(x_all, mod, g_ffn_pre, g_ffn_post,
                     w_ffn_in_b, w_ffn_out_b, l, geo, n_out_tiles)
        stream = (x_all, x_all, n_lat_tiles)
    return x_all[:n_lat_rows].reshape(bsz, seq_len, d)
```

```python
import functools
import math

import jax
import jax.numpy as jnp
import numpy as np
from jax import lax
from jax.experimental import pallas as pl
from jax.experimental.pallas import tpu as pltpu

D_MODEL = 1024
HEAD_DIM = 128
N_Q_HEADS = 8
N_KV_HEADS = 2
Q_PER_KV = N_Q_HEADS // N_KV_HEADS
KV_WIDTH = N_KV_HEADS * HEAD_DIM
WINDOW = 128
GRID_W = 64
N_FREQ = HEAD_DIM // 4
ROPE_BASE = 10000.0
N_RNN_BLOCKS = 8
RNN_BLOCK_W = D_MODEL // N_RNN_BLOCKS
LRU_C = 8.0
CONV_W = 4
CONV_LEFT = 2
D_FF = 2816
EPS = 1e-6
NEG_INF = -1e30
MOD_CHUNKS = 6

LANES = 128
SUBLANES = 8
BF16_SUBLANES = 16
VMEM_LIMIT_BYTES = 56 * 1024 * 1024

CHUNK = 256
CONV_T = 16
SCAN_UNROLL = 8
MERGE_PARTS = 1
FFN_CHUNK = 768
FFN_SUBTILES = 2
Q_BLOCK = 128
Q_SLOTS = 4
KEY_BLOCK = 128
BAND = Q_BLOCK + 2 * WINDOW
LOG2_E = math.log2(math.e)

BF16 = jnp.bfloat16
F32 = jnp.float32
F32_TINY = float(jnp.finfo(jnp.float32).tiny)


def _dot(a, b):
    return jnp.dot(a, b, preferred_element_type=F32)


def _sigmoid(x):
    return 0.5 * jnp.tanh(0.5 * x) + 0.5


GELU_C1 = math.sqrt(2.0 / math.pi)
GELU_C2 = 0.044715 * GELU_C1


def _rms(x):
    return x * lax.rsqrt(jnp.mean(x * x, axis=-1, keepdims=True) + EPS)


def _resident(shape):
    nd = len(shape)
    return pl.BlockSpec(shape, lambda *_: (0,) * nd, pipeline_mode=pl.Buffered(1))


def _resident_layer(stacked, layer):
    tail = stacked.shape[1:]
    return pl.BlockSpec((None,) + tail, lambda *_: (layer,) + (0,) * len(tail),
                        pipeline_mode=pl.Buffered(1))


def _cast_kernel(w_ref, scale_ref, o_ref):
    o_ref[...] = (w_ref[...] * scale_ref[...]).astype(BF16)


def _to_bf16(w, row_block, col_scale):
    depth, rows, cols = w.shape
    spec = pl.BlockSpec((None, row_block, cols), lambda l, r: (l, r, 0))
    return pl.pallas_call(
        _cast_kernel,
        grid=(depth, rows // row_block),
        in_specs=[spec, pl.BlockSpec((1, cols), lambda l, r: (0, 0))],
        out_specs=spec,
        out_shape=jax.ShapeDtypeStruct(w.shape, BF16),
        compiler_params=pltpu.CompilerParams(
            dimension_semantics=("arbitrary", "arbitrary"),
            vmem_limit_bytes=VMEM_LIMIT_BYTES),
        name="cast_bf16",
    )(w, col_scale)


def _mod_kernel(c_ref, w_ref, b_ref, o_ref):
    c = c_ref[...]
    s = (c * _sigmoid(c)).astype(BF16)
    o_ref[...] = _dot(s, w_ref[...].astype(BF16)) + b_ref[...]


def _modulation(cs, w_mod, b_mod):
    n_layers, d, width = w_mod.shape
    tn = 1536
    return pl.pallas_call(
        _mod_kernel,
        grid=(n_layers, width // tn),
        in_specs=[
            pl.BlockSpec((SUBLANES, d), lambda l, j: (0, 0)),
            pl.BlockSpec((None, d, tn), lambda l, j: (l, 0, j)),
            pl.BlockSpec((None, 1, tn), lambda l, j: (l, 0, j)),
        ],
        out_specs=pl.BlockSpec((None, SUBLANES, tn), lambda l, j: (l, 0, j)),
        out_shape=jax.ShapeDtypeStruct((n_layers, SUBLANES, width), F32),
        compiler_params=pltpu.CompilerParams(
            dimension_semantics=("arbitrary", "arbitrary"),
            vmem_limit_bytes=VMEM_LIMIT_BYTES),
        name="modulation",
    )(cs, w_mod, b_mod.reshape(n_layers, 1, width))


def _rope(t, cos_ref, sin_ref, n_heads):
    cos = cos_ref[...]
    sin = sin_ref[...]
    lane = lax.broadcasted_iota(jnp.int32, cos.shape, 1)
    first = (lane & (2 * N_FREQ - 1)) < N_FREQ
    outs = []
    for h in range(n_heads):
        th = t[:, h * HEAD_DIM:(h + 1) * HEAD_DIM]
        partner = jnp.where(first,
                            pltpu.roll(th, HEAD_DIM - N_FREQ, axis=1),
                            pltpu.roll(th, N_FREQ, axis=1))
        outs.append(th * cos + partner * sin)
    return jnp.concatenate(outs, axis=-1)


def _stream_specs(stream, tm, d, n_lat_tiles):
    _, ctx, ctx_off = stream
    last_ctx = ctx.shape[0] // tm - 1
    lat_spec = pl.BlockSpec((tm, d), lambda i: (jnp.minimum(i, n_lat_tiles - 1), 0))
    ctx_spec = pl.BlockSpec(
        (tm, d), lambda i: (jnp.minimum(jnp.maximum(i - n_lat_tiles, 0) + ctx_off, last_ctx), 0))
    return lat_spec, ctx_spec


def _mod_spec(geo, layer, tiles_per_step=1):
    return pl.BlockSpec(
        (None, None, MOD_CHUNKS, D_MODEL),
        lambda i: (layer, geo["mod_row"](i * tiles_per_step), 0, 0))


def _stream_tile(lat_ref, ctx_ref, n_lat_tiles):
    return jnp.where(pl.program_id(0) < n_lat_tiles, lat_ref[...], ctx_ref[...])


def _in_kernel(xl_ref, xc_ref, mod_ref, g_ref, w_ref, cq_ref, sq_ref, ck_ref, sk_ref, *rest,
               n_lat_tiles, n_casts):
    cast_in, rest = rest[:2 * n_casts], rest[2 * n_casts:]
    (xr_ref, gr_ref, q_ref, k_ref, vt_ref, gl_ref), cast_out = rest[:6], rest[6:]
    for c in range(n_casts):
        cast_out[c][...] = (cast_in[2 * c][...] * cast_in[2 * c + 1][...]).astype(BF16)
    d = D_MODEL
    x = _stream_tile(xl_ref, xc_ref, n_lat_tiles)
    h = (_rms(x) * g_ref[...]) * (1.0 + mod_ref[1:2, :]) + mod_ref[0:1, :]
    h = h.astype(BF16)
    xr = _dot(h, w_ref[:, 0:d])
    for n in range(N_RNN_BLOCKS):
        xr_ref[pl.ds(n, x.shape[0], stride=N_RNN_BLOCKS), :] = (
            xr[:, n * RNN_BLOCK_W:(n + 1) * RNN_BLOCK_W])
    gr_ref[...] = _dot(h, w_ref[:, d:2 * d]).astype(BF16)
    q = _dot(h, w_ref[:, 2 * d:3 * d])
    q_ref[...] = _rope(q, cq_ref, sq_ref, N_Q_HEADS).astype(BF16)
    k = _dot(h, w_ref[:, 3 * d:3 * d + KV_WIDTH])
    k_ref[...] = _rope(k, ck_ref, sk_ref, N_KV_HEADS).astype(BF16)
    v = _dot(h, w_ref[:, 3 * d + KV_WIDTH:3 * d + 2 * KV_WIDTH])
    for t in range(vt_ref.shape[0]):
        vt_ref[t] = v[t * KEY_BLOCK:(t + 1) * KEY_BLOCK, :].T.astype(BF16)
    gl_ref[...] = _dot(h, w_ref[:, 3 * d + 2 * KV_WIDTH:5 * d + 2 * KV_WIDTH]).astype(BF16)


def _in_proj(stream, mod, g_pre, w_in, layer, tables, geo, side_casts):
    d = D_MODEL
    tm = geo["tm"]
    n_tiles = geo["n_lat_tiles"] + geo["n_ctx_tiles"]
    nt = n_tiles * tm
    row = lambda i: (i, 0)
    tab = pl.BlockSpec((tm, LANES), lambda i: (geo["table_block"](i), 0))
    cast_in_specs, cast_out_specs, cast_shapes, cast_args = [], [], [], []
    for w, scale in side_casts:
        _, rows, cols = w.shape
        rb = next(r for r in range(BF16_SUBLANES, rows + 1, BF16_SUBLANES)
                  if rows % r == 0 and rows // r <= n_tiles)
        n_blk = rows // rb
        blk = lambda i, n_blk=n_blk: jnp.minimum(i, n_blk - 1)
        cast_in_specs += [pl.BlockSpec((None, rb, cols), lambda i, blk=blk: (layer, blk(i), 0)),
                          pl.BlockSpec((1, cols), lambda i: (0, 0))]
        cast_out_specs.append(pl.BlockSpec((rb, cols), lambda i, blk=blk: (blk(i), 0)))
        cast_shapes.append(jax.ShapeDtypeStruct((rows, cols), BF16))
        cast_args += [w, scale]
    outs = pl.pallas_call(
        functools.partial(_in_kernel, n_lat_tiles=geo["n_lat_tiles"], n_casts=len(side_casts)),
        grid=(n_tiles,),
        in_specs=[
            *_stream_specs(stream, tm, d, geo["n_lat_tiles"]),
            _mod_spec(geo, layer),
            _resident_layer(g_pre, layer),
            _resident_layer(w_in, layer),
            tab, tab, tab, tab,
            *cast_in_specs,
        ],
        out_specs=[
            pl.BlockSpec((tm * N_RNN_BLOCKS, RNN_BLOCK_W), row),
            pl.BlockSpec((tm, d), row),
            pl.BlockSpec((tm, d), row),
            pl.BlockSpec((tm, KV_WIDTH), row),
            pl.BlockSpec((tm // KEY_BLOCK, KV_WIDTH, KEY_BLOCK), lambda i: (i, 0, 0)),
            pl.BlockSpec((tm, 2 * d), row),
            *cast_out_specs,
        ],
        out_shape=[
            jax.ShapeDtypeStruct((nt * N_RNN_BLOCKS, RNN_BLOCK_W), F32),
            jax.ShapeDtypeStruct((nt, d), BF16),
            jax.ShapeDtypeStruct((nt, d), BF16),
            jax.ShapeDtypeStruct((nt, KV_WIDTH), BF16),
            jax.ShapeDtypeStruct((nt // KEY_BLOCK, KV_WIDTH, KEY_BLOCK), BF16),
            jax.ShapeDtypeStruct((nt, 2 * d), BF16),
            *cast_shapes,
        ],
        compiler_params=pltpu.CompilerParams(
            dimension_semantics=("arbitrary",),
            vmem_limit_bytes=VMEM_LIMIT_BYTES),
        name="in_proj",
    )(stream[0], stream[1], mod, g_pre, w_in, *tables, *cast_args)
    return outs[:6], outs[6:]


def _rnn_kernel(xf_ref, xfp_ref, xfn_ref, xb_ref, xbp_ref, xbn_ref,
                cw_ref, cb_ref, wg_ref, bg_ref, lam_ref,
                hf_ref, hb_ref,
                xc_s, a_s, b_s, carry_s, *, n_steps):
    j = pl.program_id(1)
    nb, bw = N_RNN_BLOCKS, RNN_BLOCK_W
    n_in = CONV_T + CONV_W - 1
    right = CONV_W - 1 - CONV_LEFT

    @pl.when(j == 0)
    def _():
        carry_s[...] = jnp.zeros_like(carry_s)

    def conv(slot, x_ref, xp_ref, xn_ref, has_prev, has_next):
        def conv_piece(v, base):
            v = v.reshape(n_in, nb, bw)
            acc = cb_ref[...] + v[0:CONV_T] * cw_ref[0]
            for k in range(1, CONV_W):
                acc = acc + v[k:k + CONV_T] * cw_ref[k]
            xc_s[slot, pl.ds(base, CONV_T * nb), :] = acc.reshape(CONV_T * nb, bw)

        halo_rows = xp_ref.shape[0]
        left = jnp.where(has_prev, xp_ref[halo_rows - CONV_LEFT * nb:halo_rows, :], 0.0)
        conv_piece(jnp.concatenate([left, x_ref[0:(n_in - CONV_LEFT) * nb, :]], axis=0), 0)
        tail = jnp.where(has_next, xn_ref[0:right * nb, :], 0.0)
        last = CHUNK - CONV_T
        conv_piece(jnp.concatenate(
            [x_ref[(last - CONV_LEFT) * nb:CHUNK * nb, :], tail], axis=0), last * nb)

        def conv_body(c, carry):
            base = pl.multiple_of(c * (CONV_T * nb), CONV_T * nb)
            conv_piece(x_ref[pl.ds(base - CONV_LEFT * nb, n_in * nb), :], base)
            return carry

        lax.fori_loop(1, CHUNK // CONV_T - 1, conv_body, 0)

    fwd_chunk = j
    bwd_chunk = jnp.where(j == 0, 0, n_steps - j)
    mid = jnp.logical_and(j >= 1, j < n_steps - 1)

    @pl.when(2 * j <= n_steps)
    def _():
        conv(fwd_chunk, xf_ref, xfp_ref, xfn_ref, j >= 2, mid)

    @pl.when(jnp.logical_and(j >= 1, 2 * j < n_steps))
    def _():
        conv(bwd_chunk, xb_ref, xbp_ref, xbn_ref, mid, j >= 2)

    def gate_unit(parity, direction, n):
        rows = pl.ds(n, CHUNK, stride=nb)
        lam = lam_ref[direction, n]
        c_half = (0.5 * LRU_C * LOG2_E) * (
            jnp.minimum(lam, 0.0) - jnp.log1p(jnp.exp(-jnp.abs(lam))))
        xh = xc_s[bwd_chunk if direction else fwd_chunk, rows, :]
        z = _dot(xh.astype(BF16), wg_ref[direction, n]) + bg_ref[direction, n]
        a = jnp.exp2(c_half * jnp.tanh(z[:, 0:bw]) + c_half)
        gated = xh * jnp.tanh(z[:, bw:2 * bw]) + xh
        om = 1.0 - a * a
        a_s[parity, direction, rows, :] = a
        b_s[parity, direction, rows, :] = (om * lax.rsqrt(jnp.maximum(om, F32_TINY))) * gated

    def scan_steps(parity, t0, t1, carry):
        hf, hb = carry
        for t in range(t0, t1):
            tf, tb = t * nb, (CHUNK - 1 - t) * nb
            hf = a_s[parity, 0, tf:tf + nb, :] * hf + b_s[parity, 0, tf:tf + nb, :]
            hb = a_s[parity, 1, tb:tb + nb, :] * hb + b_s[parity, 1, tb:tb + nb, :]
            hf_ref[tf:tf + nb, :] = hf
            hb_ref[tb:tb + nb, :] = hb
        return hf, hb

    def work(parity, do_gates, do_scan):
        units = [(direction, n) for direction in range(2) for n in range(nb)]
        per_unit = CHUNK // len(units)
        carry = (carry_s[0], carry_s[1]) if do_scan else None
        for idx, (direction, n) in enumerate(units):
            if do_gates:
                gate_unit(parity, direction, n)
            if do_scan:
                carry = scan_steps(1 - parity, idx * per_unit, (idx + 1) * per_unit, carry)
        if do_scan:
            carry_s[0], carry_s[1] = carry

    last_parity = (n_steps - 1) % 2
    pl.when(j == 0)(lambda: work(0, True, False))
    pl.when(j == n_steps)(lambda: work(1 - last_parity, False, True))
    inner = jnp.logical_and(j >= 1, j < n_steps)
    pl.when(jnp.logical_and(inner, j % 2 == 0))(lambda: work(0, True, True))
    pl.when(jnp.logical_and(inner, j % 2 == 1))(lambda: work(1, True, True))


def _rnn_branch(xr, conv_w, conv_b, w_gate, b_gate, lam, layer, geo):
    nb, bw = N_RNN_BLOCKS, RNN_BLOCK_W
    bsz, n_lat = geo["batch"], geo["lat_chunks"]
    n_steps = n_lat + 1
    halo = SUBLANES
    n_halo = xr.shape[0] // (halo * nb)
    per = CHUNK // halo

    def fwd_blk(b, s):
        return jnp.where(s == 0, bsz * n_lat + b, b * n_lat + s - 1)

    def bwd_blk(b, s):
        return jnp.where(s == 0, bsz * n_lat + b, b * n_lat + n_lat - s)

    in_step = lambda j: jnp.minimum(j, n_steps - 1)
    out_step = lambda j: jnp.maximum(j - 1, 0)

    def chunk(blk, step):
        return pl.BlockSpec((CHUNK * nb, bw), lambda b, j: (blk(b, step(j)), 0))

    def prev(blk):
        return pl.BlockSpec(
            (halo * nb, bw), lambda b, j: (jnp.maximum(blk(b, in_step(j)) * per - 1, 0), 0))

    def nxt(blk):
        return pl.BlockSpec(
            (halo * nb, bw),
            lambda b, j: (jnp.minimum((blk(b, in_step(j)) + 1) * per, n_halo - 1), 0))

    return pl.pallas_call(
        functools.partial(_rnn_kernel, n_steps=n_steps),
        grid=(bsz, n_steps + 1),
        in_specs=[
            chunk(fwd_blk, in_step), prev(fwd_blk), nxt(fwd_blk),
            chunk(bwd_blk, in_step), prev(bwd_blk), nxt(bwd_blk),
            _resident_layer(conv_w, layer),
            _resident_layer(conv_b, layer),
            _resident_layer(w_gate, layer),
            _resident_layer(b_gate, layer),
            _resident_layer(lam, layer),
        ],
        out_specs=[chunk(fwd_blk, out_step), chunk(bwd_blk, out_step)],
        out_shape=[jax.ShapeDtypeStruct(xr.shape, F32)] * 2,
        scratch_shapes=[
            pltpu.VMEM((n_steps, CHUNK * nb, bw), F32),
            pltpu.VMEM((2, 2, CHUNK * nb, bw), F32),
            pltpu.VMEM((2, 2, CHUNK * nb, bw), F32),
            pltpu.VMEM((2, nb, bw), F32),
        ],
        compiler_params=pltpu.CompilerParams(
            dimension_semantics=("arbitrary", "arbitrary"),
            vmem_limit_bytes=VMEM_LIMIT_BYTES),
        name="rglru",
    )(xr, xr, xr, xr, xr, xr, conv_w, conv_b, w_gate, b_gate, lam)


def _stack_heads(q, g):
    base = g * Q_PER_KV * HEAD_DIM
    return jnp.concatenate(
        [q[:, base + h * HEAD_DIM: base + (h + 1) * HEAD_DIM] for h in range(Q_PER_KV)], axis=0)


def _nt_dot(a, b):
    return lax.dot_general(a, b, (((1,), (1,)), ((), ())), preferred_element_type=F32)


def _attend_units(sink_ref, layer, units):
    folds = KEY_BLOCK // SUBLANES
    for u in units:
        n_q = u["q"].shape[0]
        width = Q_PER_KV * n_q
        qs = _stack_heads(u["q"], u["g"])
        sink = jnp.concatenate(
            [jnp.full((1, n_q), sink_ref[layer, u["g"] * Q_PER_KV + h] * LOG2_E, F32)
             for h in range(Q_PER_KV)], axis=1)
        m8 = jnp.broadcast_to(sink, (SUBLANES, width))
        for j, (k, bias) in enumerate(u["key_blocks"]):
            s = _nt_dot(k, qs)
            if bias is not None:
                s = s + jnp.concatenate([bias] * Q_PER_KV, axis=1)
            u["s_s"][j * KEY_BLOCK:(j + 1) * KEY_BLOCK, :] = s
            m8 = jnp.maximum(m8, jnp.max(s.reshape(folds, SUBLANES, width), axis=0))
        u["sink"] = sink
        u["m"] = jnp.max(m8, axis=0, keepdims=True)
    for u in units:
        width = Q_PER_KV * u["q"].shape[0]
        l8 = jnp.zeros((SUBLANES, width), F32)
        for j in range(len(u["key_blocks"])):
            p = jnp.exp2(u["s_s"][j * KEY_BLOCK:(j + 1) * KEY_BLOCK, :] - u["m"])
            l8 = l8 + jnp.sum(p.reshape(folds, SUBLANES, width), axis=0)
            u["p_s"][j * KEY_BLOCK:(j + 1) * KEY_BLOCK, :] = p.astype(BF16)
        u["denom"] = jnp.sum(l8, axis=0, keepdims=True) + jnp.exp2(u["sink"] - u["m"])
    for u in units:
        n_q = u["q"].shape[0]
        n_keys = len(u["key_blocks"]) * KEY_BLOCK
        ot = _dot(u["vt"], u["p_s"][0:n_keys, :]) * (1.0 / u["denom"])
        for h in range(Q_PER_KV):
            u["o_store"](h, ot[:, h * n_q:(h + 1) * n_q].T)


def _lat_attn_kernel(sink_ref, q_ref, k_ref, vt_ref, kc_ref, vtc_ref, o_ref, s_s, p_s,
                     *, seq_len, layer):
    n_blk = q_ref.shape[0] // Q_BLOCK
    band_blocks = BAND // KEY_BLOCK
    ctx_blocks = vtc_ref.shape[0]
    last_start = seq_len // KEY_BLOCK - band_blocks
    rel = (lax.broadcasted_iota(jnp.int32, (KEY_BLOCK, Q_BLOCK), 0)
           - lax.broadcasted_iota(jnp.int32, (KEY_BLOCK, Q_BLOCK), 1))

    def block_units(i, slot):
        units = []
        qb = pl.program_id(1) * n_blk + i
        jb0 = jnp.clip(qb - WINDOW // KEY_BLOCK, 0, last_start)
        biases = [jnp.where(jnp.abs(rel + (jb0 + t - qb) * KEY_BLOCK) <= WINDOW, 0.0, NEG_INF)
                  for t in range(band_blocks)]
        rows = pl.ds(pl.multiple_of(i * Q_BLOCK, Q_BLOCK), Q_BLOCK)
        q = q_ref[rows, :]
        for g in range(N_KV_HEADS):
            gs = slice(g * HEAD_DIM, (g + 1) * HEAD_DIM)
            key_blocks = [
                (k_ref[pl.ds(pl.multiple_of((jb0 + t) * KEY_BLOCK, KEY_BLOCK), KEY_BLOCK), gs],
                 biases[t]) for t in range(band_blocks)]
            key_blocks += [(kc_ref[t * KEY_BLOCK:(t + 1) * KEY_BLOCK, gs], None)
                           for t in range(ctx_blocks)]
            vt = jnp.concatenate([vt_ref[jb0 + t, gs, :] for t in range(band_blocks)]
                                 + [vtc_ref[t, gs, :] for t in range(ctx_blocks)], axis=1)

            def o_store(h, tile, g=g, rows=rows):
                c0 = (g * Q_PER_KV + h) * HEAD_DIM
                o_ref[rows, c0:c0 + HEAD_DIM] = tile.astype(BF16)

            units.append(dict(g=g, q=q, key_blocks=key_blocks, vt=vt,
                              s_s=s_s.at[slot, g], p_s=p_s.at[slot, g], o_store=o_store))
        return units

    def body(it, carry):
        units = []
        for slot in range(Q_SLOTS):
            units += block_units(it * Q_SLOTS + slot, slot)
        _attend_units(sink_ref, layer, units)
        return carry

    lax.fori_loop(0, n_blk // Q_SLOTS, body, 0)


def _ctx_attn_kernel(sink_ref, q_ref, kc_ref, vtc_ref, o_ref, s_s, p_s, *, layer):
    ctx_blocks = vtc_ref.shape[0]
    q = q_ref[...]
    units = []
    for g in range(N_KV_HEADS):
        gs = slice(g * HEAD_DIM, (g + 1) * HEAD_DIM)
        key_blocks = [(kc_ref[t * KEY_BLOCK:(t + 1) * KEY_BLOCK, gs], None)
                      for t in range(ctx_blocks)]
        vt = jnp.concatenate([vtc_ref[t, gs, :] for t in range(ctx_blocks)], axis=1)

        def o_store(h, tile, g=g):
            c0 = (g * Q_PER_KV + h) * HEAD_DIM
            o_ref[:, c0:c0 + HEAD_DIM] = tile.astype(BF16)

        units.append(dict(g=g, q=q, key_blocks=key_blocks, vt=vt,
                          s_s=s_s.at[g], p_s=p_s.at[g], o_store=o_store))
    _attend_units(sink_ref, layer, units)


def _attention(q, k, vt, sink, layer, geo, with_ctx_queries):
    nt, d = q.shape
    bsz, seq_len, n_ctx = geo["batch"], geo["seq"], geo["ctx"]
    q_sup = next(s for s in (1024, 512, 256, Q_BLOCK) if seq_len % s == 0)
    n_sup = seq_len // q_sup
    ctx_blk0 = bsz * seq_len // n_ctx
    smem = pl.BlockSpec(memory_space=pltpu.SMEM)
    ctx_keys = pl.BlockSpec((n_ctx, KV_WIDTH), lambda b, *_: (ctx_blk0 + b, 0))
    ctx_vals = pl.BlockSpec((n_ctx // KEY_BLOCK, KV_WIDTH, KEY_BLOCK),
                            lambda b, *_: (ctx_blk0 + b, 0, 0))
    o = pl.pallas_call(
        functools.partial(_lat_attn_kernel, seq_len=seq_len, layer=layer),
        grid=(bsz, n_sup),
        in_specs=[
            smem,
            pl.BlockSpec((q_sup, d), lambda b, i: (b * n_sup + i, 0)),
            pl.BlockSpec((seq_len, KV_WIDTH), lambda b, i: (b, 0)),
            pl.BlockSpec((seq_len // KEY_BLOCK, KV_WIDTH, KEY_BLOCK), lambda b, i: (b, 0, 0)),
            ctx_keys, ctx_vals,
        ],
        out_specs=pl.BlockSpec((q_sup, d), lambda b, i: (b * n_sup + i, 0)),
        out_shape=jax.ShapeDtypeStruct((bsz * seq_len, d), BF16),
        scratch_shapes=[
            pltpu.VMEM((Q_SLOTS, N_KV_HEADS, BAND + n_ctx, Q_PER_KV * Q_BLOCK), F32),
            pltpu.VMEM((Q_SLOTS, N_KV_HEADS, BAND + n_ctx, Q_PER_KV * Q_BLOCK), BF16),
        ],
        compiler_params=pltpu.CompilerParams(
            dimension_semantics=("arbitrary", "arbitrary"),
            vmem_limit_bytes=VMEM_LIMIT_BYTES),
        name="lat_attention",
    )(sink, q, k, vt, k, vt)
    if not with_ctx_queries:
        return (o, o, 0)
    o_ctx = pl.pallas_call(
        functools.partial(_ctx_attn_kernel, layer=layer),
        grid=(bsz,),
        in_specs=[
            smem,
            pl.BlockSpec((n_ctx, d), lambda b: (ctx_blk0 + b, 0)),
            ctx_keys, ctx_vals,
        ],
        out_specs=pl.BlockSpec((n_ctx, d), lambda b: (b, 0)),
        out_shape=jax.ShapeDtypeStruct((bsz * n_ctx, d), BF16),
        scratch_shapes=[
            pltpu.VMEM((N_KV_HEADS, n_ctx, Q_PER_KV * n_ctx), F32),
            pltpu.VMEM((N_KV_HEADS, n_ctx, Q_PER_KV * n_ctx), BF16),
        ],
        compiler_params=pltpu.CompilerParams(
            dimension_semantics=("arbitrary",),
            vmem_limit_bytes=VMEM_LIMIT_BYTES),
        name="ctx_attention",
    )(sink, q, k, vt)
    return (o, o_ctx, 0)


def _merge_kernel(xl_ref, xc_ref, hf_ref, hb_ref, gr_ref, ol_ref, oc_ref, gl_ref, mod_ref,
                  g_ref, wr_ref, wa_ref, wo_ref, out_ref, *, n_lat_tiles):
    d = D_MODEL
    nb = N_RNN_BLOCKS
    rows_per_part = xl_ref.shape[0] // MERGE_PARTS
    is_lat = pl.program_id(0) < n_lat_tiles
    parts =[slice(p * rows_per_part, (p + 1) * rows_per_part) for p in range(MERGE_PARTS)]

    def rnn_input(p):
        h = jnp.concatenate(
            [hf_ref[pl.ds(p * rows_per_part * nb + n, rows_per_part, stride=nb), :]
             + hb_ref[pl.ds(p * rows_per_part * nb + n, rows_per_part, stride=nb), :]
             for n in range(nb)], axis=-1)
        g = gr_ref[parts[p], :].astype(F32)
        hg = h * g
        return (hg * jnp.tanh(g * (GELU_C1 + GELU_C2 * (g * g))) + hg).astype(BF16)

    def gated_mix(p, y):
        r = parts[p]
        ya = _dot(y, wr_ref[...])
        yb = _dot(jnp.where(is_lat, ol_ref[r, :], oc_ref[r, :]), wa_ref[...])
        ta = jnp.tanh(gl_ref[r, 0:d].astype(F32))
        tb = jnp.tanh(gl_ref[r, d:2 * d].astype(F32))
        return ((ya * ta + ya) + (yb * tb + yb)).astype(BF16)

    def project(p, mix):
        r = parts[p]
        m = _dot(mix, wo_ref[...])
        x = jnp.where(is_lat, xl_ref[r, :], xc_ref[r, :])
        out_ref[r, :] = x + mod_ref[2:3, :] * (_rms(m) * g_ref[...])

    ys = [rnn_input(p) for p in range(MERGE_PARTS)]
    mixes = [gated_mix(p, ys[p]) for p in range(MERGE_PARTS)]
    for p in range(MERGE_PARTS):
        project(p, mixes[p])


def _merge(x_stream, hf, hb, gr, o_stream, gl, mod, g_post, w_o_rnn, w_o_attn, w_out,
           layer, geo, n_tiles):
    d = D_MODEL
    tm = geo["tm"]
    n_lat_tiles = geo["n_lat_tiles"]
    row = lambda i: (i, 0)
    tile = pl.BlockSpec((tm, d), row)
    slab = pl.BlockSpec((tm * N_RNN_BLOCKS, RNN_BLOCK_W), row)
    return pl.pallas_call(
        functools.partial(_merge_kernel, n_lat_tiles=n_lat_tiles),
        grid=(n_tiles,),
        in_specs=[
            *_stream_specs(x_stream, tm, d, n_lat_tiles), slab, slab, tile,
            *_stream_specs(o_stream, tm, d, n_lat_tiles),
            pl.BlockSpec((tm, 2 * d), row),
            _mod_spec(geo, layer),
            _resident_layer(g_post, layer),
            _resident(w_o_rnn.shape), _resident(w_o_attn.shape), _resident(w_out.shape),
        ],
        out_specs=tile,
        out_shape=jax.ShapeDtypeStruct((n_tiles * tm, d), F32),
        compiler_params=pltpu.CompilerParams(
            dimension_semantics=("arbitrary",),
            vmem_limit_bytes=VMEM_LIMIT_BYTES),
        name="merge",
    )(x_stream[0], x_stream[1], hf, hb, gr, o_stream[0], o_stream[1], gl, mod,
      g_post, w_o_rnn, w_o_attn, w_out)


def _ffn_kernel(x_ref, mod_ref, gpre_ref, gpost_ref, w1_ref, w2_ref, out_ref):
    sub = x_ref.shape[0] // FFN_SUBTILES
    bounds = list(range(0, D_FF, FFN_CHUNK)) + [D_FF]
    n_chunks = len(bounds) - 1

    def normed(s):
        x = x_ref[s * sub:(s + 1) * sub, :]
        h = (_rms(x) * gpre_ref[...]) * (1.0 + mod_ref[4:5, :]) + mod_ref[3:4, :]
        return h.astype(BF16)

    def first_layer(h, c):
        lo, hi = bounds[c], bounds[c + 1]
        return _dot(h, w1_ref[:, lo:hi]), _dot(h, w1_ref[:, D_FF + lo:D_FF + hi])

    h_next = normed(0)
    for s in range(FFN_SUBTILES):
        h = h_next
        if s + 1 < FFN_SUBTILES:
            h_next = normed(s + 1)
        f = None
        nxt = first_layer(h, 0)
        for c in range(n_chunks):
            half_gate, up = nxt
            if c + 1 < n_chunks:
                nxt = first_layer(h, c + 1)
            act = ((half_gate * jnp.tanh(half_gate) + half_gate) * up).astype(BF16)
            part = _dot(act, w2_ref[bounds[c]:bounds[c + 1], :])
            f = part if f is None else f + part
        rows = slice(s * sub, (s + 1) * sub)
        out_ref[rows, :] = x_ref[rows, :] + mod_ref[5:6, :] * (_rms(f) * gpost_ref[...])


def _ffn(x_all, mod, g_pre, g_post, w1, w2, layer, geo, n_tiles):
    d = x_all.shape[1]
    tm = geo["tm"] * FFN_SUBTILES
    n_blocks = n_tiles // FFN_SUBTILES
    tile = pl.BlockSpec((tm, d), lambda i: (i, 0))
    return pl.pallas_call(
        _ffn_kernel,
        grid=(n_blocks,),
        in_specs=[
            tile,
            _mod_spec(geo, layer, FFN_SUBTILES),
            _resident_layer(g_pre, layer), _resident_layer(g_post, layer),
            _resident(w1.shape), _resident(w2.shape),
        ],
        out_specs=tile,
        out_shape=jax.ShapeDtypeStruct((n_blocks * tm, d), F32),
        compiler_params=pltpu.CompilerParams(
            dimension_semantics=("arbitrary",),
            vmem_limit_bytes=VMEM_LIMIT_BYTES),
        name="ffn",
    )(x_all, mod, g_pre, g_post, w1, w2)


def _rope_tables(seq_len, pad_rows):
    f32 = np.float32
    n_rows = seq_len // GRID_W
    inv = f32(ROPE_BASE) ** (-np.arange(N_FREQ, dtype=f32) / f32(N_FREQ))
    ang = np.arange(max(n_rows, GRID_W), dtype=f32)[:, None] * inv[None, :]
    cos_u, sin_u = np.cos(ang), np.sin(ang)
    by_row = lambda t: np.repeat(t[:n_rows], GRID_W, axis=0)
    by_col = lambda t: np.tile(t[:GRID_W], (n_rows, 1))
    cos = np.concatenate([by_row(cos_u)] * 2 + [by_col(cos_u)] * 2, axis=-1)
    sin = np.concatenate(
        [-by_row(sin_u), by_row(sin_u), -by_col(sin_u), by_col(sin_u)], axis=-1)
    cos = np.concatenate([cos, np.ones((pad_rows, HEAD_DIM), f32)], axis=0)
    sin = np.concatenate([sin, np.zeros((pad_rows, HEAD_DIM), f32)], axis=0)
    scale = f32(HEAD_DIM ** -0.5 * LOG2_E)
    return tuple(jnp.asarray(t.astype(f32)) for t in (cos * scale, sin * scale, cos, sin))


def kernel(x, c, ctx, c_ctx, w_mod, b_mod, g_mix_pre, g_mix_post, g_ffn_pre, g_ffn_post, w_in, conv_w, conv_b, lru_wa, lru_ba, lru_wx, lru_bx, lru_lam, attn_sink, w_o_rnn, w_o_attn, w_out, w_ffn_in, w_ffn_out):
    bsz, seq_len, d = x.shape
    n_ctx = ctx.shape[1]
    depth = w_mod.shape[0]
    assert d == D_MODEL and n_ctx == CHUNK and seq_len % CHUNK == 0 and seq_len >= BAND
    assert bsz + 1 <= SUBLANES and seq_len % GRID_W == 0

    n_lat_rows, n_ctx_rows = bsz * seq_len, bsz * n_ctx
    big = 512 * FFN_SUBTILES
    tm = 512 if (n_ctx_rows % big == 0 and seq_len % big == 0) else CHUNK
    assert n_ctx_rows % (tm * FFN_SUBTILES) == 0 and seq_len % (tm * FFN_SUBTILES) == 0
    n_lat_tiles, n_ctx_tiles = n_lat_rows // tm, n_ctx_rows // tm
    tiles_per_batch = seq_len // tm
    geo = {
        "batch": bsz, "seq": seq_len, "ctx": n_ctx, "tm": tm,
        "lat_chunks": seq_len // CHUNK,
        "n_lat_tiles": n_lat_tiles, "n_ctx_tiles": n_ctx_tiles,
        "mod_row": lambda i: jnp.where(i < n_lat_tiles, i // tiles_per_batch, bsz),
        "table_block": lambda i: jnp.where(
            i < n_lat_tiles, i % tiles_per_batch, tiles_per_batch + i - n_lat_tiles),
    }

    cs = jnp.concatenate(
        [c, c_ctx[None, :], jnp.zeros((SUBLANES - bsz - 1, d), F32)], axis=0)
    mod = _modulation(cs, w_mod, b_mod).reshape(depth, SUBLANES, MOD_CHUNKS, d)

    tables = _rope_tables(seq_len, n_ctx_rows)

    w_gate = jnp.concatenate([lru_wa, lru_wx], axis=-1).astype(BF16)
    b_gate = 0.5 * jnp.concatenate(
        [lru_ba.reshape(depth, 2, N_RNN_BLOCKS, 1, RNN_BLOCK_W),
         lru_bx.reshape(depth, 2, N_RNN_BLOCKS, 1, RNN_BLOCK_W)], axis=-1)
    lam = lru_lam.reshape(depth, 2, N_RNN_BLOCKS, 1, RNN_BLOCK_W)
    conv_w_half = (0.5 * conv_w).reshape(depth, CONV_W, N_RNN_BLOCKS, RNN_BLOCK_W)
    conv_b_half = (0.5 * conv_b).reshape(depth, N_RNN_BLOCKS, RNN_BLOCK_W)
    g_mix_pre, g_mix_post, g_ffn_pre, g_ffn_post = (
        g.reshape(depth, 1, d) for g in (g_mix_pre, g_mix_post, g_ffn_pre, g_ffn_post))

    stream = (x.reshape(n_lat_rows, d), ctx.reshape(n_ctx_rows, d), 0)
    ones = lambda n: jnp.ones((1, n), F32)
    halves = lambda n: jnp.full((1, n), 0.5, F32)
    gl_cols = 2 * d
    w_in_b = _to_bf16(w_in, 256, jnp.concatenate(
        [ones(w_in.shape[-1] - gl_cols), halves(gl_cols)], axis=1))
    side_casts = [
        (w_o_rnn, halves(d)), (w_o_attn, ones(d)), (w_out, halves(d)),
        (w_ffn_in, jnp.concatenate([halves(D_FF), ones(D_FF)], axis=1)),
        (w_ffn_out, ones(d)),
    ]
    for l in range(depth):
        need_ctx = l < depth - 1
        n_out_tiles = n_lat_tiles + n_ctx_tiles if need_ctx else n_lat_tiles
        (xr, gr, q, k, vt, gl), layer_weights = _in_proj(
            stream, mod, g_mix_pre, w_in_b, l, tables, geo, side_casts)
        w_o_rnn_b, w_o_attn_b, w_out_b, w_ffn_in_b, w_ffn_out_b = layer_weights
        hf, hb = _rnn_branch(xr, conv_w_half, conv_b_half, w_gate, b_gate, lam, l, geo)
        o_stream = _attention(q, k, vt, attn_sink, l, geo, need_ctx)
        x_all = _merge(stream, hf, hb, gr, o_stream, gl, mod, g_mix_post,
                       w_o_rnn_b, w_o_attn_b, w_out_b, l, geo, n_out_tiles)
        x_all = _ffn(x_all, mod, g_ffn_pre, g_ffn_post,
                     w_ffn_in_b, w_ffn_out_b, l, geo, n_out_tiles)
        stream = (x_all, x_all, n_lat_tiles)
    return x_all[:n_lat_rows].reshape(bsz, seq_len, d)
```

```python
import functools
import math

import jax
import jax.numpy as jnp
import numpy as np
from jax import lax
from jax.experimental import pallas as pl
from jax.experimental.pallas import tpu as pltpu

D_MODEL = 1024
HEAD_DIM = 128
N_Q_HEADS = 8
N_KV_HEADS = 2
Q_PER_KV = N_Q_HEADS // N_KV_HEADS
KV_WIDTH = N_KV_HEADS * HEAD_DIM
WINDOW = 128
GRID_W = 64
N_FREQ = HEAD_DIM // 4
ROPE_BASE = 10000.0
N_RNN_BLOCKS = 8
RNN_BLOCK_W = D_MODEL // N_RNN_BLOCKS
LRU_C = 8.0
CONV_W = 4
CONV_LEFT = 2
D_FF = 2816
EPS = 1e-6
NEG_INF = -1e30
MOD_CHUNKS = 6

LANES = 128
SUBLANES = 8
BF16_SUBLANES = 16
VMEM_LIMIT_BYTES = 56 * 1024 * 1024

CHUNK = 256
CONV_T = 16
SCAN_UNROLL = 8
MERGE_PARTS = 1
FFN_CHUNK = 768
FFN_SUBTILES = 2
Q_BLOCK = 128
Q_SLOTS = 4
KEY_BLOCK = 128
BAND = Q_BLOCK + 2 * WINDOW
LOG2_E = math.log2(math.e)

BF16 = jnp.bfloat16
F32 = jnp.float32
F32_TINY = float(jnp.finfo(jnp.float32).tiny)


def _dot(a, b):
    return jnp.dot(a, b, preferred_element_type=F32)


def _sigmoid(x):
    return 0.5 * jnp.tanh(0.5 * x) + 0.5


GELU_C1 = math.sqrt(2.0 / math.pi)
GELU_C2 = 0.044715 * GELU_C1


def _rms(x):
    return x * lax.rsqrt(jnp.mean(x * x, axis=-1, keepdims=True) + EPS)


def _resident(shape):
    nd = len(shape)
    return pl.BlockSpec(shape, lambda *_: (0,) * nd, pipeline_mode=pl.Buffered(1))


def _resident_layer(stacked, layer):
    tail = stacked.shape[1:]
    return pl.BlockSpec((None,) + tail, lambda *_: (layer,) + (0,) * len(tail),
                        pipeline_mode=pl.Buffered(1))


def _cast_kernel(w_ref, scale_ref, o_ref):
    o_ref[...] = (w_ref[...] * scale_ref[...]).astype(BF16)


def _to_bf16(w, layer, row_block, col_scale):
    _, rows, cols = w.shape
    return pl.pallas_call(
        _cast_kernel,
        grid=(rows // row_block,),
        in_specs=[pl.BlockSpec((None, row_block, cols), lambda r: (layer, r, 0)),
                  pl.BlockSpec((1, cols), lambda r: (0, 0))],
        out_specs=pl.BlockSpec((row_block, cols), lambda r: (r, 0)),
        out_shape=jax.ShapeDtypeStruct((rows, cols), BF16),
        compiler_params=pltpu.CompilerParams(
            dimension_semantics=("arbitrary",),
            vmem_limit_bytes=VMEM_LIMIT_BYTES),
        name="cast_bf16",
    )(w, col_scale)


def _side_cast_plan(side_casts, layer, n_steps):
    in_specs, out_specs, shapes, args = [], [], [], []
    for w, scale in side_casts:
        _, rows, cols = w.shape
        rb = next(r for r in range(BF16_SUBLANES, rows + 1, BF16_SUBLANES)
                  if rows % r == 0 and rows // r <= n_steps)
        n_blk = rows // rb
        blk = lambda i, n_blk=n_blk: jnp.minimum(i, n_blk - 1)
        in_specs += [pl.BlockSpec((None, rb, cols), lambda i, blk=blk: (layer, blk(i), 0)),
                     pl.BlockSpec((1, cols), lambda i: (0, 0))]
        out_specs.append(pl.BlockSpec((rb, cols), lambda i, blk=blk: (blk(i), 0)))
        shapes.append(jax.ShapeDtypeStruct((rows, cols), BF16))
        args += [w, scale]
    return in_specs, out_specs, shapes, args


def _run_side_casts(cast_in, cast_out):
    for c, dst in enumerate(cast_out):
        dst[...] = (cast_in[2 * c][...] * cast_in[2 * c + 1][...]).astype(BF16)


def _mod_kernel(c_ref, w_ref, b_ref, o_ref):
    c = c_ref[...]
    s = (c * _sigmoid(c)).astype(BF16)
    o_ref[...] = _dot(s, w_ref[...].astype(BF16)) + b_ref[...]


def _modulation(cs, w_mod, b_mod):
    n_layers, d, width = w_mod.shape
    tn = 1536
    return pl.pallas_call(
        _mod_kernel,
        grid=(n_layers, width // tn),
        in_specs=[
            pl.BlockSpec((SUBLANES, d), lambda l, j: (0, 0)),
            pl.BlockSpec((None, d, tn), lambda l, j: (l, 0, j)),
            pl.BlockSpec((None, 1, tn), lambda l, j: (l, 0, j)),
        ],
        out_specs=pl.BlockSpec((None, SUBLANES, tn), lambda l, j: (l, 0, j)),
        out_shape=jax.ShapeDtypeStruct((n_layers, SUBLANES, width), F32),
        compiler_params=pltpu.CompilerParams(
            dimension_semantics=("arbitrary", "arbitrary"),
            vmem_limit_bytes=VMEM_LIMIT_BYTES),
        name="modulation",
    )(cs, w_mod, b_mod.reshape(n_layers, 1, width))


def _rope(t, cos_ref, sin_ref, n_heads):
    cos = cos_ref[...]
    sin = sin_ref[...]
    lane = lax.broadcasted_iota(jnp.int32, cos.shape, 1)
    first = (lane & (2 * N_FREQ - 1)) < N_FREQ
    outs = []
    for h in range(n_heads):
        th = t[:, h * HEAD_DIM:(h + 1) * HEAD_DIM]
        partner = jnp.where(first,
                            pltpu.roll(th, HEAD_DIM - N_FREQ, axis=1),
                            pltpu.roll(th, N_FREQ, axis=1))
        outs.append(th * cos + partner * sin)
    return jnp.concatenate(outs, axis=-1)


def _stream_specs(stream, tm, d, n_lat_tiles):
    _, ctx, ctx_off = stream
    last_ctx = ctx.shape[0] // tm - 1
    lat_spec = pl.BlockSpec((tm, d), lambda i: (jnp.minimum(i, n_lat_tiles - 1), 0))
    ctx_spec = pl.BlockSpec(
        (tm, d), lambda i: (jnp.minimum(jnp.maximum(i - n_lat_tiles, 0) + ctx_off, last_ctx), 0))
    return lat_spec, ctx_spec


def _mod_spec(geo, layer, tiles_per_step=1):
    return pl.BlockSpec(
        (None, None, MOD_CHUNKS, D_MODEL),
        lambda i: (layer, geo["mod_row"](i * tiles_per_step), 0, 0))


def _stream_tile(lat_ref, ctx_ref, n_lat_tiles):
    return jnp.where(pl.program_id(0) < n_lat_tiles, lat_ref[...], ctx_ref[...])


def _in_kernel(xl_ref, xc_ref, mod_ref, g_ref, w_ref, cq_ref, sq_ref, ck_ref, sk_ref, *rest,
               n_lat_tiles, n_casts):
    cast_in, rest = rest[:2 * n_casts], rest[2 * n_casts:]
    (xr_ref, gr_ref, q_ref, k_ref, vt_ref, gl_ref), cast_out = rest[:6], rest[6:]
    _run_side_casts(cast_in, cast_out)
    d = D_MODEL
    x = _stream_tile(xl_ref, xc_ref, n_lat_tiles)
    h = _rms(x) * (g_ref[...] * (1.0 + mod_ref[1:2, :])) + mod_ref[0:1, :]
    h = h.astype(BF16)
    xr = _dot(h, w_ref[:, 0:d])
    for n in range(N_RNN_BLOCKS):
        xr_ref[pl.ds(n, x.shape[0], stride=N_RNN_BLOCKS), :] = (
            xr[:, n * RNN_BLOCK_W:(n + 1) * RNN_BLOCK_W])
    gr_ref[...] = _dot(h, w_ref[:, d:2 * d]).astype(BF16)
    q = _dot(h, w_ref[:, 2 * d:3 * d])
    q_ref[...] = _rope(q, cq_ref, sq_ref, N_Q_HEADS).astype(BF16)
    k = _dot(h, w_ref[:, 3 * d:3 * d + KV_WIDTH])
    k_ref[...] = _rope(k, ck_ref, sk_ref, N_KV_HEADS).astype(BF16)
    v = _dot(h, w_ref[:, 3 * d + KV_WIDTH:3 * d + 2 * KV_WIDTH])
    for t in range(vt_ref.shape[0]):
        vt_ref[t] = v[t * KEY_BLOCK:(t + 1) * KEY_BLOCK, :].T.astype(BF16)
    gl_ref[...] = _dot(h, w_ref[:, 3 * d + 2 * KV_WIDTH:5 * d + 2 * KV_WIDTH]).astype(BF16)


def _in_proj(stream, mod, g_pre, w_in, layer, tables, geo, side_casts):
    d = D_MODEL
    tm = geo["tm"]
    n_tiles = geo["n_lat_tiles"] + geo["n_ctx_tiles"]
    nt = n_tiles * tm
    row = lambda i: (i, 0)
    tab = pl.BlockSpec((tm, LANES), lambda i: (geo["table_block"](i), 0))
    cast_in_specs, cast_out_specs, cast_shapes, cast_args = _side_cast_plan(
        side_casts, layer, n_tiles)
    outs = pl.pallas_call(
        functools.partial(_in_kernel, n_lat_tiles=geo["n_lat_tiles"], n_casts=len(side_casts)),
        grid=(n_tiles,),
        in_specs=[
            *_stream_specs(stream, tm, d, geo["n_lat_tiles"]),
            _mod_spec(geo, layer),
            _resident_layer(g_pre, layer),
            _resident(w_in.shape),
            tab, tab, tab, tab,
            *cast_in_specs,
        ],
        out_specs=[
            pl.BlockSpec((tm * N_RNN_BLOCKS, RNN_BLOCK_W), row),
            pl.BlockSpec((tm, d), row),
            pl.BlockSpec((tm, d), row),
            pl.BlockSpec((tm, KV_WIDTH), row),
            pl.BlockSpec((tm // KEY_BLOCK, KV_WIDTH, KEY_BLOCK), lambda i: (i, 0, 0)),
            pl.BlockSpec((tm, 2 * d), row),
            *cast_out_specs,
        ],
        out_shape=[
            jax.ShapeDtypeStruct((nt * N_RNN_BLOCKS, RNN_BLOCK_W), F32),
            jax.ShapeDtypeStruct((nt, d), BF16),
            jax.ShapeDtypeStruct((nt, d), BF16),
            jax.ShapeDtypeStruct((nt, KV_WIDTH), BF16),
            jax.ShapeDtypeStruct((nt // KEY_BLOCK, KV_WIDTH, KEY_BLOCK), BF16),
            jax.ShapeDtypeStruct((nt, 2 * d), BF16),
            *cast_shapes,
        ],
        compiler_params=pltpu.CompilerParams(
            dimension_semantics=("arbitrary",),
            vmem_limit_bytes=VMEM_LIMIT_BYTES),
        name="in_proj",
    )(stream[0], stream[1], mod, g_pre, w_in, *tables, *cast_args)
    return outs[:6], outs[6:]


def _rnn_kernel(xf_ref, xfp_ref, xfn_ref, xb_ref, xbp_ref, xbn_ref,
                cw_ref, cb_ref, wg_ref, bg_ref, lam_ref,
                hf_ref, hb_ref,
                xc_s, a_s, b_s, carry_s, *, n_steps):
    j = pl.program_id(1)
    nb, bw = N_RNN_BLOCKS, RNN_BLOCK_W
    n_in = CONV_T + CONV_W - 1
    right = CONV_W - 1 - CONV_LEFT

    @pl.when(j == 0)
    def _():
        carry_s[...] = jnp.zeros_like(carry_s)

    def conv(slot, x_ref, xp_ref, xn_ref, has_prev, has_next):
        def conv_piece(v, base):
            v = v.reshape(n_in, nb, bw)
            acc = cb_ref[...] + v[0:CONV_T] * cw_ref[0]
            for k in range(1, CONV_W):
                acc = acc + v[k:k + CONV_T] * cw_ref[k]
            xc_s[slot, pl.ds(base, CONV_T * nb), :] = acc.reshape(CONV_T * nb, bw)

        halo_rows = xp_ref.shape[0]
        left = jnp.where(has_prev, xp_ref[halo_rows - CONV_LEFT * nb:halo_rows, :], 0.0)
        conv_piece(jnp.concatenate([left, x_ref[0:(n_in - CONV_LEFT) * nb, :]], axis=0), 0)
        tail = jnp.where(has_next, xn_ref[0:right * nb, :], 0.0)
        last = CHUNK - CONV_T
        conv_piece(jnp.concatenate(
            [x_ref[(last - CONV_LEFT) * nb:CHUNK * nb, :], tail], axis=0), last * nb)

        def conv_body(c, carry):
            base = pl.multiple_of(c * (CONV_T * nb), CONV_T * nb)
            conv_piece(x_ref[pl.ds(base - CONV_LEFT * nb, n_in * nb), :], base)
            return carry

        lax.fori_loop(1, CHUNK // CONV_T - 1, conv_body, 0)

    fwd_chunk = j
    bwd_chunk = jnp.where(j == 0, 0, n_steps - j)
    mid = jnp.logical_and(j >= 1, j < n_steps - 1)

    @pl.when(2 * j <= n_steps)
    def _():
        conv(fwd_chunk, xf_ref, xfp_ref, xfn_ref, j >= 2, mid)

    @pl.when(jnp.logical_and(j >= 1, 2 * j < n_steps))
    def _():
        conv(bwd_chunk, xb_ref, xbp_ref, xbn_ref, mid, j >= 2)

    def gate_unit(parity, direction, n):
        rows = pl.ds(n, CHUNK, stride=nb)
        lam = lam_ref[direction, n]
        c_half = (0.5 * LRU_C * LOG2_E) * (
            jnp.minimum(lam, 0.0) - jnp.log1p(jnp.exp(-jnp.abs(lam))))
        xh = xc_s[bwd_chunk if direction else fwd_chunk, rows, :]
        z = _dot(xh.astype(BF16), wg_ref[direction, n]) + bg_ref[direction, n]
        a = jnp.exp2(c_half * jnp.tanh(z[:, 0:bw]) + c_half)
        gated = xh * jnp.tanh(z[:, bw:2 * bw]) + xh
        om = 1.0 - a * a
        a_s[parity, direction, rows, :] = a
        b_s[parity, direction, rows, :] = (om * lax.rsqrt(jnp.maximum(om, F32_TINY))) * gated

    def scan_steps(parity, t0, t1, carry):
        hf, hb = carry
        for t in range(t0, t1):
            tf, tb = t * nb, (CHUNK - 1 - t) * nb
            hf = a_s[parity, 0, tf:tf + nb, :] * hf + b_s[parity, 0, tf:tf + nb, :]
            hb = a_s[parity, 1, tb:tb + nb, :] * hb + b_s[parity, 1, tb:tb + nb, :]
            hf_ref[tf:tf + nb, :] = hf
            hb_ref[tb:tb + nb, :] = hb
        return hf, hb

    def work(parity, do_gates, do_scan):
        units = [(direction, n) for direction in range(2) for n in range(nb)]
        per_unit = CHUNK // len(units)
        carry = (carry_s[0], carry_s[1]) if do_scan else None
        for idx, (direction, n) in enumerate(units):
            if do_gates:
                gate_unit(parity, direction, n)
            if do_scan:
                carry = scan_steps(1 - parity, idx * per_unit, (idx + 1) * per_unit, carry)
        if do_scan:
            carry_s[0], carry_s[1] = carry

    last_parity = (n_steps - 1) % 2
    pl.when(j == 0)(lambda: work(0, True, False))
    pl.when(j == n_steps)(lambda: work(1 - last_parity, False, True))
    inner = jnp.logical_and(j >= 1, j < n_steps)
    pl.when(jnp.logical_and(inner, j % 2 == 0))(lambda: work(0, True, True))
    pl.when(jnp.logical_and(inner, j % 2 == 1))(lambda: work(1, True, True))


def _rnn_branch(xr, conv_w, conv_b, w_gate, b_gate, lam, layer, geo):
    nb, bw = N_RNN_BLOCKS, RNN_BLOCK_W
    bsz, n_lat = geo["batch"], geo["lat_chunks"]
    n_steps = n_lat + 1
    halo = SUBLANES
    n_halo = xr.shape[0] // (halo * nb)
    per = CHUNK // halo

    def fwd_blk(b, s):
        return jnp.where(s == 0, bsz * n_lat + b, b * n_lat + s - 1)

    def bwd_blk(b, s):
        return jnp.where(s == 0, bsz * n_lat + b, b * n_lat + n_lat - s)

    in_step = lambda j: jnp.minimum(j, n_steps - 1)
    out_step = lambda j: jnp.maximum(j - 1, 0)

    def chunk(blk, step):
        return pl.BlockSpec((CHUNK * nb, bw), lambda b, j: (blk(b, step(j)), 0))

    def prev(blk):
        return pl.BlockSpec(
            (halo * nb, bw), lambda b, j: (jnp.maximum(blk(b, in_step(j)) * per - 1, 0), 0))

    def nxt(blk):
        return pl.BlockSpec(
            (halo * nb, bw),
            lambda b, j: (jnp.minimum((blk(b, in_step(j)) + 1) * per, n_halo - 1), 0))

    return pl.pallas_call(
        functools.partial(_rnn_kernel, n_steps=n_steps),
        grid=(bsz, n_steps + 1),
        in_specs=[
            chunk(fwd_blk, in_step), prev(fwd_blk), nxt(fwd_blk),
            chunk(bwd_blk, in_step), prev(bwd_blk), nxt(bwd_blk),
            _resident_layer(conv_w, layer),
            _resident_layer(conv_b, layer),
            _resident_layer(w_gate, layer),
            _resident_layer(b_gate, layer),
            _resident_layer(lam, layer),
        ],
        out_specs=[chunk(fwd_blk, out_step), chunk(bwd_blk, out_step)],
        out_shape=[jax.ShapeDtypeStruct(xr.shape, F32)] * 2,
        scratch_shapes=[
            pltpu.VMEM((n_steps, CHUNK * nb, bw), F32),
            pltpu.VMEM((2, 2, CHUNK * nb, bw), F32),
            pltpu.VMEM((2, 2, CHUNK * nb, bw), F32),
            pltpu.VMEM((2, nb, bw), F32),
        ],
        compiler_params=pltpu.CompilerParams(
            dimension_semantics=("arbitrary", "arbitrary"),
            vmem_limit_bytes=VMEM_LIMIT_BYTES),
        name="rglru",
    )(xr, xr, xr, xr, xr, xr, conv_w, conv_b, w_gate, b_gate, lam)


def _stack_heads(q, g):
    base = g * Q_PER_KV * HEAD_DIM
    return jnp.concatenate(
        [q[:, base + h * HEAD_DIM: base + (h + 1) * HEAD_DIM] for h in range(Q_PER_KV)], axis=0)


def _nt_dot(a, b):
    return lax.dot_general(a, b, (((1,), (1,)), ((), ())), preferred_element_type=F32)


def _attend_units(sink_ref, layer, units):
    folds = KEY_BLOCK // SUBLANES
    for u in units:
        n_q = u["q"].shape[0]
        width = Q_PER_KV * n_q
        qs = _stack_heads(u["q"], u["g"])
        sink = jnp.concatenate(
            [jnp.full((1, n_q), sink_ref[layer, u["g"] * Q_PER_KV + h] * LOG2_E, F32)
             for h in range(Q_PER_KV)], axis=1)
        m8 = jnp.broadcast_to(sink, (SUBLANES, width))
        for j, (k, bias) in enumerate(u["key_blocks"]):
            s = _nt_dot(k, qs)
            if bias is not None:
                s = s + jnp.concatenate([bias] * Q_PER_KV, axis=1)
            u["s_s"][j * KEY_BLOCK:(j + 1) * KEY_BLOCK, :] = s
            m8 = jnp.maximum(m8, jnp.max(s.reshape(folds, SUBLANES, width), axis=0))
        u["sink"] = sink
        u["m"] = jnp.max(m8, axis=0, keepdims=True)
    for u in units:
        width = Q_PER_KV * u["q"].shape[0]
        l8 = jnp.zeros((SUBLANES, width), F32)
        for j in range(len(u["key_blocks"])):
            p = jnp.exp2(u["s_s"][j * KEY_BLOCK:(j + 1) * KEY_BLOCK, :] - u["m"])
            l8 = l8 + jnp.sum(p.reshape(folds, SUBLANES, width), axis=0)
            u["p_s"][j * KEY_BLOCK:(j + 1) * KEY_BLOCK, :] = p.astype(BF16)
        u["denom"] = jnp.sum(l8, axis=0, keepdims=True) + jnp.exp2(u["sink"] - u["m"])
    for u in units:
        n_q = u["q"].shape[0]
        n_keys = len(u["key_blocks"]) * KEY_BLOCK
        ot = _dot(u["vt"], u["p_s"][0:n_keys, :]) * (1.0 / u["denom"])
        for h in range(Q_PER_KV):
            u["o_store"](h, ot[:, h * n_q:(h + 1) * n_q].T)


def _lat_attn_kernel(sink_ref, q_ref, k_ref, vt_ref, kc_ref, vtc_ref, o_ref, s_s, p_s,
                     *, seq_len, layer):
    n_blk = q_ref.shape[0] // Q_BLOCK
    band_blocks = BAND // KEY_BLOCK
    ctx_blocks = vtc_ref.shape[0]
    last_start = seq_len // KEY_BLOCK - band_blocks
    rel = (lax.broadcasted_iota(jnp.int32, (KEY_BLOCK, Q_BLOCK), 0)
           - lax.broadcasted_iota(jnp.int32, (KEY_BLOCK, Q_BLOCK), 1))

    def block_units(i, slot):
        units = []
        qb = pl.program_id(1) * n_blk + i
        jb0 = jnp.clip(qb - WINDOW // KEY_BLOCK, 0, last_start)
        biases = [jnp.where(jnp.abs(rel + (jb0 + t - qb) * KEY_BLOCK) <= WINDOW, 0.0, NEG_INF)
                  for t in range(band_blocks)]
        rows = pl.ds(pl.multiple_of(i * Q_BLOCK, Q_BLOCK), Q_BLOCK)
        q = q_ref[rows, :]
        for g in range(N_KV_HEADS):
            gs = slice(g * HEAD_DIM, (g + 1) * HEAD_DIM)
            key_blocks = [
                (k_ref[pl.ds(pl.multiple_of((jb0 + t) * KEY_BLOCK, KEY_BLOCK), KEY_BLOCK), gs],
                 biases[t]) for t in range(band_blocks)]
            key_blocks += [(kc_ref[t * KEY_BLOCK:(t + 1) * KEY_BLOCK, gs], None)
                           for t in range(ctx_blocks)]
            vt = jnp.concatenate([vt_ref[jb0 + t, gs, :] for t in range(band_blocks)]
                                 + [vtc_ref[t, gs, :] for t in range(ctx_blocks)], axis=1)

            def o_store(h, tile, g=g, rows=rows):
                c0 = (g * Q_PER_KV + h) * HEAD_DIM
                o_ref[rows, c0:c0 + HEAD_DIM] = tile.astype(BF16)

            units.append(dict(g=g, q=q, key_blocks=key_blocks, vt=vt,
                              s_s=s_s.at[slot, g], p_s=p_s.at[slot, g], o_store=o_store))
        return units

    def body(it, carry):
        units = []
        for slot in range(Q_SLOTS):
            units += block_units(it * Q_SLOTS + slot, slot)
        _attend_units(sink_ref, layer, units)
        return carry

    lax.fori_loop(0, n_blk // Q_SLOTS, body, 0)


def _ctx_attn_kernel(sink_ref, q_ref, kc_ref, vtc_ref, o_ref, s_s, p_s, *, layer):
    ctx_blocks = vtc_ref.shape[0]
    q = q_ref[...]
    units = []
    for g in range(N_KV_HEADS):
        gs = slice(g * HEAD_DIM, (g + 1) * HEAD_DIM)
        key_blocks = [(kc_ref[t * KEY_BLOCK:(t + 1) * KEY_BLOCK, gs], None)
                      for t in range(ctx_blocks)]
        vt = jnp.concatenate([vtc_ref[t, gs, :] for t in range(ctx_blocks)], axis=1)

        def o_store(h, tile, g=g):
            c0 = (g * Q_PER_KV + h) * HEAD_DIM
            o_ref[:, c0:c0 + HEAD_DIM] = tile.astype(BF16)

        units.append(dict(g=g, q=q, key_blocks=key_blocks, vt=vt,
                          s_s=s_s.at[g], p_s=p_s.at[g], o_store=o_store))
    _attend_units(sink_ref, layer, units)


def _attention(q, k, vt, sink, layer, geo, with_ctx_queries):
    nt, d = q.shape
    bsz, seq_len, n_ctx = geo["batch"], geo["seq"], geo["ctx"]
    q_sup = next(s for s in (1024, 512, 256, Q_BLOCK) if seq_len % s == 0)
    n_sup = seq_len // q_sup
    ctx_blk0 = bsz * seq_len // n_ctx
    smem = pl.BlockSpec(memory_space=pltpu.SMEM)
    ctx_keys = pl.BlockSpec((n_ctx, KV_WIDTH), lambda b, *_: (ctx_blk0 + b, 0))
    ctx_vals = pl.BlockSpec((n_ctx // KEY_BLOCK, KV_WIDTH, KEY_BLOCK),
                            lambda b, *_: (ctx_blk0 + b, 0, 0))
    o = pl.pallas_call(
        functools.partial(_lat_attn_kernel, seq_len=seq_len, layer=layer),
        grid=(bsz, n_sup),
        in_specs=[
            smem,
            pl.BlockSpec((q_sup, d), lambda b, i: (b * n_sup + i, 0)),
            pl.BlockSpec((seq_len, KV_WIDTH), lambda b, i: (b, 0)),
            pl.BlockSpec((seq_len // KEY_BLOCK, KV_WIDTH, KEY_BLOCK), lambda b, i: (b, 0, 0)),
            ctx_keys, ctx_vals,
        ],
        out_specs=pl.BlockSpec((q_sup, d), lambda b, i: (b * n_sup + i, 0)),
        out_shape=jax.ShapeDtypeStruct((bsz * seq_len, d), BF16),
        scratch_shapes=[
            pltpu.VMEM((Q_SLOTS, N_KV_HEADS, BAND + n_ctx, Q_PER_KV * Q_BLOCK), F32),
            pltpu.VMEM((Q_SLOTS, N_KV_HEADS, BAND + n_ctx, Q_PER_KV * Q_BLOCK), BF16),
        ],
        compiler_params=pltpu.CompilerParams(
            dimension_semantics=("arbitrary", "arbitrary"),
            vmem_limit_bytes=VMEM_LIMIT_BYTES),
        name="lat_attention",
    )(sink, q, k, vt, k, vt)
    if not with_ctx_queries:
        return (o, o, 0)
    o_ctx = pl.pallas_call(
        functools.partial(_ctx_attn_kernel, layer=layer),
        grid=(bsz,),
        in_specs=[
            smem,
            pl.BlockSpec((n_ctx, d), lambda b: (ctx_blk0 + b, 0)),
            ctx_keys, ctx_vals,
        ],
        out_specs=pl.BlockSpec((n_ctx, d), lambda b: (b, 0)),
        out_shape=jax.ShapeDtypeStruct((bsz * n_ctx, d), BF16),
        scratch_shapes=[
            pltpu.VMEM((N_KV_HEADS, n_ctx, Q_PER_KV * n_ctx), F32),
            pltpu.VMEM((N_KV_HEADS, n_ctx, Q_PER_KV * n_ctx), BF16),
        ],
        compiler_params=pltpu.CompilerParams(
            dimension_semantics=("arbitrary",),
            vmem_limit_bytes=VMEM_LIMIT_BYTES),
        name="ctx_attention",
    )(sink, q, k, vt)
    return (o, o_ctx, 0)


def _merge_kernel(xl_ref, xc_ref, hf_ref, hb_ref, gr_ref, ol_ref, oc_ref, gl_ref, mod_ref,
                  g_ref, wr_ref, wa_ref, wo_ref, *rest, n_lat_tiles, n_casts):
    cast_in, out_ref, cast_out = rest[:2 * n_casts], rest[2 * n_casts], rest[2 * n_casts + 1:]
    _run_side_casts(cast_in, cast_out)
    d = D_MODEL
    nb = N_RNN_BLOCKS
    rows_per_part = xl_ref.shape[0] // MERGE_PARTS
    is_lat = pl.program_id(0) < n_lat_tiles
    parts =[slice(p * rows_per_part, (p + 1) * rows_per_part) for p in range(MERGE_PARTS)]

    def rnn_input(p):
        h = jnp.concatenate(
            [hf_ref[pl.ds(p * rows_per_part * nb + n, rows_per_part, stride=nb), :]
             + hb_ref[pl.ds(p * rows_per_part * nb + n, rows_per_part, stride=nb), :]
             for n in range(nb)], axis=-1)
        g = gr_ref[parts[p], :].astype(F32)
        hg = h * g
        return (hg * jnp.tanh(g * (GELU_C1 + GELU_C2 * (g * g))) + hg).astype(BF16)

    def gated_mix(p, y):
        r = parts[p]
        ya = _dot(y, wr_ref[...])
        yb = _dot(jnp.where(is_lat, ol_ref[r, :], oc_ref[r, :]), wa_ref[...])
        ta = jnp.tanh(gl_ref[r, 0:d].astype(F32))
        tb = jnp.tanh(gl_ref[r, d:2 * d].astype(F32))
        return ((ya * ta + ya) + (yb * tb + yb)).astype(BF16)

    def project(p, mix):
        r = parts[p]
        m = _dot(mix, wo_ref[...])
        x = jnp.where(is_lat, xl_ref[r, :], xc_ref[r, :])
        out_ref[r, :] = x + _rms(m) * (mod_ref[2:3, :] * g_ref[...])

    ys = [rnn_input(p) for p in range(MERGE_PARTS)]
    mixes = [gated_mix(p, ys[p]) for p in range(MERGE_PARTS)]
    for p in range(MERGE_PARTS):
        project(p, mixes[p])


def _merge(x_stream, hf, hb, gr, o_stream, gl, mod, g_post, w_o_rnn, w_o_attn, w_out,
           layer, geo, n_tiles, side_casts, cast_layer):
    d = D_MODEL
    tm = geo["tm"]
    n_lat_tiles = geo["n_lat_tiles"]
    row = lambda i: (i, 0)
    tile = pl.BlockSpec((tm, d), row)
    slab = pl.BlockSpec((tm * N_RNN_BLOCKS, RNN_BLOCK_W), row)
    cast_in_specs, cast_out_specs, cast_shapes, cast_args = _side_cast_plan(
        side_casts, cast_layer, n_tiles)
    outs = pl.pallas_call(
        functools.partial(_merge_kernel, n_lat_tiles=n_lat_tiles, n_casts=len(side_casts)),
        grid=(n_tiles,),
        in_specs=[
            *_stream_specs(x_stream, tm, d, n_lat_tiles), slab, slab, tile,
            *_stream_specs(o_stream, tm, d, n_lat_tiles),
            pl.BlockSpec((tm, 2 * d), row),
            _mod_spec(geo, layer),
            _resident_layer(g_post, layer),
            _resident(w_o_rnn.shape), _resident(w_o_attn.shape), _resident(w_out.shape),
            *cast_in_specs,
        ],
        out_specs=[tile, *cast_out_specs],
        out_shape=[jax.ShapeDtypeStruct((n_tiles * tm, d), F32), *cast_shapes],
        compiler_params=pltpu.CompilerParams(
            dimension_semantics=("arbitrary",),
            vmem_limit_bytes=VMEM_LIMIT_BYTES),
        name="merge",
    )(x_stream[0], x_stream[1], hf, hb, gr, o_stream[0], o_stream[1], gl, mod,
      g_post, w_o_rnn, w_o_attn, w_out, *cast_args)
    return outs[0], outs[1:]


def _ffn_kernel(x_ref, mod_ref, gpre_ref, gpost_ref, w1_ref, w2_ref, out_ref):
    sub = x_ref.shape[0] // FFN_SUBTILES
    bounds = list(range(0, D_FF, FFN_CHUNK)) + [D_FF]
    n_chunks = len(bounds) - 1

    def normed(s):
        x = x_ref[s * sub:(s + 1) * sub, :]
        h = _rms(x) * (gpre_ref[...] * (1.0 + mod_ref[4:5, :])) + mod_ref[3:4, :]
        return h.astype(BF16)

    def first_layer(h, c):
        lo, hi = bounds[c], bounds[c + 1]
        return _dot(h, w1_ref[:, lo:hi]), _dot(h, w1_ref[:, D_FF + lo:D_FF + hi])

    h_next = normed(0)
    for s in range(FFN_SUBTILES):
        h = h_next
        if s + 1 < FFN_SUBTILES:
            h_next = normed(s + 1)
        f = None
        nxt = first_layer(h, 0)
        for c in range(n_chunks):
            half_gate, up = nxt
            if c + 1 < n_chunks:
                nxt = first_layer(h, c + 1)
            act = ((half_gate * jnp.tanh(half_gate) + half_gate) * up).astype(BF16)
            part = _dot(act, w2_ref[bounds[c]:bounds[c + 1], :])
            f = part if f is None else f + part
        rows = slice(s * sub, (s + 1) * sub)
        out_ref[rows, :] = x_ref[rows, :] + _rms(f) * (mod_ref[5:6, :] * gpost_ref[...])


def _ffn(x_all, mod, g_pre, g_post, w1, w2, layer, geo, n_tiles):
    d = x_all.shape[1]
    tm = geo["tm"] * FFN_SUBTILES
    n_blocks = n_tiles // FFN_SUBTILES
    tile = pl.BlockSpec((tm, d), lambda i: (i, 0))
    return pl.pallas_call(
        _ffn_kernel,
        grid=(n_blocks,),
        in_specs=[
            tile,
            _mod_spec(geo, layer, FFN_SUBTILES),
            _resident_layer(g_pre, layer), _resident_layer(g_post, layer),
            _resident(w1.shape), _resident(w2.shape),
        ],
        out_specs=tile,
        out_shape=jax.ShapeDtypeStruct((n_blocks * tm, d), F32),
        compiler_params=pltpu.CompilerParams(
            dimension_semantics=("arbitrary",),
            vmem_limit_bytes=VMEM_LIMIT_BYTES),
        name="ffn",
    )(x_all, mod, g_pre, g_post, w1, w2)


def _rope_tables(seq_len, pad_rows):
    f32 = np.float32
    n_rows = seq_len // GRID_W
    inv = f32(ROPE_BASE) ** (-np.arange(N_FREQ, dtype=f32) / f32(N_FREQ))
    ang = np.arange(max(n_rows, GRID_W), dtype=f32)[:, None] * inv[None, :]
    cos_u, sin_u = np.cos(ang), np.sin(ang)
    by_row = lambda t: np.repeat(t[:n_rows], GRID_W, axis=0)
    by_col = lambda t: np.tile(t[:GRID_W], (n_rows, 1))
    cos = np.concatenate([by_row(cos_u)] * 2 + [by_col(cos_u)] * 2, axis=-1)
    sin = np.concatenate(
        [-by_row(sin_u), by_row(sin_u), -by_col(sin_u), by_col(sin_u)], axis=-1)
    cos = np.concatenate([cos, np.ones((pad_rows, HEAD_DIM), f32)], axis=0)
    sin = np.concatenate([sin, np.zeros((pad_rows, HEAD_DIM), f32)], axis=0)
    scale = f32(HEAD_DIM ** -0.5 * LOG2_E)
    return tuple(jnp.asarray(t.astype(f32)) for t in (cos * scale, sin * scale, cos, sin))


def kernel(x, c, ctx, c_ctx, w_mod, b_mod, g_mix_pre, g_mix_post, g_ffn_pre, g_ffn_post, w_in, conv_w, conv_b, lru_wa, lru_ba, lru_wx, lru_bx, lru_lam, attn_sink, w_o_rnn, w_o_attn, w_out, w_ffn_in, w_ffn_out):
    bsz, seq_len, d = x.shape
    n_ctx = ctx.shape[1]
    depth = w_mod.shape[0]
    assert d == D_MODEL and n_ctx == CHUNK and seq_len % CHUNK == 0 and seq_len >= BAND
    assert bsz + 1 <= SUBLANES and seq_len % GRID_W == 0

    n_lat_rows, n_ctx_rows = bsz * seq_len, bsz * n_ctx
    big = 512 * FFN_SUBTILES
    tm = 512 if (n_ctx_rows % big == 0 and seq_len % big == 0) else CHUNK
    assert n_ctx_rows % (tm * FFN_SUBTILES) == 0 and seq_len % (tm * FFN_SUBTILES) == 0
    n_lat_tiles, n_ctx_tiles = n_lat_rows // tm, n_ctx_rows // tm
    tiles_per_batch = seq_len // tm
    geo = {
        "batch": bsz, "seq": seq_len, "ctx": n_ctx, "tm": tm,
        "lat_chunks": seq_len // CHUNK,
        "n_lat_tiles": n_lat_tiles, "n_ctx_tiles": n_ctx_tiles,
        "mod_row": lambda i: jnp.where(i < n_lat_tiles, i // tiles_per_batch, bsz),
        "table_block": lambda i: jnp.where(
            i < n_lat_tiles, i % tiles_per_batch, tiles_per_batch + i - n_lat_tiles),
    }

    cs = jnp.concatenate(
        [c, c_ctx[None, :], jnp.zeros((SUBLANES - bsz - 1, d), F32)], axis=0)
    mod = _modulation(cs, w_mod, b_mod).reshape(depth, SUBLANES, MOD_CHUNKS, d)

    tables = _rope_tables(seq_len, n_ctx_rows)

    w_gate = jnp.concatenate([lru_wa, lru_wx], axis=-1).astype(BF16)
    b_gate = 0.5 * jnp.concatenate(
        [lru_ba.reshape(depth, 2, N_RNN_BLOCKS, 1, RNN_BLOCK_W),
         lru_bx.reshape(depth, 2, N_RNN_BLOCKS, 1, RNN_BLOCK_W)], axis=-1)
    lam = lru_lam.reshape(depth, 2, N_RNN_BLOCKS, 1, RNN_BLOCK_W)
    conv_w_half = (0.5 * conv_w).reshape(depth, CONV_W, N_RNN_BLOCKS, RNN_BLOCK_W)
    conv_b_half = (0.5 * conv_b).reshape(depth, N_RNN_BLOCKS, RNN_BLOCK_W)
    g_mix_pre, g_mix_post, g_ffn_pre, g_ffn_post = (
        g.reshape(depth, 1, d) for g in (g_mix_pre, g_mix_post, g_ffn_pre, g_ffn_post))

    stream = (x.reshape(n_lat_rows, d), ctx.reshape(n_ctx_rows, d), 0)
    ones = lambda n: jnp.ones((1, n), F32)
    halves = lambda n: jnp.full((1, n), 0.5, F32)
    gl_cols = 2 * d
    w_in_scale = jnp.concatenate([ones(w_in.shape[-1] - gl_cols), halves(gl_cols)], axis=1)
    w_in_b = _to_bf16(w_in, 0, 256, w_in_scale)
    in_proj_casts = [
        (w_o_rnn, halves(d)), (w_o_attn, ones(d)), (w_out, halves(d)),
        (w_ffn_in, jnp.concatenate([halves(D_FF), ones(D_FF)], axis=1)),
        (w_ffn_out, ones(d)),
    ]
    for l in range(depth):
        need_ctx = l < depth - 1
        n_out_tiles = n_lat_tiles + n_ctx_tiles if need_ctx else n_lat_tiles
        (xr, gr, q, k, vt, gl), layer_weights = _in_proj(
            stream, mod, g_mix_pre, w_in_b, l, tables, geo, in_proj_casts)
        w_o_rnn_b, w_o_attn_b, w_out_b, w_ffn_in_b, w_ffn_out_b = layer_weights
        hf, hb = _rnn_branch(xr, conv_w_half, conv_b_half, w_gate, b_gate, lam, l, geo)
        o_stream = _attention(q, k, vt, attn_sink, l, geo, need_ctx)
        merge_casts = [(w_in, w_in_scale)] if l + 1 < depth else []
        x_all, next_weights = _merge(
            stream, hf, hb, gr, o_stream, gl, mod, g_mix_post, w_o_rnn_b, w_o_attn_b, w_out_b,
            l, geo, n_out_tiles, merge_casts, l + 1)
        if next_weights:
            (w_in_b,) = next_weights
        x_all = _ffn(x_all, mod, g_ffn_pre, g_ffn_post,
                     w_ffn_in_b, w_ffn_out_b, l, geo, n_out_tiles)
        stream = (x_all, x_all, n_lat_tiles)
    return x_all[:n_lat_rows].reshape(bsz, seq_len, d)
```

```python
import functools
import math

import jax
import jax.numpy as jnp
import numpy as np
from jax import lax
from jax.experimental import pallas as pl
from jax.experimental.pallas import tpu as pltpu

D_MODEL = 1024
HEAD_DIM = 128
N_Q_HEADS = 8
N_KV_HEADS = 2
Q_PER_KV = N_Q_HEADS // N_KV_HEADS
KV_WIDTH = N_KV_HEADS * HEAD_DIM
WINDOW = 128
GRID_W = 64
N_FREQ = HEAD_DIM // 4
ROPE_BASE = 10000.0
N_RNN_BLOCKS = 8
RNN_BLOCK_W = D_MODEL // N_RNN_BLOCKS
LRU_C = 8.0
CONV_W = 4
CONV_LEFT = 2
D_FF = 2816
EPS = 1e-6
NEG_INF = -1e30
MOD_CHUNKS = 6

LANES = 128
SUBLANES = 8
BF16_SUBLANES = 16
VMEM_LIMIT_BYTES = 56 * 1024 * 1024

CHUNK = 256
CONV_T = 32
SCAN_UNROLL = 8
MERGE_PARTS = 1
FFN_CHUNK = 768
FFN_SUBTILES = 2
Q_BLOCK = 128
Q_SLOTS = 4
KEY_BLOCK = 128
BAND = Q_BLOCK + 2 * WINDOW
LOG2_E = math.log2(math.e)

BF16 = jnp.bfloat16
F32 = jnp.float32
F32_TINY = float(jnp.finfo(jnp.float32).tiny)


def _dot(a, b):
    return jnp.dot(a, b, preferred_element_type=F32)


def _sigmoid(x):
    return 0.5 * jnp.tanh(0.5 * x) + 0.5


GELU_C1 = math.sqrt(2.0 / math.pi)
GELU_C2 = 0.044715 * GELU_C1


def _rms(x):
    return x * lax.rsqrt(jnp.mean(x * x, axis=-1, keepdims=True) + EPS)


def _resident(shape):
    nd = len(shape)
    return pl.BlockSpec(shape, lambda *_: (0,) * nd, pipeline_mode=pl.Buffered(1))


def _resident_layer(stacked, layer):
    tail = stacked.shape[1:]
    return pl.BlockSpec((None,) + tail, lambda *_: (layer,) + (0,) * len(tail),
                        pipeline_mode=pl.Buffered(1))


def _cast_kernel(w_ref, scale_ref, o_ref):
    o_ref[...] = (w_ref[...] * scale_ref[...]).astype(BF16)


def _to_bf16(w, layer, row_block, col_scale):
    _, rows, cols = w.shape
    return pl.pallas_call(
        _cast_kernel,
        grid=(rows // row_block,),
        in_specs=[pl.BlockSpec((None, row_block, cols), lambda r: (layer, r, 0)),
                  pl.BlockSpec((1, cols), lambda r: (0, 0))],
        out_specs=pl.BlockSpec((row_block, cols), lambda r: (r, 0)),
        out_shape=jax.ShapeDtypeStruct((rows, cols), BF16),
        compiler_params=pltpu.CompilerParams(
            dimension_semantics=("arbitrary",),
            vmem_limit_bytes=VMEM_LIMIT_BYTES),
        name="cast_bf16",
    )(w, col_scale)


def _side_cast_plan(side_casts, layer, n_steps):
    in_specs, out_specs, shapes, args = [], [], [], []
    for w, scale in side_casts:
        _, rows, cols = w.shape
        rb = next(r for r in range(BF16_SUBLANES, rows + 1, BF16_SUBLANES)
                  if rows % r == 0 and rows // r <= n_steps)
        n_blk = rows // rb
        blk = lambda i, n_blk=n_blk: jnp.minimum(i, n_blk - 1)
        in_specs += [pl.BlockSpec((None, rb, cols), lambda i, blk=blk: (layer, blk(i), 0)),
                     pl.BlockSpec((1, cols), lambda i: (0, 0))]
        out_specs.append(pl.BlockSpec((rb, cols), lambda i, blk=blk: (blk(i), 0)))
        shapes.append(jax.ShapeDtypeStruct((rows, cols), BF16))
        args += [w, scale]
    return in_specs, out_specs, shapes, args


def _run_side_casts(cast_in, cast_out):
    for c, dst in enumerate(cast_out):
        dst[...] = (cast_in[2 * c][...] * cast_in[2 * c + 1][...]).astype(BF16)


def _mod_kernel(c_ref, w_ref, b_ref, o_ref):
    c = c_ref[...]
    s = (c * _sigmoid(c)).astype(BF16)
    o_ref[...] = _dot(s, w_ref[...].astype(BF16)) + b_ref[...]


def _modulation(cs, w_mod, b_mod):
    n_layers, d, width = w_mod.shape
    tn = 1536
    return pl.pallas_call(
        _mod_kernel,
        grid=(n_layers, width // tn),
        in_specs=[
            pl.BlockSpec((SUBLANES, d), lambda l, j: (0, 0)),
            pl.BlockSpec((None, d, tn), lambda l, j: (l, 0, j)),
            pl.BlockSpec((None, 1, tn), lambda l, j: (l, 0, j)),
        ],
        out_specs=pl.BlockSpec((None, SUBLANES, tn), lambda l, j: (l, 0, j)),
        out_shape=jax.ShapeDtypeStruct((n_layers, SUBLANES, width), F32),
        compiler_params=pltpu.CompilerParams(
            dimension_semantics=("arbitrary", "arbitrary"),
            vmem_limit_bytes=VMEM_LIMIT_BYTES),
        name="modulation",
    )(cs, w_mod, b_mod.reshape(n_layers, 1, width))


def _rope(t, cos_ref, sin_ref, n_heads):
    cos = cos_ref[...]
    sin = sin_ref[...]
    lane = lax.broadcasted_iota(jnp.int32, cos.shape, 1)
    first = (lane & (2 * N_FREQ - 1)) < N_FREQ
    outs = []
    for h in range(n_heads):
        th = t[:, h * HEAD_DIM:(h + 1) * HEAD_DIM]
        partner = jnp.where(first,
                            pltpu.roll(th, HEAD_DIM - N_FREQ, axis=1),
                            pltpu.roll(th, N_FREQ, axis=1))
        outs.append(th * cos + partner * sin)
    return jnp.concatenate(outs, axis=-1)


def _stream_specs(stream, tm, d, n_lat_tiles):
    _, ctx, ctx_off = stream
    last_ctx = ctx.shape[0] // tm - 1
    lat_spec = pl.BlockSpec((tm, d), lambda i: (jnp.minimum(i, n_lat_tiles - 1), 0))
    ctx_spec = pl.BlockSpec(
        (tm, d), lambda i: (jnp.minimum(jnp.maximum(i - n_lat_tiles, 0) + ctx_off, last_ctx), 0))
    return lat_spec, ctx_spec


def _mod_spec(geo, layer, tiles_per_step=1):
    return pl.BlockSpec(
        (None, None, MOD_CHUNKS, D_MODEL),
        lambda i: (layer, geo["mod_row"](i * tiles_per_step), 0, 0))


def _stream_tile(lat_ref, ctx_ref, n_lat_tiles):
    return jnp.where(pl.program_id(0) < n_lat_tiles, lat_ref[...], ctx_ref[...])


def _in_kernel(xl_ref, xc_ref, mod_ref, g_ref, w_ref, cq_ref, sq_ref, ck_ref, sk_ref, *rest,
               n_lat_tiles, n_casts):
    cast_in, rest = rest[:2 * n_casts], rest[2 * n_casts:]
    (xr_ref, gr_ref, q_ref, k_ref, vt_ref, gl_ref), cast_out = rest[:6], rest[6:]
    _run_side_casts(cast_in, cast_out)
    d = D_MODEL
    x = _stream_tile(xl_ref, xc_ref, n_lat_tiles)
    h = _rms(x) * (g_ref[...] * (1.0 + mod_ref[1:2, :])) + mod_ref[0:1, :]
    h = h.astype(BF16)
    xr = _dot(h, w_ref[:, 0:d])
    for n in range(N_RNN_BLOCKS):
        xr_ref[pl.ds(n, x.shape[0], stride=N_RNN_BLOCKS), :] = (
            xr[:, n * RNN_BLOCK_W:(n + 1) * RNN_BLOCK_W])
    gr_ref[...] = _dot(h, w_ref[:, d:2 * d]).astype(BF16)
    q = _dot(h, w_ref[:, 2 * d:3 * d])
    q_ref[...] = _rope(q, cq_ref, sq_ref, N_Q_HEADS).astype(BF16)
    k = _dot(h, w_ref[:, 3 * d:3 * d + KV_WIDTH])
    k_ref[...] = _rope(k, ck_ref, sk_ref, N_KV_HEADS).astype(BF16)
    v = _dot(h, w_ref[:, 3 * d + KV_WIDTH:3 * d + 2 * KV_WIDTH])
    for t in range(vt_ref.shape[0]):
        vt_ref[t] = v[t * KEY_BLOCK:(t + 1) * KEY_BLOCK, :].T.astype(BF16)
    gl_ref[...] = _dot(h, w_ref[:, 3 * d + 2 * KV_WIDTH:5 * d + 2 * KV_WIDTH]).astype(BF16)


def _in_proj(stream, mod, g_pre, w_in, layer, tables, geo, side_casts):
    d = D_MODEL
    tm = geo["tm"]
    n_tiles = geo["n_lat_tiles"] + geo["n_ctx_tiles"]
    nt = n_tiles * tm
    row = lambda i: (i, 0)
    tab = pl.BlockSpec((tm, LANES), lambda i: (geo["table_block"](i), 0))
    cast_in_specs, cast_out_specs, cast_shapes, cast_args = _side_cast_plan(
        side_casts, layer, n_tiles)
    outs = pl.pallas_call(
        functools.partial(_in_kernel, n_lat_tiles=geo["n_lat_tiles"], n_casts=len(side_casts)),
        grid=(n_tiles,),
        in_specs=[
            *_stream_specs(stream, tm, d, geo["n_lat_tiles"]),
            _mod_spec(geo, layer),
            _resident_layer(g_pre, layer),
            _resident(w_in.shape),
            tab, tab, tab, tab,
            *cast_in_specs,
        ],
        out_specs=[
            pl.BlockSpec((tm * N_RNN_BLOCKS, RNN_BLOCK_W), row),
            pl.BlockSpec((tm, d), row),
            pl.BlockSpec((tm, d), row),
            pl.BlockSpec((tm, KV_WIDTH), row),
            pl.BlockSpec((tm // KEY_BLOCK, KV_WIDTH, KEY_BLOCK), lambda i: (i, 0, 0)),
            pl.BlockSpec((tm, 2 * d), row),
            *cast_out_specs,
        ],
        out_shape=[
            jax.ShapeDtypeStruct((nt * N_RNN_BLOCKS, RNN_BLOCK_W), F32),
            jax.ShapeDtypeStruct((nt, d), BF16),
            jax.ShapeDtypeStruct((nt, d), BF16),
            jax.ShapeDtypeStruct((nt, KV_WIDTH), BF16),
            jax.ShapeDtypeStruct((nt // KEY_BLOCK, KV_WIDTH, KEY_BLOCK), BF16),
            jax.ShapeDtypeStruct((nt, 2 * d), BF16),
            *cast_shapes,
        ],
        compiler_params=pltpu.CompilerParams(
            dimension_semantics=("arbitrary",),
            vmem_limit_bytes=VMEM_LIMIT_BYTES),
        name="in_proj",
    )(stream[0], stream[1], mod, g_pre, w_in, *tables, *cast_args)
    return outs[:6], outs[6:]


def _rnn_kernel(xf_ref, xfp_ref, xfn_ref, xb_ref, xbp_ref, xbn_ref,
                cw_ref, cb_ref, wg_ref, bg_ref, lam_ref,
                hf_ref, hb_ref,
                xc_s, a_s, b_s, carry_s, *, n_steps):
    j = pl.program_id(1)
    nb, bw = N_RNN_BLOCKS, RNN_BLOCK_W
    n_in = CONV_T + CONV_W - 1
    right = CONV_W - 1 - CONV_LEFT

    @pl.when(j == 0)
    def _():
        carry_s[...] = jnp.zeros_like(carry_s)

    def conv(slot, x_ref, xp_ref, xn_ref, has_prev, has_next):
        def conv_piece(v, base):
            v = v.reshape(n_in, nb, bw)
            acc = cb_ref[...] + v[0:CONV_T] * cw_ref[0]
            for k in range(1, CONV_W):
                acc = acc + v[k:k + CONV_T] * cw_ref[k]
            xc_s[slot, pl.ds(base, CONV_T * nb), :] = acc.reshape(CONV_T * nb, bw)

        halo_rows = xp_ref.shape[0]
        left = jnp.where(has_prev, xp_ref[halo_rows - CONV_LEFT * nb:halo_rows, :], 0.0)
        conv_piece(jnp.concatenate([left, x_ref[0:(n_in - CONV_LEFT) * nb, :]], axis=0), 0)
        tail = jnp.where(has_next, xn_ref[0:right * nb, :], 0.0)
        last = CHUNK - CONV_T
        conv_piece(jnp.concatenate(
            [x_ref[(last - CONV_LEFT) * nb:CHUNK * nb, :], tail], axis=0), last * nb)

        def conv_body(c, carry):
            base = pl.multiple_of(c * (CONV_T * nb), CONV_T * nb)
            conv_piece(x_ref[pl.ds(base - CONV_LEFT * nb, n_in * nb), :], base)
            return carry

        lax.fori_loop(1, CHUNK // CONV_T - 1, conv_body, 0)

    fwd_chunk = j
    bwd_chunk = jnp.where(j == 0, 0, n_steps - j)
    mid = jnp.logical_and(j >= 1, j < n_steps - 1)

    @pl.when(2 * j <= n_steps)
    def _():
        conv(fwd_chunk, xf_ref, xfp_ref, xfn_ref, j >= 2, mid)

    @pl.when(jnp.logical_and(j >= 1, 2 * j < n_steps))
    def _():
        conv(bwd_chunk, xb_ref, xbp_ref, xbn_ref, mid, j >= 2)

    def gate_unit(parity, direction, n):
        rows = pl.ds(n, CHUNK, stride=nb)
        lam = lam_ref[direction, n]
        c_half = (0.5 * LRU_C * LOG2_E) * (
            jnp.minimum(lam, 0.0) - jnp.log1p(jnp.exp(-jnp.abs(lam))))
        xh = xc_s[bwd_chunk if direction else fwd_chunk, rows, :]
        z = _dot(xh.astype(BF16), wg_ref[direction, n]) + bg_ref[direction, n]
        a = jnp.exp2(c_half * jnp.tanh(z[:, 0:bw]) + c_half)
        gated = xh * jnp.tanh(z[:, bw:2 * bw]) + xh
        om = 1.0 - a * a
        a_s[parity, direction, rows, :] = a
        b_s[parity, direction, rows, :] = (om * lax.rsqrt(jnp.maximum(om, F32_TINY))) * gated

    def scan_steps(parity, t0, t1, carry):
        hf, hb = carry
        for t in range(t0, t1):
            tf, tb = t * nb, (CHUNK - 1 - t) * nb
            hf = a_s[parity, 0, tf:tf + nb, :] * hf + b_s[parity, 0, tf:tf + nb, :]
            hb = a_s[parity, 1, tb:tb + nb, :] * hb + b_s[parity, 1, tb:tb + nb, :]
            hf_ref[tf:tf + nb, :] = hf
            hb_ref[tb:tb + nb, :] = hb
        return hf, hb

    def work(parity, do_gates, do_scan):
        units = [(direction, n) for direction in range(2) for n in range(nb)]
        per_unit = CHUNK // len(units)
        carry = (carry_s[0], carry_s[1]) if do_scan else None
        for idx, (direction, n) in enumerate(units):
            if do_gates:
                gate_unit(parity, direction, n)
            if do_scan:
                carry = scan_steps(1 - parity, idx * per_unit, (idx + 1) * per_unit, carry)
        if do_scan:
            carry_s[0], carry_s[1] = carry

    last_parity = (n_steps - 1) % 2
    pl.when(j == 0)(lambda: work(0, True, False))
    pl.when(j == n_steps)(lambda: work(1 - last_parity, False, True))
    inner = jnp.logical_and(j >= 1, j < n_steps)
    pl.when(jnp.logical_and(inner, j % 2 == 0))(lambda: work(0, True, True))
    pl.when(jnp.logical_and(inner, j % 2 == 1))(lambda: work(1, True, True))


def _rnn_branch(xr, conv_w, conv_b, w_gate, b_gate, lam, layer, geo):
    nb, bw = N_RNN_BLOCKS, RNN_BLOCK_W
    bsz, n_lat = geo["batch"], geo["lat_chunks"]
    n_steps = n_lat + 1
    halo = SUBLANES
    n_halo = xr.shape[0] // (halo * nb)
    per = CHUNK // halo

    def fwd_blk(b, s):
        return jnp.where(s == 0, bsz * n_lat + b, b * n_lat + s - 1)

    def bwd_blk(b, s):
        return jnp.where(s == 0, bsz * n_lat + b, b * n_lat + n_lat - s)

    in_step = lambda j: jnp.minimum(j, n_steps - 1)
    out_step = lambda j: jnp.maximum(j - 1, 0)

    def chunk(blk, step):
        return pl.BlockSpec((CHUNK * nb, bw), lambda b, j: (blk(b, step(j)), 0))

    def prev(blk):
        return pl.BlockSpec(
            (halo * nb, bw), lambda b, j: (jnp.maximum(blk(b, in_step(j)) * per - 1, 0), 0))

    def nxt(blk):
        return pl.BlockSpec(
            (halo * nb, bw),
            lambda b, j: (jnp.minimum((blk(b, in_step(j)) + 1) * per, n_halo - 1), 0))

    return pl.pallas_call(
        functools.partial(_rnn_kernel, n_steps=n_steps),
        grid=(bsz, n_steps + 1),
        in_specs=[
            chunk(fwd_blk, in_step), prev(fwd_blk), nxt(fwd_blk),
            chunk(bwd_blk, in_step), prev(bwd_blk), nxt(bwd_blk),
            _resident_layer(conv_w, layer),
            _resident_layer(conv_b, layer),
            _resident_layer(w_gate, layer),
            _resident_layer(b_gate, layer),
            _resident_layer(lam, layer),
        ],
        out_specs=[chunk(fwd_blk, out_step), chunk(bwd_blk, out_step)],
        out_shape=[jax.ShapeDtypeStruct(xr.shape, F32)] * 2,
        scratch_shapes=[
            pltpu.VMEM((n_steps, CHUNK * nb, bw), F32),
            pltpu.VMEM((2, 2, CHUNK * nb, bw), F32),
            pltpu.VMEM((2, 2, CHUNK * nb, bw), F32),
            pltpu.VMEM((2, nb, bw), F32),
        ],
        compiler_params=pltpu.CompilerParams(
            dimension_semantics=("arbitrary", "arbitrary"),
            vmem_limit_bytes=VMEM_LIMIT_BYTES),
        name="rglru",
    )(xr, xr, xr, xr, xr, xr, conv_w, conv_b, w_gate, b_gate, lam)


def _stack_heads(q, g):
    base = g * Q_PER_KV * HEAD_DIM
    return jnp.concatenate(
        [q[:, base + h * HEAD_DIM: base + (h + 1) * HEAD_DIM] for h in range(Q_PER_KV)], axis=0)


def _nt_dot(a, b):
    return lax.dot_general(a, b, (((1,), (1,)), ((), ())), preferred_element_type=F32)


def _attend_units(sink_ref, layer, units):
    folds = KEY_BLOCK // SUBLANES
    for u in units:
        n_q = u["q"].shape[0]
        width = Q_PER_KV * n_q
        qs = _stack_heads(u["q"], u["g"])
        sink = jnp.concatenate(
            [jnp.full((1, n_q), sink_ref[layer, u["g"] * Q_PER_KV + h] * LOG2_E, F32)
             for h in range(Q_PER_KV)], axis=1)
        m8 = jnp.broadcast_to(sink, (SUBLANES, width))
        for j, (k, bias) in enumerate(u["key_blocks"]):
            s = _nt_dot(k, qs)
            if bias is not None:
                s = s + jnp.concatenate([bias] * Q_PER_KV, axis=1)
            u["s_s"][j * KEY_BLOCK:(j + 1) * KEY_BLOCK, :] = s
            m8 = jnp.maximum(m8, jnp.max(s.reshape(folds, SUBLANES, width), axis=0))
        u["sink"] = sink
        u["m"] = jnp.max(m8, axis=0, keepdims=True)
    for u in units:
        width = Q_PER_KV * u["q"].shape[0]
        l8 = jnp.zeros((SUBLANES, width), F32)
        for j in range(len(u["key_blocks"])):
            p = jnp.exp2(u["s_s"][j * KEY_BLOCK:(j + 1) * KEY_BLOCK, :] - u["m"])
            l8 = l8 + jnp.sum(p.reshape(folds, SUBLANES, width), axis=0)
            u["p_s"][j * KEY_BLOCK:(j + 1) * KEY_BLOCK, :] = p.astype(BF16)
        u["denom"] = jnp.sum(l8, axis=0, keepdims=True) + jnp.exp2(u["sink"] - u["m"])
    for u in units:
        n_q = u["q"].shape[0]
        n_keys = len(u["key_blocks"]) * KEY_BLOCK
        ot = _dot(u["vt"], u["p_s"][0:n_keys, :]) * (1.0 / u["denom"])
        for h in range(Q_PER_KV):
            u["o_store"](h, ot[:, h * n_q:(h + 1) * n_q].T)


def _lat_attn_kernel(sink_ref, q_ref, k_ref, vt_ref, kc_ref, vtc_ref, o_ref, s_s, p_s,
                     *, seq_len, layer):
    n_blk = q_ref.shape[0] // Q_BLOCK
    band_blocks = BAND // KEY_BLOCK
    ctx_blocks = vtc_ref.shape[0]
    last_start = seq_len // KEY_BLOCK - band_blocks
    rel = (lax.broadcasted_iota(jnp.int32, (KEY_BLOCK, Q_BLOCK), 0)
           - lax.broadcasted_iota(jnp.int32, (KEY_BLOCK, Q_BLOCK), 1))

    def block_units(i, slot):
        units = []
        qb = pl.program_id(1) * n_blk + i
        jb0 = jnp.clip(qb - WINDOW // KEY_BLOCK, 0, last_start)
        biases = [jnp.where(jnp.abs(rel + (jb0 + t - qb) * KEY_BLOCK) <= WINDOW, 0.0, NEG_INF)
                  for t in range(band_blocks)]
        rows = pl.ds(pl.multiple_of(i * Q_BLOCK, Q_BLOCK), Q_BLOCK)
        q = q_ref[rows, :]
        for g in range(N_KV_HEADS):
            gs = slice(g * HEAD_DIM, (g + 1) * HEAD_DIM)
            key_blocks = [
                (k_ref[pl.ds(pl.multiple_of((jb0 + t) * KEY_BLOCK, KEY_BLOCK), KEY_BLOCK), gs],
                 biases[t]) for t in range(band_blocks)]
            key_blocks += [(kc_ref[t * KEY_BLOCK:(t + 1) * KEY_BLOCK, gs], None)
                           for t in range(ctx_blocks)]
            vt = jnp.concatenate([vt_ref[jb0 + t, gs, :] for t in range(band_blocks)]
                                 + [vtc_ref[t, gs, :] for t in range(ctx_blocks)], axis=1)

            def o_store(h, tile, g=g, rows=rows):
                c0 = (g * Q_PER_KV + h) * HEAD_DIM
                o_ref[rows, c0:c0 + HEAD_DIM] = tile.astype(BF16)

            units.append(dict(g=g, q=q, key_blocks=key_blocks, vt=vt,
                              s_s=s_s.at[slot, g], p_s=p_s.at[slot, g], o_store=o_store))
        return units

    def body(it, carry):
        units = []
        for slot in range(Q_SLOTS):
            units += block_units(it * Q_SLOTS + slot, slot)
        _attend_units(sink_ref, layer, units)
        return carry

    lax.fori_loop(0, n_blk // Q_SLOTS, body, 0)


def _ctx_attn_kernel(sink_ref, q_ref, kc_ref, vtc_ref, o_ref, s_s, p_s, *, layer):
    ctx_blocks = vtc_ref.shape[0]
    q = q_ref[...]
    units = []
    for g in range(N_KV_HEADS):
        gs = slice(g * HEAD_DIM, (g + 1) * HEAD_DIM)
        key_blocks = [(kc_ref[t * KEY_BLOCK:(t + 1) * KEY_BLOCK, gs], None)
                      for t in range(ctx_blocks)]
        vt = jnp.concatenate([vtc_ref[t, gs, :] for t in range(ctx_blocks)], axis=1)

        def o_store(h, tile, g=g):
            c0 = (g * Q_PER_KV + h) * HEAD_DIM
            o_ref[:, c0:c0 + HEAD_DIM] = tile.astype(BF16)

        units.append(dict(g=g, q=q, key_blocks=key_blocks, vt=vt,
                          s_s=s_s.at[g], p_s=p_s.at[g], o_store=o_store))
    _attend_units(sink_ref, layer, units)


def _attention(q, k, vt, sink, layer, geo, with_ctx_queries):
    nt, d = q.shape
    bsz, seq_len, n_ctx = geo["batch"], geo["seq"], geo["ctx"]
    q_sup = next(s for s in (1024, 512, 256, Q_BLOCK) if seq_len % s == 0)
    n_sup = seq_len // q_sup
    ctx_blk0 = bsz * seq_len // n_ctx
    smem = pl.BlockSpec(memory_space=pltpu.SMEM)
    ctx_keys = pl.BlockSpec((n_ctx, KV_WIDTH), lambda b, *_: (ctx_blk0 + b, 0))
    ctx_vals = pl.BlockSpec((n_ctx // KEY_BLOCK, KV_WIDTH, KEY_BLOCK),
                            lambda b, *_: (ctx_blk0 + b, 0, 0))
    o = pl.pallas_call(
        functools.partial(_lat_attn_kernel, seq_len=seq_len, layer=layer),
        grid=(bsz, n_sup),
        in_specs=[
            smem,
            pl.BlockSpec((q_sup, d), lambda b, i: (b * n_sup + i, 0)),
            pl.BlockSpec((seq_len, KV_WIDTH), lambda b, i: (b, 0)),
            pl.BlockSpec((seq_len // KEY_BLOCK, KV_WIDTH, KEY_BLOCK), lambda b, i: (b, 0, 0)),
            ctx_keys, ctx_vals,
        ],
        out_specs=pl.BlockSpec((q_sup, d), lambda b, i: (b * n_sup + i, 0)),
        out_shape=jax.ShapeDtypeStruct((bsz * seq_len, d), BF16),
        scratch_shapes=[
            pltpu.VMEM((Q_SLOTS, N_KV_HEADS, BAND + n_ctx, Q_PER_KV * Q_BLOCK), F32),
            pltpu.VMEM((Q_SLOTS, N_KV_HEADS, BAND + n_ctx, Q_PER_KV * Q_BLOCK), BF16),
        ],
        compiler_params=pltpu.CompilerParams(
            dimension_semantics=("arbitrary", "arbitrary"),
            vmem_limit_bytes=VMEM_LIMIT_BYTES),
        name="lat_attention",
    )(sink, q, k, vt, k, vt)
    if not with_ctx_queries:
        return (o, o, 0)
    o_ctx = pl.pallas_call(
        functools.partial(_ctx_attn_kernel, layer=layer),
        grid=(bsz,),
        in_specs=[
            smem,
            pl.BlockSpec((n_ctx, d), lambda b: (ctx_blk0 + b, 0)),
            ctx_keys, ctx_vals,
        ],
        out_specs=pl.BlockSpec((n_ctx, d), lambda b: (b, 0)),
        out_shape=jax.ShapeDtypeStruct((bsz * n_ctx, d), BF16),
        scratch_shapes=[
            pltpu.VMEM((N_KV_HEADS, n_ctx, Q_PER_KV * n_ctx), F32),
            pltpu.VMEM((N_KV_HEADS, n_ctx, Q_PER_KV * n_ctx), BF16),
        ],
        compiler_params=pltpu.CompilerParams(
            dimension_semantics=("arbitrary",),
            vmem_limit_bytes=VMEM_LIMIT_BYTES),
        name="ctx_attention",
    )(sink, q, k, vt)
    return (o, o_ctx, 0)


def _merge_kernel(xl_ref, xc_ref, hf_ref, hb_ref, gr_ref, ol_ref, oc_ref, gl_ref, mod_ref,
                  g_ref, wr_ref, wa_ref, wo_ref, *rest, n_lat_tiles, has_ctx_tiles, n_casts):
    cast_in, out_ref, cast_out = rest[:2 * n_casts], rest[2 * n_casts], rest[2 * n_casts + 1:]
    _run_side_casts(cast_in, cast_out)
    d = D_MODEL
    nb = N_RNN_BLOCKS
    rows_per_part = xl_ref.shape[0] // MERGE_PARTS
    is_lat = pl.program_id(0) < n_lat_tiles

    def stream_rows(lat_ref, ctx_ref, r):
        return jnp.where(is_lat, lat_ref[r, :], ctx_ref[r, :]) if has_ctx_tiles else lat_ref[r, :]

    parts =[slice(p * rows_per_part, (p + 1) * rows_per_part) for p in range(MERGE_PARTS)]

    def rnn_input(p):
        h = jnp.concatenate(
            [hf_ref[pl.ds(p * rows_per_part * nb + n, rows_per_part, stride=nb), :]
             + hb_ref[pl.ds(p * rows_per_part * nb + n, rows_per_part, stride=nb), :]
             for n in range(nb)], axis=-1)
        g = gr_ref[parts[p], :].astype(F32)
        hg = h * g
        return (hg * jnp.tanh(g * (GELU_C1 + GELU_C2 * (g * g))) + hg).astype(BF16)

    def gated_mix(p, y):
        r = parts[p]
        ya = _dot(y, wr_ref[...])
        yb = _dot(stream_rows(ol_ref, oc_ref, r), wa_ref[...])
        ta = jnp.tanh(gl_ref[r, 0:d].astype(F32))
        tb = jnp.tanh(gl_ref[r, d:2 * d].astype(F32))
        return ((ya * ta + ya) + (yb * tb + yb)).astype(BF16)

    def project(p, mix):
        r = parts[p]
        m = _dot(mix, wo_ref[...])
        x = stream_rows(xl_ref, xc_ref, r)
        out_ref[r, :] = x + _rms(m) * (mod_ref[2:3, :] * g_ref[...])

    ys = [rnn_input(p) for p in range(MERGE_PARTS)]
    mixes = [gated_mix(p, ys[p]) for p in range(MERGE_PARTS)]
    for p in range(MERGE_PARTS):
        project(p, mixes[p])


def _merge(x_stream, hf, hb, gr, o_stream, gl, mod, g_post, w_o_rnn, w_o_attn, w_out,
           layer, geo, n_tiles, side_casts, cast_layer):
    d = D_MODEL
    tm = geo["tm"]
    n_lat_tiles = geo["n_lat_tiles"]
    row = lambda i: (i, 0)
    tile = pl.BlockSpec((tm, d), row)
    slab = pl.BlockSpec((tm * N_RNN_BLOCKS, RNN_BLOCK_W), row)
    cast_in_specs, cast_out_specs, cast_shapes, cast_args = _side_cast_plan(
        side_casts, cast_layer, n_tiles)
    outs = pl.pallas_call(
        functools.partial(_merge_kernel, n_lat_tiles=n_lat_tiles,
                          has_ctx_tiles=n_tiles > n_lat_tiles, n_casts=len(side_casts)),
        grid=(n_tiles,),
        in_specs=[
            *_stream_specs(x_stream, tm, d, n_lat_tiles), slab, slab, tile,
            *_stream_specs(o_stream, tm, d, n_lat_tiles),
            pl.BlockSpec((tm, 2 * d), row),
            _mod_spec(geo, layer),
            _resident_layer(g_post, layer),
            _resident(w_o_rnn.shape), _resident(w_o_attn.shape), _resident(w_out.shape),
            *cast_in_specs,
        ],
        out_specs=[tile, *cast_out_specs],
        out_shape=[jax.ShapeDtypeStruct((n_tiles * tm, d), F32), *cast_shapes],
        compiler_params=pltpu.CompilerParams(
            dimension_semantics=("arbitrary",),
            vmem_limit_bytes=VMEM_LIMIT_BYTES),
        name="merge",
    )(x_stream[0], x_stream[1], hf, hb, gr, o_stream[0], o_stream[1], gl, mod,
      g_post, w_o_rnn, w_o_attn, w_out, *cast_args)
    return outs[0], outs[1:]


def _ffn_kernel(x_ref, mod_ref, gpre_ref, gpost_ref, w1_ref, w2_ref, *rest, n_casts):
    cast_in, out_ref, cast_out = rest[:2 * n_casts], rest[2 * n_casts], rest[2 * n_casts + 1:]
    _run_side_casts(cast_in, cast_out)
    sub = x_ref.shape[0] // FFN_SUBTILES
    bounds = list(range(0, D_FF, FFN_CHUNK)) + [D_FF]
    n_chunks = len(bounds) - 1

    def normed(s):
        x = x_ref[s * sub:(s + 1) * sub, :]
        h = _rms(x) * (gpre_ref[...] * (1.0 + mod_ref[4:5, :])) + mod_ref[3:4, :]
        return h.astype(BF16)

    def first_layer(h, c):
        lo, hi = bounds[c], bounds[c + 1]
        return _dot(h, w1_ref[:, lo:hi]), _dot(h, w1_ref[:, D_FF + lo:D_FF + hi])

    h_next = normed(0)
    for s in range(FFN_SUBTILES):
        h = h_next
        if s + 1 < FFN_SUBTILES:
            h_next = normed(s + 1)
        f = None
        nxt = first_layer(h, 0)
        for c in range(n_chunks):
            half_gate, up = nxt
            if c + 1 < n_chunks:
                nxt = first_layer(h, c + 1)
            act = ((half_gate * jnp.tanh(half_gate) + half_gate) * up).astype(BF16)
            part = _dot(act, w2_ref[bounds[c]:bounds[c + 1], :])
            f = part if f is None else f + part
        rows = slice(s * sub, (s + 1) * sub)
        out_ref[rows, :] = x_ref[rows, :] + _rms(f) * (mod_ref[5:6, :] * gpost_ref[...])


def _ffn(x_all, mod, g_pre, g_post, w1, w2, layer, geo, n_tiles, side_casts, cast_layer):
    d = x_all.shape[1]
    tm = geo["tm"] * FFN_SUBTILES
    n_blocks = n_tiles // FFN_SUBTILES
    tile = pl.BlockSpec((tm, d), lambda i: (i, 0))
    cast_in_specs, cast_out_specs, cast_shapes, cast_args = _side_cast_plan(
        side_casts, cast_layer, n_blocks)
    outs = pl.pallas_call(
        functools.partial(_ffn_kernel, n_casts=len(side_casts)),
        grid=(n_blocks,),
        in_specs=[
            tile,
            _mod_spec(geo, layer, FFN_SUBTILES),
            _resident_layer(g_pre, layer), _resident_layer(g_post, layer),
            _resident(w1.shape), _resident(w2.shape),
            *cast_in_specs,
        ],
        out_specs=[tile, *cast_out_specs],
        out_shape=[jax.ShapeDtypeStruct((n_blocks * tm, d), F32), *cast_shapes],
        compiler_params=pltpu.CompilerParams(
            dimension_semantics=("arbitrary",),
            vmem_limit_bytes=VMEM_LIMIT_BYTES),
        name="ffn",
    )(x_all, mod, g_pre, g_post, w1, w2, *cast_args)
    return outs[0], outs[1:]


def _rope_tables(seq_len, pad_rows):
    f32 = np.float32
    n_rows = seq_len // GRID_W
    inv = f32(ROPE_BASE) ** (-np.arange(N_FREQ, dtype=f32) / f32(N_FREQ))
    ang = np.arange(max(n_rows, GRID_W), dtype=f32)[:, None] * inv[None, :]
    cos_u, sin_u = np.cos(ang), np.sin(ang)
    by_row = lambda t: np.repeat(t[:n_rows], GRID_W, axis=0)
    by_col = lambda t: np.tile(t[:GRID_W], (n_rows, 1))
    cos = np.concatenate([by_row(cos_u)] * 2 + [by_col(cos_u)] * 2, axis=-1)
    sin = np.concatenate(
        [-by_row(sin_u), by_row(sin_u), -by_col(sin_u), by_col(sin_u)], axis=-1)
    cos = np.concatenate([cos, np.ones((pad_rows, HEAD_DIM), f32)], axis=0)
    sin = np.concatenate([sin, np.zeros((pad_rows, HEAD_DIM), f32)], axis=0)
    scale = f32(HEAD_DIM ** -0.5 * LOG2_E)
    return tuple(jnp.asarray(t.astype(f32)) for t in (cos * scale, sin * scale, cos, sin))


def kernel(x, c, ctx, c_ctx, w_mod, b_mod, g_mix_pre, g_mix_post, g_ffn_pre, g_ffn_post, w_in, conv_w, conv_b, lru_wa, lru_ba, lru_wx, lru_bx, lru_lam, attn_sink, w_o_rnn, w_o_attn, w_out, w_ffn_in, w_ffn_out):
    bsz, seq_len, d = x.shape
    n_ctx = ctx.shape[1]
    depth = w_mod.shape[0]
    assert d == D_MODEL and n_ctx == CHUNK and seq_len % CHUNK == 0 and seq_len >= BAND
    assert bsz + 1 <= SUBLANES and seq_len % GRID_W == 0

    n_lat_rows, n_ctx_rows = bsz * seq_len, bsz * n_ctx
    big = 512 * FFN_SUBTILES
    tm = 512 if (n_ctx_rows % big == 0 and seq_len % big == 0) else CHUNK
    assert n_ctx_rows % (tm * FFN_SUBTILES) == 0 and seq_len % (tm * FFN_SUBTILES) == 0
    n_lat_tiles, n_ctx_tiles = n_lat_rows // tm, n_ctx_rows // tm
    tiles_per_batch = seq_len // tm
    geo = {
        "batch": bsz, "seq": seq_len, "ctx": n_ctx, "tm": tm,
        "lat_chunks": seq_len // CHUNK,
        "n_lat_tiles": n_lat_tiles, "n_ctx_tiles": n_ctx_tiles,
        "mod_row": lambda i: jnp.where(i < n_lat_tiles, i // tiles_per_batch, bsz),
        "table_block": lambda i: jnp.where(
            i < n_lat_tiles, i % tiles_per_batch, tiles_per_batch + i - n_lat_tiles),
    }

    cs = jnp.concatenate(
        [c, c_ctx[None, :], jnp.zeros((SUBLANES - bsz - 1, d), F32)], axis=0)
    mod = _modulation(cs, w_mod, b_mod).reshape(depth, SUBLANES, MOD_CHUNKS, d)

    tables = _rope_tables(seq_len, n_ctx_rows)

    w_gate = jnp.concatenate([lru_wa, lru_wx], axis=-1).astype(BF16)
    b_gate = 0.5 * jnp.concatenate(
        [lru_ba.reshape(depth, 2, N_RNN_BLOCKS, 1, RNN_BLOCK_W),
         lru_bx.reshape(depth, 2, N_RNN_BLOCKS, 1, RNN_BLOCK_W)], axis=-1)
    lam = lru_lam.reshape(depth, 2, N_RNN_BLOCKS, 1, RNN_BLOCK_W)
    conv_w_half = (0.5 * conv_w).reshape(depth, CONV_W, N_RNN_BLOCKS, RNN_BLOCK_W)
    conv_b_half = (0.5 * conv_b).reshape(depth, N_RNN_BLOCKS, RNN_BLOCK_W)
    g_mix_pre, g_mix_post, g_ffn_pre, g_ffn_post = (
        g.reshape(depth, 1, d) for g in (g_mix_pre, g_mix_post, g_ffn_pre, g_ffn_post))

    stream = (x.reshape(n_lat_rows, d), ctx.reshape(n_ctx_rows, d), 0)
    ones = lambda n: jnp.ones((1, n), F32)
    halves = lambda n: jnp.full((1, n), 0.5, F32)
    gl_cols = 2 * d
    w_in_scale = jnp.concatenate([ones(w_in.shape[-1] - gl_cols), halves(gl_cols)], axis=1)
    w_in_b = _to_bf16(w_in, 0, 256, w_in_scale)
    in_proj_casts = [
        (w_o_rnn, halves(d)), (w_o_attn, ones(d)), (w_out, halves(d)),
        (w_ffn_in, jnp.concatenate([halves(D_FF), ones(D_FF)], axis=1)),
        (w_ffn_out, ones(d)),
    ]
    for l in range(depth):
        need_ctx = l < depth - 1
        n_out_tiles = n_lat_tiles + n_ctx_tiles if need_ctx else n_lat_tiles
        (xr, gr, q, k, vt, gl), layer_weights = _in_proj(
            stream, mod, g_mix_pre, w_in_b, l, tables, geo, in_proj_casts)
        w_o_rnn_b, w_o_attn_b, w_out_b, w_ffn_in_b, w_ffn_out_b = layer_weights
        hf, hb = _rnn_branch(xr, conv_w_half, conv_b_half, w_gate, b_gate, lam, l, geo)
        o_stream = _attention(q, k, vt, attn_sink, l, geo, need_ctx)
        x_all, _ = _merge(
            stream, hf, hb, gr, o_stream, gl, mod, g_mix_post, w_o_rnn_b, w_o_attn_b, w_out_b,
            l, geo, n_out_tiles, [], l)
        next_casts = [(w_in, w_in_scale)] if l + 1 < depth else []
        x_all, next_weights = _ffn(
            x_all, mod, g_ffn_pre, g_ffn_post, w_ffn_in_b, w_ffn_out_b,
            l, geo, n_out_tiles, next_casts, l + 1)
        if next_weights:
            (w_in_b,) = next_weights
        stream = (x_all, x_all, n_lat_tiles)
    return x_all[:n_lat_rows].reshape(bsz, seq_len, d)
```

```python
import functools
import math

import jax
import jax.numpy as jnp
import numpy as np
from jax import lax
from jax.experimental import pallas as pl
from jax.experimental.pallas import tpu as pltpu

D_MODEL = 1024
HEAD_DIM = 128
N_Q_HEADS = 8
N_KV_HEADS = 2
Q_PER_KV = N_Q_HEADS // N_KV_HEADS
KV_WIDTH = N_KV_HEADS * HEAD_DIM
WINDOW = 128
GRID_W = 64
N_FREQ = HEAD_DIM // 4
ROPE_BASE = 10000.0
N_RNN_BLOCKS = 8
RNN_BLOCK_W = D_MODEL // N_RNN_BLOCKS
LRU_C = 8.0
CONV_W = 4
CONV_LEFT = 2
D_FF = ((8 * D_MODEL + 3 * 256 - 1) // (3 * 256)) * 256
EPS = 1e-6
NEG_INF = -1e30
MOD_CHUNKS = 6

LANES = 128
SUBLANES = 8
BF16_SUBLANES = 16
V7X_VMEM_BYTES = 64 * 1024 * 1024
VMEM_LIMIT_BYTES = V7X_VMEM_BYTES * 7 // 8

CHUNK = 256
CONV_T = 32
FFN_CHUNK = 768
FFN_SUBTILES = 2
Q_BLOCK = 128
Q_SLOTS = 4
KEY_BLOCK = 128
BAND = Q_BLOCK + 2 * WINDOW
LOG2_E = math.log2(math.e)

BF16 = jnp.bfloat16
F32 = jnp.float32
F32_TINY = float(jnp.finfo(jnp.float32).tiny)


def _dot(a, b):
    return jnp.dot(a, b, preferred_element_type=F32)


def _sigmoid(x):
    return 0.5 * jnp.tanh(0.5 * x) + 0.5


GELU_C1 = math.sqrt(2.0 / math.pi)
GELU_C2 = 0.044715 * GELU_C1


def _rms(x):
    return x * lax.rsqrt(jnp.mean(x * x, axis=-1, keepdims=True) + EPS)


def _resident(shape):
    nd = len(shape)
    return pl.BlockSpec(shape, lambda *_: (0,) * nd, pipeline_mode=pl.Buffered(1))


def _resident_layer(stacked, layer):
    tail = stacked.shape[1:]
    return pl.BlockSpec((None,) + tail, lambda *_: (layer,) + (0,) * len(tail),
                        pipeline_mode=pl.Buffered(1))


def _cast_kernel(w_ref, scale_ref, o_ref):
    o_ref[...] = (w_ref[...] * scale_ref[...]).astype(BF16)


def _to_bf16(w, layer, row_block, col_scale):
    _, rows, cols = w.shape
    return pl.pallas_call(
        _cast_kernel,
        grid=(rows // row_block,),
        in_specs=[pl.BlockSpec((None, row_block, cols), lambda r: (layer, r, 0)),
                  pl.BlockSpec((1, cols), lambda r: (0, 0))],
        out_specs=pl.BlockSpec((row_block, cols), lambda r: (r, 0)),
        out_shape=jax.ShapeDtypeStruct((rows, cols), BF16),
        compiler_params=pltpu.CompilerParams(
            dimension_semantics=("arbitrary",),
            vmem_limit_bytes=VMEM_LIMIT_BYTES),
        name="cast_bf16",
    )(w, col_scale)


def _side_cast_plan(side_casts, layer, n_steps):
    in_specs, out_specs, shapes, args = [], [], [], []
    for w, scale in side_casts:
        _, rows, cols = w.shape
        rb = next(r for r in range(BF16_SUBLANES, rows + 1, BF16_SUBLANES)
                  if rows % r == 0 and rows // r <= n_steps)
        n_blk = rows // rb
        blk = lambda i, n_blk=n_blk: jnp.minimum(i, n_blk - 1)
        in_specs += [pl.BlockSpec((None, rb, cols), lambda i, blk=blk: (layer, blk(i), 0)),
                     pl.BlockSpec((1, cols), lambda i: (0, 0))]
        out_specs.append(pl.BlockSpec((rb, cols), lambda i, blk=blk: (blk(i), 0)))
        shapes.append(jax.ShapeDtypeStruct((rows, cols), BF16))
        args += [w, scale]
    return in_specs, out_specs, shapes, args


def _run_side_casts(cast_in, cast_out):
    for c, dst in enumerate(cast_out):
        dst[...] = (cast_in[2 * c][...] * cast_in[2 * c + 1][...]).astype(BF16)


def _mod_kernel(c_ref, w_ref, b_ref, o_ref):
    c = c_ref[...]
    s = (c * _sigmoid(c)).astype(BF16)
    o_ref[...] = _dot(s, w_ref[...].astype(BF16)) + b_ref[...]


def _modulation(cs, w_mod, b_mod):
    n_layers, d, width = w_mod.shape
    tn = 1536
    return pl.pallas_call(
        _mod_kernel,
        grid=(n_layers, width // tn),
        in_specs=[
            pl.BlockSpec((SUBLANES, d), lambda l, j: (0, 0)),
            pl.BlockSpec((None, d, tn), lambda l, j: (l, 0, j)),
            pl.BlockSpec((None, 1, tn), lambda l, j: (l, 0, j)),
        ],
        out_specs=pl.BlockSpec((None, SUBLANES, tn), lambda l, j: (l, 0, j)),
        out_shape=jax.ShapeDtypeStruct((n_layers, SUBLANES, width), F32),
        compiler_params=pltpu.CompilerParams(
            dimension_semantics=("arbitrary", "arbitrary"),
            vmem_limit_bytes=VMEM_LIMIT_BYTES),
        name="modulation",
    )(cs, w_mod, b_mod.reshape(n_layers, 1, width))


def _rope(t, cos_ref, sin_ref, n_heads):
    cos = cos_ref[...]
    sin = sin_ref[...]
    lane = lax.broadcasted_iota(jnp.int32, cos.shape, 1)
    first = (lane & (2 * N_FREQ - 1)) < N_FREQ
    outs = []
    for h in range(n_heads):
        th = t[:, h * HEAD_DIM:(h + 1) * HEAD_DIM]
        partner = jnp.where(first,
                            pltpu.roll(th, HEAD_DIM - N_FREQ, axis=1),
                            pltpu.roll(th, N_FREQ, axis=1))
        outs.append(th * cos + partner * sin)
    return jnp.concatenate(outs, axis=-1)


def _stream_specs(stream, tm, d, n_lat_tiles):
    _, ctx, ctx_off = stream
    last_ctx = ctx.shape[0] // tm - 1
    lat_spec = pl.BlockSpec((tm, d), lambda i: (jnp.minimum(i, n_lat_tiles - 1), 0))
    ctx_spec = pl.BlockSpec(
        (tm, d), lambda i: (jnp.minimum(jnp.maximum(i - n_lat_tiles, 0) + ctx_off, last_ctx), 0))
    return lat_spec, ctx_spec


def _mod_spec(geo, layer, tiles_per_step=1):
    return pl.BlockSpec(
        (None, None, MOD_CHUNKS, D_MODEL),
        lambda i: (layer, geo["mod_row"](i * tiles_per_step), 0, 0))


def _stream_tile(lat_ref, ctx_ref, n_lat_tiles):
    return jnp.where(pl.program_id(0) < n_lat_tiles, lat_ref[...], ctx_ref[...])


def _in_kernel(xl_ref, xc_ref, mod_ref, g_ref, w_ref, cq_ref, sq_ref, ck_ref, sk_ref, *rest,
               n_lat_tiles, n_casts):
    cast_in, rest = rest[:2 * n_casts], rest[2 * n_casts:]
    (xr_ref, gr_ref, q_ref, k_ref, vt_ref, gl_ref), cast_out = rest[:6], rest[6:]
    _run_side_casts(cast_in, cast_out)
    d = D_MODEL
    x = _stream_tile(xl_ref, xc_ref, n_lat_tiles)
    h = _rms(x) * (g_ref[...] * (1.0 + mod_ref[1:2, :])) + mod_ref[0:1, :]
    h = h.astype(BF16)
    xr = _dot(h, w_ref[:, 0:d])
    for n in range(N_RNN_BLOCKS):
        xr_ref[pl.ds(n, x.shape[0], stride=N_RNN_BLOCKS), :] = (
            xr[:, n * RNN_BLOCK_W:(n + 1) * RNN_BLOCK_W])
    gr_ref[...] = _dot(h, w_ref[:, d:2 * d]).astype(BF16)
    q = _dot(h, w_ref[:, 2 * d:3 * d])
    q_ref[...] = _rope(q, cq_ref, sq_ref, N_Q_HEADS).astype(BF16)
    k = _dot(h, w_ref[:, 3 * d:3 * d + KV_WIDTH])
    k_ref[...] = _rope(k, ck_ref, sk_ref, N_KV_HEADS).astype(BF16)
    v = _dot(h, w_ref[:, 3 * d + KV_WIDTH:3 * d + 2 * KV_WIDTH])
    for t in range(vt_ref.shape[0]):
        vt_ref[t] = v[t * KEY_BLOCK:(t + 1) * KEY_BLOCK, :].T.astype(BF16)
    gl_ref[...] = _dot(h, w_ref[:, 3 * d + 2 * KV_WIDTH:5 * d + 2 * KV_WIDTH]).astype(BF16)


def _in_proj(stream, mod, g_pre, w_in, layer, tables, geo, side_casts):
    d = D_MODEL
    tm = geo["tm"]
    n_tiles = geo["n_lat_tiles"] + geo["n_ctx_tiles"]
    nt = n_tiles * tm
    row = lambda i: (i, 0)
    tab = pl.BlockSpec((tm, LANES), lambda i: (geo["table_block"](i), 0))
    cast_in_specs, cast_out_specs, cast_shapes, cast_args = _side_cast_plan(
        side_casts, layer, n_tiles)
    outs = pl.pallas_call(
        functools.partial(_in_kernel, n_lat_tiles=geo["n_lat_tiles"], n_casts=len(side_casts)),
        grid=(n_tiles,),
        in_specs=[
            *_stream_specs(stream, tm, d, geo["n_lat_tiles"]),
            _mod_spec(geo, layer),
            _resident_layer(g_pre, layer),
            _resident(w_in.shape),
            tab, tab, tab, tab,
            *cast_in_specs,
        ],
        out_specs=[
            pl.BlockSpec((tm * N_RNN_BLOCKS, RNN_BLOCK_W), row),
            pl.BlockSpec((tm, d), row),
            pl.BlockSpec((tm, d), row),
            pl.BlockSpec((tm, KV_WIDTH), row),
            pl.BlockSpec((tm // KEY_BLOCK, KV_WIDTH, KEY_BLOCK), lambda i: (i, 0, 0)),
            pl.BlockSpec((tm, 2 * d), row),
            *cast_out_specs,
        ],
        out_shape=[
            jax.ShapeDtypeStruct((nt * N_RNN_BLOCKS, RNN_BLOCK_W), F32),
            jax.ShapeDtypeStruct((nt, d), BF16),
            jax.ShapeDtypeStruct((nt, d), BF16),
            jax.ShapeDtypeStruct((nt, KV_WIDTH), BF16),
            jax.ShapeDtypeStruct((nt // KEY_BLOCK, KV_WIDTH, KEY_BLOCK), BF16),
            jax.ShapeDtypeStruct((nt, 2 * d), BF16),
            *cast_shapes,
        ],
        compiler_params=pltpu.CompilerParams(
            dimension_semantics=("arbitrary",),
            vmem_limit_bytes=VMEM_LIMIT_BYTES),
        name="in_proj",
    )(stream[0], stream[1], mod, g_pre, w_in, *tables, *cast_args)
    return outs[:6], outs[6:]


def _rnn_kernel(xf_ref, xfp_ref, xfn_ref, xb_ref, xbp_ref, xbn_ref,
                cw_ref, cb_ref, wg_ref, bg_ref, lam_ref,
                hf_ref, hb_ref,
                xc_s, a_s, b_s, carry_s, *, n_steps):
    j = pl.program_id(1)
    nb, bw = N_RNN_BLOCKS, RNN_BLOCK_W
    n_in = CONV_T + CONV_W - 1
    right = CONV_W - 1 - CONV_LEFT

    @pl.when(j == 0)
    def _():
        carry_s[...] = jnp.zeros_like(carry_s)

    def conv(slot, x_ref, xp_ref, xn_ref, has_prev, has_next):
        def conv_piece(v, base):
            v = v.reshape(n_in, nb, bw)
            acc = cb_ref[...] + v[0:CONV_T] * cw_ref[0]
            for k in range(1, CONV_W):
                acc = acc + v[k:k + CONV_T] * cw_ref[k]
            xc_s[slot, pl.ds(base, CONV_T * nb), :] = acc.reshape(CONV_T * nb, bw)

        halo_rows = xp_ref.shape[0]
        left = jnp.where(has_prev, xp_ref[halo_rows - CONV_LEFT * nb:halo_rows, :], 0.0)
        conv_piece(jnp.concatenate([left, x_ref[0:(n_in - CONV_LEFT) * nb, :]], axis=0), 0)
        tail = jnp.where(has_next, xn_ref[0:right * nb, :], 0.0)
        last = CHUNK - CONV_T
        conv_piece(jnp.concatenate(
            [x_ref[(last - CONV_LEFT) * nb:CHUNK * nb, :], tail], axis=0), last * nb)

        def conv_body(c, carry):
            base = pl.multiple_of(c * (CONV_T * nb), CONV_T * nb)
            conv_piece(x_ref[pl.ds(base - CONV_LEFT * nb, n_in * nb), :], base)
            return carry

        lax.fori_loop(1, CHUNK // CONV_T - 1, conv_body, 0)

    fwd_chunk = j
    bwd_chunk = jnp.where(j == 0, 0, n_steps - j)
    mid = jnp.logical_and(j >= 1, j < n_steps - 1)

    @pl.when(2 * j <= n_steps)
    def _():
        conv(fwd_chunk, xf_ref, xfp_ref, xfn_ref, j >= 2, mid)

    @pl.when(jnp.logical_and(j >= 1, 2 * j < n_steps))
    def _():
        conv(bwd_chunk, xb_ref, xbp_ref, xbn_ref, mid, j >= 2)

    def gate_unit(parity, direction, n):
        rows = pl.ds(n, CHUNK, stride=nb)
        lam = lam_ref[direction, n]
        c_half = (0.5 * LRU_C * LOG2_E) * (
            jnp.minimum(lam, 0.0) - jnp.log1p(jnp.exp(-jnp.abs(lam))))
        xh = xc_s[bwd_chunk if direction else fwd_chunk, rows, :]
        z = _dot(xh.astype(BF16), wg_ref[direction, n]) + bg_ref[direction, n]
        a = jnp.exp2(c_half * jnp.tanh(z[:, 0:bw]) + c_half)
        gated = xh * jnp.tanh(z[:, bw:2 * bw]) + xh
        om = 1.0 - a * a
        a_s[parity, direction, rows, :] = a
        b_s[parity, direction, rows, :] = (om * lax.rsqrt(jnp.maximum(om, F32_TINY))) * gated

    def scan_steps(parity, t0, t1, carry):
        hf, hb = carry
        for t in range(t0, t1):
            tf, tb = t * nb, (CHUNK - 1 - t) * nb
            hf = a_s[parity, 0, tf:tf + nb, :] * hf + b_s[parity, 0, tf:tf + nb, :]
            hb = a_s[parity, 1, tb:tb + nb, :] * hb + b_s[parity, 1, tb:tb + nb, :]
            hf_ref[tf:tf + nb, :] = hf
            hb_ref[tb:tb + nb, :] = hb
        return hf, hb

    def work(parity, do_gates, do_scan):
        units = [(direction, n) for direction in range(2) for n in range(nb)]
        per_unit = CHUNK // len(units)
        carry = (carry_s[0], carry_s[1]) if do_scan else None
        for idx, (direction, n) in enumerate(units):
            if do_gates:
                gate_unit(parity, direction, n)
            if do_scan:
                carry = scan_steps(1 - parity, idx * per_unit, (idx + 1) * per_unit, carry)
        if do_scan:
            carry_s[0], carry_s[1] = carry

    last_parity = (n_steps - 1) % 2
    pl.when(j == 0)(lambda: work(0, True, False))
    pl.when(j == n_steps)(lambda: work(1 - last_parity, False, True))
    inner = jnp.logical_and(j >= 1, j < n_steps)
    pl.when(jnp.logical_and(inner, j % 2 == 0))(lambda: work(0, True, True))
    pl.when(jnp.logical_and(inner, j % 2 == 1))(lambda: work(1, True, True))


def _rnn_branch(xr, conv_w, conv_b, w_gate, b_gate, lam, layer, geo):
    nb, bw = N_RNN_BLOCKS, RNN_BLOCK_W
    bsz, n_lat = geo["batch"], geo["lat_chunks"]
    n_steps = n_lat + 1
    halo = SUBLANES
    n_halo = xr.shape[0] // (halo * nb)
    per = CHUNK // halo

    def fwd_blk(b, s):
        return jnp.where(s == 0, bsz * n_lat + b, b * n_lat + s - 1)

    def bwd_blk(b, s):
        return jnp.where(s == 0, bsz * n_lat + b, b * n_lat + n_lat - s)

    in_step = lambda j: jnp.minimum(j, n_steps - 1)
    out_step = lambda j: jnp.maximum(j - 1, 0)

    def chunk(blk, step):
        return pl.BlockSpec((CHUNK * nb, bw), lambda b, j: (blk(b, step(j)), 0))

    def prev(blk):
        return pl.BlockSpec(
            (halo * nb, bw), lambda b, j: (jnp.maximum(blk(b, in_step(j)) * per - 1, 0), 0))

    def nxt(blk):
        return pl.BlockSpec(
            (halo * nb, bw),
            lambda b, j: (jnp.minimum((blk(b, in_step(j)) + 1) * per, n_halo - 1), 0))

    return pl.pallas_call(
        functools.partial(_rnn_kernel, n_steps=n_steps),
        grid=(bsz, n_steps + 1),
        in_specs=[
            chunk(fwd_blk, in_step), prev(fwd_blk), nxt(fwd_blk),
            chunk(bwd_blk, in_step), prev(bwd_blk), nxt(bwd_blk),
            _resident_layer(conv_w, layer),
            _resident_layer(conv_b, layer),
            _resident_layer(w_gate, layer),
            _resident_layer(b_gate, layer),
            _resident_layer(lam, layer),
        ],
        out_specs=[chunk(fwd_blk, out_step), chunk(bwd_blk, out_step)],
        out_shape=[jax.ShapeDtypeStruct(xr.shape, F32)] * 2,
        scratch_shapes=[
            pltpu.VMEM((n_steps, CHUNK * nb, bw), F32),
            pltpu.VMEM((2, 2, CHUNK * nb, bw), F32),
            pltpu.VMEM((2, 2, CHUNK * nb, bw), F32),
            pltpu.VMEM((2, nb, bw), F32),
        ],
        compiler_params=pltpu.CompilerParams(
            dimension_semantics=("arbitrary", "arbitrary"),
            vmem_limit_bytes=VMEM_LIMIT_BYTES),
        name="rglru",
    )(xr, xr, xr, xr, xr, xr, conv_w, conv_b, w_gate, b_gate, lam)


def _stack_heads(q, g):
    base = g * Q_PER_KV * HEAD_DIM
    return jnp.concatenate(
        [q[:, base + h * HEAD_DIM: base + (h + 1) * HEAD_DIM] for h in range(Q_PER_KV)], axis=0)


def _nt_dot(a, b):
    return lax.dot_general(a, b, (((1,), (1,)), ((), ())), preferred_element_type=F32)


def _attend_units(sink_ref, layer, units):
    folds = KEY_BLOCK // SUBLANES
    for u in units:
        n_q = u["q"].shape[0]
        width = Q_PER_KV * n_q
        qs = _stack_heads(u["q"], u["g"])
        sink = jnp.concatenate(
            [jnp.full((1, n_q), sink_ref[layer, u["g"] * Q_PER_KV + h] * LOG2_E, F32)
             for h in range(Q_PER_KV)], axis=1)
        m8 = jnp.broadcast_to(sink, (SUBLANES, width))
        for j, (k, bias) in enumerate(u["key_blocks"]):
            s = _nt_dot(k, qs)
            if bias is not None:
                s = s + jnp.concatenate([bias] * Q_PER_KV, axis=1)
            u["s_s"][j * KEY_BLOCK:(j + 1) * KEY_BLOCK, :] = s
            m8 = jnp.maximum(m8, jnp.max(s.reshape(folds, SUBLANES, width), axis=0))
        u["sink"] = sink
        u["m"] = jnp.max(m8, axis=0, keepdims=True)
    for u in units:
        width = Q_PER_KV * u["q"].shape[0]
        l8 = jnp.zeros((SUBLANES, width), F32)
        for j in range(len(u["key_blocks"])):
            p = jnp.exp2(u["s_s"][j * KEY_BLOCK:(j + 1) * KEY_BLOCK, :] - u["m"])
            l8 = l8 + jnp.sum(p.reshape(folds, SUBLANES, width), axis=0)
            u["p_s"][j * KEY_BLOCK:(j + 1) * KEY_BLOCK, :] = p.astype(BF16)
        u["denom"] = jnp.sum(l8, axis=0, keepdims=True) + jnp.exp2(u["sink"] - u["m"])
    for u in units:
        n_q = u["q"].shape[0]
        n_keys = len(u["key_blocks"]) * KEY_BLOCK
        ot = _dot(u["vt"], u["p_s"][0:n_keys, :]) * (1.0 / u["denom"])
        for h in range(Q_PER_KV):
            u["o_store"](h, ot[:, h * n_q:(h + 1) * n_q].T)


def _lat_attn_kernel(sink_ref, q_ref, k_ref, vt_ref, kc_ref, vtc_ref, o_ref, s_s, p_s,
                     *, seq_len, layer):
    n_blk = q_ref.shape[0] // Q_BLOCK
    band_blocks = BAND // KEY_BLOCK
    ctx_blocks = vtc_ref.shape[0]
    last_start = seq_len // KEY_BLOCK - band_blocks
    rel = (lax.broadcasted_iota(jnp.int32, (KEY_BLOCK, Q_BLOCK), 0)
           - lax.broadcasted_iota(jnp.int32, (KEY_BLOCK, Q_BLOCK), 1))

    def block_units(i, slot):
        units = []
        qb = pl.program_id(1) * n_blk + i
        jb0 = jnp.clip(qb - WINDOW // KEY_BLOCK, 0, last_start)
        biases = [jnp.where(jnp.abs(rel + (jb0 + t - qb) * KEY_BLOCK) <= WINDOW, 0.0, NEG_INF)
                  for t in range(band_blocks)]
        rows = pl.ds(pl.multiple_of(i * Q_BLOCK, Q_BLOCK), Q_BLOCK)
        q = q_ref[rows, :]
        for g in range(N_KV_HEADS):
            gs = slice(g * HEAD_DIM, (g + 1) * HEAD_DIM)
            key_blocks = [
                (k_ref[pl.ds(pl.multiple_of((jb0 + t) * KEY_BLOCK, KEY_BLOCK), KEY_BLOCK), gs],
                 biases[t]) for t in range(band_blocks)]
            key_blocks += [(kc_ref[t * KEY_BLOCK:(t + 1) * KEY_BLOCK, gs], None)
                           for t in range(ctx_blocks)]
            vt = jnp.concatenate([vt_ref[jb0 + t, gs, :] for t in range(band_blocks)]
                                 + [vtc_ref[t, gs, :] for t in range(ctx_blocks)], axis=1)

            def o_store(h, tile, g=g, rows=rows):
                c0 = (g * Q_PER_KV + h) * HEAD_DIM
                o_ref[rows, c0:c0 + HEAD_DIM] = tile.astype(BF16)

            units.append(dict(g=g, q=q, key_blocks=key_blocks, vt=vt,
                              s_s=s_s.at[slot, g], p_s=p_s.at[slot, g], o_store=o_store))
        return units

    def body(it, carry):
        units = []
        for slot in range(Q_SLOTS):
            units += block_units(it * Q_SLOTS + slot, slot)
        _attend_units(sink_ref, layer, units)
        return carry

    lax.fori_loop(0, n_blk // Q_SLOTS, body, 0)


def _ctx_attn_kernel(sink_ref, q_ref, kc_ref, vtc_ref, o_ref, s_s, p_s, *, layer):
    ctx_blocks = vtc_ref.shape[0]
    q = q_ref[...]
    units = []
    for g in range(N_KV_HEADS):
        gs = slice(g * HEAD_DIM, (g + 1) * HEAD_DIM)
        key_blocks = [(kc_ref[t * KEY_BLOCK:(t + 1) * KEY_BLOCK, gs], None)
                      for t in range(ctx_blocks)]
        vt = jnp.concatenate([vtc_ref[t, gs, :] for t in range(ctx_blocks)], axis=1)

        def o_store(h, tile, g=g):
            c0 = (g * Q_PER_KV + h) * HEAD_DIM
            o_ref[:, c0:c0 + HEAD_DIM] = tile.astype(BF16)

        units.append(dict(g=g, q=q, key_blocks=key_blocks, vt=vt,
                          s_s=s_s.at[g], p_s=p_s.at[g], o_store=o_store))
    _attend_units(sink_ref, layer, units)


def _attention(q, k, vt, sink, layer, geo, with_ctx_queries):
    nt, d = q.shape
    bsz, seq_len, n_ctx = geo["batch"], geo["seq"], geo["ctx"]
    q_sup = next(s for s in (1024, 512, 256, Q_BLOCK) if seq_len % s == 0)
    n_sup = seq_len // q_sup
    ctx_blk0 = bsz * seq_len // n_ctx
    smem = pl.BlockSpec(memory_space=pltpu.SMEM)
    ctx_keys = pl.BlockSpec((n_ctx, KV_WIDTH), lambda b, *_: (ctx_blk0 + b, 0))
    ctx_vals = pl.BlockSpec((n_ctx // KEY_BLOCK, KV_WIDTH, KEY_BLOCK),
                            lambda b, *_: (ctx_blk0 + b, 0, 0))
    o = pl.pallas_call(
        functools.partial(_lat_attn_kernel, seq_len=seq_len, layer=layer),
        grid=(bsz, n_sup),
        in_specs=[
            smem,
            pl.BlockSpec((q_sup, d), lambda b, i: (b * n_sup + i, 0)),
            pl.BlockSpec((seq_len, KV_WIDTH), lambda b, i: (b, 0)),
            pl.BlockSpec((seq_len // KEY_BLOCK, KV_WIDTH, KEY_BLOCK), lambda b, i: (b, 0, 0)),
            ctx_keys, ctx_vals,
        ],
        out_specs=pl.BlockSpec((q_sup, d), lambda b, i: (b * n_sup + i, 0)),
        out_shape=jax.ShapeDtypeStruct((bsz * seq_len, d), BF16),
        scratch_shapes=[
            pltpu.VMEM((Q_SLOTS, N_KV_HEADS, BAND + n_ctx, Q_PER_KV * Q_BLOCK), F32),
            pltpu.VMEM((Q_SLOTS, N_KV_HEADS, BAND + n_ctx, Q_PER_KV * Q_BLOCK), BF16),
        ],
        compiler_params=pltpu.CompilerParams(
            dimension_semantics=("arbitrary", "arbitrary"),
            vmem_limit_bytes=VMEM_LIMIT_BYTES),
        name="lat_attention",
    )(sink, q, k, vt, k, vt)
    if not with_ctx_queries:
        return (o, o, 0)
    o_ctx = pl.pallas_call(
        functools.partial(_ctx_attn_kernel, layer=layer),
        grid=(bsz,),
        in_specs=[
            smem,
            pl.BlockSpec((n_ctx, d), lambda b: (ctx_blk0 + b, 0)),
            ctx_keys, ctx_vals,
        ],
        out_specs=pl.BlockSpec((n_ctx, d), lambda b: (b, 0)),
        out_shape=jax.ShapeDtypeStruct((bsz * n_ctx, d), BF16),
        scratch_shapes=[
            pltpu.VMEM((N_KV_HEADS, n_ctx, Q_PER_KV * n_ctx), F32),
            pltpu.VMEM((N_KV_HEADS, n_ctx, Q_PER_KV * n_ctx), BF16),
        ],
        compiler_params=pltpu.CompilerParams(
            dimension_semantics=("arbitrary",),
            vmem_limit_bytes=VMEM_LIMIT_BYTES),
        name="ctx_attention",
    )(sink, q, k, vt)
    return (o, o_ctx, 0)


def _merge_kernel(xl_ref, xc_ref, hf_ref, hb_ref, gr_ref, ol_ref, oc_ref, gl_ref, mod_ref,
                  g_ref, wr_ref, wa_ref, wo_ref, out_ref, *, n_lat_tiles, has_ctx_tiles):
    d = D_MODEL
    nb = N_RNN_BLOCKS
    tm = xl_ref.shape[0]
    is_lat = pl.program_id(0) < n_lat_tiles

    def stream_tile(lat_ref, ctx_ref):
        return jnp.where(is_lat, lat_ref[...], ctx_ref[...]) if has_ctx_tiles else lat_ref[...]

    h = jnp.concatenate(
        [hf_ref[pl.ds(n, tm, stride=nb), :] + hb_ref[pl.ds(n, tm, stride=nb), :]
         for n in range(nb)], axis=-1)
    g = gr_ref[...].astype(F32)
    hg = h * g
    y = (hg * jnp.tanh(g * (GELU_C1 + GELU_C2 * (g * g))) + hg).astype(BF16)
    ya = _dot(y, wr_ref[...])
    yb = _dot(stream_tile(ol_ref, oc_ref), wa_ref[...])
    ta = jnp.tanh(gl_ref[:, 0:d].astype(F32))
    tb = jnp.tanh(gl_ref[:, d:2 * d].astype(F32))
    mix = ((ya * ta + ya) + (yb * tb + yb)).astype(BF16)
    m = _dot(mix, wo_ref[...])
    out_ref[...] = stream_tile(xl_ref, xc_ref) + _rms(m) * (mod_ref[2:3, :] * g_ref[...])


def _merge(x_stream, hf, hb, gr, o_stream, gl, mod, g_post, w_o_rnn, w_o_attn, w_out,
           layer, geo, n_tiles):
    d = D_MODEL
    tm = geo["tm"]
    n_lat_tiles = geo["n_lat_tiles"]
    row = lambda i: (i, 0)
    tile = pl.BlockSpec((tm, d), row)
    slab = pl.BlockSpec((tm * N_RNN_BLOCKS, RNN_BLOCK_W), row)
    return pl.pallas_call(
        functools.partial(_merge_kernel, n_lat_tiles=n_lat_tiles,
                          has_ctx_tiles=n_tiles > n_lat_tiles),
        grid=(n_tiles,),
        in_specs=[
            *_stream_specs(x_stream, tm, d, n_lat_tiles), slab, slab, tile,
            *_stream_specs(o_stream, tm, d, n_lat_tiles),
            pl.BlockSpec((tm, 2 * d), row),
            _mod_spec(geo, layer),
            _resident_layer(g_post, layer),
            _resident(w_o_rnn.shape), _resident(w_o_attn.shape), _resident(w_out.shape),
        ],
        out_specs=tile,
        out_shape=jax.ShapeDtypeStruct((n_tiles * tm, d), F32),
        compiler_params=pltpu.CompilerParams(
            dimension_semantics=("arbitrary",),
            vmem_limit_bytes=VMEM_LIMIT_BYTES),
        name="merge",
    )(x_stream[0], x_stream[1], hf, hb, gr, o_stream[0], o_stream[1], gl, mod,
      g_post, w_o_rnn, w_o_attn, w_out)


def _ffn_kernel(x_ref, mod_ref, gpre_ref, gpost_ref, w1_ref, w2_ref, *rest, n_casts):
    cast_in, out_ref, cast_out = rest[:2 * n_casts], rest[2 * n_casts], rest[2 * n_casts + 1:]
    _run_side_casts(cast_in, cast_out)
    sub = x_ref.shape[0] // FFN_SUBTILES
    bounds = list(range(0, D_FF, FFN_CHUNK)) + [D_FF]
    n_chunks = len(bounds) - 1

    def normed(s):
        x = x_ref[s * sub:(s + 1) * sub, :]
        h = _rms(x) * (gpre_ref[...] * (1.0 + mod_ref[4:5, :])) + mod_ref[3:4, :]
        return h.astype(BF16)

    def first_layer(h, c):
        lo, hi = bounds[c], bounds[c + 1]
        return _dot(h, w1_ref[:, lo:hi]), _dot(h, w1_ref[:, D_FF + lo:D_FF + hi])

    h_next = normed(0)
    for s in range(FFN_SUBTILES):
        h = h_next
        if s + 1 < FFN_SUBTILES:
            h_next = normed(s + 1)
        f = None
        nxt = first_layer(h, 0)
        for c in range(n_chunks):
            half_gate, up = nxt
            if c + 1 < n_chunks:
                nxt = first_layer(h, c + 1)
            act = ((half_gate * jnp.tanh(half_gate) + half_gate) * up).astype(BF16)
            part = _dot(act, w2_ref[bounds[c]:bounds[c + 1], :])
            f = part if f is None else f + part
        rows = slice(s * sub, (s + 1) * sub)
        out_ref[rows, :] = x_ref[rows, :] + _rms(f) * (mod_ref[5:6, :] * gpost_ref[...])


def _ffn(x_all, mod, g_pre, g_post, w1, w2, layer, geo, n_tiles, side_casts, cast_layer):
    d = x_all.shape[1]
    tm = geo["tm"] * FFN_SUBTILES
    n_blocks = n_tiles // FFN_SUBTILES
    tile = pl.BlockSpec((tm, d), lambda i: (i, 0))
    cast_in_specs, cast_out_specs, cast_shapes, cast_args = _side_cast_plan(
        side_casts, cast_layer, n_blocks)
    outs = pl.pallas_call(
        functools.partial(_ffn_kernel, n_casts=len(side_casts)),
        grid=(n_blocks,),
        in_specs=[
            tile,
            _mod_spec(geo, layer, FFN_SUBTILES),
            _resident_layer(g_pre, layer), _resident_layer(g_post, layer),
            _resident(w1.shape), _resident(w2.shape),
            *cast_in_specs,
        ],
        out_specs=[tile, *cast_out_specs],
        out_shape=[jax.ShapeDtypeStruct((n_blocks * tm, d), F32), *cast_shapes],
        compiler_params=pltpu.CompilerParams(
            dimension_semantics=("arbitrary",),
            vmem_limit_bytes=VMEM_LIMIT_BYTES),
        name="ffn",
    )(x_all, mod, g_pre, g_post, w1, w2, *cast_args)
    return outs[0], outs[1:]


def _rope_tables(seq_len, pad_rows):
    f32 = np.float32
    n_rows = seq_len // GRID_W
    inv = f32(ROPE_BASE) ** (-np.arange(N_FREQ, dtype=f32) / f32(N_FREQ))
    ang = np.arange(max(n_rows, GRID_W), dtype=f32)[:, None] * inv[None, :]
    cos_u, sin_u = np.cos(ang), np.sin(ang)
    by_row = lambda t: np.repeat(t[:n_rows], GRID_W, axis=0)
    by_col = lambda t: np.tile(t[:GRID_W], (n_rows, 1))
    cos = np.concatenate([by_row(cos_u)] * 2 + [by_col(cos_u)] * 2, axis=-1)
    sin = np.concatenate(
        [-by_row(sin_u), by_row(sin_u), -by_col(sin_u), by_col(sin_u)], axis=-1)
    cos = np.concatenate([cos, np.ones((pad_rows, HEAD_DIM), f32)], axis=0)
    sin = np.concatenate([sin, np.zeros((pad_rows, HEAD_DIM), f32)], axis=0)
    scale = f32(HEAD_DIM ** -0.5 * LOG2_E)
    return tuple(jnp.asarray(t.astype(f32)) for t in (cos * scale, sin * scale, cos, sin))


def kernel(x, c, ctx, c_ctx, w_mod, b_mod, g_mix_pre, g_mix_post, g_ffn_pre, g_ffn_post, w_in, conv_w, conv_b, lru_wa, lru_ba, lru_wx, lru_bx, lru_lam, attn_sink, w_o_rnn, w_o_attn, w_out, w_ffn_in, w_ffn_out):
    bsz, seq_len, d = x.shape
    n_ctx = ctx.shape[1]
    depth = w_mod.shape[0]
    assert d == D_MODEL and n_ctx == CHUNK and seq_len % CHUNK == 0 and seq_len >= BAND
    assert bsz + 1 <= SUBLANES and seq_len % GRID_W == 0

    n_lat_rows, n_ctx_rows = bsz * seq_len, bsz * n_ctx
    big = 512 * FFN_SUBTILES
    tm = 512 if (n_ctx_rows % big == 0 and seq_len % big == 0) else CHUNK
    assert n_ctx_rows % (tm * FFN_SUBTILES) == 0 and seq_len % (tm * FFN_SUBTILES) == 0
    n_lat_tiles, n_ctx_tiles = n_lat_rows // tm, n_ctx_rows // tm
    tiles_per_batch = seq_len // tm
    geo = {
        "batch": bsz, "seq": seq_len, "ctx": n_ctx, "tm": tm,
        "lat_chunks": seq_len // CHUNK,
        "n_lat_tiles": n_lat_tiles, "n_ctx_tiles": n_ctx_tiles,
        "mod_row": lambda i: jnp.where(i < n_lat_tiles, i // tiles_per_batch, bsz),
        "table_block": lambda i: jnp.where(
            i < n_lat_tiles, i % tiles_per_batch, tiles_per_batch + i - n_lat_tiles),
    }

    cs = jnp.concatenate(
        [c, c_ctx[None, :], jnp.zeros((SUBLANES - bsz - 1, d), F32)], axis=0)
    mod = _modulation(cs, w_mod, b_mod).reshape(depth, SUBLANES, MOD_CHUNKS, d)

    tables = _rope_tables(seq_len, n_ctx_rows)

    w_gate = jnp.concatenate([lru_wa, lru_wx], axis=-1).astype(BF16)
    b_gate = 0.5 * jnp.concatenate(
        [lru_ba.reshape(depth, 2, N_RNN_BLOCKS, 1, RNN_BLOCK_W),
         lru_bx.reshape(depth, 2, N_RNN_BLOCKS, 1, RNN_BLOCK_W)], axis=-1)
    lam = lru_lam.reshape(depth, 2, N_RNN_BLOCKS, 1, RNN_BLOCK_W)
    conv_w_half = (0.5 * conv_w).reshape(depth, CONV_W, N_RNN_BLOCKS, RNN_BLOCK_W)
    conv_b_half = (0.5 * conv_b).reshape(depth, N_RNN_BLOCKS, RNN_BLOCK_W)
    g_mix_pre, g_mix_post, g_ffn_pre, g_ffn_post = (
        g.reshape(depth, 1, d) for g in (g_mix_pre, g_mix_post, g_ffn_pre, g_ffn_post))

    stream = (x.reshape(n_lat_rows, d), ctx.reshape(n_ctx_rows, d), 0)
    ones = lambda n: jnp.ones((1, n), F32)
    halves = lambda n: jnp.full((1, n), 0.5, F32)
    gl_cols = 2 * d
    w_in_scale = jnp.concatenate([ones(w_in.shape[-1] - gl_cols), halves(gl_cols)], axis=1)
    w_in_b = _to_bf16(w_in, 0, 256, w_in_scale)
    in_proj_casts = [
        (w_o_rnn, halves(d)), (w_o_attn, ones(d)), (w_out, halves(d)),
        (w_ffn_in, jnp.concatenate([halves(D_FF), ones(D_FF)], axis=1)),
        (w_ffn_out, ones(d)),
    ]
    for l in range(depth):
        need_ctx = l < depth - 1
        n_out_tiles = n_lat_tiles + n_ctx_tiles if need_ctx else n_lat_tiles
        (xr, gr, q, k, vt, gl), layer_weights = _in_proj(
            stream, mod, g_mix_pre, w_in_b, l, tables, geo, in_proj_casts)
        w_o_rnn_b, w_o_attn_b, w_out_b, w_ffn_in_b, w_ffn_out_b = layer_weights
        hf, hb = _rnn_branch(xr, conv_w_half, conv_b_half, w_gate, b_gate, lam, l, geo)
        o_stream = _attention(q, k, vt, attn_sink, l, geo, need_ctx)
        x_all = _merge(
            stream, hf, hb, gr, o_stream, gl, mod, g_mix_post, w_o_rnn_b, w_o_attn_b, w_out_b,
            l, geo, n_out_tiles)
        next_casts = [(w_in, w_in_scale)] if l + 1 < depth else []
        x_all, next_weights = _ffn(
            x_all, mod, g_ffn_pre, g_ffn_post, w_ffn_in_b, w_ffn_out_b,
            l, geo, n_out_tiles, next_casts, l + 1)
        if next_weights:
            (w_in_b,) = next_weights
        stream = (x_all, x_all, n_lat_tiles)
    return x_all[:n_lat_rows].reshape(bsz, seq_len, d)
```

```python
import functools
import math

import jax
import jax.numpy as jnp
import numpy as np
from jax import lax
from jax.experimental import pallas as pl
from jax.experimental.pallas import tpu as pltpu

D_MODEL = 1024
HEAD_DIM = 128
N_Q_HEADS = 8
N_KV_HEADS = 2
Q_PER_KV = N_Q_HEADS // N_KV_HEADS
KV_WIDTH = N_KV_HEADS * HEAD_DIM
WINDOW = 128
GRID_W = 64
N_FREQ = HEAD_DIM // 4
ROPE_BASE = 10000.0
N_RNN_BLOCKS = 8
RNN_BLOCK_W = D_MODEL // N_RNN_BLOCKS
LRU_C = 8.0
CONV_W = 4
CONV_LEFT = 2
D_FF = ((8 * D_MODEL + 3 * 256 - 1) // (3 * 256)) * 256
EPS = 1e-6
NEG_INF = -1e30
MOD_CHUNKS = 6

LANES = 128
SUBLANES = 8
BF16_SUBLANES = 16
V7X_VMEM_BYTES = 64 * 1024 * 1024
VMEM_LIMIT_BYTES = V7X_VMEM_BYTES * 7 // 8

CHUNK = 256
CONV_T = 32
FFN_CHUNK = 768
FFN_SUBTILES = 2
Q_BLOCK = 128
Q_SLOTS = 4
KEY_BLOCK = 128
BAND = Q_BLOCK + 2 * WINDOW
LOG2_E = math.log2(math.e)

BF16 = jnp.bfloat16
F32 = jnp.float32
F32_TINY = float(jnp.finfo(jnp.float32).tiny)


def _dot(a, b):
    return jnp.dot(a, b, preferred_element_type=F32)


def _sigmoid(x):
    return 0.5 * jnp.tanh(0.5 * x) + 0.5


GELU_C1 = math.sqrt(2.0 / math.pi)
GELU_C2 = 0.044715 * GELU_C1


def _rms(x):
    return x * lax.rsqrt(jnp.mean(x * x, axis=-1, keepdims=True) + EPS)


def _resident(shape):
    nd = len(shape)
    return pl.BlockSpec(shape, lambda *_: (0,) * nd, pipeline_mode=pl.Buffered(1))


def _resident_layer(stacked, layer):
    tail = stacked.shape[1:]
    return pl.BlockSpec((None,) + tail, lambda *_: (layer,) + (0,) * len(tail),
                        pipeline_mode=pl.Buffered(1))


def _cast_kernel(w_ref, scale_ref, o_ref):
    o_ref[...] = (w_ref[...] * scale_ref[...]).astype(BF16)


def _to_bf16(w, layer, row_block, col_scale):
    _, rows, cols = w.shape
    return pl.pallas_call(
        _cast_kernel,
        grid=(rows // row_block,),
        in_specs=[pl.BlockSpec((None, row_block, cols), lambda r: (layer, r, 0)),
                  pl.BlockSpec((1, cols), lambda r: (0, 0))],
        out_specs=pl.BlockSpec((row_block, cols), lambda r: (r, 0)),
        out_shape=jax.ShapeDtypeStruct((rows, cols), BF16),
        compiler_params=pltpu.CompilerParams(
            dimension_semantics=("arbitrary",),
            vmem_limit_bytes=VMEM_LIMIT_BYTES),
        name="cast_bf16",
    )(w, col_scale)


def _side_cast_plan(side_casts, layer, n_steps):
    in_specs, out_specs, shapes, args = [], [], [], []
    for w, scale in side_casts:
        _, rows, cols = w.shape
        rb = next(r for r in range(BF16_SUBLANES, rows + 1, BF16_SUBLANES)
                  if rows % r == 0 and rows // r <= n_steps)
        n_blk = rows // rb
        blk = lambda i, n_blk=n_blk: jnp.minimum(i, n_blk - 1)
        in_specs += [pl.BlockSpec((None, rb, cols), lambda i, blk=blk: (layer, blk(i), 0)),
                     pl.BlockSpec((1, cols), lambda i: (0, 0))]
        out_specs.append(pl.BlockSpec((rb, cols), lambda i, blk=blk: (blk(i), 0)))
        shapes.append(jax.ShapeDtypeStruct((rows, cols), BF16))
        args += [w, scale]
    return in_specs, out_specs, shapes, args


def _run_side_casts(cast_in, cast_out):
    for c, dst in enumerate(cast_out):
        dst[...] = (cast_in[2 * c][...] * cast_in[2 * c + 1][...]).astype(BF16)


def _mod_kernel(c_ref, w_ref, b_ref, o_ref):
    c = c_ref[...]
    s = (c * _sigmoid(c)).astype(BF16)
    o_ref[...] = _dot(s, w_ref[...].astype(BF16)) + b_ref[...]


def _modulation(cs, w_mod, b_mod):
    n_layers, d, width = w_mod.shape
    tn = 768
    return pl.pallas_call(
        _mod_kernel,
        grid=(n_layers, width // tn),
        in_specs=[
            pl.BlockSpec((SUBLANES, d), lambda l, j: (0, 0)),
            pl.BlockSpec((None, d, tn), lambda l, j: (l, 0, j)),
            pl.BlockSpec((None, 1, tn), lambda l, j: (l, 0, j)),
        ],
        out_specs=pl.BlockSpec((None, SUBLANES, tn), lambda l, j: (l, 0, j)),
        out_shape=jax.ShapeDtypeStruct((n_layers, SUBLANES, width), F32),
        compiler_params=pltpu.CompilerParams(
            dimension_semantics=("arbitrary", "arbitrary"),
            vmem_limit_bytes=VMEM_LIMIT_BYTES),
        name="modulation",
    )(cs, w_mod, b_mod.reshape(n_layers, 1, width))


def _rope(t, cos_ref, sin_ref, n_heads):
    cos = cos_ref[...]
    sin = sin_ref[...]
    lane = lax.broadcasted_iota(jnp.int32, cos.shape, 1)
    first = (lane & (2 * N_FREQ - 1)) < N_FREQ
    outs = []
    for h in range(n_heads):
        th = t[:, h * HEAD_DIM:(h + 1) * HEAD_DIM]
        partner = jnp.where(first,
                            pltpu.roll(th, HEAD_DIM - N_FREQ, axis=1),
                            pltpu.roll(th, N_FREQ, axis=1))
        outs.append(th * cos + partner * sin)
    return jnp.concatenate(outs, axis=-1)


def _stream_specs(stream, tm, d, n_lat_tiles):
    _, ctx, ctx_off = stream
    last_ctx = ctx.shape[0] // tm - 1
    lat_spec = pl.BlockSpec((tm, d), lambda i: (jnp.minimum(i, n_lat_tiles - 1), 0))
    ctx_spec = pl.BlockSpec(
        (tm, d), lambda i: (jnp.minimum(jnp.maximum(i - n_lat_tiles, 0) + ctx_off, last_ctx), 0))
    return lat_spec, ctx_spec


def _mod_spec(geo, layer, tiles_per_step=1):
    return pl.BlockSpec(
        (None, None, MOD_CHUNKS, D_MODEL),
        lambda i: (layer, geo["mod_row"](i * tiles_per_step), 0, 0))


def _stream_tile(lat_ref, ctx_ref, n_lat_tiles):
    return jnp.where(pl.program_id(0) < n_lat_tiles, lat_ref[...], ctx_ref[...])


def _in_kernel(xl_ref, xc_ref, mod_ref, g_ref, w_ref, cq_ref, sq_ref, ck_ref, sk_ref, *rest,
               n_lat_tiles, n_casts):
    cast_in, rest = rest[:2 * n_casts], rest[2 * n_casts:]
    (xr_ref, gr_ref, q_ref, k_ref, vt_ref, gl_ref), cast_out = rest[:6], rest[6:]
    _run_side_casts(cast_in, cast_out)
    d = D_MODEL
    x = _stream_tile(xl_ref, xc_ref, n_lat_tiles)
    h = _rms(x) * (g_ref[...] * (1.0 + mod_ref[1:2, :])) + mod_ref[0:1, :]
    h = h.astype(BF16)
    xr = _dot(h, w_ref[:, 0:d])
    for n in range(N_RNN_BLOCKS):
        xr_ref[pl.ds(n, x.shape[0], stride=N_RNN_BLOCKS), :] = (
            xr[:, n * RNN_BLOCK_W:(n + 1) * RNN_BLOCK_W])
    gr_ref[...] = _dot(h, w_ref[:, d:2 * d]).astype(BF16)
    q = _dot(h, w_ref[:, 2 * d:3 * d])
    q_ref[...] = _rope(q, cq_ref, sq_ref, N_Q_HEADS).astype(BF16)
    k = _dot(h, w_ref[:, 3 * d:3 * d + KV_WIDTH])
    k_ref[...] = _rope(k, ck_ref, sk_ref, N_KV_HEADS).astype(BF16)
    v = _dot(h, w_ref[:, 3 * d + KV_WIDTH:3 * d + 2 * KV_WIDTH])
    for t in range(vt_ref.shape[0]):
        vt_ref[t] = v[t * KEY_BLOCK:(t + 1) * KEY_BLOCK, :].T.astype(BF16)
    gl_ref[...] = _dot(h, w_ref[:, 3 * d + 2 * KV_WIDTH:5 * d + 2 * KV_WIDTH]).astype(BF16)


def _in_proj(stream, mod, g_pre, w_in, layer, tables, geo, side_casts):
    d = D_MODEL
    tm = geo["tm"]
    n_tiles = geo["n_lat_tiles"] + geo["n_ctx_tiles"]
    nt = n_tiles * tm
    row = lambda i: (i, 0)
    tab = pl.BlockSpec((tm, LANES), lambda i: (geo["table_block"](i), 0))
    cast_in_specs, cast_out_specs, cast_shapes, cast_args = _side_cast_plan(
        side_casts, layer, n_tiles)
    outs = pl.pallas_call(
        functools.partial(_in_kernel, n_lat_tiles=geo["n_lat_tiles"], n_casts=len(side_casts)),
        grid=(n_tiles,),
        in_specs=[
            *_stream_specs(stream, tm, d, geo["n_lat_tiles"]),
            _mod_spec(geo, layer),
            _resident_layer(g_pre, layer),
            _resident(w_in.shape),
            tab, tab, tab, tab,
            *cast_in_specs,
        ],
        out_specs=[
            pl.BlockSpec((tm * N_RNN_BLOCKS, RNN_BLOCK_W), row),
            pl.BlockSpec((tm, d), row),
            pl.BlockSpec((tm, d), row),
            pl.BlockSpec((tm, KV_WIDTH), row),
            pl.BlockSpec((tm // KEY_BLOCK, KV_WIDTH, KEY_BLOCK), lambda i: (i, 0, 0)),
            pl.BlockSpec((tm, 2 * d), row),
            *cast_out_specs,
        ],
        out_shape=[
            jax.ShapeDtypeStruct((nt * N_RNN_BLOCKS, RNN_BLOCK_W), F32),
            jax.ShapeDtypeStruct((nt, d), BF16),
            jax.ShapeDtypeStruct((nt, d), BF16),
            jax.ShapeDtypeStruct((nt, KV_WIDTH), BF16),
            jax.ShapeDtypeStruct((nt // KEY_BLOCK, KV_WIDTH, KEY_BLOCK), BF16),
            jax.ShapeDtypeStruct((nt, 2 * d), BF16),
            *cast_shapes,
        ],
        compiler_params=pltpu.CompilerParams(
            dimension_semantics=("arbitrary",),
            vmem_limit_bytes=VMEM_LIMIT_BYTES),
        name="in_proj",
    )(stream[0], stream[1], mod, g_pre, w_in, *tables, *cast_args)
    return outs[:6], outs[6:]


def _rnn_kernel(xf_ref, xfp_ref, xfn_ref, xb_ref, xbp_ref, xbn_ref,
                cw_ref, cb_ref, wg_ref, bg_ref, lam_ref,
                hf_ref, hb_ref,
                xc_s, a_s, b_s, carry_s, *, n_steps):
    j = pl.program_id(1)
    nb, bw = N_RNN_BLOCKS, RNN_BLOCK_W
    n_in = CONV_T + CONV_W - 1
    right = CONV_W - 1 - CONV_LEFT

    @pl.when(j == 0)
    def _():
        carry_s[...] = jnp.zeros_like(carry_s)

    def conv(slot, x_ref, xp_ref, xn_ref, has_prev, has_next):
        def conv_piece(v, base):
            v = v.reshape(n_in, nb, bw)
            acc = cb_ref[...] + v[0:CONV_T] * cw_ref[0]
            for k in range(1, CONV_W):
                acc = acc + v[k:k + CONV_T] * cw_ref[k]
            xc_s[slot, pl.ds(base, CONV_T * nb), :] = acc.reshape(CONV_T * nb, bw)

        halo_rows = xp_ref.shape[0]
        left = jnp.where(has_prev, xp_ref[halo_rows - CONV_LEFT * nb:halo_rows, :], 0.0)
        conv_piece(jnp.concatenate([left, x_ref[0:(n_in - CONV_LEFT) * nb, :]], axis=0), 0)
        tail = jnp.where(has_next, xn_ref[0:right * nb, :], 0.0)
        last = CHUNK - CONV_T
        conv_piece(jnp.concatenate(
            [x_ref[(last - CONV_LEFT) * nb:CHUNK * nb, :], tail], axis=0), last * nb)

        def conv_body(c, carry):
            base = pl.multiple_of(c * (CONV_T * nb), CONV_T * nb)
            conv_piece(x_ref[pl.ds(base - CONV_LEFT * nb, n_in * nb), :], base)
            return carry

        lax.fori_loop(1, CHUNK // CONV_T - 1, conv_body, 0)

    fwd_chunk = j
    bwd_chunk = jnp.where(j == 0, 0, n_steps - j)
    mid = jnp.logical_and(j >= 1, j < n_steps - 1)

    @pl.when(2 * j <= n_steps)
    def _():
        conv(fwd_chunk, xf_ref, xfp_ref, xfn_ref, j >= 2, mid)

    @pl.when(jnp.logical_and(j >= 1, 2 * j < n_steps))
    def _():
        conv(bwd_chunk, xb_ref, xbp_ref, xbn_ref, mid, j >= 2)

    def gate_unit(parity, direction, n):
        rows = pl.ds(n, CHUNK, stride=nb)
        lam = lam_ref[direction, n]
        c_half = (0.5 * LRU_C * LOG2_E) * (
            jnp.minimum(lam, 0.0) - jnp.log1p(jnp.exp(-jnp.abs(lam))))
        xh = xc_s[bwd_chunk if direction else fwd_chunk, rows, :]
        z = _dot(xh.astype(BF16), wg_ref[direction, n]) + bg_ref[direction, n]
        a = jnp.exp2(c_half * jnp.tanh(z[:, 0:bw]) + c_half)
        gated = xh * jnp.tanh(z[:, bw:2 * bw]) + xh
        om = 1.0 - a * a
        a_s[parity, direction, rows, :] = a
        b_s[parity, direction, rows, :] = (om * lax.rsqrt(jnp.maximum(om, F32_TINY))) * gated

    def scan_steps(parity, t0, t1, carry):
        hf, hb = carry
        for t in range(t0, t1):
            tf, tb = t * nb, (CHUNK - 1 - t) * nb
            hf = a_s[parity, 0, tf:tf + nb, :] * hf + b_s[parity, 0, tf:tf + nb, :]
            hb = a_s[parity, 1, tb:tb + nb, :] * hb + b_s[parity, 1, tb:tb + nb, :]
            hf_ref[tf:tf + nb, :] = hf
            hb_ref[tb:tb + nb, :] = hb
        return hf, hb

    def work(parity, do_gates, do_scan):
        units = [(direction, n) for direction in range(2) for n in range(nb)]
        per_unit = CHUNK // len(units)
        carry = (carry_s[0], carry_s[1]) if do_scan else None
        for idx, (direction, n) in enumerate(units):
            if do_gates:
                gate_unit(parity, direction, n)
            if do_scan:
                carry = scan_steps(1 - parity, idx * per_unit, (idx + 1) * per_unit, carry)
        if do_scan:
            carry_s[0], carry_s[1] = carry

    last_parity = (n_steps - 1) % 2
    pl.when(j == 0)(lambda: work(0, True, False))
    pl.when(j == n_steps)(lambda: work(1 - last_parity, False, True))
    inner = jnp.logical_and(j >= 1, j < n_steps)
    pl.when(jnp.logical_and(inner, j % 2 == 0))(lambda: work(0, True, True))
    pl.when(jnp.logical_and(inner, j % 2 == 1))(lambda: work(1, True, True))


def _rnn_branch(xr, conv_w, conv_b, w_gate, b_gate, lam, layer, geo):
    nb, bw = N_RNN_BLOCKS, RNN_BLOCK_W
    bsz, n_lat = geo["batch"], geo["lat_chunks"]
    n_steps = n_lat + 1
    halo = SUBLANES
    n_halo = xr.shape[0] // (halo * nb)
    per = CHUNK // halo

    def fwd_blk(b, s):
        return jnp.where(s == 0, bsz * n_lat + b, b * n_lat + s - 1)

    def bwd_blk(b, s):
        return jnp.where(s == 0, bsz * n_lat + b, b * n_lat + n_lat - s)

    in_step = lambda j: jnp.minimum(j, n_steps - 1)
    out_step = lambda j: jnp.maximum(j - 1, 0)

    def chunk(blk, step):
        return pl.BlockSpec((CHUNK * nb, bw), lambda b, j: (blk(b, step(j)), 0))

    def prev(blk):
        return pl.BlockSpec(
            (halo * nb, bw), lambda b, j: (jnp.maximum(blk(b, in_step(j)) * per - 1, 0), 0))

    def nxt(blk):
        return pl.BlockSpec(
            (halo * nb, bw),
            lambda b, j: (jnp.minimum((blk(b, in_step(j)) + 1) * per, n_halo - 1), 0))

    return pl.pallas_call(
        functools.partial(_rnn_kernel, n_steps=n_steps),
        grid=(bsz, n_steps + 1),
        in_specs=[
            chunk(fwd_blk, in_step), prev(fwd_blk), nxt(fwd_blk),
            chunk(bwd_blk, in_step), prev(bwd_blk), nxt(bwd_blk),
            _resident_layer(conv_w, layer),
            _resident_layer(conv_b, layer),
            _resident_layer(w_gate, layer),
            _resident_layer(b_gate, layer),
            _resident_layer(lam, layer),
        ],
        out_specs=[chunk(fwd_blk, out_step), chunk(bwd_blk, out_step)],
        out_shape=[jax.ShapeDtypeStruct(xr.shape, F32)] * 2,
        scratch_shapes=[
            pltpu.VMEM((n_steps, CHUNK * nb, bw), F32),
            pltpu.VMEM((2, 2, CHUNK * nb, bw), F32),
            pltpu.VMEM((2, 2, CHUNK * nb, bw), F32),
            pltpu.VMEM((2, nb, bw), F32),
        ],
        compiler_params=pltpu.CompilerParams(
            dimension_semantics=("arbitrary", "arbitrary"),
            vmem_limit_bytes=VMEM_LIMIT_BYTES),
        name="rglru",
    )(xr, xr, xr, xr, xr, xr, conv_w, conv_b, w_gate, b_gate, lam)


def _stack_heads(q, g):
    base = g * Q_PER_KV * HEAD_DIM
    return jnp.concatenate(
        [q[:, base + h * HEAD_DIM: base + (h + 1) * HEAD_DIM] for h in range(Q_PER_KV)], axis=0)


def _nt_dot(a, b):
    return lax.dot_general(a, b, (((1,), (1,)), ((), ())), preferred_element_type=F32)


def _attend_units(sink_ref, layer, units):
    folds = KEY_BLOCK // SUBLANES
    for u in units:
        n_q = u["q"].shape[0]
        width = Q_PER_KV * n_q
        qs = _stack_heads(u["q"], u["g"])
        sink = jnp.concatenate(
            [jnp.full((1, n_q), sink_ref[layer, u["g"] * Q_PER_KV + h] * LOG2_E, F32)
             for h in range(Q_PER_KV)], axis=1)
        m8 = jnp.broadcast_to(sink, (SUBLANES, width))
        for j, (k, bias) in enumerate(u["key_blocks"]):
            s = _nt_dot(k, qs)
            if bias is not None:
                s = s + jnp.concatenate([bias] * Q_PER_KV, axis=1)
            u["s_s"][j * KEY_BLOCK:(j + 1) * KEY_BLOCK, :] = s
            m8 = jnp.maximum(m8, jnp.max(s.reshape(folds, SUBLANES, width), axis=0))
        u["sink"] = sink
        u["m"] = jnp.max(m8, axis=0, keepdims=True)
    for u in units:
        width = Q_PER_KV * u["q"].shape[0]
        l8 = jnp.zeros((SUBLANES, width), F32)
        for j in range(len(u["key_blocks"])):
            p = jnp.exp2(u["s_s"][j * KEY_BLOCK:(j + 1) * KEY_BLOCK, :] - u["m"])
            l8 = l8 + jnp.sum(p.reshape(folds, SUBLANES, width), axis=0)
            u["p_s"][j * KEY_BLOCK:(j + 1) * KEY_BLOCK, :] = p.astype(BF16)
        u["denom"] = jnp.sum(l8, axis=0, keepdims=True) + jnp.exp2(u["sink"] - u["m"])
    for u in units:
        n_q = u["q"].shape[0]
        n_keys = len(u["key_blocks"]) * KEY_BLOCK
        ot = _dot(u["vt"], u["p_s"][0:n_keys, :]) * (1.0 / u["denom"])
        for h in range(Q_PER_KV):
            u["o_store"](h, ot[:, h * n_q:(h + 1) * n_q].T)


def _lat_attn_kernel(sink_ref, q_ref, k_ref, vt_ref, kc_ref, vtc_ref, o_ref, s_s, p_s,
                     *, seq_len, layer):
    n_blk = q_ref.shape[0] // Q_BLOCK
    band_blocks = BAND // KEY_BLOCK
    ctx_blocks = vtc_ref.shape[0]
    last_start = seq_len // KEY_BLOCK - band_blocks
    rel = (lax.broadcasted_iota(jnp.int32, (KEY_BLOCK, Q_BLOCK), 0)
           - lax.broadcasted_iota(jnp.int32, (KEY_BLOCK, Q_BLOCK), 1))

    def block_units(i, slot):
        units = []
        qb = pl.program_id(1) * n_blk + i
        jb0 = jnp.clip(qb - WINDOW // KEY_BLOCK, 0, last_start)
        biases = [jnp.where(jnp.abs(rel + (jb0 + t - qb) * KEY_BLOCK) <= WINDOW, 0.0, NEG_INF)
                  for t in range(band_blocks)]
        rows = pl.ds(pl.multiple_of(i * Q_BLOCK, Q_BLOCK), Q_BLOCK)
        q = q_ref[rows, :]
        for g in range(N_KV_HEADS):
            gs = slice(g * HEAD_DIM, (g + 1) * HEAD_DIM)
            key_blocks = [
                (k_ref[pl.ds(pl.multiple_of((jb0 + t) * KEY_BLOCK, KEY_BLOCK), KEY_BLOCK), gs],
                 biases[t]) for t in range(band_blocks)]
            key_blocks += [(kc_ref[t * KEY_BLOCK:(t + 1) * KEY_BLOCK, gs], None)
                           for t in range(ctx_blocks)]
            vt = jnp.concatenate([vt_ref[jb0 + t, gs, :] for t in range(band_blocks)]
                                 + [vtc_ref[t, gs, :] for t in range(ctx_blocks)], axis=1)

            def o_store(h, tile, g=g, rows=rows):
                c0 = (g * Q_PER_KV + h) * HEAD_DIM
                o_ref[rows, c0:c0 + HEAD_DIM] = tile.astype(BF16)

            units.append(dict(g=g, q=q, key_blocks=key_blocks, vt=vt,
                              s_s=s_s.at[slot, g], p_s=p_s.at[slot, g], o_store=o_store))
        return units

    def body(it, carry):
        units = []
        for slot in range(Q_SLOTS):
            units += block_units(it * Q_SLOTS + slot, slot)
        _attend_units(sink_ref, layer, units)
        return carry

    lax.fori_loop(0, n_blk // Q_SLOTS, body, 0)


def _ctx_attn_kernel(sink_ref, q_ref, kc_ref, vtc_ref, o_ref, s_s, p_s, *, layer):
    ctx_blocks = vtc_ref.shape[0]
    q = q_ref[...]
    units = []
    for g in range(N_KV_HEADS):
        gs = slice(g * HEAD_DIM, (g + 1) * HEAD_DIM)
        key_blocks = [(kc_ref[t * KEY_BLOCK:(t + 1) * KEY_BLOCK, gs], None)
                      for t in range(ctx_blocks)]
        vt = jnp.concatenate([vtc_ref[t, gs, :] for t in range(ctx_blocks)], axis=1)

        def o_store(h, tile, g=g):
            c0 = (g * Q_PER_KV + h) * HEAD_DIM
            o_ref[:, c0:c0 + HEAD_DIM] = tile.astype(BF16)

        units.append(dict(g=g, q=q, key_blocks=key_blocks, vt=vt,
                          s_s=s_s.at[g], p_s=p_s.at[g], o_store=o_store))
    _attend_units(sink_ref, layer, units)


def _attention(q, k, vt, sink, layer, geo, with_ctx_queries):
    nt, d = q.shape
    bsz, seq_len, n_ctx = geo["batch"], geo["seq"], geo["ctx"]
    q_sup = next(s for s in (2048, 1024, 512) if seq_len % s == 0)
    n_sup = seq_len // q_sup
    ctx_blk0 = bsz * seq_len // n_ctx
    smem = pl.BlockSpec(memory_space=pltpu.SMEM)
    ctx_keys = pl.BlockSpec((n_ctx, KV_WIDTH), lambda b, *_: (ctx_blk0 + b, 0))
    ctx_vals = pl.BlockSpec((n_ctx // KEY_BLOCK, KV_WIDTH, KEY_BLOCK),
                            lambda b, *_: (ctx_blk0 + b, 0, 0))
    o = pl.pallas_call(
        functools.partial(_lat_attn_kernel, seq_len=seq_len, layer=layer),
        grid=(bsz, n_sup),
        in_specs=[
            smem,
            pl.BlockSpec((q_sup, d), lambda b, i: (b * n_sup + i, 0)),
            pl.BlockSpec((seq_len, KV_WIDTH), lambda b, i: (b, 0)),
            pl.BlockSpec((seq_len // KEY_BLOCK, KV_WIDTH, KEY_BLOCK), lambda b, i: (b, 0, 0)),
            ctx_keys, ctx_vals,
        ],
        out_specs=pl.BlockSpec((q_sup, d), lambda b, i: (b * n_sup + i, 0)),
        out_shape=jax.ShapeDtypeStruct((bsz * seq_len, d), BF16),
        scratch_shapes=[
            pltpu.VMEM((Q_SLOTS, N_KV_HEADS, BAND + n_ctx, Q_PER_KV * Q_BLOCK), F32),
            pltpu.VMEM((Q_SLOTS, N_KV_HEADS, BAND + n_ctx, Q_PER_KV * Q_BLOCK), BF16),
        ],
        compiler_params=pltpu.CompilerParams(
            dimension_semantics=("arbitrary", "arbitrary"),
            vmem_limit_bytes=VMEM_LIMIT_BYTES),
        name="lat_attention",
    )(sink, q, k, vt, k, vt)
    if not with_ctx_queries:
        return (o, o, 0)
    o_ctx = pl.pallas_call(
        functools.partial(_ctx_attn_kernel, layer=layer),
        grid=(bsz,),
        in_specs=[
            smem,
            pl.BlockSpec((n_ctx, d), lambda b: (ctx_blk0 + b, 0)),
            ctx_keys, ctx_vals,
        ],
        out_specs=pl.BlockSpec((n_ctx, d), lambda b: (b, 0)),
        out_shape=jax.ShapeDtypeStruct((bsz * n_ctx, d), BF16),
        scratch_shapes=[
            pltpu.VMEM((N_KV_HEADS, n_ctx, Q_PER_KV * n_ctx), F32),
            pltpu.VMEM((N_KV_HEADS, n_ctx, Q_PER_KV * n_ctx), BF16),
        ],
        compiler_params=pltpu.CompilerParams(
            dimension_semantics=("arbitrary",),
            vmem_limit_bytes=VMEM_LIMIT_BYTES),
        name="ctx_attention",
    )(sink, q, k, vt)
    return (o, o_ctx, 0)


def _merge_kernel(xl_ref, xc_ref, hf_ref, hb_ref, gr_ref, ol_ref, oc_ref, gl_ref, mod_ref,
                  g_ref, wr_ref, wa_ref, wo_ref, out_ref, *, n_lat_tiles, has_ctx_tiles):
    d = D_MODEL
    nb = N_RNN_BLOCKS
    tm = xl_ref.shape[0]
    is_lat = pl.program_id(0) < n_lat_tiles

    def stream_tile(lat_ref, ctx_ref):
        return jnp.where(is_lat, lat_ref[...], ctx_ref[...]) if has_ctx_tiles else lat_ref[...]

    h = jnp.concatenate(
        [hf_ref[pl.ds(n, tm, stride=nb), :] + hb_ref[pl.ds(n, tm, stride=nb), :]
         for n in range(nb)], axis=-1)
    g = gr_ref[...].astype(F32)
    hg = h * g
    y = (hg * jnp.tanh(g * (GELU_C1 + GELU_C2 * (g * g))) + hg).astype(BF16)
    ya = _dot(y, wr_ref[...])
    yb = _dot(stream_tile(ol_ref, oc_ref), wa_ref[...])
    ta = jnp.tanh(gl_ref[:, 0:d].astype(F32))
    tb = jnp.tanh(gl_ref[:, d:2 * d].astype(F32))
    mix = ((ya * ta + ya) + (yb * tb + yb)).astype(BF16)
    m = _dot(mix, wo_ref[...])
    out_ref[...] = stream_tile(xl_ref, xc_ref) + _rms(m) * (mod_ref[2:3, :] * g_ref[...])


def _merge(x_stream, hf, hb, gr, o_stream, gl, mod, g_post, w_o_rnn, w_o_attn, w_out,
           layer, geo, n_tiles):
    d = D_MODEL
    tm = geo["tm"]
    n_lat_tiles = geo["n_lat_tiles"]
    row = lambda i: (i, 0)
    tile = pl.BlockSpec((tm, d), row)
    slab = pl.BlockSpec((tm * N_RNN_BLOCKS, RNN_BLOCK_W), row)
    return pl.pallas_call(
        functools.partial(_merge_kernel, n_lat_tiles=n_lat_tiles,
                          has_ctx_tiles=n_tiles > n_lat_tiles),
        grid=(n_tiles,),
        in_specs=[
            *_stream_specs(x_stream, tm, d, n_lat_tiles), slab, slab, tile,
            *_stream_specs(o_stream, tm, d, n_lat_tiles),
            pl.BlockSpec((tm, 2 * d), row),
            _mod_spec(geo, layer),
            _resident_layer(g_post, layer),
            _resident(w_o_rnn.shape), _resident(w_o_attn.shape), _resident(w_out.shape),
        ],
        out_specs=tile,
        out_shape=jax.ShapeDtypeStruct((n_tiles * tm, d), F32),
        compiler_params=pltpu.CompilerParams(
            dimension_semantics=("arbitrary",),
            vmem_limit_bytes=VMEM_LIMIT_BYTES),
        name="merge",
    )(x_stream[0], x_stream[1], hf, hb, gr, o_stream[0], o_stream[1], gl, mod,
      g_post, w_o_rnn, w_o_attn, w_out)


def _ffn_kernel(x_ref, mod_ref, gpre_ref, gpost_ref, w1_ref, w2_ref, *rest, n_casts):
    cast_in, out_ref, cast_out = rest[:2 * n_casts], rest[2 * n_casts], rest[2 * n_casts + 1:]
    _run_side_casts(cast_in, cast_out)
    sub = x_ref.shape[0] // FFN_SUBTILES
    bounds = list(range(0, D_FF, FFN_CHUNK)) + [D_FF]
    n_chunks = len(bounds) - 1

    def normed(s):
        x = x_ref[s * sub:(s + 1) * sub, :]
        h = _rms(x) * (gpre_ref[...] * (1.0 + mod_ref[4:5, :])) + mod_ref[3:4, :]
        return h.astype(BF16)

    def first_layer(h, c):
        lo, hi = bounds[c], bounds[c + 1]
        return _dot(h, w1_ref[:, lo:hi]), _dot(h, w1_ref[:, D_FF + lo:D_FF + hi])

    h_next = normed(0)
    for s in range(FFN_SUBTILES):
        h = h_next
        if s + 1 < FFN_SUBTILES:
            h_next = normed(s + 1)
        f = None
        nxt = first_layer(h, 0)
        for c in range(n_chunks):
            half_gate, up = nxt
            if c + 1 < n_chunks:
                nxt = first_layer(h, c + 1)
            act = ((half_gate * jnp.tanh(half_gate) + half_gate) * up).astype(BF16)
            part = _dot(act, w2_ref[bounds[c]:bounds[c + 1], :])
            f = part if f is None else f + part
        rows = slice(s * sub, (s + 1) * sub)
        out_ref[rows, :] = x_ref[rows, :] + _rms(f) * (mod_ref[5:6, :] * gpost_ref[...])


def _ffn(x_all, mod, g_pre, g_post, w1, w2, layer, geo, n_tiles, side_casts, cast_layer):
    d = x_all.shape[1]
    tm = geo["tm"] * FFN_SUBTILES
    n_blocks = n_tiles // FFN_SUBTILES
    tile = pl.BlockSpec((tm, d), lambda i: (i, 0))
    cast_in_specs, cast_out_specs, cast_shapes, cast_args = _side_cast_plan(
        side_casts, cast_layer, n_blocks)
    outs = pl.pallas_call(
        functools.partial(_ffn_kernel, n_casts=len(side_casts)),
        grid=(n_blocks,),
        in_specs=[
            tile,
            _mod_spec(geo, layer, FFN_SUBTILES),
            _resident_layer(g_pre, layer), _resident_layer(g_post, layer),
            _resident(w1.shape), _resident(w2.shape),
            *cast_in_specs,
        ],
        out_specs=[tile, *cast_out_specs],
        out_shape=[jax.ShapeDtypeStruct((n_blocks * tm, d), F32), *cast_shapes],
        compiler_params=pltpu.CompilerParams(
            dimension_semantics=("arbitrary",),
            vmem_limit_bytes=VMEM_LIMIT_BYTES),
        name="ffn",
    )(x_all, mod, g_pre, g_post, w1, w2, *cast_args)
    return outs[0], outs[1:]


def _rope_tables(seq_len, pad_rows):
    f32 = np.float32
    n_rows = seq_len // GRID_W
    inv = f32(ROPE_BASE) ** (-np.arange(N_FREQ, dtype=f32) / f32(N_FREQ))
    ang = np.arange(max(n_rows, GRID_W), dtype=f32)[:, None] * inv[None, :]
    cos_u, sin_u = np.cos(ang), np.sin(ang)
    by_row = lambda t: np.repeat(t[:n_rows], GRID_W, axis=0)
    by_col = lambda t: np.tile(t[:GRID_W], (n_rows, 1))
    cos = np.concatenate([by_row(cos_u)] * 2 + [by_col(cos_u)] * 2, axis=-1)
    sin = np.concatenate(
        [-by_row(sin_u), by_row(sin_u), -by_col(sin_u), by_col(sin_u)], axis=-1)
    cos = np.concatenate([cos, np.ones((pad_rows, HEAD_DIM), f32)], axis=0)
    sin = np.concatenate([sin, np.zeros((pad_rows, HEAD_DIM), f32)], axis=0)
    scale = f32(HEAD_DIM ** -0.5 * LOG2_E)
    return tuple(jnp.asarray(t.astype(f32)) for t in (cos * scale, sin * scale, cos, sin))


def kernel(x, c, ctx, c_ctx, w_mod, b_mod, g_mix_pre, g_mix_post, g_ffn_pre, g_ffn_post, w_in, conv_w, conv_b, lru_wa, lru_ba, lru_wx, lru_bx, lru_lam, attn_sink, w_o_rnn, w_o_attn, w_out, w_ffn_in, w_ffn_out):
    bsz, seq_len, d = x.shape
    n_ctx = ctx.shape[1]
    depth = w_mod.shape[0]
    assert d == D_MODEL and n_ctx == CHUNK and seq_len % CHUNK == 0 and seq_len >= BAND
    assert bsz + 1 <= SUBLANES and seq_len % GRID_W == 0

    n_lat_rows, n_ctx_rows = bsz * seq_len, bsz * n_ctx
    big = 512 * FFN_SUBTILES
    tm = 512 if (n_ctx_rows % big == 0 and seq_len % big == 0) else CHUNK
    assert n_ctx_rows % (tm * FFN_SUBTILES) == 0 and seq_len % (tm * FFN_SUBTILES) == 0
    n_lat_tiles, n_ctx_tiles = n_lat_rows // tm, n_ctx_rows // tm
    tiles_per_batch = seq_len // tm
    geo = {
        "batch": bsz, "seq": seq_len, "ctx": n_ctx, "tm": tm,
        "lat_chunks": seq_len // CHUNK,
        "n_lat_tiles": n_lat_tiles, "n_ctx_tiles": n_ctx_tiles,
        "mod_row": lambda i: jnp.where(i < n_lat_tiles, i // tiles_per_batch, bsz),
        "table_block": lambda i: jnp.where(
            i < n_lat_tiles, i % tiles_per_batch, tiles_per_batch + i - n_lat_tiles),
    }

    cs = jnp.concatenate(
        [c, c_ctx[None, :], jnp.zeros((SUBLANES - bsz - 1, d), F32)], axis=0)
    mod = _modulation(cs, w_mod, b_mod).reshape(depth, SUBLANES, MOD_CHUNKS, d)

    tables = _rope_tables(seq_len, n_ctx_rows)

    w_gate = jnp.concatenate([lru_wa, lru_wx], axis=-1).astype(BF16)
    b_gate = 0.5 * jnp.concatenate(
        [lru_ba.reshape(depth, 2, N_RNN_BLOCKS, 1, RNN_BLOCK_W),
         lru_bx.reshape(depth, 2, N_RNN_BLOCKS, 1, RNN_BLOCK_W)], axis=-1)
    lam = lru_lam.reshape(depth, 2, N_RNN_BLOCKS, 1, RNN_BLOCK_W)
    conv_w_half = (0.5 * conv_w).reshape(depth, CONV_W, N_RNN_BLOCKS, RNN_BLOCK_W)
    conv_b_half = (0.5 * conv_b).reshape(depth, N_RNN_BLOCKS, RNN_BLOCK_W)
    g_mix_pre, g_mix_post, g_ffn_pre, g_ffn_post = (
        g.reshape(depth, 1, d) for g in (g_mix_pre, g_mix_post, g_ffn_pre, g_ffn_post))

    stream = (x.reshape(n_lat_rows, d), ctx.reshape(n_ctx_rows, d), 0)
    ones = lambda n: jnp.ones((1, n), F32)
    halves = lambda n: jnp.full((1, n), 0.5, F32)
    gl_cols = 2 * d
    w_in_scale = jnp.concatenate([ones(w_in.shape[-1] - gl_cols), halves(gl_cols)], axis=1)
    w_in_b = _to_bf16(w_in, 0, 256, w_in_scale)
    in_proj_casts = [
        (w_o_rnn, halves(d)), (w_o_attn, ones(d)), (w_out, halves(d)),
        (w_ffn_in, jnp.concatenate([halves(D_FF), ones(D_FF)], axis=1)),
        (w_ffn_out, ones(d)),
    ]
    for l in range(depth):
        need_ctx = l < depth - 1
        n_out_tiles = n_lat_tiles + n_ctx_tiles if need_ctx else n_lat_tiles
        (xr, gr, q, k, vt, gl), layer_weights = _in_proj(
            stream, mod, g_mix_pre, w_in_b, l, tables, geo, in_proj_casts)
        w_o_rnn_b, w_o_attn_b, w_out_b, w_ffn_in_b, w_ffn_out_b = layer_weights
        hf, hb = _rnn_branch(xr, conv_w_half, conv_b_half, w_gate, b_gate, lam, l, geo)
        o_stream = _attention(q, k, vt, attn_sink, l, geo, need_ctx)
        x_all = _merge(
            stream, hf, hb, gr, o_stream, gl, mod, g_mix_post, w_o_rnn_b, w_o_attn_b, w_out_b,
            l, geo, n_out_tiles)
        next_casts = [(w_in, w_in_scale)] if l + 1 < depth else []
        x_all, next_weights = _ffn(
            x_all, mod, g_ffn_pre, g_ffn_post, w_ffn_in_b, w_ffn_out_b,
            l, geo, n_out_tiles, next_casts, l + 1)
        if next_weights:
            (w_in_b,) = next_weights
        stream = (x_all, x_all, n_lat_tiles)
    return x_all[:n_lat_rows].reshape(bsz, seq_len, d)
```

```python
import functools
import math

import jax
import jax.numpy as jnp
import numpy as np
from jax import lax
from jax.experimental import pallas as pl
from jax.experimental.pallas import tpu as pltpu

D_MODEL = 1024
HEAD_DIM = 128
N_Q_HEADS = 8
N_KV_HEADS = 2
Q_PER_KV = N_Q_HEADS // N_KV_HEADS
KV_WIDTH = N_KV_HEADS * HEAD_DIM
WINDOW = 128
GRID_W = 64
N_FREQ = HEAD_DIM // 4
ROPE_BASE = 10000.0
N_RNN_BLOCKS = 8
RNN_BLOCK_W = D_MODEL // N_RNN_BLOCKS
LRU_C = 8.0
CONV_W = 4
CONV_LEFT = 2
D_FF = ((8 * D_MODEL + 3 * 256 - 1) // (3 * 256)) * 256
EPS = 1e-6
NEG_INF = -1e30
MOD_CHUNKS = 6

LANES = 128
SUBLANES = 8
BF16_SUBLANES = 16
V7X_VMEM_BYTES = 64 * 1024 * 1024
VMEM_LIMIT_BYTES = V7X_VMEM_BYTES * 7 // 8

CHUNK = 256
CONV_T = 32
FFN_CHUNK = 768
FFN_SUBTILES = 2
Q_BLOCK = 128
Q_SLOTS = 4
KEY_BLOCK = 128
BAND = Q_BLOCK + 2 * WINDOW
LOG2_E = math.log2(math.e)

BF16 = jnp.bfloat16
F32 = jnp.float32
F32_TINY = float(jnp.finfo(jnp.float32).tiny)


def _dot(a, b):
    return jnp.dot(a, b, preferred_element_type=F32)


def _sigmoid(x):
    return 0.5 * jnp.tanh(0.5 * x) + 0.5


GELU_C1 = math.sqrt(2.0 / math.pi)
GELU_C2 = 0.044715 * GELU_C1


def _rms(x):
    return x * lax.rsqrt(jnp.mean(x * x, axis=-1, keepdims=True) + EPS)


def _resident(shape):
    nd = len(shape)
    return pl.BlockSpec(shape, lambda *_: (0,) * nd, pipeline_mode=pl.Buffered(1))


def _resident_layer(stacked, layer):
    tail = stacked.shape[1:]
    return pl.BlockSpec((None,) + tail, lambda *_: (layer,) + (0,) * len(tail),
                        pipeline_mode=pl.Buffered(1))


def _cast_kernel(w_ref, scale_ref, o_ref):
    o_ref[...] = (w_ref[...] * scale_ref[...]).astype(BF16)


def _to_bf16(w, layer, row_block, col_scale):
    _, rows, cols = w.shape
    return pl.pallas_call(
        _cast_kernel,
        grid=(rows // row_block,),
        in_specs=[pl.BlockSpec((None, row_block, cols), lambda r: (layer, r, 0)),
                  pl.BlockSpec((1, cols), lambda r: (0, 0))],
        out_specs=pl.BlockSpec((row_block, cols), lambda r: (r, 0)),
        out_shape=jax.ShapeDtypeStruct((rows, cols), BF16),
        compiler_params=pltpu.CompilerParams(
            dimension_semantics=("arbitrary",),
            vmem_limit_bytes=VMEM_LIMIT_BYTES),
        name="cast_bf16",
    )(w, col_scale)


def _side_cast_plan(side_casts, layer, n_steps):
    in_specs, out_specs, shapes, args = [], [], [], []
    for w, scale in side_casts:
        _, rows, cols = w.shape
        rb = next(r for r in range(BF16_SUBLANES, rows + 1, BF16_SUBLANES)
                  if rows % r == 0 and rows // r <= n_steps)
        n_blk = rows // rb
        blk = lambda i, n_blk=n_blk: jnp.minimum(i, n_blk - 1)
        in_specs += [pl.BlockSpec((None, rb, cols), lambda i, blk=blk: (layer, blk(i), 0)),
                     pl.BlockSpec((1, cols), lambda i: (0, 0))]
        out_specs.append(pl.BlockSpec((rb, cols), lambda i, blk=blk: (blk(i), 0)))
        shapes.append(jax.ShapeDtypeStruct((rows, cols), BF16))
        args += [w, scale]
    return in_specs, out_specs, shapes, args


def _run_side_casts(cast_in, cast_out):
    for c, dst in enumerate(cast_out):
        dst[...] = (cast_in[2 * c][...] * cast_in[2 * c + 1][...]).astype(BF16)


def _mod_kernel(c_ref, w_ref, b_ref, o_ref):
    c = c_ref[...]
    s = (c * _sigmoid(c)).astype(BF16)
    o_ref[...] = _dot(s, w_ref[...].astype(BF16)) + b_ref[...]


def _modulation(cs, w_mod, b_mod):
    n_layers, d, width = w_mod.shape
    tn = 1536
    return pl.pallas_call(
        _mod_kernel,
        grid=(n_layers, width // tn),
        in_specs=[
            pl.BlockSpec((SUBLANES, d), lambda l, j: (0, 0)),
            pl.BlockSpec((None, d, tn), lambda l, j: (l, 0, j)),
            pl.BlockSpec((None, 1, tn), lambda l, j: (l, 0, j)),
        ],
        out_specs=pl.BlockSpec((None, SUBLANES, tn), lambda l, j: (l, 0, j)),
        out_shape=jax.ShapeDtypeStruct((n_layers, SUBLANES, width), F32),
        compiler_params=pltpu.CompilerParams(
            dimension_semantics=("arbitrary", "arbitrary"),
            vmem_limit_bytes=VMEM_LIMIT_BYTES),
        name="modulation",
    )(cs, w_mod, b_mod.reshape(n_layers, 1, width))


def _rope(t, cos_ref, sin_ref, n_heads):
    cos = cos_ref[...]
    sin = sin_ref[...]
    lane = lax.broadcasted_iota(jnp.int32, cos.shape, 1)
    first = (lane & (2 * N_FREQ - 1)) < N_FREQ
    outs = []
    for h in range(n_heads):
        th = t[:, h * HEAD_DIM:(h + 1) * HEAD_DIM]
        partner = jnp.where(first,
                            pltpu.roll(th, HEAD_DIM - N_FREQ, axis=1),
                            pltpu.roll(th, N_FREQ, axis=1))
        outs.append(th * cos + partner * sin)
    return jnp.concatenate(outs, axis=-1)


def _stream_specs(stream, tm, d, n_lat_tiles):
    _, ctx, ctx_off = stream
    last_ctx = ctx.shape[0] // tm - 1
    lat_spec = pl.BlockSpec((tm, d), lambda i: (jnp.minimum(i, n_lat_tiles - 1), 0))
    ctx_spec = pl.BlockSpec(
        (tm, d), lambda i: (jnp.minimum(jnp.maximum(i - n_lat_tiles, 0) + ctx_off, last_ctx), 0))
    return lat_spec, ctx_spec


def _mod_spec(geo, layer, tiles_per_step=1):
    return pl.BlockSpec(
        (None, None, MOD_CHUNKS, D_MODEL),
        lambda i: (layer, geo["mod_row"](i * tiles_per_step), 0, 0))


def _stream_tile(lat_ref, ctx_ref, n_lat_tiles):
    return jnp.where(pl.program_id(0) < n_lat_tiles, lat_ref[...], ctx_ref[...])


def _in_kernel(xl_ref, xc_ref, mod_ref, g_ref, w_ref, cq_ref, sq_ref, ck_ref, sk_ref, *rest,
               n_lat_tiles, n_casts):
    cast_in, rest = rest[:2 * n_casts], rest[2 * n_casts:]
    (xr_ref, gr_ref, q_ref, k_ref, vt_ref, gl_ref), cast_out = rest[:6], rest[6:]
    _run_side_casts(cast_in, cast_out)
    d = D_MODEL
    x = _stream_tile(xl_ref, xc_ref, n_lat_tiles)
    h = _rms(x) * (g_ref[...] * (1.0 + mod_ref[1:2, :])) + mod_ref[0:1, :]
    h = h.astype(BF16)
    xr = _dot(h, w_ref[:, 0:d])
    for n in range(N_RNN_BLOCKS):
        xr_ref[pl.ds(n, x.shape[0], stride=N_RNN_BLOCKS), :] = (
            xr[:, n * RNN_BLOCK_W:(n + 1) * RNN_BLOCK_W])
    gr_ref[...] = _dot(h, w_ref[:, d:2 * d]).astype(BF16)
    q = _dot(h, w_ref[:, 2 * d:3 * d])
    q_ref[...] = _rope(q, cq_ref, sq_ref, N_Q_HEADS).astype(BF16)
    k = _dot(h, w_ref[:, 3 * d:3 * d + KV_WIDTH])
    k_ref[...] = _rope(k, ck_ref, sk_ref, N_KV_HEADS).astype(BF16)
    v = _dot(h, w_ref[:, 3 * d + KV_WIDTH:3 * d + 2 * KV_WIDTH])
    for t in range(vt_ref.shape[0]):
        vt_ref[t] = v[t * KEY_BLOCK:(t + 1) * KEY_BLOCK, :].T.astype(BF16)
    gl_ref[...] = _dot(h, w_ref[:, 3 * d + 2 * KV_WIDTH:5 * d + 2 * KV_WIDTH]).astype(BF16)


def _in_proj(stream, mod, g_pre, w_in, layer, tables, geo, side_casts):
    d = D_MODEL
    tm = geo["tm"]
    n_tiles = geo["n_lat_tiles"] + geo["n_ctx_tiles"]
    nt = n_tiles * tm
    row = lambda i: (i, 0)
    tab = pl.BlockSpec((tm, LANES), lambda i: (geo["table_block"](i), 0))
    cast_in_specs, cast_out_specs, cast_shapes, cast_args = _side_cast_plan(
        side_casts, layer, n_tiles)
    outs = pl.pallas_call(
        functools.partial(_in_kernel, n_lat_tiles=geo["n_lat_tiles"], n_casts=len(side_casts)),
        grid=(n_tiles,),
        in_specs=[
            *_stream_specs(stream, tm, d, geo["n_lat_tiles"]),
            _mod_spec(geo, layer),
            _resident_layer(g_pre, layer),
            _resident(w_in.shape),
            tab, tab, tab, tab,
            *cast_in_specs,
        ],
        out_specs=[
            pl.BlockSpec((tm * N_RNN_BLOCKS, RNN_BLOCK_W), row),
            pl.BlockSpec((tm, d), row),
            pl.BlockSpec((tm, d), row),
            pl.BlockSpec((tm, KV_WIDTH), row),
            pl.BlockSpec((tm // KEY_BLOCK, KV_WIDTH, KEY_BLOCK), lambda i: (i, 0, 0)),
            pl.BlockSpec((tm, 2 * d), row),
            *cast_out_specs,
        ],
        out_shape=[
            jax.ShapeDtypeStruct((nt * N_RNN_BLOCKS, RNN_BLOCK_W), F32),
            jax.ShapeDtypeStruct((nt, d), BF16),
            jax.ShapeDtypeStruct((nt, d), BF16),
            jax.ShapeDtypeStruct((nt, KV_WIDTH), BF16),
            jax.ShapeDtypeStruct((nt // KEY_BLOCK, KV_WIDTH, KEY_BLOCK), BF16),
            jax.ShapeDtypeStruct((nt, 2 * d), BF16),
            *cast_shapes,
        ],
        compiler_params=pltpu.CompilerParams(
            dimension_semantics=("arbitrary",),
            vmem_limit_bytes=VMEM_LIMIT_BYTES),
        name="in_proj",
    )(stream[0], stream[1], mod, g_pre, w_in, *tables, *cast_args)
    return outs[:6], outs[6:]


def _rnn_kernel(xf_ref, xfp_ref, xfn_ref, xb_ref, xbp_ref, xbn_ref,
                cw_ref, cb_ref, wg_ref, bg_ref, lam_ref,
                hf_ref, hb_ref,
                xc_s, a_s, b_s, carry_s, *, n_steps):
    j = pl.program_id(1)
    nb, bw = N_RNN_BLOCKS, RNN_BLOCK_W
    n_in = CONV_T + CONV_W - 1
    right = CONV_W - 1 - CONV_LEFT

    @pl.when(j == 0)
    def _():
        carry_s[...] = jnp.zeros_like(carry_s)

    def conv(slot, x_ref, xp_ref, xn_ref, has_prev, has_next):
        def conv_piece(v, base):
            v = v.reshape(n_in, nb, bw)
            acc = cb_ref[...] + v[0:CONV_T] * cw_ref[0]
            for k in range(1, CONV_W):
                acc = acc + v[k:k + CONV_T] * cw_ref[k]
            xc_s[slot, pl.ds(base, CONV_T * nb), :] = acc.reshape(CONV_T * nb, bw)

        halo_rows = xp_ref.shape[0]
        left = jnp.where(has_prev, xp_ref[halo_rows - CONV_LEFT * nb:halo_rows, :], 0.0)
        conv_piece(jnp.concatenate([left, x_ref[0:(n_in - CONV_LEFT) * nb, :]], axis=0), 0)
        tail = jnp.where(has_next, xn_ref[0:right * nb, :], 0.0)
        last = CHUNK - CONV_T
        conv_piece(jnp.concatenate(
            [x_ref[(last - CONV_LEFT) * nb:CHUNK * nb, :], tail], axis=0), last * nb)

        def conv_body(c, carry):
            base = pl.multiple_of(c * (CONV_T * nb), CONV_T * nb)
            conv_piece(x_ref[pl.ds(base - CONV_LEFT * nb, n_in * nb), :], base)
            return carry

        lax.fori_loop(1, CHUNK // CONV_T - 1, conv_body, 0)

    fwd_chunk = j
    bwd_chunk = jnp.where(j == 0, 0, n_steps - j)
    mid = jnp.logical_and(j >= 1, j < n_steps - 1)

    @pl.when(2 * j <= n_steps)
    def _():
        conv(fwd_chunk, xf_ref, xfp_ref, xfn_ref, j >= 2, mid)

    @pl.when(jnp.logical_and(j >= 1, 2 * j < n_steps))
    def _():
        conv(bwd_chunk, xb_ref, xbp_ref, xbn_ref, mid, j >= 2)

    def gate_unit(parity, direction, n):
        rows = pl.ds(n, CHUNK, stride=nb)
        lam = lam_ref[direction, n]
        c_half = (0.5 * LRU_C * LOG2_E) * (
            jnp.minimum(lam, 0.0) - jnp.log1p(jnp.exp(-jnp.abs(lam))))
        xh = xc_s[bwd_chunk if direction else fwd_chunk, rows, :]
        z = _dot(xh.astype(BF16), wg_ref[direction, n]) + bg_ref[direction, n]
        a = jnp.exp2(c_half * jnp.tanh(z[:, 0:bw]) + c_half)
        gated = xh * jnp.tanh(z[:, bw:2 * bw]) + xh
        om = 1.0 - a * a
        a_s[parity, direction, rows, :] = a
        b_s[parity, direction, rows, :] = (om * lax.rsqrt(jnp.maximum(om, F32_TINY))) * gated

    def scan_steps(parity, t0, t1, carry):
        hf, hb = carry
        for t in range(t0, t1):
            tf, tb = t * nb, (CHUNK - 1 - t) * nb
            hf = a_s[parity, 0, tf:tf + nb, :] * hf + b_s[parity, 0, tf:tf + nb, :]
            hb = a_s[parity, 1, tb:tb + nb, :] * hb + b_s[parity, 1, tb:tb + nb, :]
            hf_ref[tf:tf + nb, :] = hf
            hb_ref[tb:tb + nb, :] = hb
        return hf, hb

    def work(parity, do_gates, do_scan):
        units = [(direction, n) for direction in range(2) for n in range(nb)]
        per_unit = CHUNK // len(units)
        carry = (carry_s[0], carry_s[1]) if do_scan else None
        for idx, (direction, n) in enumerate(units):
            if do_gates:
                gate_unit(parity, direction, n)
            if do_scan:
                carry = scan_steps(1 - parity, idx * per_unit, (idx + 1) * per_unit, carry)
        if do_scan:
            carry_s[0], carry_s[1] = carry

    last_parity = (n_steps - 1) % 2
    pl.when(j == 0)(lambda: work(0, True, False))
    pl.when(j == n_steps)(lambda: work(1 - last_parity, False, True))
    inner = jnp.logical_and(j >= 1, j < n_steps)
    pl.when(jnp.logical_and(inner, j % 2 == 0))(lambda: work(0, True, True))
    pl.when(jnp.logical_and(inner, j % 2 == 1))(lambda: work(1, True, True))


def _rnn_branch(xr, conv_w, conv_b, w_gate, b_gate, lam, layer, geo):
    nb, bw = N_RNN_BLOCKS, RNN_BLOCK_W
    bsz, n_lat = geo["batch"], geo["lat_chunks"]
    n_steps = n_lat + 1
    halo = SUBLANES
    n_halo = xr.shape[0] // (halo * nb)
    per = CHUNK // halo

    def fwd_blk(b, s):
        return jnp.where(s == 0, bsz * n_lat + b, b * n_lat + s - 1)

    def bwd_blk(b, s):
        return jnp.where(s == 0, bsz * n_lat + b, b * n_lat + n_lat - s)

    in_step = lambda j: jnp.minimum(j, n_steps - 1)
    out_step = lambda j: jnp.maximum(j - 1, 0)

    def chunk(blk, step):
        return pl.BlockSpec((CHUNK * nb, bw), lambda b, j: (blk(b, step(j)), 0))

    def prev(blk):
        return pl.BlockSpec(
            (halo * nb, bw), lambda b, j: (jnp.maximum(blk(b, in_step(j)) * per - 1, 0), 0))

    def nxt(blk):
        return pl.BlockSpec(
            (halo * nb, bw),
            lambda b, j: (jnp.minimum((blk(b, in_step(j)) + 1) * per, n_halo - 1), 0))

    return pl.pallas_call(
        functools.partial(_rnn_kernel, n_steps=n_steps),
        grid=(bsz, n_steps + 1),
        in_specs=[
            chunk(fwd_blk, in_step), prev(fwd_blk), nxt(fwd_blk),
            chunk(bwd_blk, in_step), prev(bwd_blk), nxt(bwd_blk),
            _resident_layer(conv_w, layer),
            _resident_layer(conv_b, layer),
            _resident_layer(w_gate, layer),
            _resident_layer(b_gate, layer),
            _resident_layer(lam, layer),
        ],
        out_specs=[chunk(fwd_blk, out_step), chunk(bwd_blk, out_step)],
        out_shape=[jax.ShapeDtypeStruct(xr.shape, F32)] * 2,
        scratch_shapes=[
            pltpu.VMEM((n_steps, CHUNK * nb, bw), F32),
            pltpu.VMEM((2, 2, CHUNK * nb, bw), F32),
            pltpu.VMEM((2, 2, CHUNK * nb, bw), F32),
            pltpu.VMEM((2, nb, bw), F32),
        ],
        compiler_params=pltpu.CompilerParams(
            dimension_semantics=("arbitrary", "arbitrary"),
            vmem_limit_bytes=VMEM_LIMIT_BYTES),
        name="rglru",
    )(xr, xr, xr, xr, xr, xr, conv_w, conv_b, w_gate, b_gate, lam)


def _stack_heads(q, g):
    base = g * Q_PER_KV * HEAD_DIM
    return jnp.concatenate(
        [q[:, base + h * HEAD_DIM: base + (h + 1) * HEAD_DIM] for h in range(Q_PER_KV)], axis=0)


def _nt_dot(a, b):
    return lax.dot_general(a, b, (((1,), (1,)), ((), ())), preferred_element_type=F32)


def _attend_units(sink_ref, layer, units):
    folds = KEY_BLOCK // SUBLANES

    def phase1(u):
        n_q = u["q"].shape[0]
        width = Q_PER_KV * n_q
        qs = _stack_heads(u["q"], u["g"])
        sink = jnp.concatenate(
            [jnp.full((1, n_q), sink_ref[layer, u["g"] * Q_PER_KV + h] * LOG2_E, F32)
             for h in range(Q_PER_KV)], axis=1)
        m8 = jnp.broadcast_to(sink, (SUBLANES, width))
        for j, (k, bias) in enumerate(u["key_blocks"]):
            s = _nt_dot(k, qs)
            if bias is not None:
                s = s + jnp.concatenate([bias] * Q_PER_KV, axis=1)
            u["s_s"][j * KEY_BLOCK:(j + 1) * KEY_BLOCK, :] = s
            m8 = jnp.maximum(m8, jnp.max(s.reshape(folds, SUBLANES, width), axis=0))
        u["sink"] = sink
        u["m"] = jnp.max(m8, axis=0, keepdims=True)

    def phase2(u):
        width = Q_PER_KV * u["q"].shape[0]
        l8 = jnp.zeros((SUBLANES, width), F32)
        for j in range(len(u["key_blocks"])):
            p = jnp.exp2(u["s_s"][j * KEY_BLOCK:(j + 1) * KEY_BLOCK, :] - u["m"])
            l8 = l8 + jnp.sum(p.reshape(folds, SUBLANES, width), axis=0)
            u["p_s"][j * KEY_BLOCK:(j + 1) * KEY_BLOCK, :] = p.astype(BF16)
        u["denom"] = jnp.sum(l8, axis=0, keepdims=True) + jnp.exp2(u["sink"] - u["m"])

    def phase3(u):
        n_q = u["q"].shape[0]
        n_keys = len(u["key_blocks"]) * KEY_BLOCK
        ot = _dot(u["vt"], u["p_s"][0:n_keys, :]) * (1.0 / u["denom"])
        for h in range(Q_PER_KV):
            u["o_store"](h, ot[:, h * n_q:(h + 1) * n_q].T)

    for s in range(len(units) + 2):
        if s < len(units):
            phase1(units[s])
        if 0 <= s - 1 < len(units):
            phase2(units[s - 1])
        if 0 <= s - 2 < len(units):
            phase3(units[s - 2])


def _lat_attn_kernel(sink_ref, q_ref, k_ref, vt_ref, kc_ref, vtc_ref, o_ref, s_s, p_s,
                     *, seq_len, layer):
    n_blk = q_ref.shape[0] // Q_BLOCK
    band_blocks = BAND // KEY_BLOCK
    ctx_blocks = vtc_ref.shape[0]
    last_start = seq_len // KEY_BLOCK - band_blocks
    rel = (lax.broadcasted_iota(jnp.int32, (KEY_BLOCK, Q_BLOCK), 0)
           - lax.broadcasted_iota(jnp.int32, (KEY_BLOCK, Q_BLOCK), 1))

    def block_units(i, slot):
        units = []
        qb = pl.program_id(1) * n_blk + i
        jb0 = jnp.clip(qb - WINDOW // KEY_BLOCK, 0, last_start)
        biases = [jnp.where(jnp.abs(rel + (jb0 + t - qb) * KEY_BLOCK) <= WINDOW, 0.0, NEG_INF)
                  for t in range(band_blocks)]
        rows = pl.ds(pl.multiple_of(i * Q_BLOCK, Q_BLOCK), Q_BLOCK)
        q = q_ref[rows, :]
        for g in range(N_KV_HEADS):
            gs = slice(g * HEAD_DIM, (g + 1) * HEAD_DIM)
            key_blocks = [
                (k_ref[pl.ds(pl.multiple_of((jb0 + t) * KEY_BLOCK, KEY_BLOCK), KEY_BLOCK), gs],
                 biases[t]) for t in range(band_blocks)]
            key_blocks += [(kc_ref[t * KEY_BLOCK:(t + 1) * KEY_BLOCK, gs], None)
                           for t in range(ctx_blocks)]
            vt = jnp.concatenate([vt_ref[jb0 + t, gs, :] for t in range(band_blocks)]
                                 + [vtc_ref[t, gs, :] for t in range(ctx_blocks)], axis=1)

            def o_store(h, tile, g=g, rows=rows):
                c0 = (g * Q_PER_KV + h) * HEAD_DIM
                o_ref[rows, c0:c0 + HEAD_DIM] = tile.astype(BF16)

            units.append(dict(g=g, q=q, key_blocks=key_blocks, vt=vt,
                              s_s=s_s.at[slot, g], p_s=p_s.at[slot, g], o_store=o_store))
        return units

    def body(it, carry):
        units = []
        for slot in range(Q_SLOTS):
            units += block_units(it * Q_SLOTS + slot, slot)
        _attend_units(sink_ref, layer, units)
        return carry

    lax.fori_loop(0, n_blk // Q_SLOTS, body, 0)


def _ctx_attn_kernel(sink_ref, q_ref, kc_ref, vtc_ref, o_ref, s_s, p_s, *, layer):
    ctx_blocks = vtc_ref.shape[0]
    q = q_ref[...]
    units = []
    for g in range(N_KV_HEADS):
        gs = slice(g * HEAD_DIM, (g + 1) * HEAD_DIM)
        key_blocks = [(kc_ref[t * KEY_BLOCK:(t + 1) * KEY_BLOCK, gs], None)
                      for t in range(ctx_blocks)]
        vt = jnp.concatenate([vtc_ref[t, gs, :] for t in range(ctx_blocks)], axis=1)

        def o_store(h, tile, g=g):
            c0 = (g * Q_PER_KV + h) * HEAD_DIM
            o_ref[:, c0:c0 + HEAD_DIM] = tile.astype(BF16)

        units.append(dict(g=g, q=q, key_blocks=key_blocks, vt=vt,
                          s_s=s_s.at[g], p_s=p_s.at[g], o_store=o_store))
    _attend_units(sink_ref, layer, units)


def _attention(q, k, vt, sink, layer, geo, with_ctx_queries):
    nt, d = q.shape
    bsz, seq_len, n_ctx = geo["batch"], geo["seq"], geo["ctx"]
    q_sup = next(s for s in (1024, 512, 256, Q_BLOCK) if seq_len % s == 0)
    n_sup = seq_len // q_sup
    ctx_blk0 = bsz * seq_len // n_ctx
    smem = pl.BlockSpec(memory_space=pltpu.SMEM)
    ctx_keys = pl.BlockSpec((n_ctx, KV_WIDTH), lambda b, *_: (ctx_blk0 + b, 0))
    ctx_vals = pl.BlockSpec((n_ctx // KEY_BLOCK, KV_WIDTH, KEY_BLOCK),
                            lambda b, *_: (ctx_blk0 + b, 0, 0))
    o = pl.pallas_call(
        functools.partial(_lat_attn_kernel, seq_len=seq_len, layer=layer),
        grid=(bsz, n_sup),
        in_specs=[
            smem,
            pl.BlockSpec((q_sup, d), lambda b, i: (b * n_sup + i, 0)),
            pl.BlockSpec((seq_len, KV_WIDTH), lambda b, i: (b, 0)),
            pl.BlockSpec((seq_len // KEY_BLOCK, KV_WIDTH, KEY_BLOCK), lambda b, i: (b, 0, 0)),
            ctx_keys, ctx_vals,
        ],
        out_specs=pl.BlockSpec((q_sup, d), lambda b, i: (b * n_sup + i, 0)),
        out_shape=jax.ShapeDtypeStruct((bsz * seq_len, d), BF16),
        scratch_shapes=[
            pltpu.VMEM((Q_SLOTS, N_KV_HEADS, BAND + n_ctx, Q_PER_KV * Q_BLOCK), F32),
            pltpu.VMEM((Q_SLOTS, N_KV_HEADS, BAND + n_ctx, Q_PER_KV * Q_BLOCK), BF16),
        ],
        compiler_params=pltpu.CompilerParams(
            dimension_semantics=("arbitrary", "arbitrary"),
            vmem_limit_bytes=VMEM_LIMIT_BYTES),
        name="lat_attention",
    )(sink, q, k, vt, k, vt)
    if not with_ctx_queries:
        return (o, o, 0)
    o_ctx = pl.pallas_call(
        functools.partial(_ctx_attn_kernel, layer=layer),
        grid=(bsz,),
        in_specs=[
            smem,
            pl.BlockSpec((n_ctx, d), lambda b: (ctx_blk0 + b, 0)),
            ctx_keys, ctx_vals,
        ],
        out_specs=pl.BlockSpec((n_ctx, d), lambda b: (b, 0)),
        out_shape=jax.ShapeDtypeStruct((bsz * n_ctx, d), BF16),
        scratch_shapes=[
            pltpu.VMEM((N_KV_HEADS, n_ctx, Q_PER_KV * n_ctx), F32),
            pltpu.VMEM((N_KV_HEADS, n_ctx, Q_PER_KV * n_ctx), BF16),
        ],
        compiler_params=pltpu.CompilerParams(
            dimension_semantics=("arbitrary",),
            vmem_limit_bytes=VMEM_LIMIT_BYTES),
        name="ctx_attention",
    )(sink, q, k, vt)
    return (o, o_ctx, 0)


def _merge_kernel(xl_ref, xc_ref, hf_ref, hb_ref, gr_ref, ol_ref, oc_ref, gl_ref, mod_ref,
                  g_ref, wr_ref, wa_ref, wo_ref, out_ref, *, n_lat_tiles, has_ctx_tiles):
    d = D_MODEL
    nb = N_RNN_BLOCKS
    tm = xl_ref.shape[0]
    is_lat = pl.program_id(0) < n_lat_tiles

    def stream_tile(lat_ref, ctx_ref):
        return jnp.where(is_lat, lat_ref[...], ctx_ref[...]) if has_ctx_tiles else lat_ref[...]

    h = jnp.concatenate(
        [hf_ref[pl.ds(n, tm, stride=nb), :] + hb_ref[pl.ds(n, tm, stride=nb), :]
         for n in range(nb)], axis=-1)
    g = gr_ref[...].astype(F32)
    hg = h * g
    y = (hg * jnp.tanh(g * (GELU_C1 + GELU_C2 * (g * g))) + hg).astype(BF16)
    ya = _dot(y, wr_ref[...])
    yb = _dot(stream_tile(ol_ref, oc_ref), wa_ref[...])
    ta = jnp.tanh(gl_ref[:, 0:d].astype(F32))
    tb = jnp.tanh(gl_ref[:, d:2 * d].astype(F32))
    mix = ((ya * ta + ya) + (yb * tb + yb)).astype(BF16)
    m = _dot(mix, wo_ref[...])
    out_ref[...] = stream_tile(xl_ref, xc_ref) + _rms(m) * (mod_ref[2:3, :] * g_ref[...])


def _merge(x_stream, hf, hb, gr, o_stream, gl, mod, g_post, w_o_rnn, w_o_attn, w_out,
           layer, geo, n_tiles):
    d = D_MODEL
    tm = geo["tm"]
    n_lat_tiles = geo["n_lat_tiles"]
    row = lambda i: (i, 0)
    tile = pl.BlockSpec((tm, d), row)
    slab = pl.BlockSpec((tm * N_RNN_BLOCKS, RNN_BLOCK_W), row)
    return pl.pallas_call(
        functools.partial(_merge_kernel, n_lat_tiles=n_lat_tiles,
                          has_ctx_tiles=n_tiles > n_lat_tiles),
        grid=(n_tiles,),
        in_specs=[
            *_stream_specs(x_stream, tm, d, n_lat_tiles), slab, slab, tile,
            *_stream_specs(o_stream, tm, d, n_lat_tiles),
            pl.BlockSpec((tm, 2 * d), row),
            _mod_spec(geo, layer),
            _resident_layer(g_post, layer),
            _resident(w_o_rnn.shape), _resident(w_o_attn.shape), _resident(w_out.shape),
        ],
        out_specs=tile,
        out_shape=jax.ShapeDtypeStruct((n_tiles * tm, d), F32),
        compiler_params=pltpu.CompilerParams(
            dimension_semantics=("arbitrary",),
            vmem_limit_bytes=VMEM_LIMIT_BYTES),
        name="merge",
    )(x_stream[0], x_stream[1], hf, hb, gr, o_stream[0], o_stream[1], gl, mod,
      g_post, w_o_rnn, w_o_attn, w_out)


def _ffn_kernel(x_ref, mod_ref, gpre_ref, gpost_ref, w1_ref, w2_ref, *rest, n_casts):
    cast_in, out_ref, cast_out = rest[:2 * n_casts], rest[2 * n_casts], rest[2 * n_casts + 1:]
    _run_side_casts(cast_in, cast_out)
    sub = x_ref.shape[0] // FFN_SUBTILES
    bounds = list(range(0, D_FF, FFN_CHUNK)) + [D_FF]
    n_chunks = len(bounds) - 1

    def normed(s):
        x = x_ref[s * sub:(s + 1) * sub, :]
        h = _rms(x) * (gpre_ref[...] * (1.0 + mod_ref[4:5, :])) + mod_ref[3:4, :]
        return h.astype(BF16)

    def first_layer(h, c):
        lo, hi = bounds[c], bounds[c + 1]
        return _dot(h, w1_ref[:, lo:hi]), _dot(h, w1_ref[:, D_FF + lo:D_FF + hi])

    h_next = normed(0)
    for s in range(FFN_SUBTILES):
        h = h_next
        if s + 1 < FFN_SUBTILES:
            h_next = normed(s + 1)
        f = None
        nxt = first_layer(h, 0)
        for c in range(n_chunks):
            half_gate, up = nxt
            if c + 1 < n_chunks:
                nxt = first_layer(h, c + 1)
            act = ((half_gate * jnp.tanh(half_gate) + half_gate) * up).astype(BF16)
            part = _dot(act, w2_ref[bounds[c]:bounds[c + 1], :])
            f = part if f is None else f + part
        rows = slice(s * sub, (s + 1) * sub)
        out_ref[rows, :] = x_ref[rows, :] + _rms(f) * (mod_ref[5:6, :] * gpost_ref[...])


def _ffn(x_all, mod, g_pre, g_post, w1, w2, layer, geo, n_tiles, side_casts, cast_layer):
    d = x_all.shape[1]
    tm = geo["tm"] * FFN_SUBTILES
    n_blocks = n_tiles // FFN_SUBTILES
    tile = pl.BlockSpec((tm, d), lambda i: (i, 0))
    cast_in_specs, cast_out_specs, cast_shapes, cast_args = _side_cast_plan(
        side_casts, cast_layer, n_blocks)
    outs = pl.pallas_call(
        functools.partial(_ffn_kernel, n_casts=len(side_casts)),
        grid=(n_blocks,),
        in_specs=[
            tile,
            _mod_spec(geo, layer, FFN_SUBTILES),
            _resident_layer(g_pre, layer), _resident_layer(g_post, layer),
            _resident(w1.shape), _resident(w2.shape),
            *cast_in_specs,
        ],
        out_specs=[tile, *cast_out_specs],
        out_shape=[jax.ShapeDtypeStruct((n_blocks * tm, d), F32), *cast_shapes],
        compiler_params=pltpu.CompilerParams(
            dimension_semantics=("arbitrary",),
            vmem_limit_bytes=VMEM_LIMIT_BYTES),
        name="ffn",
    )(x_all, mod, g_pre, g_post, w1, w2, *cast_args)
    return outs[0], outs[1:]


def _rope_tables(seq_len, pad_rows):
    f32 = np.float32
    n_rows = seq_len // GRID_W
    inv = f32(ROPE_BASE) ** (-np.arange(N_FREQ, dtype=f32) / f32(N_FREQ))
    ang = np.arange(max(n_rows, GRID_W), dtype=f32)[:, None] * inv[None, :]
    cos_u, sin_u = np.cos(ang), np.sin(ang)
    by_row = lambda t: np.repeat(t[:n_rows], GRID_W, axis=0)
    by_col = lambda t: np.tile(t[:GRID_W], (n_rows, 1))
    cos = np.concatenate([by_row(cos_u)] * 2 + [by_col(cos_u)] * 2, axis=-1)
    sin = np.concatenate(
        [-by_row(sin_u), by_row(sin_u), -by_col(sin_u), by_col(sin_u)], axis=-1)
    cos = np.concatenate([cos, np.ones((pad_rows, HEAD_DIM), f32)], axis=0)
    sin = np.concatenate([sin, np.zeros((pad_rows, HEAD_DIM), f32)], axis=0)
    scale = f32(HEAD_DIM ** -0.5 * LOG2_E)
    return tuple(jnp.asarray(t.astype(f32)) for t in (cos * scale, sin * scale, cos, sin))


def kernel(x, c, ctx, c_ctx, w_mod, b_mod, g_mix_pre, g_mix_post, g_ffn_pre, g_ffn_post, w_in, conv_w, conv_b, lru_wa, lru_ba, lru_wx, lru_bx, lru_lam, attn_sink, w_o_rnn, w_o_attn, w_out, w_ffn_in, w_ffn_out):
    bsz, seq_len, d = x.shape
    n_ctx = ctx.shape[1]
    depth = w_mod.shape[0]
    assert d == D_MODEL and n_ctx == CHUNK and seq_len % CHUNK == 0 and seq_len >= BAND
    assert bsz + 1 <= SUBLANES and seq_len % GRID_W == 0

    n_lat_rows, n_ctx_rows = bsz * seq_len, bsz * n_ctx
    big = 512 * FFN_SUBTILES
    tm = 512 if (n_ctx_rows % big == 0 and seq_len % big == 0) else CHUNK
    assert n_ctx_rows % (tm * FFN_SUBTILES) == 0 and seq_len % (tm * FFN_SUBTILES) == 0
    n_lat_tiles, n_ctx_tiles = n_lat_rows // tm, n_ctx_rows // tm
    tiles_per_batch = seq_len // tm
    geo = {
        "batch": bsz, "seq": seq_len, "ctx": n_ctx, "tm": tm,
        "lat_chunks": seq_len // CHUNK,
        "n_lat_tiles": n_lat_tiles, "n_ctx_tiles": n_ctx_tiles,
        "mod_row": lambda i: jnp.where(i < n_lat_tiles, i // tiles_per_batch, bsz),
        "table_block": lambda i: jnp.where(
            i < n_lat_tiles, i % tiles_per_batch, tiles_per_batch + i - n_lat_tiles),
    }

    cs = jnp.concatenate(
        [c, c_ctx[None, :], jnp.zeros((SUBLANES - bsz - 1, d), F32)], axis=0)
    mod = _modulation(cs, w_mod, b_mod).reshape(depth, SUBLANES, MOD_CHUNKS, d)

    tables = _rope_tables(seq_len, n_ctx_rows)

    w_gate = jnp.concatenate([lru_wa, lru_wx], axis=-1).astype(BF16)
    b_gate = 0.5 * jnp.concatenate(
        [lru_ba.reshape(depth, 2, N_RNN_BLOCKS, 1, RNN_BLOCK_W),
         lru_bx.reshape(depth, 2, N_RNN_BLOCKS, 1, RNN_BLOCK_W)], axis=-1)
    lam = lru_lam.reshape(depth, 2, N_RNN_BLOCKS, 1, RNN_BLOCK_W)
    conv_w_half = (0.5 * conv_w).reshape(depth, CONV_W, N_RNN_BLOCKS, RNN_BLOCK_W)
    conv_b_half = (0.5 * conv_b).reshape(depth, N_RNN_BLOCKS, RNN_BLOCK_W)
    g_mix_pre, g_mix_post, g_ffn_pre, g_ffn_post = (
        g.reshape(depth, 1, d) for g in (g_mix_pre, g_mix_post, g_ffn_pre, g_ffn_post))

    stream = (x.reshape(n_lat_rows, d), ctx.reshape(n_ctx_rows, d), 0)
    ones = lambda n: jnp.ones((1, n), F32)
    halves = lambda n: jnp.full((1, n), 0.5, F32)
    gl_cols = 2 * d
    w_in_scale = jnp.concatenate([ones(w_in.shape[-1] - gl_cols), halves(gl_cols)], axis=1)
    w_in_b = _to_bf16(w_in, 0, 256, w_in_scale)
    in_proj_casts = [
        (w_o_rnn, halves(d)), (w_o_attn, ones(d)), (w_out, halves(d)),
        (w_ffn_in, jnp.concatenate([halves(D_FF), ones(D_FF)], axis=1)),
        (w_ffn_out, ones(d)),
    ]
    for l in range(depth):
        need_ctx = l < depth - 1
        n_out_tiles = n_lat_tiles + n_ctx_tiles if need_ctx else n_lat_tiles
        (xr, gr, q, k, vt, gl), layer_weights = _in_proj(
            stream, mod, g_mix_pre, w_in_b, l, tables, geo, in_proj_casts)
        w_o_rnn_b, w_o_attn_b, w_out_b, w_ffn_in_b, w_ffn_out_b = layer_weights
        hf, hb = _rnn_branch(xr, conv_w_half, conv_b_half, w_gate, b_gate, lam, l, geo)
        o_stream = _attention(q, k, vt, attn_sink, l, geo, need_ctx)
        x_all = _merge(
            stream, hf, hb, gr, o_stream, gl, mod, g_mix_post, w_o_rnn_b, w_o_attn_b, w_out_b,
            l, geo, n_out_tiles)
        next_casts = [(w_in, w_in_scale)] if l + 1 < depth else []
        x_all, next_weights = _ffn(
            x_all, mod, g_ffn_pre, g_ffn_post, w_ffn_in_b, w_ffn_out_b,
            l, geo, n_out_tiles, next_casts, l + 1)
        if next_weights:
            (w_in_b,) = next_weights
        stream = (x_all, x_all, n_lat_tiles)
    return x_all[:n_lat_rows].reshape(bsz, seq_len, d)
```

```python
import functools
import math

import jax
import jax.numpy as jnp
import numpy as np
from jax import lax
from jax.experimental import pallas as pl
from jax.experimental.pallas import tpu as pltpu

D_MODEL = 1024
HEAD_DIM = 128
N_Q_HEADS = 8
N_KV_HEADS = 2
Q_PER_KV = N_Q_HEADS // N_KV_HEADS
KV_WIDTH = N_KV_HEADS * HEAD_DIM
WINDOW = 128
GRID_W = 64
N_FREQ = HEAD_DIM // 4
ROPE_BASE = 10000.0
N_RNN_BLOCKS = 8
RNN_BLOCK_W = D_MODEL // N_RNN_BLOCKS
LRU_C = 8.0
CONV_W = 4
CONV_LEFT = 2
D_FF = ((8 * D_MODEL + 3 * 256 - 1) // (3 * 256)) * 256
EPS = 1e-6
NEG_INF = -1e30
MOD_CHUNKS = 6

LANES = 128
SUBLANES = 8
BF16_SUBLANES = 16
V7X_VMEM_BYTES = 64 * 1024 * 1024
VMEM_LIMIT_BYTES = V7X_VMEM_BYTES * 7 // 8

CHUNK = 256
CONV_T = 32
FFN_CHUNK = 768
FFN_SUBTILES = 2
Q_BLOCK = 128
Q_SLOTS = 4
KEY_BLOCK = 128
BAND = Q_BLOCK + 2 * WINDOW
LOG2_E = math.log2(math.e)

BF16 = jnp.bfloat16
F32 = jnp.float32
F32_TINY = float(jnp.finfo(jnp.float32).tiny)


def _dot(a, b):
    return jnp.dot(a, b, preferred_element_type=F32)


def _sigmoid(x):
    return 0.5 * jnp.tanh(0.5 * x) + 0.5


GELU_C1 = math.sqrt(2.0 / math.pi)
GELU_C2 = 0.044715 * GELU_C1


def _rms(x):
    return x * lax.rsqrt(jnp.mean(x * x, axis=-1, keepdims=True) + EPS)


def _resident(shape):
    nd = len(shape)
    return pl.BlockSpec(shape, lambda *_: (0,) * nd, pipeline_mode=pl.Buffered(1))


def _resident_layer(stacked, layer):
    tail = stacked.shape[1:]
    return pl.BlockSpec((None,) + tail, lambda *_: (layer,) + (0,) * len(tail),
                        pipeline_mode=pl.Buffered(1))


def _cast_kernel(w_ref, scale_ref, o_ref):
    o_ref[...] = (w_ref[...] * scale_ref[...]).astype(BF16)


def _to_bf16(w, layer, row_block, col_scale):
    _, rows, cols = w.shape
    return pl.pallas_call(
        _cast_kernel,
        grid=(rows // row_block,),
        in_specs=[pl.BlockSpec((None, row_block, cols), lambda r: (layer, r, 0)),
                  pl.BlockSpec((1, cols), lambda r: (0, 0))],
        out_specs=pl.BlockSpec((row_block, cols), lambda r: (r, 0)),
        out_shape=jax.ShapeDtypeStruct((rows, cols), BF16),
        compiler_params=pltpu.CompilerParams(
            dimension_semantics=("arbitrary",),
            vmem_limit_bytes=VMEM_LIMIT_BYTES),
        name="cast_bf16",
    )(w, col_scale)


def _side_cast_plan(side_casts, layer, n_steps):
    in_specs, out_specs, shapes, args = [], [], [], []
    for w, scale in side_casts:
        _, rows, cols = w.shape
        rb = next(r for r in range(BF16_SUBLANES, rows + 1, BF16_SUBLANES)
                  if rows % r == 0 and rows // r <= n_steps)
        n_blk = rows // rb
        blk = lambda i, n_blk=n_blk: jnp.minimum(i, n_blk - 1)
        in_specs += [pl.BlockSpec((None, rb, cols), lambda i, blk=blk: (layer, blk(i), 0)),
                     pl.BlockSpec((1, cols), lambda i: (0, 0))]
        out_specs.append(pl.BlockSpec((rb, cols), lambda i, blk=blk: (blk(i), 0)))
        shapes.append(jax.ShapeDtypeStruct((rows, cols), BF16))
        args += [w, scale]
    return in_specs, out_specs, shapes, args


def _run_side_casts(cast_in, cast_out):
    for c, dst in enumerate(cast_out):
        dst[...] = (cast_in[2 * c][...] * cast_in[2 * c + 1][...]).astype(BF16)


def _mod_kernel(c_ref, w_ref, b_ref, o_ref):
    c = c_ref[...]
    s = (c * _sigmoid(c)).astype(BF16)
    o_ref[...] = _dot(s, w_ref[...].astype(BF16)) + b_ref[...]


def _modulation(cs, w_mod, b_mod):
    n_layers, d, width = w_mod.shape
    tn = 1536
    return pl.pallas_call(
        _mod_kernel,
        grid=(n_layers, width // tn),
        in_specs=[
            pl.BlockSpec((SUBLANES, d), lambda l, j: (0, 0)),
            pl.BlockSpec((None, d, tn), lambda l, j: (l, 0, j)),
            pl.BlockSpec((None, 1, tn), lambda l, j: (l, 0, j)),
        ],
        out_specs=pl.BlockSpec((None, SUBLANES, tn), lambda l, j: (l, 0, j)),
        out_shape=jax.ShapeDtypeStruct((n_layers, SUBLANES, width), F32),
        compiler_params=pltpu.CompilerParams(
            dimension_semantics=("arbitrary", "arbitrary"),
            vmem_limit_bytes=VMEM_LIMIT_BYTES),
        name="modulation",
    )(cs, w_mod, b_mod.reshape(n_layers, 1, width))


def _rope(t, cos_ref, sin_ref, n_heads):
    cos = cos_ref[...]
    sin = sin_ref[...]
    lane = lax.broadcasted_iota(jnp.int32, cos.shape, 1)
    first = (lane & (2 * N_FREQ - 1)) < N_FREQ
    outs = []
    for h in range(n_heads):
        th = t[:, h * HEAD_DIM:(h + 1) * HEAD_DIM]
        partner = jnp.where(first,
                            pltpu.roll(th, HEAD_DIM - N_FREQ, axis=1),
                            pltpu.roll(th, N_FREQ, axis=1))
        outs.append(th * cos + partner * sin)
    return jnp.concatenate(outs, axis=-1)


def _stream_specs(stream, tm, d, n_lat_tiles):
    _, ctx, ctx_off = stream
    last_ctx = ctx.shape[0] // tm - 1
    lat_spec = pl.BlockSpec((tm, d), lambda i: (jnp.minimum(i, n_lat_tiles - 1), 0))
    ctx_spec = pl.BlockSpec(
        (tm, d), lambda i: (jnp.minimum(jnp.maximum(i - n_lat_tiles, 0) + ctx_off, last_ctx), 0))
    return lat_spec, ctx_spec


def _mod_spec(geo, layer, tiles_per_step=1):
    return pl.BlockSpec(
        (None, None, MOD_CHUNKS, D_MODEL),
        lambda i: (layer, geo["mod_row"](i * tiles_per_step), 0, 0))


def _stream_tile(lat_ref, ctx_ref, n_lat_tiles):
    return jnp.where(pl.program_id(0) < n_lat_tiles, lat_ref[...], ctx_ref[...])


def _in_kernel(xl_ref, xc_ref, mod_ref, g_ref, w_ref, cq_ref, sq_ref, ck_ref, sk_ref, *rest,
               n_lat_tiles, n_casts):
    cast_in, rest = rest[:2 * n_casts], rest[2 * n_casts:]
    (xr_ref, gr_ref, q_ref, k_ref, vt_ref, gl_ref), cast_out = rest[:6], rest[6:]
    _run_side_casts(cast_in, cast_out)
    d = D_MODEL
    x = _stream_tile(xl_ref, xc_ref, n_lat_tiles)
    h = _rms(x) * (g_ref[...] * (1.0 + mod_ref[1:2, :])) + mod_ref[0:1, :]
    h = h.astype(BF16)
    xr = _dot(h, w_ref[:, 0:d])
    for n in range(N_RNN_BLOCKS):
        xr_ref[pl.ds(n, x.shape[0], stride=N_RNN_BLOCKS), :] = (
            xr[:, n * RNN_BLOCK_W:(n + 1) * RNN_BLOCK_W])
    gr_ref[...] = _dot(h, w_ref[:, d:2 * d]).astype(BF16)
    q = _dot(h, w_ref[:, 2 * d:3 * d])
    q_ref[...] = _rope(q, cq_ref, sq_ref, N_Q_HEADS).astype(BF16)
    k = _dot(h, w_ref[:, 3 * d:3 * d + KV_WIDTH])
    k_ref[...] = _rope(k, ck_ref, sk_ref, N_KV_HEADS).astype(BF16)
    v = _dot(h, w_ref[:, 3 * d + KV_WIDTH:3 * d + 2 * KV_WIDTH])
    for t in range(vt_ref.shape[0]):
        vt_ref[t] = v[t * KEY_BLOCK:(t + 1) * KEY_BLOCK, :].T.astype(BF16)
    gl_ref[...] = _dot(h, w_ref[:, 3 * d + 2 * KV_WIDTH:5 * d + 2 * KV_WIDTH]).astype(BF16)


def _in_proj(stream, mod, g_pre, w_in, layer, tables, geo, side_casts):
    d = D_MODEL
    tm = geo["tm"]
    n_tiles = geo["n_lat_tiles"] + geo["n_ctx_tiles"]
    nt = n_tiles * tm
    row = lambda i: (i, 0)
    tab = pl.BlockSpec((tm, LANES), lambda i: (geo["table_block"](i), 0))
    cast_in_specs, cast_out_specs, cast_shapes, cast_args = _side_cast_plan(
        side_casts, layer, n_tiles)
    outs = pl.pallas_call(
        functools.partial(_in_kernel, n_lat_tiles=geo["n_lat_tiles"], n_casts=len(side_casts)),
        grid=(n_tiles,),
        in_specs=[
            *_stream_specs(stream, tm, d, geo["n_lat_tiles"]),
            _mod_spec(geo, layer),
            _resident_layer(g_pre, layer),
            _resident(w_in.shape),
            tab, tab, tab, tab,
            *cast_in_specs,
        ],
        out_specs=[
            pl.BlockSpec((tm * N_RNN_BLOCKS, RNN_BLOCK_W), row),
            pl.BlockSpec((tm, d), row),
            pl.BlockSpec((tm, d), row),
            pl.BlockSpec((tm, KV_WIDTH), row),
            pl.BlockSpec((tm // KEY_BLOCK, KV_WIDTH, KEY_BLOCK), lambda i: (i, 0, 0)),
            pl.BlockSpec((tm, 2 * d), row),
            *cast_out_specs,
        ],
        out_shape=[
            jax.ShapeDtypeStruct((nt * N_RNN_BLOCKS, RNN_BLOCK_W), F32),
            jax.ShapeDtypeStruct((nt, d), BF16),
            jax.ShapeDtypeStruct((nt, d), BF16),
            jax.ShapeDtypeStruct((nt, KV_WIDTH), BF16),
            jax.ShapeDtypeStruct((nt // KEY_BLOCK, KV_WIDTH, KEY_BLOCK), BF16),
            jax.ShapeDtypeStruct((nt, 2 * d), BF16),
            *cast_shapes,
        ],
        compiler_params=pltpu.CompilerParams(
            dimension_semantics=("arbitrary",),
            vmem_limit_bytes=VMEM_LIMIT_BYTES),
        name="in_proj",
    )(stream[0], stream[1], mod, g_pre, w_in, *tables, *cast_args)
    return outs[:6], outs[6:]


def _rnn_kernel(xf_ref, xfp_ref, xfn_ref, xb_ref, xbp_ref, xbn_ref,
                cw_ref, cb_ref, wg_ref, bg_ref, lam_ref,
                hf_ref, hb_ref,
                xc_s, a_s, b_s, carry_s, *, n_steps):
    j = pl.program_id(1)
    nb, bw = N_RNN_BLOCKS, RNN_BLOCK_W
    n_in = CONV_T + CONV_W - 1
    right = CONV_W - 1 - CONV_LEFT

    @pl.when(j == 0)
    def _():
        carry_s[...] = jnp.zeros_like(carry_s)

    def conv(slot, x_ref, xp_ref, xn_ref, has_prev, has_next):
        def conv_piece(v, base):
            v = v.reshape(n_in, nb, bw)
            acc = cb_ref[...] + v[0:CONV_T] * cw_ref[0]
            for k in range(1, CONV_W):
                acc = acc + v[k:k + CONV_T] * cw_ref[k]
            xc_s[slot, pl.ds(base, CONV_T * nb), :] = acc.reshape(CONV_T * nb, bw)

        halo_rows = xp_ref.shape[0]
        left = jnp.where(has_prev, xp_ref[halo_rows - CONV_LEFT * nb:halo_rows, :], 0.0)
        conv_piece(jnp.concatenate([left, x_ref[0:(n_in - CONV_LEFT) * nb, :]], axis=0), 0)
        tail = jnp.where(has_next, xn_ref[0:right * nb, :], 0.0)
        last = CHUNK - CONV_T
        conv_piece(jnp.concatenate(
            [x_ref[(last - CONV_LEFT) * nb:CHUNK * nb, :], tail], axis=0), last * nb)

        def conv_body(c, carry):
            base = pl.multiple_of(c * (CONV_T * nb), CONV_T * nb)
            conv_piece(x_ref[pl.ds(base - CONV_LEFT * nb, n_in * nb), :], base)
            return carry

        lax.fori_loop(1, CHUNK // CONV_T - 1, conv_body, 0)

    fwd_chunk = j
    bwd_chunk = jnp.where(j == 0, 0, n_steps - j)
    mid = jnp.logical_and(j >= 1, j < n_steps - 1)

    @pl.when(2 * j <= n_steps)
    def _():
        conv(fwd_chunk, xf_ref, xfp_ref, xfn_ref, j >= 2, mid)

    @pl.when(jnp.logical_and(j >= 1, 2 * j < n_steps))
    def _():
        conv(bwd_chunk, xb_ref, xbp_ref, xbn_ref, mid, j >= 2)

    def gate_unit(parity, direction, n):
        rows = pl.ds(n, CHUNK, stride=nb)
        lam = lam_ref[direction, n]
        c_half = (0.5 * LRU_C * LOG2_E) * (
            jnp.minimum(lam, 0.0) - jnp.log1p(jnp.exp(-jnp.abs(lam))))
        xh = xc_s[bwd_chunk if direction else fwd_chunk, rows, :]
        z = _dot(xh.astype(BF16), wg_ref[direction, n]) + bg_ref[direction, n]
        a = jnp.exp2(c_half * jnp.tanh(z[:, 0:bw]) + c_half)
        gated = xh * jnp.tanh(z[:, bw:2 * bw]) + xh
        om = 1.0 - a * a
        a_s[parity, direction, rows, :] = a
        b_s[parity, direction, rows, :] = (om * lax.rsqrt(jnp.maximum(om, F32_TINY))) * gated

    def scan_steps(parity, t0, t1, carry):
        hf, hb = carry
        for t in range(t0, t1):
            tf, tb = t * nb, (CHUNK - 1 - t) * nb
            hf = a_s[parity, 0, tf:tf + nb, :] * hf + b_s[parity, 0, tf:tf + nb, :]
            hb = a_s[parity, 1, tb:tb + nb, :] * hb + b_s[parity, 1, tb:tb + nb, :]
            hf_ref[tf:tf + nb, :] = hf
            hb_ref[tb:tb + nb, :] = hb
        return hf, hb

    def work(parity, do_gates, do_scan):
        units = [(direction, n) for direction in range(2) for n in range(nb)]
        per_unit = CHUNK // len(units)
        carry = (carry_s[0], carry_s[1]) if do_scan else None
        for idx, (direction, n) in enumerate(units):
            if do_gates:
                gate_unit(parity, direction, n)
            if do_scan:
                carry = scan_steps(1 - parity, idx * per_unit, (idx + 1) * per_unit, carry)
        if do_scan:
            carry_s[0], carry_s[1] = carry

    last_parity = (n_steps - 1) % 2
    pl.when(j == 0)(lambda: work(0, True, False))
    pl.when(j == n_steps)(lambda: work(1 - last_parity, False, True))
    inner = jnp.logical_and(j >= 1, j < n_steps)
    pl.when(jnp.logical_and(inner, j % 2 == 0))(lambda: work(0, True, True))
    pl.when(jnp.logical_and(inner, j % 2 == 1))(lambda: work(1, True, True))


def _rnn_branch(xr, conv_w, conv_b, w_gate, b_gate, lam, layer, geo):
    nb, bw = N_RNN_BLOCKS, RNN_BLOCK_W
    bsz, n_lat = geo["batch"], geo["lat_chunks"]
    n_steps = n_lat + 1
    halo = SUBLANES
    n_halo = xr.shape[0] // (halo * nb)
    per = CHUNK // halo

    def fwd_blk(b, s):
        return jnp.where(s == 0, bsz * n_lat + b, b * n_lat + s - 1)

    def bwd_blk(b, s):
        return jnp.where(s == 0, bsz * n_lat + b, b * n_lat + n_lat - s)

    in_step = lambda j: jnp.minimum(j, n_steps - 1)
    out_step = lambda j: jnp.maximum(j - 1, 0)

    def chunk(blk, step):
        return pl.BlockSpec((CHUNK * nb, bw), lambda b, j: (blk(b, step(j)), 0))

    def prev(blk):
        return pl.BlockSpec(
            (halo * nb, bw), lambda b, j: (jnp.maximum(blk(b, in_step(j)) * per - 1, 0), 0))

    def nxt(blk):
        return pl.BlockSpec(
            (halo * nb, bw),
            lambda b, j: (jnp.minimum((blk(b, in_step(j)) + 1) * per, n_halo - 1), 0))

    return pl.pallas_call(
        functools.partial(_rnn_kernel, n_steps=n_steps),
        grid=(bsz, n_steps + 1),
        in_specs=[
            chunk(fwd_blk, in_step), prev(fwd_blk), nxt(fwd_blk),
            chunk(bwd_blk, in_step), prev(bwd_blk), nxt(bwd_blk),
            _resident_layer(conv_w, layer),
            _resident_layer(conv_b, layer),
            _resident_layer(w_gate, layer),
            _resident_layer(b_gate, layer),
            _resident_layer(lam, layer),
        ],
        out_specs=[chunk(fwd_blk, out_step), chunk(bwd_blk, out_step)],
        out_shape=[jax.ShapeDtypeStruct(xr.shape, F32)] * 2,
        scratch_shapes=[
            pltpu.VMEM((n_steps, CHUNK * nb, bw), F32),
            pltpu.VMEM((2, 2, CHUNK * nb, bw), F32),
            pltpu.VMEM((2, 2, CHUNK * nb, bw), F32),
            pltpu.VMEM((2, nb, bw), F32),
        ],
        compiler_params=pltpu.CompilerParams(
            dimension_semantics=("arbitrary", "arbitrary"),
            vmem_limit_bytes=VMEM_LIMIT_BYTES),
        name="rglru",
    )(xr, xr, xr, xr, xr, xr, conv_w, conv_b, w_gate, b_gate, lam)


def _stack_heads(q, g):
    base = g * Q_PER_KV * HEAD_DIM
    return jnp.concatenate(
        [q[:, base + h * HEAD_DIM: base + (h + 1) * HEAD_DIM] for h in range(Q_PER_KV)], axis=0)


def _nt_dot(a, b):
    return lax.dot_general(a, b, (((1,), (1,)), ((), ())), preferred_element_type=F32)


def _attend_units(sink_ref, layer, units):
    folds = KEY_BLOCK // SUBLANES
    for u in units:
        n_q = u["q"].shape[0]
        width = Q_PER_KV * n_q
        qs = _stack_heads(u["q"], u["g"])
        sink = jnp.concatenate(
            [jnp.full((1, n_q), sink_ref[layer, u["g"] * Q_PER_KV + h] * LOG2_E, F32)
             for h in range(Q_PER_KV)], axis=1)
        m8 = jnp.broadcast_to(sink, (SUBLANES, width))
        for j, (k, bias) in enumerate(u["key_blocks"]):
            s = _nt_dot(k, qs)
            if bias is not None:
                s = s + jnp.concatenate([bias] * Q_PER_KV, axis=1)
            u["s_s"][j * KEY_BLOCK:(j + 1) * KEY_BLOCK, :] = s
            m8 = jnp.maximum(m8, jnp.max(s.reshape(folds, SUBLANES, width), axis=0))
        u["sink"] = sink
        u["m"] = jnp.max(m8, axis=0, keepdims=True)
    for u in units:
        width = Q_PER_KV * u["q"].shape[0]
        l8 = jnp.zeros((SUBLANES, width), F32)
        for j in range(len(u["key_blocks"])):
            p = jnp.exp2(u["s_s"][j * KEY_BLOCK:(j + 1) * KEY_BLOCK, :] - u["m"])
            l8 = l8 + jnp.sum(p.reshape(folds, SUBLANES, width), axis=0)
            u["p_s"][j * KEY_BLOCK:(j + 1) * KEY_BLOCK, :] = p.astype(BF16)
        u["denom"] = jnp.sum(l8, axis=0, keepdims=True) + jnp.exp2(u["sink"] - u["m"])
    for u in units:
        n_q = u["q"].shape[0]
        n_keys = len(u["key_blocks"]) * KEY_BLOCK
        ot = _dot(u["vt"], u["p_s"][0:n_keys, :]) * (1.0 / u["denom"])
        for h in range(Q_PER_KV):
            u["o_store"](h, ot[:, h * n_q:(h + 1) * n_q].T)


def _lat_attn_kernel(sink_ref, q_ref, k_ref, vt_ref, kc_ref, vtc_ref, o_ref, s_s, p_s,
                     *, seq_len, layer):
    n_blk = q_ref.shape[0] // Q_BLOCK
    band_blocks = BAND // KEY_BLOCK
    ctx_blocks = vtc_ref.shape[0]
    last_start = seq_len // KEY_BLOCK - band_blocks
    rel = (lax.broadcasted_iota(jnp.int32, (KEY_BLOCK, Q_BLOCK), 0)
           - lax.broadcasted_iota(jnp.int32, (KEY_BLOCK, Q_BLOCK), 1))

    def block_units(i, slot):
        units = []
        qb = pl.program_id(1) * n_blk + i
        jb0 = jnp.clip(qb - WINDOW // KEY_BLOCK, 0, last_start)
        biases = [jnp.where(jnp.abs(rel + (jb0 + t - qb) * KEY_BLOCK) <= WINDOW, 0.0, NEG_INF)
                  for t in range(band_blocks)]
        rows = pl.ds(pl.multiple_of(i * Q_BLOCK, Q_BLOCK), Q_BLOCK)
        q = q_ref[rows, :]
        for g in range(N_KV_HEADS):
            gs = slice(g * HEAD_DIM, (g + 1) * HEAD_DIM)
            key_blocks = [
                (k_ref[pl.ds(pl.multiple_of((jb0 + t) * KEY_BLOCK, KEY_BLOCK), KEY_BLOCK), gs],
                 biases[t]) for t in range(band_blocks)]
            key_blocks += [(kc_ref[t * KEY_BLOCK:(t + 1) * KEY_BLOCK, gs], None)
                           for t in range(ctx_blocks)]
            vt = jnp.concatenate([vt_ref[jb0 + t, gs, :] for t in range(band_blocks)]
                                 + [vtc_ref[t, gs, :] for t in range(ctx_blocks)], axis=1)

            def o_store(h, tile, g=g, rows=rows):
                c0 = (g * Q_PER_KV + h) * HEAD_DIM
                o_ref[rows, c0:c0 + HEAD_DIM] = tile.astype(BF16)

            units.append(dict(g=g, q=q, key_blocks=key_blocks, vt=vt,
                              s_s=s_s.at[slot, g], p_s=p_s.at[slot, g], o_store=o_store))
        return units

    def body(it, carry):
        units = []
        for slot in range(Q_SLOTS):
            units += block_units(it * Q_SLOTS + slot, slot)
        _attend_units(sink_ref, layer, units)
        return carry

    lax.fori_loop(0, n_blk // Q_SLOTS, body, 0)


def _ctx_attn_kernel(sink_ref, q_ref, kc_ref, vtc_ref, o_ref, s_s, p_s, *, layer):
    ctx_blocks = vtc_ref.shape[0]
    q = q_ref[...]
    units = []
    for g in range(N_KV_HEADS):
        gs = slice(g * HEAD_DIM, (g + 1) * HEAD_DIM)
        key_blocks = [(kc_ref[t * KEY_BLOCK:(t + 1) * KEY_BLOCK, gs], None)
                      for t in range(ctx_blocks)]
        vt = jnp.concatenate([vtc_ref[t, gs, :] for t in range(ctx_blocks)], axis=1)

        def o_store(h, tile, g=g):
            c0 = (g * Q_PER_KV + h) * HEAD_DIM
            o_ref[:, c0:c0 + HEAD_DIM] = tile.astype(BF16)

        units.append(dict(g=g, q=q, key_blocks=key_blocks, vt=vt,
                          s_s=s_s.at[g], p_s=p_s.at[g], o_store=o_store))
    _attend_units(sink_ref, layer, units)


def _attention(q, k, vt, sink, layer, geo, with_ctx_queries):
    nt, d = q.shape
    bsz, seq_len, n_ctx = geo["batch"], geo["seq"], geo["ctx"]
    q_sup = Q_SLOTS * Q_BLOCK
    n_sup = seq_len // q_sup
    ctx_blk0 = bsz * seq_len // n_ctx
    smem = pl.BlockSpec(memory_space=pltpu.SMEM)
    ctx_keys = pl.BlockSpec((n_ctx, KV_WIDTH), lambda b, *_: (ctx_blk0 + b, 0))
    ctx_vals = pl.BlockSpec((n_ctx // KEY_BLOCK, KV_WIDTH, KEY_BLOCK),
                            lambda b, *_: (ctx_blk0 + b, 0, 0))
    o = pl.pallas_call(
        functools.partial(_lat_attn_kernel, seq_len=seq_len, layer=layer),
        grid=(bsz, n_sup),
        in_specs=[
            smem,
            pl.BlockSpec((q_sup, d), lambda b, i: (b * n_sup + i, 0)),
            pl.BlockSpec((seq_len, KV_WIDTH), lambda b, i: (b, 0)),
            pl.BlockSpec((seq_len // KEY_BLOCK, KV_WIDTH, KEY_BLOCK), lambda b, i: (b, 0, 0)),
            ctx_keys, ctx_vals,
        ],
        out_specs=pl.BlockSpec((q_sup, d), lambda b, i: (b * n_sup + i, 0)),
        out_shape=jax.ShapeDtypeStruct((bsz * seq_len, d), BF16),
        scratch_shapes=[
            pltpu.VMEM((Q_SLOTS, N_KV_HEADS, BAND + n_ctx, Q_PER_KV * Q_BLOCK), F32),
            pltpu.VMEM((Q_SLOTS, N_KV_HEADS, BAND + n_ctx, Q_PER_KV * Q_BLOCK), BF16),
        ],
        compiler_params=pltpu.CompilerParams(
            dimension_semantics=("arbitrary", "arbitrary"),
            vmem_limit_bytes=VMEM_LIMIT_BYTES),
        name="lat_attention",
    )(sink, q, k, vt, k, vt)
    if not with_ctx_queries:
        return (o, o, 0)
    o_ctx = pl.pallas_call(
        functools.partial(_ctx_attn_kernel, layer=layer),
        grid=(bsz,),
        in_specs=[
            smem,
            pl.BlockSpec((n_ctx, d), lambda b: (ctx_blk0 + b, 0)),
            ctx_keys, ctx_vals,
        ],
        out_specs=pl.BlockSpec((n_ctx, d), lambda b: (b, 0)),
        out_shape=jax.ShapeDtypeStruct((bsz * n_ctx, d), BF16),
        scratch_shapes=[
            pltpu.VMEM((N_KV_HEADS, n_ctx, Q_PER_KV * n_ctx), F32),
            pltpu.VMEM((N_KV_HEADS, n_ctx, Q_PER_KV * n_ctx), BF16),
        ],
        compiler_params=pltpu.CompilerParams(
            dimension_semantics=("arbitrary",),
            vmem_limit_bytes=VMEM_LIMIT_BYTES),
        name="ctx_attention",
    )(sink, q, k, vt)
    return (o, o_ctx, 0)


def _merge_kernel(xl_ref, xc_ref, hf_ref, hb_ref, gr_ref, ol_ref, oc_ref, gl_ref, mod_ref,
                  g_ref, wr_ref, wa_ref, wo_ref, out_ref, *, n_lat_tiles, has_ctx_tiles):
    d = D_MODEL
    nb = N_RNN_BLOCKS
    tm = xl_ref.shape[0]
    is_lat = pl.program_id(0) < n_lat_tiles

    def stream_tile(lat_ref, ctx_ref):
        return jnp.where(is_lat, lat_ref[...], ctx_ref[...]) if has_ctx_tiles else lat_ref[...]

    h = jnp.concatenate(
        [hf_ref[pl.ds(n, tm, stride=nb), :] + hb_ref[pl.ds(n, tm, stride=nb), :]
         for n in range(nb)], axis=-1)
    g = gr_ref[...].astype(F32)
    hg = h * g
    y = (hg * jnp.tanh(g * (GELU_C1 + GELU_C2 * (g * g))) + hg).astype(BF16)
    ya = _dot(y, wr_ref[...])
    yb = _dot(stream_tile(ol_ref, oc_ref), wa_ref[...])
    ta = jnp.tanh(gl_ref[:, 0:d].astype(F32))
    tb = jnp.tanh(gl_ref[:, d:2 * d].astype(F32))
    mix = ((ya * ta + ya) + (yb * tb + yb)).astype(BF16)
    m = _dot(mix, wo_ref[...])
    out_ref[...] = stream_tile(xl_ref, xc_ref) + _rms(m) * (mod_ref[2:3, :] * g_ref[...])


def _merge(x_stream, hf, hb, gr, o_stream, gl, mod, g_post, w_o_rnn, w_o_attn, w_out,
           layer, geo, n_tiles):
    d = D_MODEL
    tm = geo["tm"]
    n_lat_tiles = geo["n_lat_tiles"]
    row = lambda i: (i, 0)
    tile = pl.BlockSpec((tm, d), row)
    slab = pl.BlockSpec((tm * N_RNN_BLOCKS, RNN_BLOCK_W), row)
    return pl.pallas_call(
        functools.partial(_merge_kernel, n_lat_tiles=n_lat_tiles,
                          has_ctx_tiles=n_tiles > n_lat_tiles),
        grid=(n_tiles,),
        in_specs=[
            *_stream_specs(x_stream, tm, d, n_lat_tiles), slab, slab, tile,
            *_stream_specs(o_stream, tm, d, n_lat_tiles),
            pl.BlockSpec((tm, 2 * d), row),
            _mod_spec(geo, layer),
            _resident_layer(g_post, layer),
            _resident(w_o_rnn.shape), _resident(w_o_attn.shape), _resident(w_out.shape),
        ],
        out_specs=tile,
        out_shape=jax.ShapeDtypeStruct((n_tiles * tm, d), F32),
        compiler_params=pltpu.CompilerParams(
            dimension_semantics=("arbitrary",),
            vmem_limit_bytes=VMEM_LIMIT_BYTES),
        name="merge",
    )(x_stream[0], x_stream[1], hf, hb, gr, o_stream[0], o_stream[1], gl, mod,
      g_post, w_o_rnn, w_o_attn, w_out)


def _ffn_kernel(x_ref, mod_ref, gpre_ref, gpost_ref, w1_ref, w2_ref, *rest, n_casts):
    cast_in, out_ref, cast_out = rest[:2 * n_casts], rest[2 * n_casts], rest[2 * n_casts + 1:]
    _run_side_casts(cast_in, cast_out)
    sub = x_ref.shape[0] // FFN_SUBTILES
    bounds = list(range(0, D_FF, FFN_CHUNK)) + [D_FF]
    n_chunks = len(bounds) - 1

    def normed(s):
        x = x_ref[s * sub:(s + 1) * sub, :]
        h = _rms(x) * (gpre_ref[...] * (1.0 + mod_ref[4:5, :])) + mod_ref[3:4, :]
        return h.astype(BF16)

    def first_layer(h, c):
        lo, hi = bounds[c], bounds[c + 1]
        return _dot(h, w1_ref[:, lo:hi]), _dot(h, w1_ref[:, D_FF + lo:D_FF + hi])

    h_next = normed(0)
    for s in range(FFN_SUBTILES):
        h = h_next
        if s + 1 < FFN_SUBTILES:
            h_next = normed(s + 1)
        f = None
        nxt = first_layer(h, 0)
        for c in range(n_chunks):
            half_gate, up = nxt
            if c + 1 < n_chunks:
                nxt = first_layer(h, c + 1)
            act = ((half_gate * jnp.tanh(half_gate) + half_gate) * up).astype(BF16)
            part = _dot(act, w2_ref[bounds[c]:bounds[c + 1], :])
            f = part if f is None else f + part
        rows = slice(s * sub, (s + 1) * sub)
        out_ref[rows, :] = x_ref[rows, :] + _rms(f) * (mod_ref[5:6, :] * gpost_ref[...])


def _ffn(x_all, mod, g_pre, g_post, w1, w2, layer, geo, n_tiles, side_casts, cast_layer):
    d = x_all.shape[1]
    tm = geo["tm"] * FFN_SUBTILES
    n_blocks = n_tiles // FFN_SUBTILES
    tile = pl.BlockSpec((tm, d), lambda i: (i, 0))
    cast_in_specs, cast_out_specs, cast_shapes, cast_args = _side_cast_plan(
        side_casts, cast_layer, n_blocks)
    outs = pl.pallas_call(
        functools.partial(_ffn_kernel, n_casts=len(side_casts)),
        grid=(n_blocks,),
        in_specs=[
            tile,
            _mod_spec(geo, layer, FFN_SUBTILES),
            _resident_layer(g_pre, layer), _resident_layer(g_post, layer),
            _resident(w1.shape), _resident(w2.shape),
            *cast_in_specs,
        ],
        out_specs=[tile, *cast_out_specs],
        out_shape=[jax.ShapeDtypeStruct((n_blocks * tm, d), F32), *cast_shapes],
        compiler_params=pltpu.CompilerParams(
            dimension_semantics=("arbitrary",),
            vmem_limit_bytes=VMEM_LIMIT_BYTES),
        name="ffn",
    )(x_all, mod, g_pre, g_post, w1, w2, *cast_args)
    return outs[0], outs[1:]


def _rope_tables(seq_len, pad_rows):
    f32 = np.float32
    n_rows = seq_len // GRID_W
    inv = f32(ROPE_BASE) ** (-np.arange(N_FREQ, dtype=f32) / f32(N_FREQ))
    ang = np.arange(max(n_rows, GRID_W), dtype=f32)[:, None] * inv[None, :]
    cos_u, sin_u = np.cos(ang), np.sin(ang)
    by_row = lambda t: np.repeat(t[:n_rows], GRID_W, axis=0)
    by_col = lambda t: np.tile(t[:GRID_W], (n_rows, 1))
    cos = np.concatenate([by_row(cos_u)] * 2 + [by_col(cos_u)] * 2, axis=-1)
    sin = np.concatenate(
        [-by_row(sin_u), by_row(sin_u), -by_col(sin_u), by_col(sin_u)], axis=-1)
    cos = np.concatenate([cos, np.ones((pad_rows, HEAD_DIM), f32)], axis=0)
    sin = np.concatenate([sin, np.zeros((pad_rows, HEAD_DIM), f32)], axis=0)
    scale = f32(HEAD_DIM ** -0.5 * LOG2_E)
    return tuple(jnp.asarray(t.astype(f32)) for t in (cos * scale, sin * scale, cos, sin))


def kernel(x, c, ctx, c_ctx, w_mod, b_mod, g_mix_pre, g_mix_post, g_ffn_pre, g_ffn_post, w_in, conv_w, conv_b, lru_wa, lru_ba, lru_wx, lru_bx, lru_lam, attn_sink, w_o_rnn, w_o_attn, w_out, w_ffn_in, w_ffn_out):
    bsz, seq_len, d = x.shape
    n_ctx = ctx.shape[1]
    depth = w_mod.shape[0]
    assert d == D_MODEL and n_ctx == CHUNK and seq_len % CHUNK == 0 and seq_len >= BAND
    assert bsz + 1 <= SUBLANES and seq_len % GRID_W == 0

    n_lat_rows, n_ctx_rows = bsz * seq_len, bsz * n_ctx
    big = 512 * FFN_SUBTILES
    tm = 512 if (n_ctx_rows % big == 0 and seq_len % big == 0) else CHUNK
    assert n_ctx_rows % (tm * FFN_SUBTILES) == 0 and seq_len % (tm * FFN_SUBTILES) == 0
    n_lat_tiles, n_ctx_tiles = n_lat_rows // tm, n_ctx_rows // tm
    tiles_per_batch = seq_len // tm
    geo = {
        "batch": bsz, "seq": seq_len, "ctx": n_ctx, "tm": tm,
        "lat_chunks": seq_len // CHUNK,
        "n_lat_tiles": n_lat_tiles, "n_ctx_tiles": n_ctx_tiles,
        "mod_row": lambda i: jnp.where(i < n_lat_tiles, i // tiles_per_batch, bsz),
        "table_block": lambda i: jnp.where(
            i < n_lat_tiles, i % tiles_per_batch, tiles_per_batch + i - n_lat_tiles),
    }

    cs = jnp.concatenate(
        [c, c_ctx[None, :], jnp.zeros((SUBLANES - bsz - 1, d), F32)], axis=0)
    mod = _modulation(cs, w_mod, b_mod).reshape(depth, SUBLANES, MOD_CHUNKS, d)

    tables = _rope_tables(seq_len, n_ctx_rows)

    w_gate = jnp.concatenate([lru_wa, lru_wx], axis=-1).astype(BF16)
    b_gate = 0.5 * jnp.concatenate(
        [lru_ba.reshape(depth, 2, N_RNN_BLOCKS, 1, RNN_BLOCK_W),
         lru_bx.reshape(depth, 2, N_RNN_BLOCKS, 1, RNN_BLOCK_W)], axis=-1)
    lam = lru_lam.reshape(depth, 2, N_RNN_BLOCKS, 1, RNN_BLOCK_W)
    conv_w_half = (0.5 * conv_w).reshape(depth, CONV_W, N_RNN_BLOCKS, RNN_BLOCK_W)
    conv_b_half = (0.5 * conv_b).reshape(depth, N_RNN_BLOCKS, RNN_BLOCK_W)
    g_mix_pre, g_mix_post, g_ffn_pre, g_ffn_post = (
        g.reshape(depth, 1, d) for g in (g_mix_pre, g_mix_post, g_ffn_pre, g_ffn_post))

    stream = (x.reshape(n_lat_rows, d), ctx.reshape(n_ctx_rows, d), 0)
    ones = lambda n: jnp.ones((1, n), F32)
    halves = lambda n: jnp.full((1, n), 0.5, F32)
    gl_cols = 2 * d
    w_in_scale = jnp.concatenate([ones(w_in.shape[-1] - gl_cols), halves(gl_cols)], axis=1)
    w_in_b = _to_bf16(w_in, 0, 256, w_in_scale)
    in_proj_casts = [
        (w_o_rnn, halves(d)), (w_o_attn, ones(d)), (w_out, halves(d)),
        (w_ffn_in, jnp.concatenate([halves(D_FF), ones(D_FF)], axis=1)),
        (w_ffn_out, ones(d)),
    ]
    for l in range(depth):
        need_ctx = l < depth - 1
        n_out_tiles = n_lat_tiles + n_ctx_tiles if need_ctx else n_lat_tiles
        (xr, gr, q, k, vt, gl), layer_weights = _in_proj(
            stream, mod, g_mix_pre, w_in_b, l, tables, geo, in_proj_casts)
        w_o_rnn_b, w_o_attn_b, w_out_b, w_ffn_in_b, w_ffn_out_b = layer_weights
        hf, hb = _rnn_branch(xr, conv_w_half, conv_b_half, w_gate, b_gate, lam, l, geo)
        o_stream = _attention(q, k, vt, attn_sink, l, geo, need_ctx)
        x_all = _merge(
            stream, hf, hb, gr, o_stream, gl, mod, g_mix_post, w_o_rnn_b, w_o_attn_b, w_out_b,
            l, geo, n_out_tiles)
        next_casts = [(w_in, w_in_scale)] if l + 1 < depth else []
        x_all, next_weights = _ffn(
            x_all, mod, g_ffn_pre, g_ffn_post, w_ffn_in_b, w_ffn_out_b,
            l, geo, n_out_tiles, next_casts, l + 1)
        if next_weights:
            (w_in_b,) = next_weights
        stream = (x_all, x_all, n_lat_tiles)
    return x_all[:n_lat_rows].reshape(bsz, seq_len, d)
```
